```python
import math
import jax
import jax.numpy as jnp
from jax import lax
import numpy as np

D_MODEL = 2048
BATCH = 1
SEQ = 8192
DEPTH = 1
DEC_BATCH = 32
DEC_SEQ = 4
PAST_LEN = 16384
PAGE_SIZE = 128

N_HEADS = 8
N_KV_HEADS = 4
GQA_REP = N_HEADS // N_KV_HEADS
HEAD_DIM = 64
V_DIM = 2 * HEAD_DIM
Q_WIDTH = N_HEADS * 2 * HEAD_DIM
K_WIDTH = N_KV_HEADS * 2 * HEAD_DIM
V_WIDTH = N_KV_HEADS * V_DIM
ATT_WIDTH = N_HEADS * V_DIM
Q_BLOCK = 128
D_SSM = D_MODEL
SSM_HEAD_DIM = 64
N_SSM_HEADS = D_SSM // SSM_HEAD_DIM
N_GROUPS = 4
D_STATE = 128
CONV_WIDTH = 4
CONV_DIM = D_SSM + 2 * N_GROUPS * D_STATE
SSD_CHUNK = 128
N_EXPERT_GROUPS = 4
EXPERTS_PER_GROUP = 8
N_EXPERTS = N_EXPERT_GROUPS * EXPERTS_PER_GROUP
TOP_K_IN_GROUP = 2
D_EXPERT = D_MODEL // 4

IN_WIDTH = Q_WIDTH + K_WIDTH + V_WIDTH + D_SSM + CONV_DIM + N_SSM_HEADS + 2 * D_MODEL
EPS = 1e-6

kernel_name = 'hybrid_diffattn_ssd_hmoe_step'


def rms_norm(x, w):
    xf = x.astype(jnp.float32)
    y = xf * lax.rsqrt(jnp.mean(xf * xf, axis=-1, keepdims=True) + EPS)
    return (y * w.astype(jnp.float32)).astype(x.dtype)


def alibi_slopes():
    return 2.0 ** (-8.0 * jnp.arange(1, N_HEADS + 1, dtype=jnp.float32) / N_HEADS)


def diff_lambda(lp, lam_init):
    f32 = lambda a: a.astype(jnp.float32)
    return (jnp.exp(jnp.sum(f32(lp['lambda_q1']) * f32(lp['lambda_k1'])))
            - jnp.exp(jnp.sum(f32(lp['lambda_q2']) * f32(lp['lambda_k2']))) + lam_init)


def project(x, lp):
    b, l, _ = x.shape
    u = rms_norm(x, lp['norm_attn_w'])
    sizes = (Q_WIDTH, K_WIDTH, V_WIDTH, D_SSM, CONV_DIM, N_SSM_HEADS, D_MODEL, D_MODEL)
    idx = np.cumsum(sizes)[:-1].tolist()
    q, k, v, z, xbc, dt, ga, gs = jnp.split(u @ lp['w_in'], idx, axis=-1)
    q = rms_norm(q.reshape(b, l, N_HEADS, 2, HEAD_DIM), lp['q_norm_w'])
    k = rms_norm(k.reshape(b, l, N_KV_HEADS, 2, HEAD_DIM), lp['k_norm_w'])
    v = v.reshape(b, l, N_KV_HEADS, V_DIM)
    return q, k, v, z, xbc, dt, ga, gs


def diff_attention(q, k, v, q_pos, k_pos, lam):
    b, lq = q.shape[:2]
    qg = q.reshape(b, lq, N_KV_HEADS, GQA_REP, 2, HEAD_DIM)
    s = jnp.einsum('bqgrcd,bkgcd->bgrcqk', qg, k,
                   preferred_element_type=jnp.float32) * (HEAD_DIM ** -0.5)
    dist = (q_pos[:, None] - k_pos[None, :]).astype(jnp.float32)
    s = s - alibi_slopes().reshape(N_KV_HEADS, GQA_REP, 1, 1, 1) * dist
    s = jnp.where(q_pos[:, None] >= k_pos[None, :], s, -jnp.inf)
    p = jax.nn.softmax(s, axis=-1)
    w = p[:, :, :, 0] - lam * p[:, :, :, 1]
    o = jnp.einsum('bgrqk,bkgv->bqgrv', w, v.astype(jnp.float32))
    return o.reshape(b, lq, N_HEADS, V_DIM).astype(v.dtype)


def prompt_attention(q, k, v, lam):
    b, l = q.shape[:2]
    k_pos = jnp.arange(l, dtype=jnp.int32)

    def block(i):
        start = i * Q_BLOCK
        qb = lax.dynamic_slice_in_dim(q, start, Q_BLOCK, axis=1)
        q_pos = start + jnp.arange(Q_BLOCK, dtype=jnp.int32)
        return diff_attention(qb, k, v, q_pos, k_pos, lam)

    out = lax.map(block, jnp.arange(l // Q_BLOCK, dtype=jnp.int32))
    return jnp.moveaxis(out, 0, 1).reshape(b, l, N_HEADS, V_DIM)


def segsum(a):
    t = a.shape[-1]
    rep = jnp.broadcast_to(a[..., :, None], a.shape + (t,))
    rep = jnp.where(jnp.tril(jnp.ones((t, t), bool), -1), rep, 0.0)
    cs = jnp.cumsum(rep, axis=-2)
    return jnp.where(jnp.tril(jnp.ones((t, t), bool), 0), cs, -jnp.inf)


def ssd_scan(x, dt, a, bmat, cmat, chunk, init_state):
    b, l, h, p = x.shape
    nc = l // chunk
    rep = h // bmat.shape[2]
    bh = jnp.repeat(bmat, rep, axis=2).reshape(b, nc, chunk, h, D_STATE)
    ch = jnp.repeat(cmat, rep, axis=2).reshape(b, nc, chunk, h, D_STATE)
    xc = (x * dt[..., None]).reshape(b, nc, chunk, h, p)
    adt = jnp.moveaxis((a * dt).reshape(b, nc, chunk, h), 3, 1)
    a_cs = jnp.cumsum(adt, axis=-1)
    decay_in = jnp.exp(segsum(adt))
    cb = jnp.einsum('bclhn,bcshn->bhcls', ch, bh)
    y_diag = jnp.einsum('bhcls,bcshp->bclhp', cb * decay_in, xc)
    decay_states = jnp.exp(a_cs[..., -1:] - a_cs)
    states = jnp.einsum('bclhn,bhcl,bclhp->bchpn', bh, decay_states, xc)
    states = jnp.concatenate([init_state[:, None], states], axis=1)
    chunk_decay = jnp.exp(segsum(jnp.pad(a_cs[..., -1], ((0, 0), (0, 0), (1, 0)))))
    states = jnp.einsum('bhzc,bchpn->bzhpn', chunk_decay, states)
    prev_states, final_state = states[:, :-1], states[:, -1]
    y_off = jnp.einsum('bclhn,bchpn,bhcl->bclhp', ch, prev_states, jnp.exp(a_cs))
    return (y_diag + y_off).reshape(b, l, h, p), final_state


def ssm_branch(z, xbc, dt_raw, conv_buf, init_state, lp):
    b, l, _ = xbc.shape
    full = jnp.concatenate([conv_buf.astype(xbc.dtype), xbc], axis=1)
    acc = lp['conv_b']
    for tap in range(CONV_WIDTH):
        acc = acc + full[:, tap:tap + l] * lp['conv_w'][tap]
    xbc_c = jax.nn.silu(acc)
    new_conv = full[:, l:]
    xs, bs, cs = jnp.split(xbc_c, [D_SSM, D_SSM + N_GROUPS * D_STATE], axis=-1)
    xs = xs.reshape(b, l, N_SSM_HEADS, SSM_HEAD_DIM).astype(jnp.float32)
    bs = bs.reshape(b, l, N_GROUPS, D_STATE).astype(jnp.float32)
    cs = cs.reshape(b, l, N_GROUPS, D_STATE).astype(jnp.float32)
    dt = jax.nn.softplus(dt_raw.astype(jnp.float32) + lp['dt_bias'].astype(jnp.float32))
    a = -jnp.exp(lp['a_log'].astype(jnp.float32))
    chunk = SSD_CHUNK if l % SSD_CHUNK == 0 else l
    y, final_state = ssd_scan(xs, dt, a, bs, cs, chunk, init_state.astype(jnp.float32))
    y = y + lp['d_skip'].astype(jnp.float32)[:, None] * xs
    y = y.reshape(b, l, D_SSM) * jax.nn.silu(z.astype(jnp.float32))
    y = rms_norm(y.reshape(b, l, N_GROUPS, D_SSM // N_GROUPS),
                 lp['ssm_norm_w'].reshape(N_GROUPS, D_SSM // N_GROUPS)).reshape(b, l, D_SSM)
    return y.astype(z.dtype), final_state, new_conv


def hier_moe(u, lp):
    shp = u.shape
    t = u.reshape(-1, shp[-1])
    n = t.shape[0]
    g_logits = (t @ lp['w_group_router']).astype(jnp.float32) + lp['b_group_router'].astype(jnp.float32)
    g_p, g_idx = lax.top_k(jax.nn.softmax(g_logits, axis=-1), 1)
    e_logits = ((t @ lp['w_expert_router']).astype(jnp.float32)
                + lp['b_expert_router'].astype(jnp.float32)).reshape(n, N_EXPERT_GROUPS, EXPERTS_PER_GROUP)
    e_sel = jnp.take_along_axis(e_logits, g_idx[:, :, None], axis=1)[:, 0]
    e_p, e_idx = lax.top_k(jax.nn.softmax(e_sel, axis=-1), TOP_K_IN_GROUP)
    e_p = e_p / jnp.sum(e_p, axis=-1, keepdims=True)
    expert_id = g_idx * EXPERTS_PER_GROUP + e_idx
    combine = jnp.sum(jax.nn.one_hot(expert_id, N_EXPERTS, dtype=jnp.float32)
                      * (g_p * e_p)[..., None], axis=1)
    hg = jnp.einsum('nd,edf->nef', t, lp['w_gate'])
    hu = jnp.einsum('nd,edf->nef', t, lp['w_up'])
    h = jax.nn.silu(hg) * hu * combine[..., None].astype(t.dtype)
    return jnp.einsum('nef,efd->nd', h, lp['w_down']).reshape(shp)


def finish(x, o, s, ga, gs, lp, lam_init):
    b, l, _ = x.shape
    o = (rms_norm(o, lp['subln_w']) * (1.0 - lam_init)).reshape(b, l, ATT_WIDTH)
    merged = jax.nn.sigmoid(ga) * (o @ lp['w_att_out']) + jax.nn.sigmoid(gs) * (s @ lp['w_ssm_out'])
    h = x + merged @ lp['w_o']
    return h + hier_moe(rms_norm(h, lp['norm_ffn_w']), lp)


def layer_prompt(x, lp, lam_init):
    b, l, _ = x.shape
    q, k, v, z, xbc, dt, ga, gs = project(x, lp)
    o = prompt_attention(q, k, v, diff_lambda(lp, lam_init))
    conv0 = jnp.zeros((b, CONV_WIDTH - 1, CONV_DIM), x.dtype)
    ssm0 = jnp.zeros((b, N_SSM_HEADS, SSM_HEAD_DIM, D_STATE), jnp.float32)
    s, ssm_state, conv_state = ssm_branch(z, xbc, dt, conv0, ssm0, lp)
    y = finish(x, o, s, ga, gs, lp, lam_init)
    return y, k.reshape(b, l, N_KV_HEADS, 2 * HEAD_DIM), v, ssm_state, conv_state


def layer_sample(x, k_past, v_past, ssm_state, conv_buf, lp, lam_init):
    b, l, _ = x.shape
    past = k_past.shape[1]
    q, k, v, z, xbc, dt, ga, gs = project(x, lp)
    k_all = jnp.concatenate([k_past.reshape(b, past, N_KV_HEADS, 2, HEAD_DIM).astype(k.dtype), k], axis=1)
    v_all = jnp.concatenate([v_past.astype(v.dtype), v], axis=1)
    q_pos = past + jnp.arange(l, dtype=jnp.int32)
    k_pos = jnp.arange(past + l, dtype=jnp.int32)
    o = diff_attention(q, k_all, v_all, q_pos, k_pos, diff_lambda(lp, lam_init))
    s, new_ssm, new_conv = ssm_branch(z, xbc, dt, conv_buf, ssm_state, lp)
    y = finish(x, o, s, ga, gs, lp, lam_init)
    return y, k.reshape(b, l, N_KV_HEADS, 2 * HEAD_DIM), v, new_ssm, new_conv


def setup_inputs(seed: int = 0) -> dict:
    key = jax.random.key(seed)
    ks = iter(jax.random.split(key, 40))
    f32 = jnp.float32
    nrm = lambda shape, scale: jax.random.normal(next(ks), shape, f32) * scale
    gain = lambda shape: 1.0 + 0.02 * jax.random.normal(next(ks), shape, f32)
    n_pages = PAST_LEN // PAGE_SIZE
    n_used = DEC_BATCH * n_pages
    n_phys = n_used + max(1, n_used // 4)
    x_prompt = nrm((BATCH, SEQ, D_MODEL), 1.0)
    x_sample = nrm((DEC_BATCH, DEC_SEQ, D_MODEL), 1.0)
    cache_k = nrm((DEPTH, n_phys, PAGE_SIZE, N_KV_HEADS, 2 * HEAD_DIM), 1.0)
    cache_v = nrm((DEPTH, n_phys, PAGE_SIZE, N_KV_HEADS, V_DIM), 1.0)
    state_ssm = nrm((DEPTH, DEC_BATCH, N_SSM_HEADS, SSM_HEAD_DIM, D_STATE), 0.5)
    state_conv = nrm((DEPTH, DEC_BATCH, CONV_WIDTH - 1, CONV_DIM), 1.0)
    page_table = jax.random.permutation(next(ks), n_phys)[:n_used].reshape(DEC_BATCH, n_pages).astype(jnp.int32)
    dt0 = jnp.exp(jax.random.uniform(next(ks), (DEPTH, N_SSM_HEADS), f32)
                  * (math.log(0.1) - math.log(1e-3)) + math.log(1e-3))
    dt_bias = dt0 + jnp.log(-jnp.expm1(-dt0))
    a_log = jnp.log(jax.random.uniform(next(ks), (DEPTH, N_SSM_HEADS), f32, 1.0, 16.0))
    return {
        'x_prompt': x_prompt,
        'x_sample': x_sample,
        'cache_k': cache_k,
        'cache_v': cache_v,
        'state_ssm': state_ssm,
        'state_conv': state_conv,
        'page_table': page_table,
        'norm_attn_w': gain((DEPTH, D_MODEL)),
        'w_in': nrm((DEPTH, D_MODEL, IN_WIDTH), D_MODEL ** -0.5),
        'q_norm_w': gain((DEPTH, HEAD_DIM)),
        'k_norm_w': gain((DEPTH, HEAD_DIM)),
        'lambda_q1': nrm((DEPTH, HEAD_DIM), 0.1),
        'lambda_k1': nrm((DEPTH, HEAD_DIM), 0.1),
        'lambda_q2': nrm((DEPTH, HEAD_DIM), 0.1),
        'lambda_k2': nrm((DEPTH, HEAD_DIM), 0.1),
        'subln_w': gain((DEPTH, V_DIM)),
        'w_att_out': nrm((DEPTH, ATT_WIDTH, D_MODEL), ATT_WIDTH ** -0.5),
        'conv_w': nrm((DEPTH, CONV_WIDTH, CONV_DIM), CONV_WIDTH ** -0.5),
        'conv_b': nrm((DEPTH, CONV_DIM), 0.02),
        'dt_bias': dt_bias,
        'a_log': a_log,
        'd_skip': gain((DEPTH, N_SSM_HEADS)),
        'ssm_norm_w': gain((DEPTH, D_SSM)),
        'w_ssm_out': nrm((DEPTH, D_SSM, D_MODEL), D_SSM ** -0.5),
        'w_o': nrm((DEPTH, D_MODEL, D_MODEL), D_MODEL ** -0.5),
        'norm_ffn_w': gain((DEPTH, D_MODEL)),
        'w_group_router': nrm((DEPTH, D_MODEL, N_EXPERT_GROUPS), D_MODEL ** -0.5),
        'b_group_router': nrm((DEPTH, N_EXPERT_GROUPS), 0.01),
        'w_expert_router': nrm((DEPTH, D_MODEL, N_EXPERTS), D_MODEL ** -0.5),
        'b_expert_router': nrm((DEPTH, N_EXPERTS), 0.01),
        'w_gate': nrm((DEPTH, N_EXPERTS, D_MODEL, D_EXPERT), D_MODEL ** -0.5),
        'w_up': nrm((DEPTH, N_EXPERTS, D_MODEL, D_EXPERT), D_MODEL ** -0.5),
        'w_down': nrm((DEPTH, N_EXPERTS, D_EXPERT, D_MODEL), D_EXPERT ** -0.5),
    }


def reference(x_prompt, x_sample, cache_k, cache_v, state_ssm, state_conv, page_table,
              norm_attn_w, w_in, q_norm_w, k_norm_w, lambda_q1, lambda_k1, lambda_q2, lambda_k2,
              subln_w, w_att_out, conv_w, conv_b, dt_bias, a_log, d_skip, ssm_norm_w, w_ssm_out,
              w_o, norm_ffn_w, w_group_router, b_group_router, w_expert_router, b_expert_router,
              w_gate, w_up, w_down):
    db = x_sample.shape[0]
    past = page_table.shape[1] * PAGE_SIZE
    xp, xs = x_prompt, x_sample
    kp_l, vp_l, sp_l, cp_l, ks_l, vs_l, ss_l, cs_l = [], [], [], [], [], [], [], []
    for layer in range(DEPTH):
        lp = dict(norm_attn_w=norm_attn_w[layer], w_in=w_in[layer], q_norm_w=q_norm_w[layer],
                  k_norm_w=k_norm_w[layer], lambda_q1=lambda_q1[layer], lambda_k1=lambda_k1[layer],
                  lambda_q2=lambda_q2[layer], lambda_k2=lambda_k2[layer], subln_w=subln_w[layer],
                  w_att_out=w_att_out[layer], conv_w=conv_w[layer], conv_b=conv_b[layer],
                  dt_bias=dt_bias[layer], a_log=a_log[layer], d_skip=d_skip[layer],
                  ssm_norm_w=ssm_norm_w[layer], w_ssm_out=w_ssm_out[layer], w_o=w_o[layer],
                  norm_ffn_w=norm_ffn_w[layer], w_group_router=w_group_router[layer],
                  b_group_router=b_group_router[layer], w_expert_router=w_expert_router[layer],
                  b_expert_router=b_expert_router[layer], w_gate=w_gate[layer], w_up=w_up[layer],
                  w_down=w_down[layer])
        lam_init = 0.8 - 0.6 * math.exp(-0.3 * layer)
        xp, kp, vp, sp, cp = layer_prompt(xp, lp, lam_init)
        k_past = cache_k[layer, page_table].reshape(db, past, N_KV_HEADS, 2 * HEAD_DIM)
        v_past = cache_v[layer, page_table].reshape(db, past, N_KV_HEADS, V_DIM)
        xs, kn, vn, sn, cn = layer_sample(xs, k_past, v_past, state_ssm[layer], state_conv[layer], lp, lam_init)
        kp_l.append(kp); vp_l.append(vp); sp_l.append(sp); cp_l.append(cp)
        ks_l.append(kn); vs_l.append(vn); ss_l.append(sn); cs_l.append(cn)
    k_prompt, v_prompt = jnp.stack(kp_l), jnp.stack(vp_l)
    ssm_prompt, conv_prompt = jnp.stack(sp_l), jnp.stack(cp_l)
    k_sample, v_sample = jnp.stack(ks_l), jnp.stack(vs_l)
    ssm_sample, conv_sample = jnp.stack(ss_l), jnp.stack(cs_l)
    return (xp, xs, k_prompt, v_prompt, ssm_prompt, conv_prompt, k_sample, v_sample, ssm_sample, conv_sample)
```

```python
import functools
import math

import jax
import jax.numpy as jnp
import numpy as np
from jax import lax
from jax.experimental import pallas as pl
from jax.experimental.pallas import tpu as pltpu

F32 = jnp.float32
BF16 = jnp.bfloat16

D_MODEL = 2048
N_HEADS = 8
N_KV_HEADS = 4
GQA_REP = N_HEADS // N_KV_HEADS
HEAD_DIM = 64
V_DIM = 2 * HEAD_DIM
Q_WIDTH = N_HEADS * 2 * HEAD_DIM
K_WIDTH = N_KV_HEADS * 2 * HEAD_DIM
V_WIDTH = N_KV_HEADS * V_DIM
ATT_WIDTH = N_HEADS * V_DIM
D_SSM = D_MODEL
SSM_HEAD_DIM = 64
N_SSM_HEADS = D_SSM // SSM_HEAD_DIM
N_GROUPS = 4
HEADS_PER_GROUP = N_SSM_HEADS // N_GROUPS
D_STATE = 128
CONV_WIDTH = 4
CONV_DIM = D_SSM + 2 * N_GROUPS * D_STATE
SSD_CHUNK = 128
N_EXPERT_GROUPS = 4
EXPERTS_PER_GROUP = 8
N_EXPERTS = N_EXPERT_GROUPS * EXPERTS_PER_GROUP
D_EXPERT = D_MODEL // 4
PAGE_SIZE = 128
EPS = 1e-6
LAM_INIT = 0.8 - 0.6 * math.exp(-0.3 * 0)

LANES = 128
SUBLANES = 8
NEG_BIG = -1e30
VMEM_LIMIT = 56 * 1024 * 1024

COL_Q = 0
COL_K = COL_Q + Q_WIDTH
COL_V = COL_K + K_WIDTH
COL_Z = COL_V + V_WIDTH
COL_X = COL_Z + D_SSM
COL_B = COL_X + D_SSM
COL_C = COL_B + N_GROUPS * D_STATE
COL_GA = COL_C + N_GROUPS * D_STATE
COL_GS = COL_GA + D_MODEL
PROJ_WIDTH = COL_GS + D_MODEL

ALIBI_SLOPES = [2.0 ** (-8.0 * (h + 1) / N_HEADS) for h in range(N_HEADS)]


def _cparams(sem):
    return pltpu.CompilerParams(dimension_semantics=sem, vmem_limit_bytes=VMEM_LIMIT)


def _dot(a, b):
    return jnp.dot(a, b, preferred_element_type=F32)


def _dot_nt(a, b):
    return lax.dot_general(a, b, (((1,), (1,)), ((), ())), preferred_element_type=F32)


def _dot_tn(a, b):
    return lax.dot_general(a, b, (((0,), (0,)), ((), ())), preferred_element_type=F32)


def _split2(x):
    hi = x.astype(BF16)
    lo = (x - hi.astype(F32)).astype(BF16)
    return hi, lo


def _split3(x):
    hi = x.astype(BF16)
    r = x - hi.astype(F32)
    mid = r.astype(BF16)
    lo = (r - mid.astype(F32)).astype(BF16)
    return hi, mid, lo


def _dot_x2(x, sel):
    hi, lo = _split2(x)
    return _dot(hi, sel) + _dot(lo, sel)


def _dot_x3(x, sel):
    hi, mid, lo = _split3(x)
    return _dot(hi, sel) + _dot(mid, sel) + _dot(lo, sel)


def _sigmoid(x):
    return 1.0 / (1.0 + jnp.exp(-x))


def _silu(x):
    return x * _sigmoid(x)


def _softplus(x):
    return jnp.maximum(x, 0.0) + jnp.log1p(jnp.exp(-jnp.abs(x)))


def _proj_kernel(x_ref, nw_ref, w_ref, wdt_ref, o_ref, dt_ref, u_scr):
    @pl.when(pl.program_id(1) == 0)
    def _():
        x = x_ref[...]
        ms = jnp.mean(x * x, axis=-1, keepdims=True)
        u = (x * lax.rsqrt(ms + EPS) * nw_ref[...]).astype(BF16)
        u_scr[...] = u
        dt_ref[...] = _dot(u, wdt_ref[...])

    o_ref[...] = _dot(u_scr[...], w_ref[...])


def _proj(x, norm_w, w_main, w_dt, tm, tn):
    t = x.shape[0]
    return pl.pallas_call(
        _proj_kernel,
        grid=(t // tm, PROJ_WIDTH // tn),
        in_specs=[
            pl.BlockSpec((tm, D_MODEL), lambda i, j: (i, 0)),
            pl.BlockSpec((1, D_MODEL), lambda i, j: (0, 0)),
            pl.BlockSpec((D_MODEL, tn), lambda i, j: (0, j)),
            pl.BlockSpec((D_MODEL, LANES), lambda i, j: (0, 0)),
        ],
        out_specs=[
            pl.BlockSpec((tm, tn), lambda i, j: (i, j)),
            pl.BlockSpec((tm, LANES), lambda i, j: (i, 0)),
        ],
        out_shape=[
            jax.ShapeDtypeStruct((t, PROJ_WIDTH), F32),
            jax.ShapeDtypeStruct((t, LANES), F32),
        ],
        scratch_shapes=[pltpu.VMEM((tm, D_MODEL), BF16)],
        compiler_params=_cparams(("arbitrary", "arbitrary")),
        name="proj",
    )(x, norm_w, w_main, w_dt)


def _qknorm_kernel(p_ref, qw_ref, kw_ref, g_ref, qn_ref, kn_ref, kb_ref):
    gsum = g_ref[...]
    n_q = Q_WIDTH // LANES
    for c in range((Q_WIDTH + K_WIDTH) // LANES):
        x = p_ref[:, c * LANES:(c + 1) * LANES]
        ss = _dot_x2(x * x, gsum)
        y = x * lax.rsqrt(ss * (1.0 / HEAD_DIM) + EPS)
        if c < n_q:
            qn_ref[:, c * LANES:(c + 1) * LANES] = (
                y * qw_ref[...] * (HEAD_DIM ** -0.5)).astype(BF16)
        else:
            kn = y * kw_ref[...]
            kn_ref[:, (c - n_q) * LANES:(c - n_q + 1) * LANES] = kn
            kb_ref[:, (c - n_q) * LANES:(c - n_q + 1) * LANES] = kn.astype(BF16)


def _qk_norm(proj, q_norm_w, k_norm_w, tm):
    t = proj.shape[0]
    group = np.kron(np.eye(LANES // HEAD_DIM), np.ones((HEAD_DIM, HEAD_DIM)))
    qw = jnp.tile(q_norm_w, LANES // HEAD_DIM).reshape(1, LANES)
    kw = jnp.tile(k_norm_w, LANES // HEAD_DIM).reshape(1, LANES)
    return pl.pallas_call(
        _qknorm_kernel,
        grid=(t // tm,),
        in_specs=[
            pl.BlockSpec((tm, Q_WIDTH + K_WIDTH), lambda i: (i, 0)),
            pl.BlockSpec((1, LANES), lambda i: (0, 0)),
            pl.BlockSpec((1, LANES), lambda i: (0, 0)),
            pl.BlockSpec((LANES, LANES), lambda i: (0, 0)),
        ],
        out_specs=[
            pl.BlockSpec((tm, Q_WIDTH), lambda i: (i, 0)),
            pl.BlockSpec((tm, K_WIDTH), lambda i: (i, 0)),
            pl.BlockSpec((tm, K_WIDTH), lambda i: (i, 0)),
        ],
        out_shape=[
            jax.ShapeDtypeStruct((t, Q_WIDTH), BF16),
            jax.ShapeDtypeStruct((t, K_WIDTH), F32),
            jax.ShapeDtypeStruct((t, K_WIDTH), BF16),
        ],
        compiler_params=_cparams(("arbitrary",)),
        name="qk_norm",
    )(proj, qw, kw, jnp.asarray(group, BF16))


def _diff_lambda(lam_ref):
    lamv = lam_ref[...]
    s1 = jnp.sum(lamv[0:1] * lamv[1:2], axis=1, keepdims=True)
    s2 = jnp.sum(lamv[2:3] * lamv[3:4], axis=1, keepdims=True)
    return jnp.exp(s1) - jnp.exp(s2) + LAM_INIT


def _subln(o, w):
    ms = jnp.mean(o * o, axis=-1, keepdims=True)
    return o * lax.rsqrt(ms + EPS) * w * (1.0 - LAM_INIT)


def _attn_p_kernel(qi_ref, ki_ref, slope_ref, q_ref, k_ref, v_ref, lam_ref, sw_ref,
                   o_ref, qm_scr, m_scr, l_scr, acc_scr, *, tq):
    g = pl.program_id(0)
    t = pl.program_id(1)
    qi = qi_ref[t]
    ki = ki_ref[t]
    n_sub = GQA_REP * 2

    @pl.when(ki == 0)
    def _():
        lane = lax.broadcasted_iota(jnp.int32, (tq, LANES), 1)
        for r in range(GQA_REP):
            qq = q_ref[:, r * LANES:(r + 1) * LANES]
            qm_scr[2 * r] = jnp.where(lane < HEAD_DIM, qq, jnp.zeros_like(qq))
            qm_scr[2 * r + 1] = jnp.where(lane >= HEAD_DIM, qq, jnp.zeros_like(qq))
        m_scr[...] = jnp.full(m_scr.shape, NEG_BIG, F32)
        l_scr[...] = jnp.zeros(l_scr.shape, F32)
        acc_scr[...] = jnp.zeros(acc_scr.shape, F32)

    k = k_ref[...]
    v = v_ref[...].astype(BF16)
    row = lax.broadcasted_iota(jnp.int32, (tq, tq), 0)
    col = lax.broadcasted_iota(jnp.int32, (tq, tq), 1)
    dist = (row - col + (qi - ki) * tq).astype(F32)
    for r in range(GQA_REP):
        slope = slope_ref[g * GQA_REP + r]
        bias = jnp.where(dist >= 0.0, -slope * dist, NEG_BIG)
        for c in range(2):
            idx = 2 * r + c
            s = _dot_nt(qm_scr[idx], k) + bias
            m_prev = m_scr[idx]
            m_new = jnp.maximum(m_prev, jnp.max(s, axis=1, keepdims=True))
            alpha = jnp.exp(m_prev - m_new)
            p = jnp.exp(s - m_new)
            l_scr[idx] = alpha * l_scr[idx] + jnp.sum(p, axis=1, keepdims=True)
            acc_scr[idx] = alpha * acc_scr[idx] + _dot(p.astype(BF16), v)
            m_scr[idx] = m_new

    @pl.when(ki == qi)
    def _():
        lam = _diff_lambda(lam_ref)
        for r in range(GQA_REP):
            o = (acc_scr[2 * r] / l_scr[2 * r]
                 - lam * (acc_scr[2 * r + 1] / l_scr[2 * r + 1]))
            o_ref[:, r * LANES:(r + 1) * LANES] = _subln(o, sw_ref[...]).astype(BF16)


def _attn_prompt(qn, kb, proj, lam_vecs, subln_w, tq):
    t = qn.shape[0]
    nq = t // tq
    pairs = [(i, j) for i in range(nq) for j in range(i + 1)]
    qi_tab = jnp.asarray([p[0] for p in pairs], jnp.int32)
    ki_tab = jnp.asarray([p[1] for p in pairs], jnp.int32)
    slopes = jnp.asarray(ALIBI_SLOPES, F32)
    v_col = COL_V // V_DIM
    grid_spec = pltpu.PrefetchScalarGridSpec(
        num_scalar_prefetch=3,
        grid=(N_KV_HEADS, len(pairs)),
        in_specs=[
            pl.BlockSpec((tq, GQA_REP * LANES), lambda g, t, qi, ki, sl: (qi[t], g)),
            pl.BlockSpec((tq, LANES), lambda g, t, qi, ki, sl: (ki[t], g)),
            pl.BlockSpec((tq, V_DIM), lambda g, t, qi, ki, sl: (ki[t], v_col + g)),
            pl.BlockSpec((4, HEAD_DIM), lambda g, t, qi, ki, sl: (0, 0)),
            pl.BlockSpec((1, V_DIM), lambda g, t, qi, ki, sl: (0, 0)),
        ],
        out_specs=pl.BlockSpec((tq, GQA_REP * V_DIM), lambda g, t, qi, ki, sl: (qi[t], g)),
        scratch_shapes=[
            pltpu.VMEM((2 * GQA_REP, tq, LANES), BF16),
            pltpu.VMEM((2 * GQA_REP, tq, 1), F32),
            pltpu.VMEM((2 * GQA_REP, tq, 1), F32),
            pltpu.VMEM((2 * GQA_REP, tq, V_DIM), F32),
        ],
    )
    return pl.pallas_call(
        functools.partial(_attn_p_kernel, tq=tq),
        grid_spec=grid_spec,
        out_shape=jax.ShapeDtypeStruct((t, ATT_WIDTH), BF16),
        compiler_params=_cparams(("arbitrary", "arbitrary")),
        name="attn_p",
    )(qi_tab, ki_tab, slopes, qn, kb, proj, lam_vecs, subln_w.reshape(1, V_DIM))


PAGES_PER_STEP = 8
ROWS_S = 2 * 4 * N_HEADS


def _attn_s_kernel(pt_ref, q_ref, b0_ref, sl_ref, bn_ref, kn_ref, vn_ref, lam_ref, sw_ref,
                   *rest, n_steps):
    k_refs = rest[:PAGES_PER_STEP]
    v_refs = rest[PAGES_PER_STEP:2 * PAGES_PER_STEP]
    o_ref = rest[2 * PAGES_PER_STEP]
    m_scr, l_scr, acc_scr = rest[2 * PAGES_PER_STEP + 1:]
    s_id = pl.program_id(1)

    @pl.when(s_id == 0)
    def _():
        m_scr[...] = jnp.full(m_scr.shape, NEG_BIG, F32)
        l_scr[...] = jnp.zeros(l_scr.shape, F32)
        acc_scr[...] = jnp.zeros(acc_scr.shape, F32)

    q = q_ref[0]

    def update(scores, values):
        m_prev = m_scr[...]
        m_new = m_prev
        for sc in scores:
            m_new = jnp.maximum(m_new, jnp.max(sc, axis=1, keepdims=True))
        alpha = jnp.exp(m_prev - m_new)
        l_new = alpha * l_scr[...]
        acc = alpha * acc_scr[...]
        for sc, vv in zip(scores, values):
            p = jnp.exp(sc - m_new)
            l_new = l_new + jnp.sum(p, axis=1, keepdims=True)
            acc = acc + _dot(p.astype(BF16), vv)
        m_scr[...] = m_new
        l_scr[...] = l_new
        acc_scr[...] = acc

    scores, values = [], []
    for i in range(PAGES_PER_STEP):
        page = (s_id * PAGES_PER_STEP + i).astype(F32)
        bias = b0_ref[...] + sl_ref[...] * page
        scores.append(_dot_nt(q, k_refs[i][0].astype(BF16)) + bias)
        values.append(v_refs[i][0].astype(BF16))
    update(scores, values)

    @pl.when(s_id == n_steps - 1)
    def _():
        sc = _dot_nt(q, kn_ref[0].astype(BF16)) + bn_ref[...]
        update([sc], [vn_ref[0].astype(BF16)])
        lam = _diff_lambda(lam_ref)
        half = ROWS_S // 2
        o1 = acc_scr[0:half] / l_scr[0:half]
        o2 = acc_scr[half:ROWS_S] / l_scr[half:ROWS_S]
        o_ref[0] = _subln(o1 - lam * o2, sw_ref[...]).astype(BF16)


def _attn_sample(qn_s, kn_s, v_s, cache_k, cache_v, page_table, lam_vecs, subln_w):
    db, n_pages = page_table.shape
    dec_seq = qn_s.shape[0] // db
    past = n_pages * PAGE_SIZE
    n_steps = n_pages // PAGES_PER_STEP
    page_rows = PAGE_SIZE * N_KV_HEADS
    n_phys = cache_k.shape[0]
    ck = cache_k.reshape(n_phys, page_rows, 2 * HEAD_DIM)
    cv = cache_v.reshape(n_phys, page_rows, V_DIM)

    q5 = qn_s.reshape(db, dec_seq, N_HEADS, 2, HEAD_DIM)
    zeros = jnp.zeros_like(q5[:, :, :, 0])
    q_all = jnp.stack([jnp.concatenate([q5[:, :, :, 0], zeros], axis=-1),
                       jnp.concatenate([zeros, q5[:, :, :, 1]], axis=-1)], axis=1)
    q_all = q_all.reshape(db, ROWS_S, LANES)

    r = np.arange(ROWS_S)
    tok_r = (r % (dec_seq * N_HEADS)) // N_HEADS
    head_r = r % N_HEADS
    slope_r = np.asarray(ALIBI_SLOPES)[head_r]
    c = np.arange(page_rows)
    key_c, grp_c = c // N_KV_HEADS, c % N_KV_HEADS
    same = (head_r[:, None] // GQA_REP) == grp_c[None, :]
    b0 = np.where(same, -slope_r[:, None] * (past + tok_r[:, None] - key_c[None, :]), NEG_BIG)
    sl = np.broadcast_to((slope_r * PAGE_SIZE)[:, None], (ROWS_S, 1))
    cn = np.arange(LANES)
    tok_c, grp_n = cn // N_KV_HEADS, cn % N_KV_HEADS
    ok = ((head_r[:, None] // GQA_REP) == grp_n[None, :]) & (tok_c[None, :] <= tok_r[:, None])
    bn = np.where(ok, -slope_r[:, None] * (tok_r[:, None] - tok_c[None, :]), NEG_BIG)

    new_rows = dec_seq * N_KV_HEADS
    kn_pad = jnp.pad(kn_s.reshape(db, new_rows, LANES), ((0, 0), (0, LANES - new_rows), (0, 0)))
    vn_pad = jnp.pad(v_s.reshape(db, new_rows, LANES), ((0, 0), (0, LANES - new_rows), (0, 0)))

    def const(shape):
        return pl.BlockSpec(shape, lambda b, s, pt: (0,) * len(shape))

    def page_spec(i):
        return pl.BlockSpec(
            (1, page_rows, LANES),
            lambda b, s, pt: (pt[b * n_pages + s * PAGES_PER_STEP + i], 0, 0))

    grid_spec = pltpu.PrefetchScalarGridSpec(
        num_scalar_prefetch=1,
        grid=(db, n_steps),
        in_specs=[
            pl.BlockSpec((1, ROWS_S, LANES), lambda b, s, pt: (b, 0, 0)),
            const((ROWS_S, page_rows)),
            const((ROWS_S, 1)),
            const((ROWS_S, LANES)),
            pl.BlockSpec((1, LANES, LANES), lambda b, s, pt: (b, 0, 0)),
            pl.BlockSpec((1, LANES, LANES), lambda b, s, pt: (b, 0, 0)),
            const((4, HEAD_DIM)),
            const((1, V_DIM)),
        ] + [page_spec(i) for i in range(PAGES_PER_STEP)] * 2,
        out_specs=pl.BlockSpec((1, ROWS_S // 2, V_DIM), lambda b, s, pt: (b, 0, 0)),
        scratch_shapes=[
            pltpu.VMEM((ROWS_S, 1), F32),
            pltpu.VMEM((ROWS_S, 1), F32),
            pltpu.VMEM((ROWS_S, V_DIM), F32),
        ],
    )
    o = pl.pallas_call(
        functools.partial(_attn_s_kernel, n_steps=n_steps),
        grid_spec=grid_spec,
        out_shape=jax.ShapeDtypeStruct((db, ROWS_S // 2, V_DIM), BF16),
        compiler_params=_cparams(("arbitrary", "arbitrary")),
        name="attn_s",
    )(page_table.reshape(-1), q_all, jnp.asarray(b0, F32), jnp.asarray(sl, F32),
      jnp.asarray(bn, F32), kn_pad, vn_pad, lam_vecs, subln_w.reshape(1, V_DIM),
      *([ck] * PAGES_PER_STEP), *([cv] * PAGES_PER_STEP))
    return o.reshape(db * dec_seq, ATT_WIDTH)


HALO = SUBLANES


def _ssd_kernel(xs_ref, b_ref, c_ref, z_ref, dt_ref, dtt_ref, halo_ref, init_ref,
                cw_ref, cb_ref, dtb_ref, dtbt_ref, a_ref, at_ref, dsk_ref, nw_ref,
                tri_ref, trit_ref, exp_ref, sel_ref,
                y_ref, fin_ref, win_scr, state_scr, *, rows_in, n_valid):
    ci = pl.program_id(1)
    n_chunks = pl.num_programs(1)
    lc = SSD_CHUNK
    bc_w = N_GROUPS * D_STATE

    @pl.when(ci == 0)
    def _():
        state_scr[...] = init_ref[0]
        win_scr[0:HALO, :] = halo_ref[0]

    if rows_in < lc:
        win_scr[HALO:HALO + lc, :] = jnp.zeros((lc, CONV_DIM), F32)
    win_scr[HALO:HALO + rows_in, 0:D_SSM] = xs_ref[0]
    win_scr[HALO:HALO + rows_in, D_SSM:D_SSM + bc_w] = b_ref[0]
    win_scr[HALO:HALO + rows_in, D_SSM + bc_w:CONV_DIM] = c_ref[0]

    acc = cb_ref[...]
    for tap in range(CONV_WIDTH):
        off = HALO - (CONV_WIDTH - 1) + tap
        acc = acc + win_scr[off:off + lc, :] * cw_ref[tap:tap + 1, :]
    conv = _silu(acc)
    win_scr[0:HALO, :] = win_scr[lc:lc + HALO, :]
    xs = conv[:, 0:D_SSM]
    bm = conv[:, D_SSM:D_SSM + bc_w].astype(BF16)
    cm = conv[:, D_SSM + bc_w:CONV_DIM].astype(BF16)

    if rows_in < lc:
        dt_in = jnp.concatenate([dt_ref[0], jnp.zeros((lc - rows_in, LANES), F32)], axis=0)
        dtt_in = jnp.concatenate(
            [dtt_ref[0], jnp.zeros((N_SSM_HEADS, lc - rows_in), F32)], axis=1)
    else:
        dt_in, dtt_in = dt_ref[0], dtt_ref[0]
    rowi = lax.broadcasted_iota(jnp.int32, (lc, LANES), 0)
    coli = lax.broadcasted_iota(jnp.int32, (N_SSM_HEADS, lc), 1)
    dt = jnp.where(rowi < n_valid, _softplus(dt_in + dtb_ref[...]), 0.0)
    dtt = jnp.where(coli < n_valid, _softplus(dtt_in + dtbt_ref[...]), 0.0)
    a_cs = _dot_x3_left(tri_ref[...], dt * a_ref[...])
    a_cst = _dot_x3(dtt * at_ref[...], trit_ref[...])
    a_last = a_cs[lc - 1:lc, :]
    exp_cs = jnp.exp(a_cs)
    exp_rest = jnp.exp(a_last - a_cs)
    expand = exp_ref[...]
    dtx = _dot_x2(dt, expand)
    ecx = _dot_x2(exp_cs, expand)
    erx = _dot_x2(exp_rest, expand)
    xc = xs * dtx
    xcb = xc.astype(BF16)
    xcd = (xc * erx).astype(BF16)

    last_t = jnp.exp(a_cst[:, lc - 1:lc])
    rdec = _dot_x2_left(sel_ref[...], jnp.broadcast_to(last_t, (N_SSM_HEADS, D_STATE)))

    tril = (lax.broadcasted_iota(jnp.int32, (lc, lc), 0)
            >= lax.broadcasted_iota(jnp.int32, (lc, lc), 1))
    lane = lax.broadcasted_iota(jnp.int32, (lc, LANES), 1)
    gw = HEADS_PER_GROUP * SSM_HEAD_DIM
    y_parts = []
    for g in range(N_GROUPS):
        bg = bm[:, g * D_STATE:(g + 1) * D_STATE]
        cg = cm[:, g * D_STATE:(g + 1) * D_STATE]
        cb = _dot_nt(cg, bg)
        st = state_scr[g * gw:(g + 1) * gw, :]
        y_off = _dot_nt(cg, st.astype(BF16)) * ecx[:, g * gw:(g + 1) * gw]
        new_st = _dot_tn(xcd[:, g * gw:(g + 1) * gw], bg)
        state_scr[g * gw:(g + 1) * gw, :] = st * rdec[g * gw:(g + 1) * gw, :] + new_st
        for j in range(HEADS_PER_GROUP // 2):
            pair = g * (HEADS_PER_GROUP // 2) + j
            blk = xcb[:, pair * LANES:(pair + 1) * LANES]
            y_pair = None
            for half in range(2):
                h = 2 * pair + half
                seg = a_cs[:, h:h + 1] - a_cst[h:h + 1, :]
                decay = jnp.exp(jnp.where(tril, seg, NEG_BIG))
                mh = (cb * decay).astype(BF16)
                keep = (lane < SSM_HEAD_DIM) if half == 0 else (lane >= SSM_HEAD_DIM)
                part = _dot(mh, jnp.where(keep, blk, jnp.zeros_like(blk)))
                y_pair = part if y_pair is None else y_pair + part
            y_parts.append(y_pair + y_off[:, (pair % (HEADS_PER_GROUP // 2)) * LANES:
                                          (pair % (HEADS_PER_GROUP // 2) + 1) * LANES])
    y = jnp.concatenate(y_parts, axis=1)
    y = y + dsk_ref[...] * xs
    if rows_in < lc:
        z = jnp.concatenate([z_ref[0], jnp.zeros((lc - rows_in, D_SSM), F32)], axis=0)
    else:
        z = z_ref[0]
    y = y * _silu(z)
    gn = D_SSM // N_GROUPS
    outs = []
    for g in range(N_GROUPS):
        yg = y[:, g * gn:(g + 1) * gn]
        ms = jnp.mean(yg * yg, axis=-1, keepdims=True)
        outs.append(yg * lax.rsqrt(ms + EPS) * nw_ref[:, g * gn:(g + 1) * gn])
    out = jnp.concatenate(outs, axis=1).astype(BF16)
    y_ref[0] = out[0:rows_in]

    @pl.when(ci == n_chunks - 1)
    def _():
        fin_ref[0] = state_scr[...]


def _dot_x3_left(sel, x):
    hi, mid, lo = _split3(x)
    return _dot(sel, hi) + _dot(sel, mid) + _dot(sel, lo)


def _dot_x2_left(sel, x):
    hi, lo = _split2(x)
    return _dot(sel, hi) + _dot(sel, lo)


def _ssd(src, col_blocks, dt_raw, halo, init_state, prm, rows_in, n_valid):
    nb, seq = src.shape[0], src.shape[1]
    n_chunks = max(1, seq // SSD_CHUNK)
    bc_w = N_GROUPS * D_STATE
    dtt = jnp.swapaxes(dt_raw[:, :, :N_SSM_HEADS], 1, 2)
    tri = np.tril(np.ones((SSD_CHUNK, SSD_CHUNK)))
    expand = np.zeros((LANES, D_SSM))
    expand[np.arange(D_SSM) // SSM_HEAD_DIM, np.arange(D_SSM)] = 1.0
    sel = expand[:N_SSM_HEADS].T
    cx, cbk, cck, cz = col_blocks

    def const(shape):
        return pl.BlockSpec(shape, lambda b, c: (0,) * len(shape))

    return pl.pallas_call(
        functools.partial(_ssd_kernel, rows_in=rows_in, n_valid=n_valid),
        grid=(nb, n_chunks),
        in_specs=[
            pl.BlockSpec((1, rows_in, D_SSM), lambda b, c: (b, c, cx)),
            pl.BlockSpec((1, rows_in, bc_w), lambda b, c: (b, c, cbk)),
            pl.BlockSpec((1, rows_in, bc_w), lambda b, c: (b, c, cck)),
            pl.BlockSpec((1, rows_in, D_SSM), lambda b, c: (b, c, cz)),
            pl.BlockSpec((1, rows_in, LANES), lambda b, c: (b, c, 0)),
            pl.BlockSpec((1, N_SSM_HEADS, rows_in), lambda b, c: (b, 0, c)),
            pl.BlockSpec((1, HALO, CONV_DIM), lambda b, c: (b, 0, 0)),
            pl.BlockSpec((1, D_SSM, D_STATE), lambda b, c: (b, 0, 0)),
            const((CONV_WIDTH, CONV_DIM)),
            const((1, CONV_DIM)),
            const((1, LANES)),
            const((N_SSM_HEADS, 1)),
            const((1, LANES)),
            const((N_SSM_HEADS, 1)),
            const((1, D_SSM)),
            const((1, D_SSM)),
            const((SSD_CHUNK, SSD_CHUNK)),
            const((SSD_CHUNK, SSD_CHUNK)),
            const((LANES, D_SSM)),
            const((D_SSM, N_SSM_HEADS)),
        ],
        out_specs=[
            pl.BlockSpec((1, rows_in, D_SSM), lambda b, c: (b, c, 0)),
            pl.BlockSpec((1, D_SSM, D_STATE), lambda b, c: (b, 0, 0)),
        ],
        out_shape=[
            jax.ShapeDtypeStruct((nb, seq, D_SSM), BF16),
            jax.ShapeDtypeStruct((nb, D_SSM, D_STATE), F32),
        ],
        scratch_shapes=[
            pltpu.VMEM((HALO + SSD_CHUNK, CONV_DIM), F32),
            pltpu.VMEM((D_SSM, D_STATE), F32),
        ],
        compiler_params=_cparams(("arbitrary", "arbitrary")),
        name="ssd",
    )(src, src, src, src, dt_raw, dtt, halo, init_state,
      prm["conv_w"], prm["conv_b"], prm["dt_bias"], prm["dt_bias_t"], prm["a"], prm["a_t"],
      prm["d_skip"], prm["ssm_norm_w"],
      jnp.asarray(tri, BF16), jnp.asarray(tri.T, BF16), jnp.asarray(expand, BF16),
      jnp.asarray(sel, BF16))


def _merge_kernel(o_ref, s_ref, wa_ref, ws_ref, ga_ref, gs_ref, out_ref):
    a = _dot(o_ref[...], wa_ref[...])
    s = _dot(s_ref[...], ws_ref[...])
    out_ref[...] = (_sigmoid(ga_ref[...]) * a + _sigmoid(gs_ref[...]) * s).astype(BF16)


def _merge(o, s, wa, ws, proj, tm, tn):
    t = o.shape[0]
    ga0, gs0 = COL_GA // tn, COL_GS // tn
    return pl.pallas_call(
        _merge_kernel,
        grid=(t // tm, D_MODEL // tn),
        in_specs=[
            pl.BlockSpec((tm, ATT_WIDTH), lambda i, j: (i, 0)),
            pl.BlockSpec((tm, D_SSM), lambda i, j: (i, 0)),
            pl.BlockSpec((ATT_WIDTH, tn), lambda i, j: (0, j)),
            pl.BlockSpec((D_SSM, tn), lambda i, j: (0, j)),
            pl.BlockSpec((tm, tn), lambda i, j: (i, ga0 + j)),
            pl.BlockSpec((tm, tn), lambda i, j: (i, gs0 + j)),
        ],
        out_specs=pl.BlockSpec((tm, tn), lambda i, j: (i, j)),
        out_shape=jax.ShapeDtypeStruct((t, D_MODEL), BF16),
        compiler_params=_cparams(("arbitrary", "arbitrary")),
        name="merge",
    )(o, s, wa, ws, proj, proj)


def _resid_kernel(x_ref, m_ref, wo_ref, nw_ref, wrh_ref, wrl_ref, br_ref,
                  h_ref, u_ref, comb_ref):
    h = x_ref[...] + _dot(m_ref[...], wo_ref[...])
    h_ref[...] = h
    ms = jnp.mean(h * h, axis=-1, keepdims=True)
    u = h * lax.rsqrt(ms + EPS) * nw_ref[...]
    u_hi, u_lo = _split2(u)
    u_ref[...] = u_hi
    logits = (_dot(u_hi, wrh_ref[...]) + _dot(u_lo, wrh_ref[...])
              + _dot(u_hi, wrl_ref[...]) + br_ref[...])
    lane = lax.broadcasted_iota(jnp.int32, logits.shape, 1)
    lane_f = lane.astype(F32)
    far = float(2 * LANES)

    def first_max(vals):
        top = jnp.max(vals, axis=1, keepdims=True)
        idx = jnp.min(jnp.where(vals == top, lane_f, far), axis=1, keepdims=True)
        return top, idx

    is_group = (lane >= N_EXPERTS) & (lane < N_EXPERTS + N_EXPERT_GROUPS)
    gl = jnp.where(is_group, logits, NEG_BIG)
    g_top, g_idx = first_max(gl)
    g_p = 1.0 / jnp.sum(jnp.exp(gl - g_top), axis=1, keepdims=True)
    lo_lane = (g_idx - N_EXPERTS) * EXPERTS_PER_GROUP
    in_group = (lane_f >= lo_lane) & (lane_f < lo_lane + EXPERTS_PER_GROUP)
    el = jnp.where(in_group, logits, NEG_BIG)
    m1, i1 = first_max(el)
    el2 = jnp.where(lane_f == i1, NEG_BIG, el)
    m2, i2 = first_max(el2)
    e = jnp.exp(m2 - m1)
    w1 = 1.0 / (1.0 + e)
    w2 = e / (1.0 + e)
    comb_ref[...] = (jnp.where(lane_f == i1, g_p * w1, 0.0)
                     + jnp.where(lane_f == i2, g_p * w2, 0.0))


def _resid(x, merged, wo, norm_w, wr_hi, wr_lo, br, tm):
    t = x.shape[0]

    def const(shape):
        return pl.BlockSpec(shape, lambda i: (0,) * len(shape))

    return pl.pallas_call(
        _resid_kernel,
        grid=(t // tm,),
        in_specs=[
            pl.BlockSpec((tm, D_MODEL), lambda i: (i, 0)),
            pl.BlockSpec((tm, D_MODEL), lambda i: (i, 0)),
            const((D_MODEL, D_MODEL)),
            const((1, D_MODEL)),
            const((D_MODEL, LANES)),
            const((D_MODEL, LANES)),
            const((1, LANES)),
        ],
        out_specs=[
            pl.BlockSpec((tm, D_MODEL), lambda i: (i, 0)),
            pl.BlockSpec((tm, D_MODEL), lambda i: (i, 0)),
            pl.BlockSpec((tm, LANES), lambda i: (i, 0)),
        ],
        out_shape=[
            jax.ShapeDtypeStruct((t, D_MODEL), F32),
            jax.ShapeDtypeStruct((t, D_MODEL), BF16),
            jax.ShapeDtypeStruct((t, LANES), F32),
        ],
        compiler_params=_cparams(("arbitrary",)),
        name="resid",
    )(x, merged, wo, norm_w, wr_hi, wr_lo, br)


def _moe_kernel(u_ref, comb_ref, h_ref, wg_ref, wu_ref, wd_ref, y_ref):
    e = pl.program_id(1)

    @pl.when(e == 0)
    def _():
        y_ref[...] = h_ref[...]

    u = u_ref[...]
    lane = lax.broadcasted_iota(jnp.int32, comb_ref.shape, 1)
    cw = jnp.sum(jnp.where(lane == e, comb_ref[...], 0.0), axis=1, keepdims=True)
    hid = _silu(_dot(u, wg_ref[0])) * _dot(u, wu_ref[0]) * cw
    y_ref[...] += _dot(hid.astype(BF16), wd_ref[0])


def _moe(u, comb, h, wg, wu, wd, tm):
    t = u.shape[0]
    return pl.pallas_call(
        _moe_kernel,
        grid=(t // tm, N_EXPERTS),
        in_specs=[
            pl.BlockSpec((tm, D_MODEL), lambda i, e: (i, 0)),
            pl.BlockSpec((tm, LANES), lambda i, e: (i, 0)),
            pl.BlockSpec((tm, D_MODEL), lambda i, e: (i, 0)),
            pl.BlockSpec((1, D_MODEL, D_EXPERT), lambda i, e: (e, 0, 0)),
            pl.BlockSpec((1, D_MODEL, D_EXPERT), lambda i, e: (e, 0, 0)),
            pl.BlockSpec((1, D_EXPERT, D_MODEL), lambda i, e: (e, 0, 0)),
        ],
        out_specs=pl.BlockSpec((tm, D_MODEL), lambda i, e: (i, 0)),
        out_shape=jax.ShapeDtypeStruct((t, D_MODEL), F32),
        compiler_params=_cparams(("arbitrary", "arbitrary")),
        name="moe",
    )(u, comb, h, wg, wu, wd)


def _layer_tokens(x2d, w, tm_proj, tm_small):
    proj, dt_raw = _proj(x2d, w["norm_attn_w"], w["w_main"], w["w_dt"], tm_proj, 1024)
    qn, kn, kb = _qk_norm(proj, w["q_norm_w"], w["k_norm_w"], tm_small)
    return proj, dt_raw, qn, kn, kb


def _finish(x2d, o, s, proj, w, tm, tm_resid):
    merged = _merge(o, s, w["w_att_out"], w["w_ssm_out"], proj, tm, 512)
    h, u, comb = _resid(x2d, merged, w["w_o"], w["norm_ffn_w"], w["wr_hi"], w["wr_lo"],
                        w["br"], tm_resid)
    return _moe(u, comb, h, w["w_gate"], w["w_up"], w["w_down"], tm)


def kernel(x_prompt, x_sample, cache_k, cache_v, state_ssm, state_conv, page_table, norm_attn_w, w_in, q_norm_w, k_norm_w, lambda_q1, lambda_k1, lambda_q2, lambda_k2, subln_w, w_att_out, conv_w, conv_b, dt_bias, a_log, d_skip, ssm_norm_w, w_ssm_out, w_o, norm_ffn_w, w_group_router, b_group_router, w_expert_router, b_expert_router, w_gate, w_up, w_down):
    layer = 0
    nb, seq, _ = x_prompt.shape
    db, dec_seq, _ = x_sample.shape

    w_in_l = w_in[layer]
    c_dt = Q_WIDTH + K_WIDTH + V_WIDTH + D_SSM + CONV_DIM
    w_main = jnp.concatenate([w_in_l[:, :c_dt], w_in_l[:, c_dt + N_SSM_HEADS:]], axis=1).astype(BF16)
    w_dt = jnp.pad(w_in_l[:, c_dt:c_dt + N_SSM_HEADS], ((0, 0), (0, LANES - N_SSM_HEADS))).astype(BF16)
    wr = jnp.concatenate([w_expert_router[layer], w_group_router[layer]], axis=1)
    wr = jnp.pad(wr, ((0, 0), (0, LANES - wr.shape[1])))
    wr_hi = wr.astype(BF16)
    wr_lo = (wr - wr_hi.astype(F32)).astype(BF16)
    br = jnp.concatenate([b_expert_router[layer], b_group_router[layer]])
    br = jnp.pad(br, (0, LANES - br.shape[0])).reshape(1, LANES)
    pad_h = (0, LANES - N_SSM_HEADS)
    w = dict(
        norm_attn_w=norm_attn_w[layer].reshape(1, D_MODEL), w_main=w_main, w_dt=w_dt,
        q_norm_w=q_norm_w[layer], k_norm_w=k_norm_w[layer],
        w_att_out=w_att_out[layer].astype(BF16), w_ssm_out=w_ssm_out[layer].astype(BF16),
        w_o=w_o[layer].astype(BF16), norm_ffn_w=norm_ffn_w[layer].reshape(1, D_MODEL),
        wr_hi=wr_hi, wr_lo=wr_lo, br=br,
        w_gate=w_gate[layer].astype(BF16), w_up=w_up[layer].astype(BF16),
        w_down=w_down[layer].astype(BF16),
    )
    ssm_prm = dict(
        conv_w=conv_w[layer], conv_b=conv_b[layer].reshape(1, CONV_DIM),
        dt_bias=jnp.pad(dt_bias[layer], pad_h).reshape(1, LANES),
        dt_bias_t=dt_bias[layer].reshape(N_SSM_HEADS, 1),
        a=jnp.pad(-jnp.exp(a_log[layer]), pad_h).reshape(1, LANES),
        a_t=(-jnp.exp(a_log[layer])).reshape(N_SSM_HEADS, 1),
        d_skip=jnp.repeat(d_skip[layer], SSM_HEAD_DIM).reshape(1, D_SSM),
        ssm_norm_w=ssm_norm_w[layer].reshape(1, D_SSM),
    )
    lam_vecs = jnp.stack([lambda_q1[layer], lambda_k1[layer], lambda_q2[layer], lambda_k2[layer]])
    sw = subln_w[layer]
    ssd_cols = (COL_X // D_SSM, COL_B // (N_GROUPS * D_STATE), COL_C // (N_GROUPS * D_STATE),
                COL_Z // D_SSM)

    xp = x_prompt.reshape(nb * seq, D_MODEL)
    proj_p, dt_p, qn_p, kn_p, kb_p = _layer_tokens(xp, w, 512, 512)
    o_p = _attn_prompt(qn_p, kb_p, proj_p, lam_vecs, sw, 512)
    s_p, ssm_p = _ssd(
        proj_p.reshape(nb, seq, PROJ_WIDTH), ssd_cols, dt_p.reshape(nb, seq, LANES),
        jnp.zeros((nb, HALO, CONV_DIM), F32), jnp.zeros((nb, D_SSM, D_STATE), F32),
        ssm_prm, SSD_CHUNK, SSD_CHUNK)
    y_p = _finish(xp, o_p, s_p.reshape(nb * seq, D_SSM), proj_p, w, 512, 256)
    keep = CONV_WIDTH - 1
    conv_p = proj_p.reshape(nb, seq, PROJ_WIDTH)[:, seq - keep:, COL_X:COL_X + CONV_DIM]

    xs = x_sample.reshape(db * dec_seq, D_MODEL)
    n_tok = db * dec_seq
    proj_s, dt_s, qn_s, kn_s, kb_s = _layer_tokens(xs, w, n_tok, n_tok)
    v_s = proj_s[:, COL_V:COL_V + V_WIDTH]
    o_s = _attn_sample(qn_s, kn_s, v_s, cache_k[layer], cache_v[layer], page_table, lam_vecs, sw)
    rows_s = SUBLANES
    pad_rows = ((0, 0), (0, rows_s - dec_seq), (0, 0))
    src_s = jnp.pad(proj_s[:, COL_Z:COL_GA].reshape(db, dec_seq, COL_GA - COL_Z), pad_rows)
    halo_s = jnp.pad(state_conv[layer], ((0, 0), (HALO - (CONV_WIDTH - 1), 0), (0, 0)))
    cols_s = ((COL_X - COL_Z) // D_SSM, (COL_B - COL_Z) // (N_GROUPS * D_STATE),
              (COL_C - COL_Z) // (N_GROUPS * D_STATE), 0)
    s_s, ssm_s = _ssd(
        src_s, cols_s, jnp.pad(dt_s.reshape(db, dec_seq, LANES), pad_rows), halo_s,
        state_ssm[layer].reshape(db, D_SSM, D_STATE), ssm_prm, rows_s, dec_seq)
    s_s = s_s[:, :dec_seq].reshape(n_tok, D_SSM)
    y_s = _finish(xs, o_s, s_s, proj_s, w, n_tok, n_tok)
    conv_s = proj_s.reshape(db, dec_seq, PROJ_WIDTH)[:, dec_seq - keep:, COL_X:COL_X + CONV_DIM]

    return (
        y_p.reshape(nb, seq, D_MODEL),
        y_s.reshape(db, dec_seq, D_MODEL),
        kn_p.reshape(1, nb, seq, N_KV_HEADS, 2 * HEAD_DIM),
        proj_p[:, COL_V:COL_V + V_WIDTH].reshape(1, nb, seq, N_KV_HEADS, V_DIM),
        ssm_p.reshape(1, nb, N_SSM_HEADS, SSM_HEAD_DIM, D_STATE),
        conv_p[None],
        kn_s.reshape(1, db, dec_seq, N_KV_HEADS, 2 * HEAD_DIM),
        v_s.reshape(1, db, dec_seq, N_KV_HEADS, V_DIM),
        ssm_s.reshape(1, db, N_SSM_HEADS, SSM_HEAD_DIM, D_STATE),
        conv_s[None],
    )
```

```python
import functools
import math

import jax
import jax.numpy as jnp
import ml_dtypes
import numpy as np
from jax import lax
from jax.experimental import pallas as pl
from jax.experimental.pallas import tpu as pltpu

F32 = jnp.float32
BF16 = jnp.bfloat16

D_MODEL = 2048
N_HEADS = 8
N_KV_HEADS = 4
GQA_REP = N_HEADS // N_KV_HEADS
HEAD_DIM = 64
V_DIM = 2 * HEAD_DIM
Q_WIDTH = N_HEADS * 2 * HEAD_DIM
K_WIDTH = N_KV_HEADS * 2 * HEAD_DIM
V_WIDTH = N_KV_HEADS * V_DIM
ATT_WIDTH = N_HEADS * V_DIM
D_SSM = D_MODEL
SSM_HEAD_DIM = 64
N_SSM_HEADS = D_SSM // SSM_HEAD_DIM
N_GROUPS = 4
HEADS_PER_GROUP = N_SSM_HEADS // N_GROUPS
D_STATE = 128
CONV_WIDTH = 4
CONV_DIM = D_SSM + 2 * N_GROUPS * D_STATE
SSD_CHUNK = 128
N_EXPERT_GROUPS = 4
EXPERTS_PER_GROUP = 8
N_EXPERTS = N_EXPERT_GROUPS * EXPERTS_PER_GROUP
D_EXPERT = D_MODEL // 4
PAGE_SIZE = 128
EPS = 1e-6
LAM_INIT = 0.8 - 0.6 * math.exp(-0.3 * 0)

LANES = 128
SUBLANES = 8
NEG_BIG = -1e30
VMEM_LIMIT = 56 * 1024 * 1024

COL_Q = 0
COL_K = COL_Q + Q_WIDTH
COL_V = COL_K + K_WIDTH
COL_Z = COL_V + V_WIDTH
COL_X = COL_Z + D_SSM
COL_B = COL_X + D_SSM
COL_C = COL_B + N_GROUPS * D_STATE
COL_GA = COL_C + N_GROUPS * D_STATE
COL_GS = COL_GA + D_MODEL
PROJ_WIDTH = COL_GS + D_MODEL

ALIBI_SLOPES = [2.0 ** (-8.0 * (h + 1) / N_HEADS) for h in range(N_HEADS)]


def _cparams(sem):
    return pltpu.CompilerParams(dimension_semantics=sem, vmem_limit_bytes=VMEM_LIMIT)


def _dot(a, b):
    return jnp.dot(a, b, preferred_element_type=F32)


def _dot_nt(a, b):
    return lax.dot_general(a, b, (((1,), (1,)), ((), ())), preferred_element_type=F32)


def _dot_tn(a, b):
    return lax.dot_general(a, b, (((0,), (0,)), ((), ())), preferred_element_type=F32)


def _split2(x):
    hi = x.astype(BF16)
    lo = (x - hi.astype(F32)).astype(BF16)
    return hi, lo


def _split3(x):
    hi = x.astype(BF16)
    r = x - hi.astype(F32)
    mid = r.astype(BF16)
    lo = (r - mid.astype(F32)).astype(BF16)
    return hi, mid, lo


def _dot_x2(x, sel):
    hi, lo = _split2(x)
    return _dot(hi, sel) + _dot(lo, sel)


def _dot_x3(x, sel):
    hi, mid, lo = _split3(x)
    return _dot(hi, sel) + _dot(mid, sel) + _dot(lo, sel)


def _sigmoid(x):
    return 1.0 / (1.0 + jnp.exp(-x))


def _silu(x):
    return x * _sigmoid(x)


def _softplus(x):
    return jnp.maximum(x, 0.0) + jnp.log1p(jnp.exp(-jnp.abs(x)))


def _proj_kernel(x_ref, nw_ref, w_ref, wdt_ref, o_ref, dt_ref, u_scr):
    @pl.when(pl.program_id(1) == 0)
    def _():
        x = x_ref[...]
        ms = jnp.mean(x * x, axis=-1, keepdims=True)
        u = (x * lax.rsqrt(ms + EPS) * nw_ref[...]).astype(BF16)
        u_scr[...] = u
        dt_ref[...] = _dot(u, wdt_ref[...])

    o_ref[...] = _dot(u_scr[...], w_ref[...])


def _proj(x, norm_w, w_main, w_dt, tm, tn):
    t = x.shape[0]
    return pl.pallas_call(
        _proj_kernel,
        grid=(t // tm, PROJ_WIDTH // tn),
        in_specs=[
            pl.BlockSpec((tm, D_MODEL), lambda i, j: (i, 0)),
            pl.BlockSpec((1, D_MODEL), lambda i, j: (0, 0)),
            pl.BlockSpec((D_MODEL, tn), lambda i, j: (0, j)),
            pl.BlockSpec((D_MODEL, LANES), lambda i, j: (0, 0)),
        ],
        out_specs=[
            pl.BlockSpec((tm, tn), lambda i, j: (i, j)),
            pl.BlockSpec((tm, LANES), lambda i, j: (i, 0)),
        ],
        out_shape=[
            jax.ShapeDtypeStruct((t, PROJ_WIDTH), F32),
            jax.ShapeDtypeStruct((t, LANES), F32),
        ],
        scratch_shapes=[pltpu.VMEM((tm, D_MODEL), BF16)],
        compiler_params=_cparams(("arbitrary", "arbitrary")),
        name="proj",
    )(x, norm_w, w_main, w_dt)


LOG2E = math.log2(math.e)
Q_SCALE = LOG2E * HEAD_DIM ** -0.5


def _qknorm_kernel(p_ref, qw_ref, kw_ref, g_ref, qn_ref, kn_ref, kb_ref, vb_ref):
    gsum = g_ref[...]
    n_q = Q_WIDTH // LANES
    for c in range((Q_WIDTH + K_WIDTH) // LANES):
        x = p_ref[:, c * LANES:(c + 1) * LANES]
        ss = _dot_x2(x * x, gsum)
        y = x * lax.rsqrt(ss * (1.0 / HEAD_DIM) + EPS)
        if c < n_q:
            qn_ref[:, c * LANES:(c + 1) * LANES] = (y * qw_ref[...] * Q_SCALE).astype(BF16)
        else:
            kn = y * kw_ref[...]
            kn_ref[:, (c - n_q) * LANES:(c - n_q + 1) * LANES] = kn
            kb_ref[:, (c - n_q) * LANES:(c - n_q + 1) * LANES] = kn.astype(BF16)
    vb_ref[...] = p_ref[:, COL_V:COL_V + V_WIDTH].astype(BF16)


def _qk_norm(proj, q_norm_w, k_norm_w, tm):
    t = proj.shape[0]
    group = np.kron(np.eye(LANES // HEAD_DIM), np.ones((HEAD_DIM, HEAD_DIM)))
    qw = jnp.tile(q_norm_w, LANES // HEAD_DIM).reshape(1, LANES)
    kw = jnp.tile(k_norm_w, LANES // HEAD_DIM).reshape(1, LANES)
    return pl.pallas_call(
        _qknorm_kernel,
        grid=(t // tm,),
        in_specs=[
            pl.BlockSpec((tm, Q_WIDTH + K_WIDTH + V_WIDTH), lambda i: (i, 0)),
            pl.BlockSpec((1, LANES), lambda i: (0, 0)),
            pl.BlockSpec((1, LANES), lambda i: (0, 0)),
            pl.BlockSpec((LANES, LANES), lambda i: (0, 0)),
        ],
        out_specs=[
            pl.BlockSpec((tm, Q_WIDTH), lambda i: (i, 0)),
            pl.BlockSpec((tm, K_WIDTH), lambda i: (i, 0)),
            pl.BlockSpec((tm, K_WIDTH), lambda i: (i, 0)),
            pl.BlockSpec((tm, V_WIDTH), lambda i: (i, 0)),
        ],
        out_shape=[
            jax.ShapeDtypeStruct((t, Q_WIDTH), BF16),
            jax.ShapeDtypeStruct((t, K_WIDTH), F32),
            jax.ShapeDtypeStruct((t, K_WIDTH), BF16),
            jax.ShapeDtypeStruct((t, V_WIDTH), BF16),
        ],
        compiler_params=_cparams(("arbitrary",)),
        name="qk_norm",
    )(proj, qw, kw, jnp.asarray(group, BF16))


def _diff_lambda(lam_ref):
    lamv = lam_ref[...]
    s1 = jnp.sum(lamv[0:1] * lamv[1:2], axis=1, keepdims=True)
    s2 = jnp.sum(lamv[2:3] * lamv[3:4], axis=1, keepdims=True)
    return jnp.exp(s1) - jnp.exp(s2) + LAM_INIT


def _subln(o, w):
    ms = jnp.mean(o * o, axis=-1, keepdims=True)
    return o * lax.rsqrt(ms + EPS) * w * (1.0 - LAM_INIT)


N_SLOPE_PARTS = 3


def _bf16_parts(x, n):
    parts, rem = [], np.float32(x)
    for _ in range(n):
        p = np.float32(rem.astype(ml_dtypes.bfloat16))
        parts.append(float(p))
        rem = np.float32(rem - p)
    return parts


def _alibi_tables(tk):
    qcols = np.zeros((N_HEADS, 16, LANES), np.float32)
    csum = np.zeros((N_HEADS,), np.float32)
    for h, slope in enumerate(ALIBI_SLOPES):
        parts = _bf16_parts(slope * LOG2E, N_SLOPE_PARTS)
        csum[h] = np.float32(sum(np.float32(p) for p in parts))
        for i, p in enumerate(parts):
            qcols[h, :, i] = p * LANES
            qcols[h, :, N_SLOPE_PARTS + i] = p
    pos = np.arange(tk)
    kcols = np.zeros((tk, LANES), np.float32)
    kcols[:, 0:N_SLOPE_PARTS] = (pos // LANES)[:, None]
    kcols[:, N_SLOPE_PARTS:2 * N_SLOPE_PARTS] = (pos % LANES)[:, None]
    return jnp.asarray(qcols, BF16), jnp.asarray(kcols, BF16), jnp.asarray(csum, F32)


def _attn_p_kernel(qi_ref, ki_ref, cf_ref, q_ref, k_ref, v_ref, qc_ref, kc_ref, lam_ref, sw_ref,
                   o_ref, qa_scr, m_scr, l_scr, acc_scr, *, tq):
    g = pl.program_id(0)
    t = pl.program_id(1)
    qi = qi_ref[t]
    ki = ki_ref[t]
    n_sub = GQA_REP * 2
    rows = n_sub * tq
    n_chunk = tq // LANES

    @pl.when(ki == 0)
    def _():
        lane = lax.broadcasted_iota(jnp.int32, (tq, LANES), 1)
        for r in range(GQA_REP):
            qq = q_ref[:, r * LANES:(r + 1) * LANES]
            qc = jnp.broadcast_to(qc_ref[r, 0:1, :], (tq, LANES))
            for c in range(2):
                idx = 2 * r + c
                keep = (lane < HEAD_DIM) if c == 0 else (lane >= HEAD_DIM)
                qa_scr[idx * tq:(idx + 1) * tq, 0:LANES] = jnp.where(keep, qq, jnp.zeros_like(qq))
                qa_scr[idx * tq:(idx + 1) * tq, LANES:2 * LANES] = qc
        m_scr[...] = jnp.full(m_scr.shape, NEG_BIG, F32)
        l_scr[...] = jnp.zeros(l_scr.shape, F32)
        acc_scr[...] = jnp.zeros(acc_scr.shape, F32)

    def step(diag):
        k_aug = jnp.concatenate([k_ref[...], kc_ref[...]], axis=1)
        s_all = _dot_nt(qa_scr[...], k_aug)
        if diag:
            row_in = lax.broadcasted_iota(jnp.int32, (rows, tq), 0) & (tq - 1)
            col = lax.broadcasted_iota(jnp.int32, (rows, tq), 1)
            s_all = jnp.where(col <= row_in, s_all, NEG_BIG)
        block_dist = ((qi - ki) * tq).astype(F32)
        ps, alphas = [], []
        for r in range(GQA_REP):
            off = -cf_ref[g * GQA_REP + r] * block_dist
            lo, hi = r * 2 * tq, (r + 1) * 2 * tq
            chunks = [s_all[lo:hi, j * LANES:(j + 1) * LANES] for j in range(n_chunk)]
            m_prev = m_scr[lo:hi]
            m_blk = jnp.max(functools.reduce(jnp.maximum, chunks), axis=1, keepdims=True) + off
            m_new = jnp.maximum(m_prev, m_blk)
            alpha = jnp.exp2(m_prev - m_new)
            m_sub = m_new - off
            pj = [jnp.exp2(ch - m_sub) for ch in chunks]
            l_scr[lo:hi] = alpha * l_scr[lo:hi] + functools.reduce(jnp.add, pj)
            m_scr[lo:hi] = m_new
            ps.append(jnp.concatenate(pj, axis=1).astype(BF16))
            alphas.append(alpha)
        pv = _dot(jnp.concatenate(ps, axis=0), v_ref[...])
        acc_scr[...] = jnp.concatenate(alphas, axis=0) * acc_scr[...] + pv

    @pl.when(ki < qi)
    def _():
        step(False)

    @pl.when(ki == qi)
    def _():
        step(True)
        lam = _diff_lambda(lam_ref)
        for r in range(GQA_REP):
            i1, i2 = 2 * r * tq, (2 * r + 1) * tq
            l1 = jnp.sum(l_scr[i1:i1 + tq], axis=1, keepdims=True)
            l2 = jnp.sum(l_scr[i2:i2 + tq], axis=1, keepdims=True)
            o = acc_scr[i1:i1 + tq] / l1 - lam * (acc_scr[i2:i2 + tq] / l2)
            o_ref[:, r * LANES:(r + 1) * LANES] = _subln(o, sw_ref[...]).astype(BF16)


def _attn_prompt(qn, kb, vb, lam_vecs, subln_w, tq):
    t = qn.shape[0]
    nq = t // tq
    pairs = [(i, j) for i in range(nq) for j in range(i + 1)]
    qi_tab = jnp.asarray([p[0] for p in pairs], jnp.int32)
    ki_tab = jnp.asarray([p[1] for p in pairs], jnp.int32)
    qcols, kcols, csum = _alibi_tables(tq)
    n_sub = 2 * GQA_REP
    grid_spec = pltpu.PrefetchScalarGridSpec(
        num_scalar_prefetch=3,
        grid=(N_KV_HEADS, len(pairs)),
        in_specs=[
            pl.BlockSpec((tq, GQA_REP * LANES), lambda g, t, qi, ki, cf: (qi[t], g)),
            pl.BlockSpec((tq, LANES), lambda g, t, qi, ki, cf: (ki[t], g)),
            pl.BlockSpec((tq, V_DIM), lambda g, t, qi, ki, cf: (ki[t], g)),
            pl.BlockSpec((GQA_REP, 16, LANES), lambda g, t, qi, ki, cf: (g, 0, 0)),
            pl.BlockSpec((tq, LANES), lambda g, t, qi, ki, cf: (0, 0)),
            pl.BlockSpec((4, HEAD_DIM), lambda g, t, qi, ki, cf: (0, 0)),
            pl.BlockSpec((1, V_DIM), lambda g, t, qi, ki, cf: (0, 0)),
        ],
        out_specs=pl.BlockSpec((tq, GQA_REP * V_DIM), lambda g, t, qi, ki, cf: (qi[t], g)),
        scratch_shapes=[
            pltpu.VMEM((n_sub * tq, 2 * LANES), BF16),
            pltpu.VMEM((n_sub * tq, LANES), F32),
            pltpu.VMEM((n_sub * tq, LANES), F32),
            pltpu.VMEM((n_sub * tq, V_DIM), F32),
        ],
    )
    return pl.pallas_call(
        functools.partial(_attn_p_kernel, tq=tq),
        grid_spec=grid_spec,
        out_shape=jax.ShapeDtypeStruct((t, ATT_WIDTH), BF16),
        compiler_params=_cparams(("arbitrary", "arbitrary")),
        name="attn_p",
    )(qi_tab, ki_tab, csum, qn, kb, vb, qcols, kcols, lam_vecs, subln_w.reshape(1, V_DIM))


PAGES_PER_STEP = 16
ROWS_S = 2 * 4 * N_HEADS


def _attn_s_kernel(pt_ref, q_ref, d0_ref, mask_ref, sl_ref, bn_ref, kn_ref, vn_ref, lam_ref,
                   sw_ref, *rest, n_steps):
    k_refs = rest[:PAGES_PER_STEP]
    v_refs = rest[PAGES_PER_STEP:2 * PAGES_PER_STEP]
    o_ref = rest[2 * PAGES_PER_STEP]
    m_scr, l_scr, acc_scr = rest[2 * PAGES_PER_STEP + 1:]
    s_id = pl.program_id(1)

    @pl.when(s_id == 0)
    def _():
        m_scr[...] = jnp.full(m_scr.shape, NEG_BIG, F32)
        l_scr[...] = jnp.zeros(l_scr.shape, F32)
        acc_scr[...] = jnp.zeros(acc_scr.shape, F32)

    q = q_ref[0]

    def update(scores, values):
        m_prev = m_scr[...]
        m_new = m_prev
        for sc in scores:
            m_new = jnp.maximum(m_new, jnp.max(sc, axis=1, keepdims=True))
        alpha = jnp.exp2(m_prev - m_new)
        l_new = alpha * l_scr[...]
        acc = alpha * acc_scr[...]
        for sc, vv in zip(scores, values):
            p = jnp.exp2(sc - m_new)
            l_new = l_new + jnp.sum(p, axis=1, keepdims=True)
            acc = acc + _dot(p.astype(BF16), vv)
        m_scr[...] = m_new
        l_scr[...] = l_new
        acc_scr[...] = acc

    scores, values = [], []
    for i in range(PAGES_PER_STEP):
        page_start = ((s_id * PAGES_PER_STEP + i) * PAGE_SIZE).astype(F32)
        bias = sl_ref[...] * (d0_ref[...] - page_start) + mask_ref[...]
        scores.append(_dot_nt(q, k_refs[i][0].astype(BF16)) + bias)
        values.append(v_refs[i][0].astype(BF16))
    update(scores, values)

    @pl.when(s_id == n_steps - 1)
    def _():
        sc = _dot_nt(q, kn_ref[0].astype(BF16)) + bn_ref[...]
        update([sc], [vn_ref[0].astype(BF16)])
        lam = _diff_lambda(lam_ref)
        half = ROWS_S // 2
        o1 = acc_scr[0:half] / l_scr[0:half]
        o2 = acc_scr[half:ROWS_S] / l_scr[half:ROWS_S]
        o_ref[0] = _subln(o1 - lam * o2, sw_ref[...]).astype(BF16)


def _attn_sample(qn_s, kn_s, v_s, cache_k, cache_v, page_table, lam_vecs, subln_w):
    db, n_pages = page_table.shape
    dec_seq = qn_s.shape[0] // db
    past = n_pages * PAGE_SIZE
    n_steps = n_pages // PAGES_PER_STEP
    page_rows = PAGE_SIZE * N_KV_HEADS
    n_phys = cache_k.shape[0]
    ck = cache_k.reshape(n_phys, page_rows, 2 * HEAD_DIM)
    cv = cache_v.reshape(n_phys, page_rows, V_DIM)

    q5 = qn_s.reshape(db, dec_seq, N_HEADS, 2, HEAD_DIM)
    zeros = jnp.zeros_like(q5[:, :, :, 0])
    q_all = jnp.stack([jnp.concatenate([q5[:, :, :, 0], zeros], axis=-1),
                       jnp.concatenate([zeros, q5[:, :, :, 1]], axis=-1)], axis=1)
    q_all = q_all.reshape(db, ROWS_S, LANES)

    r = np.arange(ROWS_S)
    tok_r = (r % (dec_seq * N_HEADS)) // N_HEADS
    head_r = r % N_HEADS
    slope_r = np.asarray(ALIBI_SLOPES)[head_r] * LOG2E
    c = np.arange(page_rows)
    key_c, grp_c = c // N_KV_HEADS, c % N_KV_HEADS
    same = (head_r[:, None] // GQA_REP) == grp_c[None, :]
    d0 = np.broadcast_to(past + tok_r[:, None] - key_c[None, :], (ROWS_S, page_rows))
    mask = np.where(same, 0.0, NEG_BIG)
    sl = np.broadcast_to(-slope_r[:, None], (ROWS_S, 1))
    cn = np.arange(LANES)
    tok_c, grp_n = cn // N_KV_HEADS, cn % N_KV_HEADS
    ok = ((head_r[:, None] // GQA_REP) == grp_n[None, :]) & (tok_c[None, :] <= tok_r[:, None])
    bn = np.where(ok, -slope_r[:, None] * (tok_r[:, None] - tok_c[None, :]), NEG_BIG)

    new_rows = dec_seq * N_KV_HEADS
    kn_pad = jnp.pad(kn_s.reshape(db, new_rows, LANES), ((0, 0), (0, LANES - new_rows), (0, 0)))
    vn_pad = jnp.pad(v_s.reshape(db, new_rows, LANES), ((0, 0), (0, LANES - new_rows), (0, 0)))

    def const(shape):
        return pl.BlockSpec(shape, lambda b, s, pt: (0,) * len(shape))

    def page_spec(i):
        return pl.BlockSpec(
            (1, page_rows, LANES),
            lambda b, s, pt: (pt[b * n_pages + s * PAGES_PER_STEP + i], 0, 0))

    grid_spec = pltpu.PrefetchScalarGridSpec(
        num_scalar_prefetch=1,
        grid=(db, n_steps),
        in_specs=[
            pl.BlockSpec((1, ROWS_S, LANES), lambda b, s, pt: (b, 0, 0)),
            const((ROWS_S, page_rows)),
            const((ROWS_S, page_rows)),
            const((ROWS_S, 1)),
            const((ROWS_S, LANES)),
            pl.BlockSpec((1, LANES, LANES), lambda b, s, pt: (b, 0, 0)),
            pl.BlockSpec((1, LANES, LANES), lambda b, s, pt: (b, 0, 0)),
            const((4, HEAD_DIM)),
            const((1, V_DIM)),
        ] + [page_spec(i) for i in range(PAGES_PER_STEP)] * 2,
        out_specs=pl.BlockSpec((1, ROWS_S // 2, V_DIM), lambda b, s, pt: (b, 0, 0)),
        scratch_shapes=[
            pltpu.VMEM((ROWS_S, 1), F32),
            pltpu.VMEM((ROWS_S, 1), F32),
            pltpu.VMEM((ROWS_S, V_DIM), F32),
        ],
    )
    o = pl.pallas_call(
        functools.partial(_attn_s_kernel, n_steps=n_steps),
        grid_spec=grid_spec,
        out_shape=jax.ShapeDtypeStruct((db, ROWS_S // 2, V_DIM), BF16),
        compiler_params=_cparams(("arbitrary", "arbitrary")),
        name="attn_s",
    )(page_table.reshape(-1), q_all, jnp.asarray(d0, F32), jnp.asarray(mask, F32),
      jnp.asarray(sl, F32), jnp.asarray(bn, F32), kn_pad, vn_pad, lam_vecs,
      subln_w.reshape(1, V_DIM),
      *([ck] * PAGES_PER_STEP), *([cv] * PAGES_PER_STEP))
    return o.reshape(db * dec_seq, ATT_WIDTH)


HALO = SUBLANES


def _ssd_kernel(xs_ref, b_ref, c_ref, z_ref, dt_ref, dtt_ref, halo_ref, init_ref,
                cw_ref, cb_ref, dtb_ref, dtbt_ref, a_ref, at_ref, dsk_ref, nw_ref,
                tri_ref, trit_ref, exp_ref, sel_ref,
                y_ref, fin_ref, win_scr, state_scr, *, rows_in, n_valid):
    ci = pl.program_id(1)
    n_chunks = pl.num_programs(1)
    lc = SSD_CHUNK
    bc_w = N_GROUPS * D_STATE

    @pl.when(ci == 0)
    def _():
        state_scr[...] = init_ref[0]
        win_scr[0:HALO, :] = halo_ref[0]

    if rows_in < lc:
        win_scr[HALO:HALO + lc, :] = jnp.zeros((lc, CONV_DIM), F32)
    win_scr[HALO:HALO + rows_in, 0:D_SSM] = xs_ref[0]
    win_scr[HALO:HALO + rows_in, D_SSM:D_SSM + bc_w] = b_ref[0]
    win_scr[HALO:HALO + rows_in, D_SSM + bc_w:CONV_DIM] = c_ref[0]

    acc = cb_ref[...]
    for tap in range(CONV_WIDTH):
        off = HALO - (CONV_WIDTH - 1) + tap
        acc = acc + win_scr[off:off + lc, :] * cw_ref[tap:tap + 1, :]
    conv = _silu(acc)
    win_scr[0:HALO, :] = win_scr[lc:lc + HALO, :]
    xs = conv[:, 0:D_SSM]
    bm = conv[:, D_SSM:D_SSM + bc_w].astype(BF16)
    cm = conv[:, D_SSM + bc_w:CONV_DIM].astype(BF16)

    if rows_in < lc:
        dt_in = jnp.concatenate([dt_ref[0], jnp.zeros((lc - rows_in, LANES), F32)], axis=0)
        dtt_in = jnp.concatenate(
            [dtt_ref[0], jnp.zeros((N_SSM_HEADS, lc - rows_in), F32)], axis=1)
    else:
        dt_in, dtt_in = dt_ref[0], dtt_ref[0]
    rowi = lax.broadcasted_iota(jnp.int32, (lc, LANES), 0)
    coli = lax.broadcasted_iota(jnp.int32, (N_SSM_HEADS, lc), 1)
    dt = jnp.where(rowi < n_valid, _softplus(dt_in + dtb_ref[...]), 0.0)
    dtt = jnp.where(coli < n_valid, _softplus(dtt_in + dtbt_ref[...]), 0.0)
    a_cs = _dot_x3_left(tri_ref[...], dt * a_ref[...])
    a_cst = _dot_x3(dtt * at_ref[...], trit_ref[...])
    a_last = a_cs[lc - 1:lc, :]
    exp_cs = jnp.exp(a_cs)
    exp_rest = jnp.exp(a_last - a_cs)
    expand = exp_ref[...]
    dtx = _dot_x2(dt, expand)
    ecx = _dot_x2(exp_cs, expand)
    erx = _dot_x2(exp_rest, expand)
    xc = xs * dtx
    xcb = xc.astype(BF16)
    xcd = (xc * erx).astype(BF16)

    last_t = jnp.exp(a_cst[:, lc - 1:lc])
    rdec = _dot_x2_left(sel_ref[...], jnp.broadcast_to(last_t, (N_SSM_HEADS, D_STATE)))

    tril = (lax.broadcasted_iota(jnp.int32, (lc, lc), 0)
            >= lax.broadcasted_iota(jnp.int32, (lc, lc), 1))
    lane = lax.broadcasted_iota(jnp.int32, (lc, LANES), 1)
    gw = HEADS_PER_GROUP * SSM_HEAD_DIM
    y_parts = []
    for g in range(N_GROUPS):
        bg = bm[:, g * D_STATE:(g + 1) * D_STATE]
        cg = cm[:, g * D_STATE:(g + 1) * D_STATE]
        cb = _dot_nt(cg, bg)
        st = state_scr[g * gw:(g + 1) * gw, :]
        y_off = _dot_nt(cg, st.astype(BF16)) * ecx[:, g * gw:(g + 1) * gw]
        new_st = _dot_tn(xcd[:, g * gw:(g + 1) * gw], bg)
        state_scr[g * gw:(g + 1) * gw, :] = st * rdec[g * gw:(g + 1) * gw, :] + new_st
        for j in range(HEADS_PER_GROUP // 2):
            pair = g * (HEADS_PER_GROUP // 2) + j
            blk = xcb[:, pair * LANES:(pair + 1) * LANES]
            y_pair = None
            for half in range(2):
                h = 2 * pair + half
                seg = a_cs[:, h:h + 1] - a_cst[h:h + 1, :]
                decay = jnp.exp(jnp.where(tril, seg, NEG_BIG))
                mh = (cb * decay).astype(BF16)
                keep = (lane < SSM_HEAD_DIM) if half == 0 else (lane >= SSM_HEAD_DIM)
                part = _dot(mh, jnp.where(keep, blk, jnp.zeros_like(blk)))
                y_pair = part if y_pair is None else y_pair + part
            y_parts.append(y_pair + y_off[:, (pair % (HEADS_PER_GROUP // 2)) * LANES:
                                          (pair % (HEADS_PER_GROUP // 2) + 1) * LANES])
    y = jnp.concatenate(y_parts, axis=1)
    y = y + dsk_ref[...] * xs
    if rows_in < lc:
        z = jnp.concatenate([z_ref[0], jnp.zeros((lc - rows_in, D_SSM), F32)], axis=0)
    else:
        z = z_ref[0]
    y = y * _silu(z)
    gn = D_SSM // N_GROUPS
    outs = []
    for g in range(N_GROUPS):
        yg = y[:, g * gn:(g + 1) * gn]
        ms = jnp.mean(yg * yg, axis=-1, keepdims=True)
        outs.append(yg * lax.rsqrt(ms + EPS) * nw_ref[:, g * gn:(g + 1) * gn])
    out = jnp.concatenate(outs, axis=1).astype(BF16)
    y_ref[0] = out[0:rows_in]

    @pl.when(ci == n_chunks - 1)
    def _():
        fin_ref[0] = state_scr[...]


def _dot_x3_left(sel, x):
    hi, mid, lo = _split3(x)
    return _dot(sel, hi) + _dot(sel, mid) + _dot(sel, lo)


def _dot_x2_left(sel, x):
    hi, lo = _split2(x)
    return _dot(sel, hi) + _dot(sel, lo)


def _ssd(src, col_blocks, dt_raw, halo, init_state, prm, rows_in, n_valid):
    nb, seq = src.shape[0], src.shape[1]
    n_chunks = max(1, seq // SSD_CHUNK)
    bc_w = N_GROUPS * D_STATE
    dtt = jnp.swapaxes(dt_raw[:, :, :N_SSM_HEADS], 1, 2)
    tri = np.tril(np.ones((SSD_CHUNK, SSD_CHUNK)))
    expand = np.zeros((LANES, D_SSM))
    expand[np.arange(D_SSM) // SSM_HEAD_DIM, np.arange(D_SSM)] = 1.0
    sel = expand[:N_SSM_HEADS].T
    cx, cbk, cck, cz = col_blocks

    def const(shape):
        return pl.BlockSpec(shape, lambda b, c: (0,) * len(shape))

    return pl.pallas_call(
        functools.partial(_ssd_kernel, rows_in=rows_in, n_valid=n_valid),
        grid=(nb, n_chunks),
        in_specs=[
            pl.BlockSpec((1, rows_in, D_SSM), lambda b, c: (b, c, cx)),
            pl.BlockSpec((1, rows_in, bc_w), lambda b, c: (b, c, cbk)),
            pl.BlockSpec((1, rows_in, bc_w), lambda b, c: (b, c, cck)),
            pl.BlockSpec((1, rows_in, D_SSM), lambda b, c: (b, c, cz)),
            pl.BlockSpec((1, rows_in, LANES), lambda b, c: (b, c, 0)),
            pl.BlockSpec((1, N_SSM_HEADS, rows_in), lambda b, c: (b, 0, c)),
            pl.BlockSpec((1, HALO, CONV_DIM), lambda b, c: (b, 0, 0)),
            pl.BlockSpec((1, D_SSM, D_STATE), lambda b, c: (b, 0, 0)),
            const((CONV_WIDTH, CONV_DIM)),
            const((1, CONV_DIM)),
            const((1, LANES)),
            const((N_SSM_HEADS, 1)),
            const((1, LANES)),
            const((N_SSM_HEADS, 1)),
            const((1, D_SSM)),
            const((1, D_SSM)),
            const((SSD_CHUNK, SSD_CHUNK)),
            const((SSD_CHUNK, SSD_CHUNK)),
            const((LANES, D_SSM)),
            const((D_SSM, N_SSM_HEADS)),
        ],
        out_specs=[
            pl.BlockSpec((1, rows_in, D_SSM), lambda b, c: (b, c, 0)),
            pl.BlockSpec((1, D_SSM, D_STATE), lambda b, c: (b, 0, 0)),
        ],
        out_shape=[
            jax.ShapeDtypeStruct((nb, seq, D_SSM), BF16),
            jax.ShapeDtypeStruct((nb, D_SSM, D_STATE), F32),
        ],
        scratch_shapes=[
            pltpu.VMEM((HALO + SSD_CHUNK, CONV_DIM), F32),
            pltpu.VMEM((D_SSM, D_STATE), F32),
        ],
        compiler_params=_cparams(("arbitrary", "arbitrary")),
        name="ssd",
    )(src, src, src, src, dt_raw, dtt, halo, init_state,
      prm["conv_w"], prm["conv_b"], prm["dt_bias"], prm["dt_bias_t"], prm["a"], prm["a_t"],
      prm["d_skip"], prm["ssm_norm_w"],
      jnp.asarray(tri, BF16), jnp.asarray(tri.T, BF16), jnp.asarray(expand, BF16),
      jnp.asarray(sel, BF16))


def _merge_kernel(o_ref, s_ref, wa_ref, ws_ref, ga_ref, gs_ref, out_ref):
    a = _dot(o_ref[...], wa_ref[...])
    s = _dot(s_ref[...], ws_ref[...])
    out_ref[...] = (_sigmoid(ga_ref[...]) * a + _sigmoid(gs_ref[...]) * s).astype(BF16)


def _merge(o, s, wa, ws, proj, tm, tn):
    t = o.shape[0]
    ga0, gs0 = COL_GA // tn, COL_GS // tn
    return pl.pallas_call(
        _merge_kernel,
        grid=(t // tm, D_MODEL // tn),
        in_specs=[
            pl.BlockSpec((tm, ATT_WIDTH), lambda i, j: (i, 0)),
            pl.BlockSpec((tm, D_SSM), lambda i, j: (i, 0)),
            pl.BlockSpec((ATT_WIDTH, tn), lambda i, j: (0, j)),
            pl.BlockSpec((D_SSM, tn), lambda i, j: (0, j)),
            pl.BlockSpec((tm, tn), lambda i, j: (i, ga0 + j)),
            pl.BlockSpec((tm, tn), lambda i, j: (i, gs0 + j)),
        ],
        out_specs=pl.BlockSpec((tm, tn), lambda i, j: (i, j)),
        out_shape=jax.ShapeDtypeStruct((t, D_MODEL), BF16),
        compiler_params=_cparams(("arbitrary", "arbitrary")),
        name="merge",
    )(o, s, wa, ws, proj, proj)


ROUTE_E1, ROUTE_E2, ROUTE_W1, ROUTE_W2 = 0, 1, 2, 3


def _resid_kernel(x_ref, m_ref, wo_ref, nw_ref, wrh_ref, wrl_ref, br_ref, *rest, n_real):
    h_ref, u_ref, route_ref = rest[-3:]

    @pl.when(pl.program_id(0) >= n_real)
    def _():
        h_ref[...] = jnp.zeros(h_ref.shape, F32)
        u_ref[...] = jnp.zeros(u_ref.shape, F32)
        route_ref[...] = jnp.zeros(route_ref.shape, F32)

    @pl.when(pl.program_id(0) < n_real)
    def _():
        _resid_tile(x_ref, m_ref, wo_ref, nw_ref, wrh_ref, wrl_ref, br_ref,
                    h_ref, u_ref, route_ref)


def _resid_tile(x_ref, m_ref, wo_ref, nw_ref, wrh_ref, wrl_ref, br_ref, h_ref, u_ref, route_ref):
    h = x_ref[...] + _dot(m_ref[...], wo_ref[...])
    h_ref[...] = h
    ms = jnp.mean(h * h, axis=-1, keepdims=True)
    u = h * lax.rsqrt(ms + EPS) * nw_ref[...]
    u_hi, u_lo = _split2(u)
    u_ref[...] = u
    logits = (_dot(u_hi, wrh_ref[...]) + _dot(u_lo, wrh_ref[...])
              + _dot(u_hi, wrl_ref[...]) + br_ref[...])
    lane = lax.broadcasted_iota(jnp.int32, logits.shape, 1)
    lane_f = lane.astype(F32)
    far = float(2 * LANES)

    def first_max(vals):
        top = jnp.max(vals, axis=1, keepdims=True)
        idx = jnp.min(jnp.where(vals == top, lane_f, far), axis=1, keepdims=True)
        return top, idx

    is_group = (lane >= N_EXPERTS) & (lane < N_EXPERTS + N_EXPERT_GROUPS)
    gl = jnp.where(is_group, logits, NEG_BIG)
    g_top, g_idx = first_max(gl)
    g_p = 1.0 / jnp.sum(jnp.exp(gl - g_top), axis=1, keepdims=True)
    lo_lane = (g_idx - N_EXPERTS) * EXPERTS_PER_GROUP
    in_group = (lane_f >= lo_lane) & (lane_f < lo_lane + EXPERTS_PER_GROUP)
    el = jnp.where(in_group, logits, NEG_BIG)
    m1, i1 = first_max(el)
    el2 = jnp.where(lane_f == i1, NEG_BIG, el)
    m2, i2 = first_max(el2)
    e = jnp.exp(m2 - m1)
    w1 = 1.0 / (1.0 + e)
    w2 = e / (1.0 + e)
    route = jnp.where(lane == ROUTE_E1, i1, 0.0)
    route = jnp.where(lane == ROUTE_E2, i2, route)
    route = jnp.where(lane == ROUTE_W1, g_p * w1, route)
    route_ref[...] = jnp.where(lane == ROUTE_W2, g_p * w2, route)


def _resid(x, merged, wo, norm_w, wr_hi, wr_lo, br, tm, t_all, row_off, bufs):
    t = x.shape[0]
    blk_off = row_off // tm
    n_real = t // tm
    n_fill = pl.cdiv(t_all - t, tm) if bufs is None else 0

    def const(shape):
        return pl.BlockSpec(shape, lambda i: (0,) * len(shape))

    in_specs = [
        pl.BlockSpec((tm, D_MODEL), lambda i: (jnp.minimum(i, n_real - 1), 0)),
        pl.BlockSpec((tm, D_MODEL), lambda i: (jnp.minimum(i, n_real - 1), 0)),
        const((D_MODEL, D_MODEL)),
        const((1, D_MODEL)),
        const((D_MODEL, LANES)),
        const((D_MODEL, LANES)),
        const((1, LANES)),
    ]
    args = [x, merged, wo, norm_w, wr_hi, wr_lo, br]
    aliases = {}
    if bufs is not None:
        aliases = {len(args) + k: k for k in range(len(bufs))}
        in_specs += [pl.BlockSpec(memory_space=pl.ANY)] * len(bufs)
        args += list(bufs)
    return pl.pallas_call(
        functools.partial(_resid_kernel, n_real=n_real),
        grid=(n_real + n_fill,),
        in_specs=in_specs,
        out_specs=[
            pl.BlockSpec((tm, D_MODEL), lambda i: (i + blk_off, 0)),
            pl.BlockSpec((tm, D_MODEL), lambda i: (i + blk_off, 0)),
            pl.BlockSpec((tm, LANES), lambda i: (i + blk_off, 0)),
        ],
        out_shape=[
            jax.ShapeDtypeStruct((t_all, D_MODEL), F32),
            jax.ShapeDtypeStruct((t_all, D_MODEL), F32),
            jax.ShapeDtypeStruct((t_all, LANES), F32),
        ],
        input_output_aliases=aliases,
        compiler_params=_cparams(("arbitrary",)),
        name="resid",
    )(*args)


MOE_TILE = 256
COMB_TILE = 128
DMA_UNROLL = 8


def _route_plan(route, n_tiles):
    n_pairs = 2 * route.shape[0]
    pair_e = route[:, ROUTE_E1:ROUTE_E2 + 1].astype(jnp.int32).reshape(-1)
    onehot = (pair_e[:, None] == jnp.arange(N_EXPERTS, dtype=jnp.int32)[None, :]).astype(jnp.int32)
    csum = jnp.cumsum(onehot, axis=0)
    rank = jnp.sum((csum - onehot) * onehot, axis=1)
    tiles_e = (csum[-1] + MOE_TILE - 1) // MOE_TILE
    tile_end = jnp.cumsum(tiles_e)
    first_row = (tile_end - tiles_e) * MOE_TILE
    slot = jnp.sum(onehot * first_row[None, :], axis=1) + rank
    tok_of_slot = jnp.zeros((n_tiles * MOE_TILE,), jnp.int32).at[slot].set(
        jnp.arange(n_pairs, dtype=jnp.int32) // 2)
    tile_expert = jnp.sum(jnp.arange(n_tiles, dtype=jnp.int32)[:, None] >= tile_end[None, :], axis=1)
    tile_expert = jnp.minimum(tile_expert, N_EXPERTS - 1).astype(jnp.int32)
    n_used = tile_end[-1:].astype(jnp.int32)
    slot_tab = slot.reshape(-1, COMB_TILE, 2).transpose(0, 2, 1).reshape(-1).astype(jnp.int32)
    return tile_expert, n_used, tok_of_slot, slot_tab


def _gmm_kernel(te_ref, nu_ref, tok_ref, u_hbm, wg_ref, wu_ref, wd_ref, o_ref,
                xbuf, sem, wgb, wub, wdb):
    t = pl.program_id(0)
    n_used = nu_ref[0]
    cur = lax.rem(t, 2)

    @pl.when(jnp.logical_or(t == 0, te_ref[t] != te_ref[jnp.maximum(t - 1, 0)]))
    def _():
        wgb[...] = wg_ref[0].astype(BF16)
        wub[...] = wu_ref[0].astype(BF16)
        wdb[...] = wd_ref[0].astype(BF16)

    def row_copy(tile, r, buf):
        tok = tok_ref[tile * MOE_TILE + r]
        return pltpu.make_async_copy(
            u_hbm.at[pl.ds(tok, 1), :], xbuf.at[buf, pl.ds(r, 1), :], sem.at[buf])

    def issue(tile, buf):
        def body(r, carry):
            row_copy(tile, r, buf).start()
            return carry
        lax.fori_loop(0, MOE_TILE, body, 0, unroll=DMA_UNROLL)

    def wait(tile, buf):
        def body(r, carry):
            row_copy(tile, r, buf).wait()
            return carry
        lax.fori_loop(0, MOE_TILE, body, 0, unroll=DMA_UNROLL)

    @pl.when(t == 0)
    def _():
        issue(0, 0)

    @pl.when(t + 1 < n_used)
    def _():
        issue(t + 1, 1 - cur)

    @pl.when(t < n_used)
    def _():
        wait(t, cur)
        x = xbuf[cur].astype(BF16)
        hid = _silu(_dot(x, wgb[...])) * _dot(x, wub[...])
        o_ref[...] = _dot(hid.astype(BF16), wdb[...])

    @pl.when(t >= n_used)
    def _():
        o_ref[...] = jnp.zeros(o_ref.shape, F32)


def _gmm(u_all, plan, wg, wu, wd, n_tiles):
    tile_expert, n_used, tok_of_slot, _ = plan
    grid_spec = pltpu.PrefetchScalarGridSpec(
        num_scalar_prefetch=3,
        grid=(n_tiles,),
        in_specs=[
            pl.BlockSpec(memory_space=pl.ANY),
            pl.BlockSpec((1, D_MODEL, D_EXPERT), lambda t, te, nu, tok: (te[t], 0, 0)),
            pl.BlockSpec((1, D_MODEL, D_EXPERT), lambda t, te, nu, tok: (te[t], 0, 0)),
            pl.BlockSpec((1, D_EXPERT, D_MODEL), lambda t, te, nu, tok: (te[t], 0, 0)),
        ],
        out_specs=pl.BlockSpec((MOE_TILE, D_MODEL), lambda t, te, nu, tok: (t, 0)),
        scratch_shapes=[
            pltpu.VMEM((2, MOE_TILE, D_MODEL), F32),
            pltpu.SemaphoreType.DMA((2,)),
            pltpu.VMEM((D_MODEL, D_EXPERT), BF16),
            pltpu.VMEM((D_MODEL, D_EXPERT), BF16),
            pltpu.VMEM((D_EXPERT, D_MODEL), BF16),
        ],
    )
    return pl.pallas_call(
        _gmm_kernel,
        grid_spec=grid_spec,
        out_shape=jax.ShapeDtypeStruct((n_tiles * MOE_TILE, D_MODEL), F32),
        compiler_params=_cparams(("arbitrary",)),
        name="gmm",
    )(tile_expert, n_used, tok_of_slot, u_all, wg, wu, wd)


def _combine_kernel(slot_ref, route_ref, h_ref, o_hbm, yp_ref, ys_ref, gbuf, sem, *, n_prompt):
    i = pl.program_id(0)
    n = pl.num_programs(0)
    cur = lax.rem(i, 2)
    rows = 2 * COMB_TILE

    def row_copy(tile, j, buf):
        slot = slot_ref[tile * rows + j]
        return pltpu.make_async_copy(
            o_hbm.at[pl.ds(slot, 1), :], gbuf.at[buf, pl.ds(j, 1), :], sem.at[buf])

    def issue(tile, buf):
        def body(j, carry):
            row_copy(tile, j, buf).start()
            return carry
        lax.fori_loop(0, rows, body, 0, unroll=DMA_UNROLL)

    def wait(tile, buf):
        def body(j, carry):
            row_copy(tile, j, buf).wait()
            return carry
        lax.fori_loop(0, rows, body, 0, unroll=DMA_UNROLL)

    @pl.when(i == 0)
    def _():
        issue(0, 0)

    @pl.when(i + 1 < n)
    def _():
        issue(i + 1, 1 - cur)

    wait(i, cur)
    w1 = route_ref[:, ROUTE_W1:ROUTE_W1 + 1]
    w2 = route_ref[:, ROUTE_W2:ROUTE_W2 + 1]
    y = h_ref[...] + w1 * gbuf[cur, 0:COMB_TILE, :] + w2 * gbuf[cur, COMB_TILE:rows, :]

    @pl.when(i < n_prompt)
    def _():
        yp_ref[...] = y

    @pl.when(i >= n_prompt)
    def _():
        ys_ref[...] = y


def _combine(route, h_all, o_sorted, plan, t_prompt):
    t_all = h_all.shape[0]
    n_prompt = t_prompt // COMB_TILE
    slot_tab = plan[3]
    grid_spec = pltpu.PrefetchScalarGridSpec(
        num_scalar_prefetch=1,
        grid=(t_all // COMB_TILE,),
        in_specs=[
            pl.BlockSpec((COMB_TILE, LANES), lambda i, st: (i, 0)),
            pl.BlockSpec((COMB_TILE, D_MODEL), lambda i, st: (i, 0)),
            pl.BlockSpec(memory_space=pl.ANY),
        ],
        out_specs=[
            pl.BlockSpec((COMB_TILE, D_MODEL), lambda i, st: (jnp.minimum(i, n_prompt - 1), 0)),
            pl.BlockSpec((COMB_TILE, D_MODEL), lambda i, st: (0, 0)),
        ],
        scratch_shapes=[
            pltpu.VMEM((2, 2 * COMB_TILE, D_MODEL), F32),
            pltpu.SemaphoreType.DMA((2,)),
        ],
    )
    return pl.pallas_call(
        functools.partial(_combine_kernel, n_prompt=n_prompt),
        grid_spec=grid_spec,
        out_shape=[
            jax.ShapeDtypeStruct((t_prompt, D_MODEL), F32),
            jax.ShapeDtypeStruct((t_all - t_prompt, D_MODEL), F32),
        ],
        compiler_params=_cparams(("arbitrary",)),
        name="combine",
    )(slot_tab, route, h_all, o_sorted)


def _layer_tokens(x2d, w, tm_proj, tm_small):
    proj, dt_raw = _proj(x2d, w["norm_attn_w"], w["w_main"], w["w_dt"], tm_proj, 1024)
    qn, kn, kb, vb = _qk_norm(proj, w["q_norm_w"], w["k_norm_w"], tm_small)
    return proj, dt_raw, qn, kn, kb, vb


def _branch_merge(x2d, o, s, proj, w, tm, tm_resid, t_all, row_off, bufs):
    merged = _merge(o, s, w["w_att_out"], w["w_ssm_out"], proj, tm, 512)
    return _resid(x2d, merged, w["w_o"], w["norm_ffn_w"], w["wr_hi"], w["wr_lo"], w["br"],
                  tm_resid, t_all, row_off, bufs)


def kernel(x_prompt, x_sample, cache_k, cache_v, state_ssm, state_conv, page_table, norm_attn_w, w_in, q_norm_w, k_norm_w, lambda_q1, lambda_k1, lambda_q2, lambda_k2, subln_w, w_att_out, conv_w, conv_b, dt_bias, a_log, d_skip, ssm_norm_w, w_ssm_out, w_o, norm_ffn_w, w_group_router, b_group_router, w_expert_router, b_expert_router, w_gate, w_up, w_down):
    layer = 0
    nb, seq, _ = x_prompt.shape
    db, dec_seq, _ = x_sample.shape

    w_in_l = w_in[layer]
    c_dt = Q_WIDTH + K_WIDTH + V_WIDTH + D_SSM + CONV_DIM
    w_main = jnp.concatenate([w_in_l[:, :c_dt], w_in_l[:, c_dt + N_SSM_HEADS:]], axis=1).astype(BF16)
    w_dt = jnp.pad(w_in_l[:, c_dt:c_dt + N_SSM_HEADS], ((0, 0), (0, LANES - N_SSM_HEADS))).astype(BF16)
    wr = jnp.concatenate([w_expert_router[layer], w_group_router[layer]], axis=1)
    wr = jnp.pad(wr, ((0, 0), (0, LANES - wr.shape[1])))
    wr_hi = wr.astype(BF16)
    wr_lo = (wr - wr_hi.astype(F32)).astype(BF16)
    br = jnp.concatenate([b_expert_router[layer], b_group_router[layer]])
    br = jnp.pad(br, (0, LANES - br.shape[0])).reshape(1, LANES)
    pad_h = (0, LANES - N_SSM_HEADS)
    w = dict(
        norm_attn_w=norm_attn_w[layer].reshape(1, D_MODEL), w_main=w_main, w_dt=w_dt,
        q_norm_w=q_norm_w[layer], k_norm_w=k_norm_w[layer],
        w_att_out=w_att_out[layer].astype(BF16), w_ssm_out=w_ssm_out[layer].astype(BF16),
        w_o=w_o[layer].astype(BF16), norm_ffn_w=norm_ffn_w[layer].reshape(1, D_MODEL),
        wr_hi=wr_hi, wr_lo=wr_lo, br=br,
        w_gate=w_gate[layer], w_up=w_up[layer], w_down=w_down[layer],
    )
    ssm_prm = dict(
        conv_w=conv_w[layer], conv_b=conv_b[layer].reshape(1, CONV_DIM),
        dt_bias=jnp.pad(dt_bias[layer], pad_h).reshape(1, LANES),
        dt_bias_t=dt_bias[layer].reshape(N_SSM_HEADS, 1),
        a=jnp.pad(-jnp.exp(a_log[layer]), pad_h).reshape(1, LANES),
        a_t=(-jnp.exp(a_log[layer])).reshape(N_SSM_HEADS, 1),
        d_skip=jnp.repeat(d_skip[layer], SSM_HEAD_DIM).reshape(1, D_SSM),
        ssm_norm_w=ssm_norm_w[layer].reshape(1, D_SSM),
    )
    lam_vecs = jnp.stack([lambda_q1[layer], lambda_k1[layer], lambda_q2[layer], lambda_k2[layer]])
    sw = subln_w[layer]
    ssd_cols = (COL_X // D_SSM, COL_B // (N_GROUPS * D_STATE), COL_C // (N_GROUPS * D_STATE),
                COL_Z // D_SSM)

    xp = x_prompt.reshape(nb * seq, D_MODEL)
    n_tok = db * dec_seq
    t_prompt = nb * seq
    t_all = t_prompt + n_tok
    proj_p, dt_p, qn_p, kn_p, kb_p, vb_p = _layer_tokens(xp, w, 512, 512)
    o_p = _attn_prompt(qn_p, kb_p, vb_p, lam_vecs, sw, 512)
    s_p, ssm_p = _ssd(
        proj_p.reshape(nb, seq, PROJ_WIDTH), ssd_cols, dt_p.reshape(nb, seq, LANES),
        jnp.zeros((nb, HALO, CONV_DIM), F32), jnp.zeros((nb, D_SSM, D_STATE), F32),
        ssm_prm, SSD_CHUNK, SSD_CHUNK)
    bufs = _branch_merge(xp, o_p, s_p.reshape(t_prompt, D_SSM), proj_p, w, 512, 256,
                         t_all, 0, None)
    keep = CONV_WIDTH - 1
    conv_p = proj_p.reshape(nb, seq, PROJ_WIDTH)[:, seq - keep:, COL_X:COL_X + CONV_DIM]

    xs = x_sample.reshape(db * dec_seq, D_MODEL)
    proj_s, dt_s, qn_s, kn_s, _, _ = _layer_tokens(xs, w, n_tok, n_tok)
    v_s = proj_s[:, COL_V:COL_V + V_WIDTH]
    o_s = _attn_sample(qn_s, kn_s, v_s, cache_k[layer], cache_v[layer], page_table, lam_vecs, sw)
    rows_s = SUBLANES
    pad_rows = ((0, 0), (0, rows_s - dec_seq), (0, 0))
    src_s = jnp.pad(proj_s[:, COL_Z:COL_GA].reshape(db, dec_seq, COL_GA - COL_Z), pad_rows)
    halo_s = jnp.pad(state_conv[layer], ((0, 0), (HALO - (CONV_WIDTH - 1), 0), (0, 0)))
    cols_s = ((COL_X - COL_Z) // D_SSM, (COL_B - COL_Z) // (N_GROUPS * D_STATE),
              (COL_C - COL_Z) // (N_GROUPS * D_STATE), 0)
    s_s, ssm_s = _ssd(
        src_s, cols_s, jnp.pad(dt_s.reshape(db, dec_seq, LANES), pad_rows), halo_s,
        state_ssm[layer].reshape(db, D_SSM, D_STATE), ssm_prm, rows_s, dec_seq)
    s_s = s_s[:, :dec_seq].reshape(n_tok, D_SSM)
    h_all, u_all, route = _branch_merge(xs, o_s, s_s, proj_s, w, n_tok, n_tok,
                                        t_all, t_prompt, bufs)

    n_tiles = 2 * t_all // MOE_TILE + N_EXPERTS
    plan = _route_plan(route, n_tiles)
    o_sorted = _gmm(u_all, plan, w["w_gate"], w["w_up"], w["w_down"], n_tiles)
    y_p, y_s = _combine(route, h_all, o_sorted, plan, t_prompt)
    conv_s =proj_s.reshape(db, dec_seq, PROJ_WIDTH)[:, dec_seq - keep:, COL_X:COL_X + CONV_DIM]

    return (
        y_p.reshape(nb, seq, D_MODEL),
        y_s.reshape(db, dec_seq, D_MODEL),
        kn_p.reshape(1, nb, seq, N_KV_HEADS, 2 * HEAD_DIM),
        proj_p[:, COL_V:COL_V + V_WIDTH].reshape(1, nb, seq, N_KV_HEADS, V_DIM),
        ssm_p.reshape(1, nb, N_SSM_HEADS, SSM_HEAD_DIM, D_STATE),
        conv_p[None],
        kn_s.reshape(1, db, dec_seq, N_KV_HEADS, 2 * HEAD_DIM),
        v_s.reshape(1, db, dec_seq, N_KV_HEADS, V_DIM),
        ssm_s.reshape(1, db, N_SSM_HEADS, SSM_HEAD_DIM, D_STATE),
        conv_s[None],
    )
```

```python
import functools
import math

import jax
import jax.numpy as jnp
import ml_dtypes
import numpy as np
from jax import lax
from jax.experimental import pallas as pl
from jax.experimental.pallas import tpu as pltpu

F32 = jnp.float32
BF16 = jnp.bfloat16

D_MODEL = 2048
N_HEADS = 8
N_KV_HEADS = 4
GQA_REP = N_HEADS // N_KV_HEADS
HEAD_DIM = 64
V_DIM = 2 * HEAD_DIM
Q_WIDTH = N_HEADS * 2 * HEAD_DIM
K_WIDTH = N_KV_HEADS * 2 * HEAD_DIM
V_WIDTH = N_KV_HEADS * V_DIM
ATT_WIDTH = N_HEADS * V_DIM
D_SSM = D_MODEL
SSM_HEAD_DIM = 64
N_SSM_HEADS = D_SSM // SSM_HEAD_DIM
N_GROUPS = 4
HEADS_PER_GROUP = N_SSM_HEADS // N_GROUPS
D_STATE = 128
CONV_WIDTH = 4
CONV_DIM = D_SSM + 2 * N_GROUPS * D_STATE
SSD_CHUNK = 128
N_EXPERT_GROUPS = 4
EXPERTS_PER_GROUP = 8
N_EXPERTS = N_EXPERT_GROUPS * EXPERTS_PER_GROUP
D_EXPERT = D_MODEL // 4
PAGE_SIZE = 128
EPS = 1e-6
LAM_INIT = 0.8 - 0.6 * math.exp(-0.3 * 0)

LANES = 128
SUBLANES = 8
NEG_BIG = -1e30
VMEM_LIMIT = 56 * 1024 * 1024

COL_Q = 0
COL_K = COL_Q + Q_WIDTH
COL_V = COL_K + K_WIDTH
COL_Z = COL_V + V_WIDTH
COL_X = COL_Z + D_SSM
COL_B = COL_X + D_SSM
COL_C = COL_B + N_GROUPS * D_STATE
COL_GA = COL_C + N_GROUPS * D_STATE
COL_GS = COL_GA + D_MODEL
PROJ_WIDTH = COL_GS + D_MODEL

ALIBI_SLOPES = [2.0 ** (-8.0 * (h + 1) / N_HEADS) for h in range(N_HEADS)]


def _cparams(sem):
    return pltpu.CompilerParams(dimension_semantics=sem, vmem_limit_bytes=VMEM_LIMIT)


def _dot(a, b):
    return jnp.dot(a, b, preferred_element_type=F32)


def _dot_nt(a, b):
    return lax.dot_general(a, b, (((1,), (1,)), ((), ())), preferred_element_type=F32)


def _dot_tn(a, b):
    return lax.dot_general(a, b, (((0,), (0,)), ((), ())), preferred_element_type=F32)


def _split2(x):
    hi = x.astype(BF16)
    lo = (x - hi.astype(F32)).astype(BF16)
    return hi, lo


def _split3(x):
    hi = x.astype(BF16)
    r = x - hi.astype(F32)
    mid = r.astype(BF16)
    lo = (r - mid.astype(F32)).astype(BF16)
    return hi, mid, lo


def _dot_x2(x, sel):
    hi, lo = _split2(x)
    return _dot(hi, sel) + _dot(lo, sel)


def _dot_x3(x, sel):
    hi, mid, lo = _split3(x)
    return _dot(hi, sel) + _dot(mid, sel) + _dot(lo, sel)


def _sigmoid(x):
    return 1.0 / (1.0 + jnp.exp(-x))


def _silu(x):
    return x * _sigmoid(x)


def _softplus(x):
    return jnp.maximum(x, 0.0) + jnp.log1p(jnp.exp(-jnp.abs(x)))


NORM_ROWS = 256


def _proj_kernel(x_ref, nw_ref, wa_ref, wb_ref, wdt_ref, o_ref, dt_ref, u_scr, *, n_a):
    j = pl.program_id(1)

    @pl.when(j == 0)
    def _():
        tm = x_ref.shape[0]
        for lo in range(0, tm, min(tm, NORM_ROWS)):
            hi = lo + min(tm, NORM_ROWS)
            x = x_ref[lo:hi, :]
            ms = jnp.mean(x * x, axis=-1, keepdims=True)
            u_scr[lo:hi, :] = (x * lax.rsqrt(ms + EPS) * nw_ref[...]).astype(BF16)
        dt_ref[...] = _dot(u_scr[...], wdt_ref[...])

    @pl.when(j < n_a)
    def _():
        o_ref[...] = _dot(u_scr[...], wa_ref[...])

    @pl.when(j >= n_a)
    def _():
        o_ref[...] = _dot(u_scr[...], wb_ref[...])


def _proj(x, norm_w, w_a, w_b, w_dt, tm, tn):
    t = x.shape[0]
    n_a = w_a.shape[1] // tn
    return pl.pallas_call(
        functools.partial(_proj_kernel, n_a=n_a),
        grid=(t // tm, PROJ_WIDTH // tn),
        in_specs=[
            pl.BlockSpec((tm, D_MODEL), lambda i, j: (i, 0)),
            pl.BlockSpec((1, D_MODEL), lambda i, j: (0, 0)),
            pl.BlockSpec((D_MODEL, tn), lambda i, j: (0, jnp.minimum(j, n_a - 1))),
            pl.BlockSpec((D_MODEL, tn), lambda i, j: (0, jnp.maximum(j - n_a, 0))),
            pl.BlockSpec((D_MODEL, LANES), lambda i, j: (0, 0)),
        ],
        out_specs=[
            pl.BlockSpec((tm, tn), lambda i, j: (i, j)),
            pl.BlockSpec((tm, LANES), lambda i, j: (i, 0)),
        ],
        out_shape=[
            jax.ShapeDtypeStruct((t, PROJ_WIDTH), F32),
            jax.ShapeDtypeStruct((t, LANES), F32),
        ],
        scratch_shapes=[pltpu.VMEM((tm, D_MODEL), BF16)],
        compiler_params=_cparams(("arbitrary", "arbitrary")),
        name="proj",
    )(x, norm_w, w_a, w_b, w_dt)


LOG2E = math.log2(math.e)
Q_SCALE = LOG2E * HEAD_DIM ** -0.5


def _qknorm_kernel(p_ref, qw_ref, kw_ref, g_ref, qn_ref, kn_ref, kb_ref, vb_ref):
    gsum = g_ref[...]
    n_q = Q_WIDTH // LANES
    for c in range((Q_WIDTH + K_WIDTH) // LANES):
        x = p_ref[:, c * LANES:(c + 1) * LANES]
        ss = _dot_x2(x * x, gsum)
        y = x * lax.rsqrt(ss * (1.0 / HEAD_DIM) + EPS)
        if c < n_q:
            qn_ref[:, c * LANES:(c + 1) * LANES] = (y * qw_ref[...] * Q_SCALE).astype(BF16)
        else:
            kn = y * kw_ref[...]
            kn_ref[:, (c - n_q) * LANES:(c - n_q + 1) * LANES] = kn
            kb_ref[:, (c - n_q) * LANES:(c - n_q + 1) * LANES] = kn.astype(BF16)
    vb_ref[...] = p_ref[:, COL_V:COL_V + V_WIDTH].astype(BF16)


def _qk_norm(proj, q_norm_w, k_norm_w, tm):
    t = proj.shape[0]
    group = np.kron(np.eye(LANES // HEAD_DIM), np.ones((HEAD_DIM, HEAD_DIM)))
    qw = jnp.tile(q_norm_w, LANES // HEAD_DIM).reshape(1, LANES)
    kw = jnp.tile(k_norm_w, LANES // HEAD_DIM).reshape(1, LANES)
    return pl.pallas_call(
        _qknorm_kernel,
        grid=(t // tm,),
        in_specs=[
            pl.BlockSpec((tm, Q_WIDTH + K_WIDTH + V_WIDTH), lambda i: (i, 0)),
            pl.BlockSpec((1, LANES), lambda i: (0, 0)),
            pl.BlockSpec((1, LANES), lambda i: (0, 0)),
            pl.BlockSpec((LANES, LANES), lambda i: (0, 0)),
        ],
        out_specs=[
            pl.BlockSpec((tm, Q_WIDTH), lambda i: (i, 0)),
            pl.BlockSpec((tm, K_WIDTH), lambda i: (i, 0)),
            pl.BlockSpec((tm, K_WIDTH), lambda i: (i, 0)),
            pl.BlockSpec((tm, V_WIDTH), lambda i: (i, 0)),
        ],
        out_shape=[
            jax.ShapeDtypeStruct((t, Q_WIDTH), BF16),
            jax.ShapeDtypeStruct((t, K_WIDTH), F32),
            jax.ShapeDtypeStruct((t, K_WIDTH), BF16),
            jax.ShapeDtypeStruct((t, V_WIDTH), BF16),
        ],
        compiler_params=_cparams(("arbitrary",)),
        name="qk_norm",
    )(proj, qw, kw, jnp.asarray(group, BF16))


def _diff_lambda(lam_ref):
    lamv = lam_ref[...]
    s1 = jnp.sum(lamv[0:1] * lamv[1:2], axis=1, keepdims=True)
    s2 = jnp.sum(lamv[2:3] * lamv[3:4], axis=1, keepdims=True)
    return jnp.exp(s1) - jnp.exp(s2) + LAM_INIT


def _subln(o, w):
    ms = jnp.mean(o * o, axis=-1, keepdims=True)
    return o * lax.rsqrt(ms + EPS) * w * (1.0 - LAM_INIT)


N_SLOPE_PARTS = 3


def _bf16_parts(x, n):
    parts, rem = [], np.float32(x)
    for _ in range(n):
        p = np.float32(rem.astype(ml_dtypes.bfloat16))
        parts.append(float(p))
        rem = np.float32(rem - p)
    return parts


def _alibi_tables(tk):
    qcols = np.zeros((N_HEADS, 16, LANES), np.float32)
    csum = np.zeros((N_HEADS,), np.float32)
    for h, slope in enumerate(ALIBI_SLOPES):
        parts = _bf16_parts(slope * LOG2E, N_SLOPE_PARTS)
        csum[h] = np.float32(sum(np.float32(p) for p in parts))
        for i, p in enumerate(parts):
            qcols[h, :, i] = p * LANES
            qcols[h, :, N_SLOPE_PARTS + i] = p
    pos = np.arange(tk)
    kcols = np.zeros((tk, LANES), np.float32)
    kcols[:, 0:N_SLOPE_PARTS] = (pos // LANES)[:, None]
    kcols[:, N_SLOPE_PARTS:2 * N_SLOPE_PARTS] = (pos % LANES)[:, None]
    return jnp.asarray(qcols, BF16), jnp.asarray(kcols, BF16), jnp.asarray(csum, F32)


def _attn_p_kernel(qi_ref, ki_ref, cf_ref, q_ref, k_ref, v_ref, qc_ref, kc_ref, lam_ref, sw_ref,
                   o_ref, qa_scr, m_scr, l_scr, acc_scr, *, tq):
    g = pl.program_id(0)
    t = pl.program_id(1)
    qi = qi_ref[t]
    ki = ki_ref[t]
    n_sub = GQA_REP * 2
    rows = n_sub * tq
    n_chunk = tq // LANES

    @pl.when(ki == 0)
    def _():
        lane = lax.broadcasted_iota(jnp.int32, (tq, LANES), 1)
        for r in range(GQA_REP):
            qq = q_ref[:, r * LANES:(r + 1) * LANES]
            qc = jnp.broadcast_to(qc_ref[r, 0:1, :], (tq, LANES))
            for c in range(2):
                idx = 2 * r + c
                keep = (lane < HEAD_DIM) if c == 0 else (lane >= HEAD_DIM)
                qa_scr[idx * tq:(idx + 1) * tq, 0:LANES] = jnp.where(keep, qq, jnp.zeros_like(qq))
                qa_scr[idx * tq:(idx + 1) * tq, LANES:2 * LANES] = qc
        m_scr[...] = jnp.full(m_scr.shape, NEG_BIG, F32)
        l_scr[...] = jnp.zeros(l_scr.shape, F32)
        acc_scr[...] = jnp.zeros(acc_scr.shape, F32)

    def step(diag):
        k_aug = jnp.concatenate([k_ref[...], kc_ref[...]], axis=1)
        s_all = _dot_nt(qa_scr[...], k_aug)
        if diag:
            row_in = lax.broadcasted_iota(jnp.int32, (rows, tq), 0) & (tq - 1)
            col = lax.broadcasted_iota(jnp.int32, (rows, tq), 1)
            s_all = jnp.where(col <= row_in, s_all, NEG_BIG)
        block_dist = ((qi - ki) * tq).astype(F32)
        ps, alphas = [], []
        for r in range(GQA_REP):
            off = -cf_ref[g * GQA_REP + r] * block_dist
            lo, hi = r * 2 * tq, (r + 1) * 2 * tq
            chunks = [s_all[lo:hi, j * LANES:(j + 1) * LANES] for j in range(n_chunk)]
            m_prev = m_scr[lo:hi]
            m_blk = jnp.max(functools.reduce(jnp.maximum, chunks), axis=1, keepdims=True) + off
            m_new = jnp.maximum(m_prev, m_blk)
            alpha = jnp.exp2(m_prev - m_new)
            m_sub = m_new - off
            pj = [jnp.exp2(ch - m_sub) for ch in chunks]
            l_scr[lo:hi] = alpha * l_scr[lo:hi] + functools.reduce(jnp.add, pj)
            m_scr[lo:hi] = m_new
            ps.append(jnp.concatenate(pj, axis=1).astype(BF16))
            alphas.append(alpha)
        pv = _dot(jnp.concatenate(ps, axis=0), v_ref[...])
        acc_scr[...] = jnp.concatenate(alphas, axis=0) * acc_scr[...] + pv

    @pl.when(ki < qi)
    def _():
        step(False)

    @pl.when(ki == qi)
    def _():
        step(True)
        lam = _diff_lambda(lam_ref)
        for r in range(GQA_REP):
            i1, i2 = 2 * r * tq, (2 * r + 1) * tq
            l1 = jnp.sum(l_scr[i1:i1 + tq], axis=1, keepdims=True)
            l2 = jnp.sum(l_scr[i2:i2 + tq], axis=1, keepdims=True)
            o = acc_scr[i1:i1 + tq] / l1 - lam * (acc_scr[i2:i2 + tq] / l2)
            o_ref[:, r * LANES:(r + 1) * LANES] = _subln(o, sw_ref[...]).astype(BF16)


def _attn_prompt(qn, kb, vb, lam_vecs, subln_w, tq):
    t = qn.shape[0]
    nq = t // tq
    pairs = [(i, j) for i in range(nq) for j in range(i + 1)]
    qi_tab = jnp.asarray([p[0] for p in pairs], jnp.int32)
    ki_tab = jnp.asarray([p[1] for p in pairs], jnp.int32)
    qcols, kcols, csum = _alibi_tables(tq)
    n_sub = 2 * GQA_REP
    grid_spec = pltpu.PrefetchScalarGridSpec(
        num_scalar_prefetch=3,
        grid=(N_KV_HEADS, len(pairs)),
        in_specs=[
            pl.BlockSpec((tq, GQA_REP * LANES), lambda g, t, qi, ki, cf: (qi[t], g)),
            pl.BlockSpec((tq, LANES), lambda g, t, qi, ki, cf: (ki[t], g)),
            pl.BlockSpec((tq, V_DIM), lambda g, t, qi, ki, cf: (ki[t], g)),
            pl.BlockSpec((GQA_REP, 16, LANES), lambda g, t, qi, ki, cf: (g, 0, 0)),
            pl.BlockSpec((tq, LANES), lambda g, t, qi, ki, cf: (0, 0)),
            pl.BlockSpec((4, HEAD_DIM), lambda g, t, qi, ki, cf: (0, 0)),
            pl.BlockSpec((1, V_DIM), lambda g, t, qi, ki, cf: (0, 0)),
        ],
        out_specs=pl.BlockSpec((tq, GQA_REP * V_DIM), lambda g, t, qi, ki, cf: (qi[t], g)),
        scratch_shapes=[
            pltpu.VMEM((n_sub * tq, 2 * LANES), BF16),
            pltpu.VMEM((n_sub * tq, LANES), F32),
            pltpu.VMEM((n_sub * tq, LANES), F32),
            pltpu.VMEM((n_sub * tq, V_DIM), F32),
        ],
    )
    return pl.pallas_call(
        functools.partial(_attn_p_kernel, tq=tq),
        grid_spec=grid_spec,
        out_shape=jax.ShapeDtypeStruct((t, ATT_WIDTH), BF16),
        compiler_params=_cparams(("arbitrary", "arbitrary")),
        name="attn_p",
    )(qi_tab, ki_tab, csum, qn, kb, vb, qcols, kcols, lam_vecs, subln_w.reshape(1, V_DIM))


PAGES_PER_STEP = 16
PAGE_GROUP = 4
ROWS_S = 2 * 4 * N_HEADS


def _attn_s_kernel(pt_ref, q_ref, d0_ref, mask_ref, sl_ref, bn_ref, kn_ref, vn_ref, lam_ref,
                   sw_ref, ck_hbm, cv_hbm, o_ref, m_scr, l_scr, acc_scr, kbuf, vbuf, sem,
                   *, n_steps):
    s_id = pl.program_id(1)
    step = pl.program_id(0) * n_steps + s_id
    n_total = pl.num_programs(0) * n_steps
    cur = lax.rem(step, 2)

    def page_copies(n, i, buf):
        page = pt_ref[n * PAGES_PER_STEP + i]
        return (pltpu.make_async_copy(ck_hbm.at[page], kbuf.at[buf, i], sem.at[0, buf]),
                pltpu.make_async_copy(cv_hbm.at[page], vbuf.at[buf, i], sem.at[1, buf]))

    def start_pages(n, buf):
        for i in range(PAGES_PER_STEP):
            for cp in page_copies(n, i, buf):
                cp.start()

    @pl.when(step == 0)
    def _():
        start_pages(0, 0)

    @pl.when(s_id == 0)
    def _():
        m_scr[...] = jnp.full(m_scr.shape, NEG_BIG, F32)
        l_scr[...] = jnp.zeros(l_scr.shape, F32)
        acc_scr[...] = jnp.zeros(acc_scr.shape, F32)

    for i in range(PAGES_PER_STEP):
        for cp in page_copies(step, i, cur):
            cp.wait()

    nxt = jnp.minimum(step + 1, n_total - 1)
    start_pages(nxt, 1 - cur)

    q = q_ref[0]
    k_refs = [kbuf.at[cur, i] for i in range(PAGES_PER_STEP)]
    v_refs = [vbuf.at[cur, i] for i in range(PAGES_PER_STEP)]

    def update(scores, values):
        m_prev = m_scr[...]
        m_new = m_prev
        for sc in scores:
            m_new = jnp.maximum(m_new, jnp.max(sc, axis=1, keepdims=True))
        alpha = jnp.exp2(m_prev - m_new)
        l_new = alpha * l_scr[...]
        acc = alpha * acc_scr[...]
        for sc, vv in zip(scores, values):
            p = jnp.exp2(sc - m_new)
            l_new = l_new + jnp.sum(p, axis=1, keepdims=True)
            acc = acc + _dot(p.astype(BF16), vv)
        m_scr[...] = m_new
        l_scr[...] = l_new
        acc_scr[...] = acc

    for first in range(0, PAGES_PER_STEP, PAGE_GROUP):
        scores, values = [], []
        for i in range(first, first + PAGE_GROUP):
            page_start = ((s_id * PAGES_PER_STEP + i) * PAGE_SIZE).astype(F32)
            bias = sl_ref[...] * (d0_ref[...] - page_start) + mask_ref[...]
            scores.append(_dot_nt(q, k_refs[i][...].astype(BF16)) + bias)
            values.append(v_refs[i][...].astype(BF16))
        update(scores, values)

    @pl.when(s_id == n_steps - 1)
    def _():
        sc = _dot_nt(q, kn_ref[0].astype(BF16)) + bn_ref[...]
        update([sc], [vn_ref[0].astype(BF16)])
        lam = _diff_lambda(lam_ref)
        half = ROWS_S // 2
        o1 = acc_scr[0:half] / l_scr[0:half]
        o2 = acc_scr[half:ROWS_S] / l_scr[half:ROWS_S]
        o_ref[0] = _subln(o1 - lam * o2, sw_ref[...]).astype(BF16)

    @pl.when(step == n_total - 1)
    def _():
        for i in range(PAGES_PER_STEP):
            for cp in page_copies(step, i, 1 - cur):
                cp.wait()


def _attn_sample(qn_s, kn_s, v_s, cache_k, cache_v, page_table, lam_vecs, subln_w):
    db, n_pages = page_table.shape
    dec_seq = qn_s.shape[0] // db
    past = n_pages * PAGE_SIZE
    n_steps = n_pages // PAGES_PER_STEP
    page_rows = PAGE_SIZE * N_KV_HEADS
    n_phys = cache_k.shape[0]
    ck = cache_k.reshape(n_phys, page_rows, 2 * HEAD_DIM)
    cv = cache_v.reshape(n_phys, page_rows, V_DIM)

    q5 = qn_s.reshape(db, dec_seq, N_HEADS, 2, HEAD_DIM)
    zeros = jnp.zeros_like(q5[:, :, :, 0])
    q_all = jnp.stack([jnp.concatenate([q5[:, :, :, 0], zeros], axis=-1),
                       jnp.concatenate([zeros, q5[:, :, :, 1]], axis=-1)], axis=1)
    q_all = q_all.reshape(db, ROWS_S, LANES)

    r = np.arange(ROWS_S)
    tok_r = (r % (dec_seq * N_HEADS)) // N_HEADS
    head_r = r % N_HEADS
    slope_r = np.asarray(ALIBI_SLOPES)[head_r] * LOG2E
    c = np.arange(page_rows)
    key_c, grp_c = c // N_KV_HEADS, c % N_KV_HEADS
    same = (head_r[:, None] // GQA_REP) == grp_c[None, :]
    d0 = np.broadcast_to(past + tok_r[:, None] - key_c[None, :], (ROWS_S, page_rows))
    mask = np.where(same, 0.0, NEG_BIG)
    sl = np.broadcast_to(-slope_r[:, None], (ROWS_S, 1))
    cn = np.arange(LANES)
    tok_c, grp_n = cn // N_KV_HEADS, cn % N_KV_HEADS
    ok = ((head_r[:, None] // GQA_REP) == grp_n[None, :]) & (tok_c[None, :] <= tok_r[:, None])
    bn = np.where(ok, -slope_r[:, None] * (tok_r[:, None] - tok_c[None, :]), NEG_BIG)

    new_rows = dec_seq * N_KV_HEADS
    kn_pad = jnp.pad(kn_s.reshape(db, new_rows, LANES), ((0, 0), (0, LANES - new_rows), (0, 0)))
    vn_pad = jnp.pad(v_s.reshape(db, new_rows, LANES), ((0, 0), (0, LANES - new_rows), (0, 0)))

    def const(shape):
        return pl.BlockSpec(shape, lambda b, s, pt: (0,) * len(shape))

    grid_spec = pltpu.PrefetchScalarGridSpec(
        num_scalar_prefetch=1,
        grid=(db, n_steps),
        in_specs=[
            pl.BlockSpec((1, ROWS_S, LANES), lambda b, s, pt: (b, 0, 0)),
            const((ROWS_S, page_rows)),
            const((ROWS_S, page_rows)),
            const((ROWS_S, 1)),
            const((ROWS_S, LANES)),
            pl.BlockSpec((1, LANES, LANES), lambda b, s, pt: (b, 0, 0)),
            pl.BlockSpec((1, LANES, LANES), lambda b, s, pt: (b, 0, 0)),
            const((4, HEAD_DIM)),
            const((1, V_DIM)),
            pl.BlockSpec(memory_space=pl.ANY),
            pl.BlockSpec(memory_space=pl.ANY),
        ],
        out_specs=pl.BlockSpec((1, ROWS_S // 2, V_DIM), lambda b, s, pt: (b, 0, 0)),
        scratch_shapes=[
            pltpu.VMEM((ROWS_S, 1), F32),
            pltpu.VMEM((ROWS_S, 1), F32),
            pltpu.VMEM((ROWS_S, V_DIM), F32),
            pltpu.VMEM((2, PAGES_PER_STEP, page_rows, LANES), F32),
            pltpu.VMEM((2, PAGES_PER_STEP, page_rows, LANES), F32),
            pltpu.SemaphoreType.DMA((2, 2)),
        ],
    )
    o = pl.pallas_call(
        functools.partial(_attn_s_kernel, n_steps=n_steps),
        grid_spec=grid_spec,
        out_shape=jax.ShapeDtypeStruct((db, ROWS_S // 2, V_DIM), BF16),
        compiler_params=_cparams(("arbitrary", "arbitrary")),
        name="attn_s",
    )(page_table.reshape(-1), q_all, jnp.asarray(d0, F32), jnp.asarray(mask, F32),
      jnp.asarray(sl, F32), jnp.asarray(bn, F32), kn_pad, vn_pad, lam_vecs,
      subln_w.reshape(1, V_DIM),
      ck, cv)
    return o.reshape(db * dec_seq, ATT_WIDTH)


HALO = SUBLANES


def _ssd_kernel(xs_ref, b_ref, c_ref, z_ref, dt_ref, dtt_ref, halo_ref, init_ref,
                cw_ref, cb_ref, dtb_ref, dtbt_ref, a_ref, at_ref, dsk_ref, nw_ref,
                tri_ref, trit_ref, exp_ref, sel_ref,
                y_ref, fin_ref, win_scr, state_scr, *, rows_in, n_valid):
    ci = pl.program_id(1)
    n_chunks = pl.num_programs(1)
    lc = SSD_CHUNK
    bc_w = N_GROUPS * D_STATE

    @pl.when(ci == 0)
    def _():
        state_scr[...] = init_ref[0]
        win_scr[0:HALO, :] = halo_ref[0]

    if rows_in < lc:
        win_scr[HALO:HALO + lc, :] = jnp.zeros((lc, CONV_DIM), F32)
    win_scr[HALO:HALO + rows_in, 0:D_SSM] = xs_ref[0]
    win_scr[HALO:HALO + rows_in, D_SSM:D_SSM + bc_w] = b_ref[0]
    win_scr[HALO:HALO + rows_in, D_SSM + bc_w:CONV_DIM] = c_ref[0]

    acc = cb_ref[...]
    for tap in range(CONV_WIDTH):
        off = HALO - (CONV_WIDTH - 1) + tap
        acc = acc + win_scr[off:off + lc, :] * cw_ref[tap:tap + 1, :]
    conv = _silu(acc)
    win_scr[0:HALO, :] = win_scr[lc:lc + HALO, :]
    xs = conv[:, 0:D_SSM]
    bm = conv[:, D_SSM:D_SSM + bc_w].astype(BF16)
    cm = conv[:, D_SSM + bc_w:CONV_DIM].astype(BF16)

    if rows_in < lc:
        dt_in = jnp.concatenate([dt_ref[0], jnp.zeros((lc - rows_in, LANES), F32)], axis=0)
        dtt_in = jnp.concatenate(
            [dtt_ref[0], jnp.zeros((N_SSM_HEADS, lc - rows_in), F32)], axis=1)
    else:
        dt_in, dtt_in = dt_ref[0], dtt_ref[0]
    rowi = lax.broadcasted_iota(jnp.int32, (lc, LANES), 0)
    coli = lax.broadcasted_iota(jnp.int32, (N_SSM_HEADS, lc), 1)
    dt = jnp.where(rowi < n_valid, _softplus(dt_in + dtb_ref[...]), 0.0)
    dtt = jnp.where(coli < n_valid, _softplus(dtt_in + dtbt_ref[...]), 0.0)
    a_cs = _dot_x3_left(tri_ref[...], dt * a_ref[...])
    a_cst = _dot_x3(dtt * at_ref[...], trit_ref[...])
    a_last = a_cs[lc - 1:lc, :]
    exp_cs = jnp.exp(a_cs)
    exp_rest = jnp.exp(a_last - a_cs)
    expand = exp_ref[...]
    dtx = _dot_x2(dt, expand)
    ecx = _dot_x2(exp_cs, expand)
    erx = _dot_x2(exp_rest, expand)
    xc = xs * dtx
    xcb = xc.astype(BF16)
    xcd = (xc * erx).astype(BF16)

    last_t = jnp.exp(a_cst[:, lc - 1:lc])
    rdec = _dot_x2_left(sel_ref[...], jnp.broadcast_to(last_t, (N_SSM_HEADS, D_STATE)))

    tril = (lax.broadcasted_iota(jnp.int32, (lc, lc), 0)
            >= lax.broadcasted_iota(jnp.int32, (lc, lc), 1))
    lane = lax.broadcasted_iota(jnp.int32, (lc, LANES), 1)
    gw = HEADS_PER_GROUP * SSM_HEAD_DIM
    y_parts = []
    for g in range(N_GROUPS):
        bg = bm[:, g * D_STATE:(g + 1) * D_STATE]
        cg = cm[:, g * D_STATE:(g + 1) * D_STATE]
        cb = _dot_nt(cg, bg)
        st = state_scr[g * gw:(g + 1) * gw, :]
        y_off = _dot_nt(cg, st.astype(BF16)) * ecx[:, g * gw:(g + 1) * gw]
        new_st = _dot_tn(xcd[:, g * gw:(g + 1) * gw], bg)
        state_scr[g * gw:(g + 1) * gw, :] = st * rdec[g * gw:(g + 1) * gw, :] + new_st
        for j in range(HEADS_PER_GROUP // 2):
            pair = g * (HEADS_PER_GROUP // 2) + j
            blk = xcb[:, pair * LANES:(pair + 1) * LANES]
            y_pair = None
            for half in range(2):
                h = 2 * pair + half
                seg = a_cs[:, h:h + 1] - a_cst[h:h + 1, :]
                decay = jnp.exp(jnp.where(tril, seg, NEG_BIG))
                mh = (cb * decay).astype(BF16)
                keep = (lane < SSM_HEAD_DIM) if half == 0 else (lane >= SSM_HEAD_DIM)
                part = _dot(mh, jnp.where(keep, blk, jnp.zeros_like(blk)))
                y_pair = part if y_pair is None else y_pair + part
            y_parts.append(y_pair + y_off[:, (pair % (HEADS_PER_GROUP // 2)) * LANES:
                                          (pair % (HEADS_PER_GROUP // 2) + 1) * LANES])
    y = jnp.concatenate(y_parts, axis=1)
    y = y + dsk_ref[...] * xs
    if rows_in < lc:
        z = jnp.concatenate([z_ref[0], jnp.zeros((lc - rows_in, D_SSM), F32)], axis=0)
    else:
        z = z_ref[0]
    y = y * _silu(z)
    gn = D_SSM // N_GROUPS
    outs = []
    for g in range(N_GROUPS):
        yg = y[:, g * gn:(g + 1) * gn]
        ms = jnp.mean(yg * yg, axis=-1, keepdims=True)
        outs.append(yg * lax.rsqrt(ms + EPS) * nw_ref[:, g * gn:(g + 1) * gn])
    out = jnp.concatenate(outs, axis=1).astype(BF16)
    y_ref[0] = out[0:rows_in]

    @pl.when(ci == n_chunks - 1)
    def _():
        fin_ref[0] = state_scr[...]


def _dot_x3_left(sel, x):
    hi, mid, lo = _split3(x)
    return _dot(sel, hi) + _dot(sel, mid) + _dot(sel, lo)


def _dot_x2_left(sel, x):
    hi, lo = _split2(x)
    return _dot(sel, hi) + _dot(sel, lo)


def _ssd(src, col_blocks, dt_raw, halo, init_state, prm, rows_in, n_valid):
    nb, seq = src.shape[0], src.shape[1]
    n_chunks = max(1, seq // SSD_CHUNK)
    bc_w = N_GROUPS * D_STATE
    dtt = jnp.swapaxes(dt_raw[:, :, :N_SSM_HEADS], 1, 2)
    tri = np.tril(np.ones((SSD_CHUNK, SSD_CHUNK)))
    expand = np.zeros((LANES, D_SSM))
    expand[np.arange(D_SSM) // SSM_HEAD_DIM, np.arange(D_SSM)] = 1.0
    sel = expand[:N_SSM_HEADS].T
    cx, cbk, cck, cz = col_blocks

    def const(shape):
        return pl.BlockSpec(shape, lambda b, c: (0,) * len(shape))

    return pl.pallas_call(
        functools.partial(_ssd_kernel, rows_in=rows_in, n_valid=n_valid),
        grid=(nb, n_chunks),
        in_specs=[
            pl.BlockSpec((1, rows_in, D_SSM), lambda b, c: (b, c, cx)),
            pl.BlockSpec((1, rows_in, bc_w), lambda b, c: (b, c, cbk)),
            pl.BlockSpec((1, rows_in, bc_w), lambda b, c: (b, c, cck)),
            pl.BlockSpec((1, rows_in, D_SSM), lambda b, c: (b, c, cz)),
            pl.BlockSpec((1, rows_in, LANES), lambda b, c: (b, c, 0)),
            pl.BlockSpec((1, N_SSM_HEADS, rows_in), lambda b, c: (b, 0, c)),
            pl.BlockSpec((1, HALO, CONV_DIM), lambda b, c: (b, 0, 0)),
            pl.BlockSpec((1, D_SSM, D_STATE), lambda b, c: (b, 0, 0)),
            const((CONV_WIDTH, CONV_DIM)),
            const((1, CONV_DIM)),
            const((1, LANES)),
            const((N_SSM_HEADS, 1)),
            const((1, LANES)),
            const((N_SSM_HEADS, 1)),
            const((1, D_SSM)),
            const((1, D_SSM)),
            const((SSD_CHUNK, SSD_CHUNK)),
            const((SSD_CHUNK, SSD_CHUNK)),
            const((LANES, D_SSM)),
            const((D_SSM, N_SSM_HEADS)),
        ],
        out_specs=[
            pl.BlockSpec((1, rows_in, D_SSM), lambda b, c: (b, c, 0)),
            pl.BlockSpec((1, D_SSM, D_STATE), lambda b, c: (b, 0, 0)),
        ],
        out_shape=[
            jax.ShapeDtypeStruct((nb, seq, D_SSM), BF16),
            jax.ShapeDtypeStruct((nb, D_SSM, D_STATE), F32),
        ],
        scratch_shapes=[
            pltpu.VMEM((HALO + SSD_CHUNK, CONV_DIM), F32),
            pltpu.VMEM((D_SSM, D_STATE), F32),
        ],
        compiler_params=_cparams(("arbitrary", "arbitrary")),
        name="ssd",
    )(src, src, src, src, dt_raw, dtt, halo, init_state,
      prm["conv_w"], prm["conv_b"], prm["dt_bias"], prm["dt_bias_t"], prm["a"], prm["a_t"],
      prm["d_skip"], prm["ssm_norm_w"],
      jnp.asarray(tri, BF16), jnp.asarray(tri.T, BF16), jnp.asarray(expand, BF16),
      jnp.asarray(sel, BF16))


def _merge_kernel(o_ref, s_ref, wa_ref, ws_ref, ga_ref, gs_ref, out_ref):
    a = _dot(o_ref[...], wa_ref[...])
    s = _dot(s_ref[...], ws_ref[...])
    out_ref[...] = (_sigmoid(ga_ref[...]) * a + _sigmoid(gs_ref[...]) * s).astype(BF16)


def _merge(o, s, wa, ws, proj, tm, tn):
    t = o.shape[0]
    ga0, gs0 = COL_GA // tn, COL_GS // tn
    return pl.pallas_call(
        _merge_kernel,
        grid=(t // tm, D_MODEL // tn),
        in_specs=[
            pl.BlockSpec((tm, ATT_WIDTH), lambda i, j: (i, 0)),
            pl.BlockSpec((tm, D_SSM), lambda i, j: (i, 0)),
            pl.BlockSpec((ATT_WIDTH, tn), lambda i, j: (0, j)),
            pl.BlockSpec((D_SSM, tn), lambda i, j: (0, j)),
            pl.BlockSpec((tm, tn), lambda i, j: (i, ga0 + j)),
            pl.BlockSpec((tm, tn), lambda i, j: (i, gs0 + j)),
        ],
        out_specs=pl.BlockSpec((tm, tn), lambda i, j: (i, j)),
        out_shape=jax.ShapeDtypeStruct((t, D_MODEL), BF16),
        compiler_params=_cparams(("arbitrary", "arbitrary")),
        name="merge",
    )(o, s, wa, ws, proj, proj)


ROUTE_E1, ROUTE_E2, ROUTE_W1, ROUTE_W2 = 0, 1, 2, 3


def _resid_kernel(x_ref, m_ref, wo_ref, nw_ref, wrh_ref, wrl_ref, br_ref, *rest, n_real):
    h_ref, u_ref, route_ref = rest[-3:]

    @pl.when(pl.program_id(0) >= n_real)
    def _():
        h_ref[...] = jnp.zeros(h_ref.shape, F32)
        u_ref[...] = jnp.zeros(u_ref.shape, F32)
        route_ref[...] = jnp.zeros(route_ref.shape, F32)

    @pl.when(pl.program_id(0) < n_real)
    def _():
        _resid_tile(x_ref, m_ref, wo_ref, nw_ref, wrh_ref, wrl_ref, br_ref,
                    h_ref, u_ref, route_ref)


def _resid_tile(x_ref, m_ref, wo_ref, nw_ref, wrh_ref, wrl_ref, br_ref, h_ref, u_ref, route_ref):
    h = x_ref[...] + _dot(m_ref[...], wo_ref[...])
    h_ref[...] = h
    ms = jnp.mean(h * h, axis=-1, keepdims=True)
    u = h * lax.rsqrt(ms + EPS) * nw_ref[...]
    u_hi, u_lo = _split2(u)
    u_ref[...] = u
    logits = (_dot(u_hi, wrh_ref[...]) + _dot(u_lo, wrh_ref[...])
              + _dot(u_hi, wrl_ref[...]) + br_ref[...])
    lane = lax.broadcasted_iota(jnp.int32, logits.shape, 1)
    lane_f = lane.astype(F32)
    far = float(2 * LANES)

    def first_max(vals):
        top = jnp.max(vals, axis=1, keepdims=True)
        idx = jnp.min(jnp.where(vals == top, lane_f, far), axis=1, keepdims=True)
        return top, idx

    is_group = (lane >= N_EXPERTS) & (lane < N_EXPERTS + N_EXPERT_GROUPS)
    gl = jnp.where(is_group, logits, NEG_BIG)
    g_top, g_idx = first_max(gl)
    g_p = 1.0 / jnp.sum(jnp.exp(gl - g_top), axis=1, keepdims=True)
    lo_lane = (g_idx - N_EXPERTS) * EXPERTS_PER_GROUP
    in_group = (lane_f >= lo_lane) & (lane_f < lo_lane + EXPERTS_PER_GROUP)
    el = jnp.where(in_group, logits, NEG_BIG)
    m1, i1 = first_max(el)
    el2 = jnp.where(lane_f == i1, NEG_BIG, el)
    m2, i2 = first_max(el2)
    e = jnp.exp(m2 - m1)
    w1 = 1.0 / (1.0 + e)
    w2 = e / (1.0 + e)
    route = jnp.where(lane == ROUTE_E1, i1, 0.0)
    route = jnp.where(lane == ROUTE_E2, i2, route)
    route = jnp.where(lane == ROUTE_W1, g_p * w1, route)
    route_ref[...] = jnp.where(lane == ROUTE_W2, g_p * w2, route)


def _resid(x, merged, wo, norm_w, wr_hi, wr_lo, br, tm, t_all, row_off, bufs):
    t = x.shape[0]
    blk_off = row_off // tm
    n_real = t // tm
    n_fill = pl.cdiv(t_all - t, tm) if bufs is None else 0

    def const(shape):
        return pl.BlockSpec(shape, lambda i: (0,) * len(shape))

    in_specs = [
        pl.BlockSpec((tm, D_MODEL), lambda i: (jnp.minimum(i, n_real - 1), 0)),
        pl.BlockSpec((tm, D_MODEL), lambda i: (jnp.minimum(i, n_real - 1), 0)),
        const((D_MODEL, D_MODEL)),
        const((1, D_MODEL)),
        const((D_MODEL, LANES)),
        const((D_MODEL, LANES)),
        const((1, LANES)),
    ]
    args = [x, merged, wo, norm_w, wr_hi, wr_lo, br]
    aliases = {}
    if bufs is not None:
        aliases = {len(args) + k: k for k in range(len(bufs))}
        in_specs += [pl.BlockSpec(memory_space=pl.ANY)] * len(bufs)
        args += list(bufs)
    return pl.pallas_call(
        functools.partial(_resid_kernel, n_real=n_real),
        grid=(n_real + n_fill,),
        in_specs=in_specs,
        out_specs=[
            pl.BlockSpec((tm, D_MODEL), lambda i: (i + blk_off, 0)),
            pl.BlockSpec((tm, D_MODEL), lambda i: (i + blk_off, 0)),
            pl.BlockSpec((tm, LANES), lambda i: (i + blk_off, 0)),
        ],
        out_shape=[
            jax.ShapeDtypeStruct((t_all, D_MODEL), F32),
            jax.ShapeDtypeStruct((t_all, D_MODEL), F32),
            jax.ShapeDtypeStruct((t_all, LANES), F32),
        ],
        input_output_aliases=aliases,
        compiler_params=_cparams(("arbitrary",)),
        name="resid",
    )(*args)


MOE_TILE = 256
COMB_TILE = 128
DMA_UNROLL = 8


def _route_plan(route, n_tiles):
    n_pairs = 2 * route.shape[0]
    pair_e = route[:, ROUTE_E1:ROUTE_E2 + 1].astype(jnp.int32).reshape(-1)
    onehot = (pair_e[:, None] == jnp.arange(N_EXPERTS, dtype=jnp.int32)[None, :]).astype(jnp.int32)
    csum = jnp.cumsum(onehot, axis=0)
    rank = jnp.sum((csum - onehot) * onehot, axis=1)
    tiles_e = (csum[-1] + MOE_TILE - 1) // MOE_TILE
    tile_end = jnp.cumsum(tiles_e)
    first_row = (tile_end - tiles_e) * MOE_TILE
    slot = jnp.sum(onehot * first_row[None, :], axis=1) + rank
    tok_of_slot = jnp.zeros(((n_tiles + 1) * MOE_TILE,), jnp.int32).at[slot].set(
        jnp.arange(n_pairs, dtype=jnp.int32) // 2)
    tile_expert = jnp.sum(
        jnp.arange(n_tiles + 1, dtype=jnp.int32)[:, None] >= tile_end[None, :], axis=1)
    tile_expert = jnp.minimum(tile_expert, N_EXPERTS - 1).astype(jnp.int32)
    n_used = tile_end[-1:].astype(jnp.int32)
    slot_tab = slot.reshape(-1, COMB_TILE, 2).transpose(0, 2, 1).reshape(-1).astype(jnp.int32)
    return tile_expert, n_used, tok_of_slot, slot_tab


def _gmm_kernel(te_ref, nu_ref, tok_ref, u_hbm, wg_ref, wu_ref, wd_ref, o_ref,
                xbuf, sem, wgb, wub, wdb):
    t = pl.program_id(0)
    n_used = nu_ref[0]
    cur = lax.rem(t, 2)

    @pl.when(jnp.logical_or(t == 0, te_ref[t] != te_ref[jnp.maximum(t - 1, 0)]))
    def _():
        wgb[...] = wg_ref[0].astype(BF16)
        wub[...] = wu_ref[0].astype(BF16)
        wdb[...] = wd_ref[0].astype(BF16)

    def row_copy(tile, r, buf):
        tok = tok_ref[tile * MOE_TILE + r]
        return pltpu.make_async_copy(
            u_hbm.at[pl.ds(tok, 1), :], xbuf.at[buf, pl.ds(r, 1), :], sem.at[buf])

    def issue(tile, buf):
        def body(r, carry):
            row_copy(tile, r, buf).start()
            return carry
        lax.fori_loop(0, MOE_TILE, body, 0, unroll=DMA_UNROLL)

    def wait(tile, buf):
        def body(r, carry):
            row_copy(tile, r, buf).wait()
            return carry
        lax.fori_loop(0, MOE_TILE, body, 0, unroll=DMA_UNROLL)

    @pl.when(t == 0)
    def _():
        issue(0, 0)

    @pl.when(t < n_used)
    def _():
        wait(t, cur)
        x = xbuf[cur].astype(BF16)
        for r in range(MOE_TILE):
            row_copy(t + 1, r, 1 - cur).start()
        hid = _silu(_dot(x, wgb[...])) * _dot(x, wub[...])
        o_ref[...] = _dot(hid.astype(BF16), wdb[...])

    @pl.when(t == n_used)
    def _():
        wait(t, cur)

    @pl.when(t >= n_used)
    def _():
        o_ref[...] = jnp.zeros(o_ref.shape, F32)


def _gmm(u_all, plan, wg, wu, wd, n_tiles):
    tile_expert, n_used, tok_of_slot, _ = plan
    grid_spec = pltpu.PrefetchScalarGridSpec(
        num_scalar_prefetch=3,
        grid=(n_tiles + 1,),
        in_specs=[
            pl.BlockSpec(memory_space=pl.ANY),
            pl.BlockSpec((1, D_MODEL, D_EXPERT), lambda t, te, nu, tok: (te[t], 0, 0)),
            pl.BlockSpec((1, D_MODEL, D_EXPERT), lambda t, te, nu, tok: (te[t], 0, 0)),
            pl.BlockSpec((1, D_EXPERT, D_MODEL), lambda t, te, nu, tok: (te[t], 0, 0)),
        ],
        out_specs=pl.BlockSpec((MOE_TILE, D_MODEL), lambda t, te, nu, tok: (t, 0)),
        scratch_shapes=[
            pltpu.VMEM((2, MOE_TILE, D_MODEL), F32),
            pltpu.SemaphoreType.DMA((2,)),
            pltpu.VMEM((D_MODEL, D_EXPERT), BF16),
            pltpu.VMEM((D_MODEL, D_EXPERT), BF16),
            pltpu.VMEM((D_EXPERT, D_MODEL), BF16),
        ],
    )
    return pl.pallas_call(
        _gmm_kernel,
        grid_spec=grid_spec,
        out_shape=jax.ShapeDtypeStruct(((n_tiles + 1) * MOE_TILE, D_MODEL), F32),
        compiler_params=_cparams(("arbitrary",)),
        name="gmm",
    )(tile_expert, n_used, tok_of_slot, u_all, wg, wu, wd)


def _combine_kernel(slot_ref, route_ref, h_ref, o_hbm, yp_ref, ys_ref, gbuf, sem, *, n_prompt):
    i = pl.program_id(0)
    n = pl.num_programs(0)
    cur = lax.rem(i, 2)
    rows = 2 * COMB_TILE

    def row_copy(tile, j, buf):
        slot = slot_ref[tile * rows + j]
        return pltpu.make_async_copy(
            o_hbm.at[pl.ds(slot, 1), :], gbuf.at[buf, pl.ds(j, 1), :], sem.at[buf])

    def issue(tile, buf):
        def body(j, carry):
            row_copy(tile, j, buf).start()
            return carry
        lax.fori_loop(0, rows, body, 0, unroll=DMA_UNROLL)

    def wait(tile, buf):
        def body(j, carry):
            row_copy(tile, j, buf).wait()
            return carry
        lax.fori_loop(0, rows, body, 0, unroll=DMA_UNROLL)

    @pl.when(i == 0)
    def _():
        issue(0, 0)

    @pl.when(i + 1 < n)
    def _():
        issue(i + 1, 1 - cur)

    wait(i, cur)
    w1 = route_ref[:, ROUTE_W1:ROUTE_W1 + 1]
    w2 = route_ref[:, ROUTE_W2:ROUTE_W2 + 1]
    y = h_ref[...] + w1 * gbuf[cur, 0:COMB_TILE, :] + w2 * gbuf[cur, COMB_TILE:rows, :]

    @pl.when(i < n_prompt)
    def _():
        yp_ref[...] = y

    @pl.when(i >= n_prompt)
    def _():
        ys_ref[...] = y


def _combine(route, h_all, o_sorted, plan, t_prompt):
    t_all = h_all.shape[0]
    n_prompt = t_prompt // COMB_TILE
    slot_tab = plan[3]
    grid_spec = pltpu.PrefetchScalarGridSpec(
        num_scalar_prefetch=1,
        grid=(t_all // COMB_TILE,),
        in_specs=[
            pl.BlockSpec((COMB_TILE, LANES), lambda i, st: (i, 0)),
            pl.BlockSpec((COMB_TILE, D_MODEL), lambda i, st: (i, 0)),
            pl.BlockSpec(memory_space=pl.ANY),
        ],
        out_specs=[
            pl.BlockSpec((COMB_TILE, D_MODEL), lambda i, st: (jnp.minimum(i, n_prompt - 1), 0)),
            pl.BlockSpec((COMB_TILE, D_MODEL), lambda i, st: (0, 0)),
        ],
        scratch_shapes=[
            pltpu.VMEM((2, 2 * COMB_TILE, D_MODEL), F32),
            pltpu.SemaphoreType.DMA((2,)),
        ],
    )
    return pl.pallas_call(
        functools.partial(_combine_kernel, n_prompt=n_prompt),
        grid_spec=grid_spec,
        out_shape=[
            jax.ShapeDtypeStruct((t_prompt, D_MODEL), F32),
            jax.ShapeDtypeStruct((t_all - t_prompt, D_MODEL), F32),
        ],
        compiler_params=_cparams(("arbitrary",)),
        name="combine",
    )(slot_tab, route, h_all, o_sorted)


def _layer_tokens(x2d, w, tm_proj, tm_small):
    proj, dt_raw = _proj(x2d, w["norm_attn_w"], w["w_a"], w["w_b"], w["w_dt"], tm_proj, 1024)
    qn, kn, kb, vb = _qk_norm(proj, w["q_norm_w"], w["k_norm_w"], tm_small)
    return proj, dt_raw, qn, kn, kb, vb


def _branch_merge(x2d, o, s, proj, w, tm, tm_resid, t_all, row_off, bufs):
    merged = _merge(o, s, w["w_att_out"], w["w_ssm_out"], proj, tm, 512)
    return _resid(x2d, merged, w["w_o"], w["norm_ffn_w"], w["wr_hi"], w["wr_lo"], w["br"],
                  tm_resid, t_all, row_off, bufs)


def kernel(x_prompt, x_sample, cache_k, cache_v, state_ssm, state_conv, page_table, norm_attn_w, w_in, q_norm_w, k_norm_w, lambda_q1, lambda_k1, lambda_q2, lambda_k2, subln_w, w_att_out, conv_w, conv_b, dt_bias, a_log, d_skip, ssm_norm_w, w_ssm_out, w_o, norm_ffn_w, w_group_router, b_group_router, w_expert_router, b_expert_router, w_gate, w_up, w_down):
    layer = 0
    nb, seq, _ = x_prompt.shape
    db, dec_seq, _ = x_sample.shape

    w_in_l = w_in[layer]
    c_dt = Q_WIDTH + K_WIDTH + V_WIDTH + D_SSM + CONV_DIM
    w_a = w_in_l[:, :c_dt].astype(BF16)
    w_b = w_in_l[:, c_dt + N_SSM_HEADS:].astype(BF16)
    w_dt = jnp.pad(w_in_l[:, c_dt:c_dt + N_SSM_HEADS], ((0, 0), (0, LANES - N_SSM_HEADS))).astype(BF16)
    wr = jnp.concatenate([w_expert_router[layer], w_group_router[layer]], axis=1)
    wr = jnp.pad(wr, ((0, 0), (0, LANES - wr.shape[1])))
    wr_hi = wr.astype(BF16)
    wr_lo = (wr - wr_hi.astype(F32)).astype(BF16)
    br = jnp.concatenate([b_expert_router[layer], b_group_router[layer]])
    br = jnp.pad(br, (0, LANES - br.shape[0])).reshape(1, LANES)
    pad_h = (0, LANES - N_SSM_HEADS)
    w = dict(
        norm_attn_w=norm_attn_w[layer].reshape(1, D_MODEL), w_a=w_a, w_b=w_b, w_dt=w_dt,
        q_norm_w=q_norm_w[layer], k_norm_w=k_norm_w[layer],
        w_att_out=w_att_out[layer].astype(BF16), w_ssm_out=w_ssm_out[layer].astype(BF16),
        w_o=w_o[layer].astype(BF16), norm_ffn_w=norm_ffn_w[layer].reshape(1, D_MODEL),
        wr_hi=wr_hi, wr_lo=wr_lo, br=br,
        w_gate=w_gate[layer], w_up=w_up[layer], w_down=w_down[layer],
    )
    ssm_prm = dict(
        conv_w=conv_w[layer], conv_b=conv_b[layer].reshape(1, CONV_DIM),
        dt_bias=jnp.pad(dt_bias[layer], pad_h).reshape(1, LANES),
        dt_bias_t=dt_bias[layer].reshape(N_SSM_HEADS, 1),
        a=jnp.pad(-jnp.exp(a_log[layer]), pad_h).reshape(1, LANES),
        a_t=(-jnp.exp(a_log[layer])).reshape(N_SSM_HEADS, 1),
        d_skip=jnp.repeat(d_skip[layer], SSM_HEAD_DIM).reshape(1, D_SSM),
        ssm_norm_w=ssm_norm_w[layer].reshape(1, D_SSM),
    )
    lam_vecs = jnp.stack([lambda_q1[layer], lambda_k1[layer], lambda_q2[layer], lambda_k2[layer]])
    sw = subln_w[layer]
    ssd_cols = (COL_X // D_SSM, COL_B // (N_GROUPS * D_STATE), COL_C // (N_GROUPS * D_STATE),
                COL_Z // D_SSM)

    xp = x_prompt.reshape(nb * seq, D_MODEL)
    n_tok = db * dec_seq
    t_prompt = nb * seq
    t_all = t_prompt + n_tok
    proj_p, dt_p, qn_p, kn_p, kb_p, vb_p = _layer_tokens(xp, w, 1024, 512)
    o_p = _attn_prompt(qn_p, kb_p, vb_p, lam_vecs, sw, 512)
    s_p, ssm_p = _ssd(
        proj_p.reshape(nb, seq, PROJ_WIDTH), ssd_cols, dt_p.reshape(nb, seq, LANES),
        jnp.zeros((nb, HALO, CONV_DIM), F32), jnp.zeros((nb, D_SSM, D_STATE), F32),
        ssm_prm, SSD_CHUNK, SSD_CHUNK)
    bufs = _branch_merge(xp, o_p, s_p.reshape(t_prompt, D_SSM), proj_p, w, 1024, 256,
                         t_all, 0, None)
    keep = CONV_WIDTH - 1
    conv_p = proj_p.reshape(nb, seq, PROJ_WIDTH)[:, seq - keep:, COL_X:COL_X + CONV_DIM]

    xs = x_sample.reshape(db * dec_seq, D_MODEL)
    proj_s, dt_s, qn_s, kn_s, _, _ = _layer_tokens(xs, w, n_tok, n_tok)
    v_s = proj_s[:, COL_V:COL_V + V_WIDTH]
    o_s = _attn_sample(qn_s, kn_s, v_s, cache_k[layer], cache_v[layer], page_table, lam_vecs, sw)
    rows_s = SUBLANES
    pad_rows = ((0, 0), (0, rows_s - dec_seq), (0, 0))
    src_s = jnp.pad(proj_s[:, COL_Z:COL_GA].reshape(db, dec_seq, COL_GA - COL_Z), pad_rows)
    halo_s = jnp.pad(state_conv[layer], ((0, 0), (HALO - (CONV_WIDTH - 1), 0), (0, 0)))
    cols_s = ((COL_X - COL_Z) // D_SSM, (COL_B - COL_Z) // (N_GROUPS * D_STATE),
              (COL_C - COL_Z) // (N_GROUPS * D_STATE), 0)
    s_s, ssm_s = _ssd(
        src_s, cols_s, jnp.pad(dt_s.reshape(db, dec_seq, LANES), pad_rows), halo_s,
        state_ssm[layer].reshape(db, D_SSM, D_STATE), ssm_prm, rows_s, dec_seq)
    s_s = s_s[:, :dec_seq].reshape(n_tok, D_SSM)
    h_all, u_all, route = _branch_merge(xs, o_s, s_s, proj_s, w, n_tok, n_tok,
                                        t_all, t_prompt, bufs)

    n_tiles = 2 * t_all // MOE_TILE + N_EXPERTS
    plan = _route_plan(route, n_tiles)
    o_sorted = _gmm(u_all, plan, w["w_gate"], w["w_up"], w["w_down"], n_tiles)
    y_p, y_s = _combine(route, h_all, o_sorted, plan, t_prompt)
    conv_s =proj_s.reshape(db, dec_seq, PROJ_WIDTH)[:, dec_seq - keep:, COL_X:COL_X + CONV_DIM]

    return (
        y_p.reshape(nb, seq, D_MODEL),
        y_s.reshape(db, dec_seq, D_MODEL),
        kn_p.reshape(1, nb, seq, N_KV_HEADS, 2 * HEAD_DIM),
        proj_p[:, COL_V:COL_V + V_WIDTH].reshape(1, nb, seq, N_KV_HEADS, V_DIM),
        ssm_p.reshape(1, nb, N_SSM_HEADS, SSM_HEAD_DIM, D_STATE),
        conv_p[None],
        kn_s.reshape(1, db, dec_seq, N_KV_HEADS, 2 * HEAD_DIM),
        v_s.reshape(1, db, dec_seq, N_KV_HEADS, V_DIM),
        ssm_s.reshape(1, db, N_SSM_HEADS, SSM_HEAD_DIM, D_STATE),
        conv_s[None],
    )
```

```python
import functools
import math

import jax
import jax.numpy as jnp
import ml_dtypes
import numpy as np
from jax import lax
from jax.experimental import pallas as pl
from jax.experimental.pallas import tpu as pltpu

F32 = jnp.float32
BF16 = jnp.bfloat16

D_MODEL = 2048
N_HEADS = 8
N_KV_HEADS = 4
GQA_REP = N_HEADS // N_KV_HEADS
HEAD_DIM = 64
V_DIM = 2 * HEAD_DIM
Q_WIDTH = N_HEADS * 2 * HEAD_DIM
K_WIDTH = N_KV_HEADS * 2 * HEAD_DIM
V_WIDTH = N_KV_HEADS * V_DIM
ATT_WIDTH = N_HEADS * V_DIM
D_SSM = D_MODEL
SSM_HEAD_DIM = 64
N_SSM_HEADS = D_SSM // SSM_HEAD_DIM
N_GROUPS = 4
HEADS_PER_GROUP = N_SSM_HEADS // N_GROUPS
D_STATE = 128
CONV_WIDTH = 4
CONV_DIM = D_SSM + 2 * N_GROUPS * D_STATE
SSD_CHUNK = 128
N_EXPERT_GROUPS = 4
EXPERTS_PER_GROUP = 8
N_EXPERTS = N_EXPERT_GROUPS * EXPERTS_PER_GROUP
D_EXPERT = D_MODEL // 4
PAGE_SIZE = 128
EPS = 1e-6
LAM_INIT = 0.8 - 0.6 * math.exp(-0.3 * 0)

LANES = 128
SUBLANES = 8
NEG_BIG = -1e30
VMEM_LIMIT = 56 * 1024 * 1024

COL_Q = 0
COL_K = COL_Q + Q_WIDTH
COL_V = COL_K + K_WIDTH
COL_Z = COL_V + V_WIDTH
COL_X = COL_Z + D_SSM
COL_B = COL_X + D_SSM
COL_C = COL_B + N_GROUPS * D_STATE
COL_GA = COL_C + N_GROUPS * D_STATE
COL_GS = COL_GA + D_MODEL
PROJ_WIDTH = COL_GS + D_MODEL

ALIBI_SLOPES = [2.0 ** (-8.0 * (h + 1) / N_HEADS) for h in range(N_HEADS)]


def _cparams(sem):
    return pltpu.CompilerParams(dimension_semantics=sem, vmem_limit_bytes=VMEM_LIMIT)


def _dot(a, b):
    return jnp.dot(a, b, preferred_element_type=F32)


def _dot_nt(a, b):
    return lax.dot_general(a, b, (((1,), (1,)), ((), ())), preferred_element_type=F32)


def _dot_tn(a, b):
    return lax.dot_general(a, b, (((0,), (0,)), ((), ())), preferred_element_type=F32)


def _split2(x):
    hi = x.astype(BF16)
    lo = (x - hi.astype(F32)).astype(BF16)
    return hi, lo


def _split3(x):
    hi = x.astype(BF16)
    r = x - hi.astype(F32)
    mid = r.astype(BF16)
    lo = (r - mid.astype(F32)).astype(BF16)
    return hi, mid, lo


def _dot_x2(x, sel):
    hi, lo = _split2(x)
    return _dot(hi, sel) + _dot(lo, sel)


def _dot_x3(x, sel):
    hi, mid, lo = _split3(x)
    return _dot(hi, sel) + _dot(mid, sel) + _dot(lo, sel)


D_HALF = D_MODEL // 2
HI_HALF_MASK = -65536


def _pack_bf16_pair(lo, hi):
    lo_bits = lax.bitcast_convert_type(lo.astype(BF16).astype(F32), jnp.int32)
    hi_bits = lax.bitcast_convert_type(hi.astype(BF16).astype(F32), jnp.int32)
    return lax.shift_right_logical(lo_bits, jnp.int32(16)) | (hi_bits & jnp.int32(HI_HALF_MASK))


def _unpack_bf16_pair(words):
    lo = lax.bitcast_convert_type(lax.shift_left(words, jnp.int32(16)), F32)
    hi = lax.bitcast_convert_type(words & jnp.int32(HI_HALF_MASK), F32)
    return lo, hi


def _sigmoid(x):
    return 1.0 / (1.0 + jnp.exp(-x))


def _silu(x):
    return x * _sigmoid(x)


def _softplus(x):
    return jnp.maximum(x, 0.0) + jnp.log1p(jnp.exp(-jnp.abs(x)))


NORM_ROWS = 256


def _proj_kernel(x_ref, nw_ref, wa_ref, wb_ref, wdt_ref, o_ref, dt_ref, u_scr, *, n_a):
    j = pl.program_id(1)

    @pl.when(j == 0)
    def _():
        tm = x_ref.shape[0]
        for lo in range(0, tm, min(tm, NORM_ROWS)):
            hi = lo + min(tm, NORM_ROWS)
            x = x_ref[lo:hi, :]
            ms = jnp.mean(x * x, axis=-1, keepdims=True)
            u_scr[lo:hi, :] = (x * lax.rsqrt(ms + EPS) * nw_ref[...]).astype(BF16)
        dt_ref[...] = _dot(u_scr[...], wdt_ref[...])

    @pl.when(j < n_a)
    def _():
        o_ref[...] = _dot(u_scr[...], wa_ref[...])

    @pl.when(j >= n_a)
    def _():
        o_ref[...] = _dot(u_scr[...], wb_ref[...])


def _proj(x, norm_w, w_a, w_b, w_dt, tm, tn):
    t = x.shape[0]
    n_a = w_a.shape[1] // tn
    return pl.pallas_call(
        functools.partial(_proj_kernel, n_a=n_a),
        grid=(t // tm, PROJ_WIDTH // tn),
        in_specs=[
            pl.BlockSpec((tm, D_MODEL), lambda i, j: (i, 0)),
            pl.BlockSpec((1, D_MODEL), lambda i, j: (0, 0)),
            pl.BlockSpec((D_MODEL, tn), lambda i, j: (0, jnp.minimum(j, n_a - 1))),
            pl.BlockSpec((D_MODEL, tn), lambda i, j: (0, jnp.maximum(j - n_a, 0))),
            pl.BlockSpec((D_MODEL, LANES), lambda i, j: (0, 0)),
        ],
        out_specs=[
            pl.BlockSpec((tm, tn), lambda i, j: (i, j)),
            pl.BlockSpec((tm, LANES), lambda i, j: (i, 0)),
        ],
        out_shape=[
            jax.ShapeDtypeStruct((t, PROJ_WIDTH), F32),
            jax.ShapeDtypeStruct((t, LANES), F32),
        ],
        scratch_shapes=[pltpu.VMEM((tm, D_MODEL), BF16)],
        compiler_params=_cparams(("arbitrary", "arbitrary")),
        name="proj",
    )(x, norm_w, w_a, w_b, w_dt)


LOG2E = math.log2(math.e)
Q_SCALE = LOG2E * HEAD_DIM ** -0.5


def _qknorm_kernel(p_ref, qw_ref, kw_ref, g_ref, qn_ref, kn_ref, kb_ref, vb_ref):
    gsum = g_ref[...]
    n_q = Q_WIDTH // LANES
    for c in range((Q_WIDTH + K_WIDTH) // LANES):
        x = p_ref[:, c * LANES:(c + 1) * LANES]
        ss = _dot_x2(x * x, gsum)
        y = x * lax.rsqrt(ss * (1.0 / HEAD_DIM) + EPS)
        if c < n_q:
            qn_ref[:, c * LANES:(c + 1) * LANES] = (y * qw_ref[...] * Q_SCALE).astype(BF16)
        else:
            kn = y * kw_ref[...]
            kn_ref[:, (c - n_q) * LANES:(c - n_q + 1) * LANES] = kn
            kb_ref[:, (c - n_q) * LANES:(c - n_q + 1) * LANES] = kn.astype(BF16)
    vb_ref[...] = p_ref[:, COL_V:COL_V + V_WIDTH].astype(BF16)


def _qk_norm(proj, q_norm_w, k_norm_w, tm):
    t = proj.shape[0]
    group = np.kron(np.eye(LANES // HEAD_DIM), np.ones((HEAD_DIM, HEAD_DIM)))
    qw = jnp.tile(q_norm_w, LANES // HEAD_DIM).reshape(1, LANES)
    kw = jnp.tile(k_norm_w, LANES // HEAD_DIM).reshape(1, LANES)
    return pl.pallas_call(
        _qknorm_kernel,
        grid=(t // tm,),
        in_specs=[
            pl.BlockSpec((tm, Q_WIDTH + K_WIDTH + V_WIDTH), lambda i: (i, 0)),
            pl.BlockSpec((1, LANES), lambda i: (0, 0)),
            pl.BlockSpec((1, LANES), lambda i: (0, 0)),
            pl.BlockSpec((LANES, LANES), lambda i: (0, 0)),
        ],
        out_specs=[
            pl.BlockSpec((tm, Q_WIDTH), lambda i: (i, 0)),
            pl.BlockSpec((tm, K_WIDTH), lambda i: (i, 0)),
            pl.BlockSpec((tm, K_WIDTH), lambda i: (i, 0)),
            pl.BlockSpec((tm, V_WIDTH), lambda i: (i, 0)),
        ],
        out_shape=[
            jax.ShapeDtypeStruct((t, Q_WIDTH), BF16),
            jax.ShapeDtypeStruct((t, K_WIDTH), F32),
            jax.ShapeDtypeStruct((t, K_WIDTH), BF16),
            jax.ShapeDtypeStruct((t, V_WIDTH), BF16),
        ],
        compiler_params=_cparams(("arbitrary",)),
        name="qk_norm",
    )(proj, qw, kw, jnp.asarray(group, BF16))


def _diff_lambda(lam_ref):
    lamv = lam_ref[...]
    s1 = jnp.sum(lamv[0:1] * lamv[1:2], axis=1, keepdims=True)
    s2 = jnp.sum(lamv[2:3] * lamv[3:4], axis=1, keepdims=True)
    return jnp.exp(s1) - jnp.exp(s2) + LAM_INIT


def _subln(o, w):
    ms = jnp.mean(o * o, axis=-1, keepdims=True)
    return o * lax.rsqrt(ms + EPS) * w * (1.0 - LAM_INIT)


N_SLOPE_PARTS = 3


def _bf16_parts(x, n):
    parts, rem = [], np.float32(x)
    for _ in range(n):
        p = np.float32(rem.astype(ml_dtypes.bfloat16))
        parts.append(float(p))
        rem = np.float32(rem - p)
    return parts


def _alibi_tables(tk):
    qcols = np.zeros((N_HEADS, 16, LANES), np.float32)
    csum = np.zeros((N_HEADS,), np.float32)
    for h, slope in enumerate(ALIBI_SLOPES):
        parts = _bf16_parts(slope * LOG2E, N_SLOPE_PARTS)
        csum[h] = np.float32(sum(np.float32(p) for p in parts))
        for i, p in enumerate(parts):
            qcols[h, :, i] = p * LANES
            qcols[h, :, N_SLOPE_PARTS + i] = p
    pos = np.arange(tk)
    kcols = np.zeros((tk, LANES), np.float32)
    kcols[:, 0:N_SLOPE_PARTS] = (pos // LANES)[:, None]
    kcols[:, N_SLOPE_PARTS:2 * N_SLOPE_PARTS] = (pos % LANES)[:, None]
    return jnp.asarray(qcols, BF16), jnp.asarray(kcols, BF16), jnp.asarray(csum, F32)


def _attn_p_kernel(qi_ref, ki_ref, cf_ref, q_ref, k_ref, v_ref, qc_ref, kc_ref, lam_ref, sw_ref,
                   o_ref, qa_scr, m_scr, l_scr, acc_scr, *, tq):
    g = pl.program_id(0)
    t = pl.program_id(1)
    qi = qi_ref[t]
    ki = ki_ref[t]
    n_sub = GQA_REP * 2
    rows = n_sub * tq
    n_chunk = tq // LANES

    @pl.when(ki == 0)
    def _():
        lane = lax.broadcasted_iota(jnp.int32, (tq, LANES), 1)
        for r in range(GQA_REP):
            qq = q_ref[:, r * LANES:(r + 1) * LANES]
            qc = jnp.broadcast_to(qc_ref[r, 0:1, :], (tq, LANES))
            for c in range(2):
                idx = 2 * r + c
                keep = (lane < HEAD_DIM) if c == 0 else (lane >= HEAD_DIM)
                qa_scr[idx * tq:(idx + 1) * tq, 0:LANES] = jnp.where(keep, qq, jnp.zeros_like(qq))
                qa_scr[idx * tq:(idx + 1) * tq, LANES:2 * LANES] = qc
        m_scr[...] = jnp.full(m_scr.shape, NEG_BIG, F32)
        l_scr[...] = jnp.zeros(l_scr.shape, F32)
        acc_scr[...] = jnp.zeros(acc_scr.shape, F32)

    def step(diag):
        k_aug = jnp.concatenate([k_ref[...], kc_ref[...]], axis=1)
        s_all = _dot_nt(qa_scr[...], k_aug)
        if diag:
            row_in = lax.broadcasted_iota(jnp.int32, (rows, tq), 0) & (tq - 1)
            col = lax.broadcasted_iota(jnp.int32, (rows, tq), 1)
            s_all = jnp.where(col <= row_in, s_all, NEG_BIG)
        block_dist = ((qi - ki) * tq).astype(F32)
        ps, alphas = [], []
        for r in range(GQA_REP):
            off = -cf_ref[g * GQA_REP + r] * block_dist
            lo, hi = r * 2 * tq, (r + 1) * 2 * tq
            chunks = [s_all[lo:hi, j * LANES:(j + 1) * LANES] for j in range(n_chunk)]
            m_prev = m_scr[lo:hi]
            m_blk = jnp.max(functools.reduce(jnp.maximum, chunks), axis=1, keepdims=True) + off
            m_new = jnp.maximum(m_prev, m_blk)
            alpha = jnp.exp2(m_prev - m_new)
            m_sub = m_new - off
            pj = [jnp.exp2(ch - m_sub) for ch in chunks]
            l_scr[lo:hi] = alpha * l_scr[lo:hi] + functools.reduce(jnp.add, pj)
            m_scr[lo:hi] = m_new
            ps.append(jnp.concatenate(pj, axis=1).astype(BF16))
            alphas.append(alpha)
        pv = _dot(jnp.concatenate(ps, axis=0), v_ref[...])
        acc_scr[...] = jnp.concatenate(alphas, axis=0) * acc_scr[...] + pv

    @pl.when(ki < qi)
    def _():
        step(False)

    @pl.when(ki == qi)
    def _():
        step(True)
        lam = _diff_lambda(lam_ref)
        for r in range(GQA_REP):
            i1, i2 = 2 * r * tq, (2 * r + 1) * tq
            l1 = jnp.sum(l_scr[i1:i1 + tq], axis=1, keepdims=True)
            l2 = jnp.sum(l_scr[i2:i2 + tq], axis=1, keepdims=True)
            o = acc_scr[i1:i1 + tq] / l1 - lam * (acc_scr[i2:i2 + tq] / l2)
            o_ref[:, r * LANES:(r + 1) * LANES] = _subln(o, sw_ref[...]).astype(BF16)


def _attn_prompt(qn, kb, vb, lam_vecs, subln_w, tq):
    t = qn.shape[0]
    nq = t // tq
    pairs = [(i, j) for i in range(nq) for j in range(i + 1)]
    qi_tab = jnp.asarray([p[0] for p in pairs], jnp.int32)
    ki_tab = jnp.asarray([p[1] for p in pairs], jnp.int32)
    qcols, kcols, csum = _alibi_tables(tq)
    n_sub = 2 * GQA_REP
    grid_spec = pltpu.PrefetchScalarGridSpec(
        num_scalar_prefetch=3,
        grid=(N_KV_HEADS, len(pairs)),
        in_specs=[
            pl.BlockSpec((tq, GQA_REP * LANES), lambda g, t, qi, ki, cf: (qi[t], g)),
            pl.BlockSpec((tq, LANES), lambda g, t, qi, ki, cf: (ki[t], g)),
            pl.BlockSpec((tq, V_DIM), lambda g, t, qi, ki, cf: (ki[t], g)),
            pl.BlockSpec((GQA_REP, 16, LANES), lambda g, t, qi, ki, cf: (g, 0, 0)),
            pl.BlockSpec((tq, LANES), lambda g, t, qi, ki, cf: (0, 0)),
            pl.BlockSpec((4, HEAD_DIM), lambda g, t, qi, ki, cf: (0, 0)),
            pl.BlockSpec((1, V_DIM), lambda g, t, qi, ki, cf: (0, 0)),
        ],
        out_specs=pl.BlockSpec((tq, GQA_REP * V_DIM), lambda g, t, qi, ki, cf: (qi[t], g)),
        scratch_shapes=[
            pltpu.VMEM((n_sub * tq, 2 * LANES), BF16),
            pltpu.VMEM((n_sub * tq, LANES), F32),
            pltpu.VMEM((n_sub * tq, LANES), F32),
            pltpu.VMEM((n_sub * tq, V_DIM), F32),
        ],
    )
    return pl.pallas_call(
        functools.partial(_attn_p_kernel, tq=tq),
        grid_spec=grid_spec,
        out_shape=jax.ShapeDtypeStruct((t, ATT_WIDTH), BF16),
        compiler_params=_cparams(("arbitrary", "arbitrary")),
        name="attn_p",
    )(qi_tab, ki_tab, csum, qn, kb, vb, qcols, kcols, lam_vecs, subln_w.reshape(1, V_DIM))


PAGES_PER_STEP = 8
N_PAGE_BUFS = 3
PAGE_LOOKAHEAD = N_PAGE_BUFS - 1
PAGE_GROUP = 4
ROWS_S = 2 * 4 * N_HEADS


def _attn_s_kernel(pt_ref, q_ref, d0_ref, mask_ref, sl_ref, bn_ref, kn_ref, vn_ref, lam_ref,
                   sw_ref, ck_hbm, cv_hbm, o_ref, m_scr, l_scr, acc_scr, kbuf, vbuf, sem,
                   *, n_steps):
    s_id = pl.program_id(1)
    step = pl.program_id(0) * n_steps + s_id
    n_total = pl.num_programs(0) * n_steps
    cur = lax.rem(step, N_PAGE_BUFS)

    def page_copies(n, i, buf):
        page = pt_ref[n * PAGES_PER_STEP + i]
        return (pltpu.make_async_copy(ck_hbm.at[page], kbuf.at[buf, i], sem.at[0, buf]),
                pltpu.make_async_copy(cv_hbm.at[page], vbuf.at[buf, i], sem.at[1, buf]))

    def start_pages(n, buf):
        for i in range(PAGES_PER_STEP):
            for cp in page_copies(n, i, buf):
                cp.start()

    @pl.when(step == 0)
    def _():
        for n in range(PAGE_LOOKAHEAD):
            start_pages(n, n)

    @pl.when(s_id == 0)
    def _():
        m_scr[...] = jnp.full(m_scr.shape, NEG_BIG, F32)
        l_scr[...] = jnp.zeros(l_scr.shape, F32)
        acc_scr[...] = jnp.zeros(acc_scr.shape, F32)

    for i in range(PAGES_PER_STEP):
        for cp in page_copies(step, i, cur):
            cp.wait()

    nxt = jnp.minimum(step + PAGE_LOOKAHEAD, n_total - 1)
    start_pages(nxt, lax.rem(step + PAGE_LOOKAHEAD, N_PAGE_BUFS))

    q = q_ref[0]
    k_refs = [kbuf.at[cur, i] for i in range(PAGES_PER_STEP)]
    v_refs = [vbuf.at[cur, i] for i in range(PAGES_PER_STEP)]

    def update(scores, values):
        m_prev = m_scr[...]
        m_new = m_prev
        for sc in scores:
            m_new = jnp.maximum(m_new, jnp.max(sc, axis=1, keepdims=True))
        alpha = jnp.exp2(m_prev - m_new)
        l_new = alpha * l_scr[...]
        acc = alpha * acc_scr[...]
        for sc, vv in zip(scores, values):
            p = jnp.exp2(sc - m_new)
            l_new = l_new + jnp.sum(p, axis=1, keepdims=True)
            acc = acc + _dot(p.astype(BF16), vv)
        m_scr[...] = m_new
        l_scr[...] = l_new
        acc_scr[...] = acc

    for first in range(0, PAGES_PER_STEP, PAGE_GROUP):
        scores, values = [], []
        for i in range(first, first + PAGE_GROUP):
            page_start = ((s_id * PAGES_PER_STEP + i) * PAGE_SIZE).astype(F32)
            bias = sl_ref[...] * (d0_ref[...] - page_start) + mask_ref[...]
            scores.append(_dot_nt(q, k_refs[i][...].astype(BF16)) + bias)
            values.append(v_refs[i][...].astype(BF16))
        update(scores, values)

    @pl.when(s_id == n_steps - 1)
    def _():
        sc = _dot_nt(q, kn_ref[0].astype(BF16)) + bn_ref[...]
        update([sc], [vn_ref[0].astype(BF16)])
        lam = _diff_lambda(lam_ref)
        half = ROWS_S // 2
        o1 = acc_scr[0:half] / l_scr[0:half]
        o2 = acc_scr[half:ROWS_S] / l_scr[half:ROWS_S]
        o_ref[0] = _subln(o1 - lam * o2, sw_ref[...]).astype(BF16)

    @pl.when(step == n_total - 1)
    def _():
        for ahead in range(1, PAGE_LOOKAHEAD + 1):
            for i in range(PAGES_PER_STEP):
                for cp in page_copies(step, i, lax.rem(step + ahead, N_PAGE_BUFS)):
                    cp.wait()


def _attn_sample(qn_s, kn_s, v_s, cache_k, cache_v, page_table, lam_vecs, subln_w):
    db, n_pages = page_table.shape
    dec_seq = qn_s.shape[0] // db
    past = n_pages * PAGE_SIZE
    n_steps = n_pages // PAGES_PER_STEP
    page_rows = PAGE_SIZE * N_KV_HEADS
    n_phys = cache_k.shape[0]
    ck = cache_k.reshape(n_phys, page_rows, 2 * HEAD_DIM)
    cv = cache_v.reshape(n_phys, page_rows, V_DIM)

    q5 = qn_s.reshape(db, dec_seq, N_HEADS, 2, HEAD_DIM)
    zeros = jnp.zeros_like(q5[:, :, :, 0])
    q_all = jnp.stack([jnp.concatenate([q5[:, :, :, 0], zeros], axis=-1),
                       jnp.concatenate([zeros, q5[:, :, :, 1]], axis=-1)], axis=1)
    q_all = q_all.reshape(db, ROWS_S, LANES)

    r = np.arange(ROWS_S)
    tok_r = (r % (dec_seq * N_HEADS)) // N_HEADS
    head_r = r % N_HEADS
    slope_r = np.asarray(ALIBI_SLOPES)[head_r] * LOG2E
    c = np.arange(page_rows)
    key_c, grp_c = c // N_KV_HEADS, c % N_KV_HEADS
    same = (head_r[:, None] // GQA_REP) == grp_c[None, :]
    d0 = np.broadcast_to(past + tok_r[:, None] - key_c[None, :], (ROWS_S, page_rows))
    mask = np.where(same, 0.0, NEG_BIG)
    sl = np.broadcast_to(-slope_r[:, None], (ROWS_S, 1))
    cn = np.arange(LANES)
    tok_c, grp_n = cn // N_KV_HEADS, cn % N_KV_HEADS
    ok = ((head_r[:, None] // GQA_REP) == grp_n[None, :]) & (tok_c[None, :] <= tok_r[:, None])
    bn = np.where(ok, -slope_r[:, None] * (tok_r[:, None] - tok_c[None, :]), NEG_BIG)

    new_rows = dec_seq * N_KV_HEADS
    kn_pad = jnp.pad(kn_s.reshape(db, new_rows, LANES), ((0, 0), (0, LANES - new_rows), (0, 0)))
    vn_pad = jnp.pad(v_s.reshape(db, new_rows, LANES), ((0, 0), (0, LANES - new_rows), (0, 0)))

    def const(shape):
        return pl.BlockSpec(shape, lambda b, s, pt: (0,) * len(shape))

    grid_spec = pltpu.PrefetchScalarGridSpec(
        num_scalar_prefetch=1,
        grid=(db, n_steps),
        in_specs=[
            pl.BlockSpec((1, ROWS_S, LANES), lambda b, s, pt: (b, 0, 0)),
            const((ROWS_S, page_rows)),
            const((ROWS_S, page_rows)),
            const((ROWS_S, 1)),
            const((ROWS_S, LANES)),
            pl.BlockSpec((1, LANES, LANES), lambda b, s, pt: (b, 0, 0)),
            pl.BlockSpec((1, LANES, LANES), lambda b, s, pt: (b, 0, 0)),
            const((4, HEAD_DIM)),
            const((1, V_DIM)),
            pl.BlockSpec(memory_space=pl.ANY),
            pl.BlockSpec(memory_space=pl.ANY),
        ],
        out_specs=pl.BlockSpec((1, ROWS_S // 2, V_DIM), lambda b, s, pt: (b, 0, 0)),
        scratch_shapes=[
            pltpu.VMEM((ROWS_S, 1), F32),
            pltpu.VMEM((ROWS_S, 1), F32),
            pltpu.VMEM((ROWS_S, V_DIM), F32),
            pltpu.VMEM((N_PAGE_BUFS, PAGES_PER_STEP, page_rows, LANES), F32),
            pltpu.VMEM((N_PAGE_BUFS, PAGES_PER_STEP, page_rows, LANES), F32),
            pltpu.SemaphoreType.DMA((2, N_PAGE_BUFS)),
        ],
    )
    o = pl.pallas_call(
        functools.partial(_attn_s_kernel, n_steps=n_steps),
        grid_spec=grid_spec,
        out_shape=jax.ShapeDtypeStruct((db, ROWS_S // 2, V_DIM), BF16),
        compiler_params=_cparams(("arbitrary", "arbitrary")),
        name="attn_s",
    )(page_table.reshape(-1), q_all, jnp.asarray(d0, F32), jnp.asarray(mask, F32),
      jnp.asarray(sl, F32), jnp.asarray(bn, F32), kn_pad, vn_pad, lam_vecs,
      subln_w.reshape(1, V_DIM),
      ck, cv)
    return o.reshape(db * dec_seq, ATT_WIDTH)


HALO = SUBLANES


def _ssd_kernel(xs_ref, b_ref, c_ref, z_ref, dt_ref, dtt_ref, halo_ref, init_ref,
                cw_ref, cb_ref, dtb_ref, dtbt_ref, a_ref, at_ref, dsk_ref, nw_ref,
                tri_ref, trit_ref, exp_ref, sel_ref,
                y_ref, fin_ref, win_scr, state_scr, *, rows_in, n_valid):
    ci = pl.program_id(1)
    n_chunks = pl.num_programs(1)
    lc = SSD_CHUNK
    bc_w = N_GROUPS * D_STATE

    @pl.when(ci == 0)
    def _():
        state_scr[...] = init_ref[0]
        win_scr[0:HALO, :] = halo_ref[0]

    if rows_in < lc:
        win_scr[HALO:HALO + lc, :] = jnp.zeros((lc, CONV_DIM), F32)
    win_scr[HALO:HALO + rows_in, 0:D_SSM] = xs_ref[0]
    win_scr[HALO:HALO + rows_in, D_SSM:D_SSM + bc_w] = b_ref[0]
    win_scr[HALO:HALO + rows_in, D_SSM + bc_w:CONV_DIM] = c_ref[0]

    acc = cb_ref[...]
    for tap in range(CONV_WIDTH):
        off = HALO - (CONV_WIDTH - 1) + tap
        acc = acc + win_scr[off:off + lc, :] * cw_ref[tap:tap + 1, :]
    conv = _silu(acc)
    win_scr[0:HALO, :] = win_scr[lc:lc + HALO, :]
    xs = conv[:, 0:D_SSM]
    bm = conv[:, D_SSM:D_SSM + bc_w].astype(BF16)
    cm = conv[:, D_SSM + bc_w:CONV_DIM].astype(BF16)

    if rows_in < lc:
        dt_in = jnp.concatenate([dt_ref[0], jnp.zeros((lc - rows_in, LANES), F32)], axis=0)
        dtt_in = jnp.concatenate(
            [dtt_ref[0], jnp.zeros((N_SSM_HEADS, lc - rows_in), F32)], axis=1)
    else:
        dt_in, dtt_in = dt_ref[0], dtt_ref[0]
    rowi = lax.broadcasted_iota(jnp.int32, (lc, LANES), 0)
    coli = lax.broadcasted_iota(jnp.int32, (N_SSM_HEADS, lc), 1)
    dt = jnp.where(rowi < n_valid, _softplus(dt_in + dtb_ref[...]), 0.0)
    dtt = jnp.where(coli < n_valid, _softplus(dtt_in + dtbt_ref[...]), 0.0)
    a_cs = _dot_x3_left(tri_ref[...], dt * a_ref[...])
    a_cst = _dot_x3(dtt * at_ref[...], trit_ref[...])
    a_last = a_cs[lc - 1:lc, :]
    exp_cs = jnp.exp(a_cs)
    exp_rest = jnp.exp(a_last - a_cs)
    expand = exp_ref[...]
    dtx = _dot_x2(dt, expand)
    ecx = _dot_x2(exp_cs, expand)
    erx = _dot_x2(exp_rest, expand)
    xc = xs * dtx
    xcb = xc.astype(BF16)
    xcd = (xc * erx).astype(BF16)

    last_t = jnp.exp(a_cst[:, lc - 1:lc])
    rdec = _dot_x2_left(sel_ref[...], jnp.broadcast_to(last_t, (N_SSM_HEADS, D_STATE)))

    tril = (lax.broadcasted_iota(jnp.int32, (lc, lc), 0)
            >= lax.broadcasted_iota(jnp.int32, (lc, lc), 1))
    lane = lax.broadcasted_iota(jnp.int32, (lc, LANES), 1)
    gw = HEADS_PER_GROUP * SSM_HEAD_DIM
    y_parts = []
    for g in range(N_GROUPS):
        bg = bm[:, g * D_STATE:(g + 1) * D_STATE]
        cg = cm[:, g * D_STATE:(g + 1) * D_STATE]
        cb = _dot_nt(cg, bg)
        st = state_scr[g * gw:(g + 1) * gw, :]
        y_off = _dot_nt(cg, st.astype(BF16)) * ecx[:, g * gw:(g + 1) * gw]
        new_st = _dot_tn(xcd[:, g * gw:(g + 1) * gw], bg)
        state_scr[g * gw:(g + 1) * gw, :] = st * rdec[g * gw:(g + 1) * gw, :] + new_st
        for j in range(HEADS_PER_GROUP // 2):
            pair = g * (HEADS_PER_GROUP // 2) + j
            blk = xcb[:, pair * LANES:(pair + 1) * LANES]
            y_pair = None
            for half in range(2):
                h = 2 * pair + half
                seg = a_cs[:, h:h + 1] - a_cst[h:h + 1, :]
                decay = jnp.exp(jnp.where(tril, seg, NEG_BIG))
                mh = (cb * decay).astype(BF16)
                keep = (lane < SSM_HEAD_DIM) if half == 0 else (lane >= SSM_HEAD_DIM)
                part = _dot(mh, jnp.where(keep, blk, jnp.zeros_like(blk)))
                y_pair = part if y_pair is None else y_pair + part
            y_parts.append(y_pair + y_off[:, (pair % (HEADS_PER_GROUP // 2)) * LANES:
                                          (pair % (HEADS_PER_GROUP // 2) + 1) * LANES])
    y = jnp.concatenate(y_parts, axis=1)
    y = y + dsk_ref[...] * xs
    if rows_in < lc:
        z = jnp.concatenate([z_ref[0], jnp.zeros((lc - rows_in, D_SSM), F32)], axis=0)
    else:
        z = z_ref[0]
    y = y * _silu(z)
    gn = D_SSM // N_GROUPS
    outs = []
    for g in range(N_GROUPS):
        yg = y[:, g * gn:(g + 1) * gn]
        ms = jnp.mean(yg * yg, axis=-1, keepdims=True)
        outs.append(yg * lax.rsqrt(ms + EPS) * nw_ref[:, g * gn:(g + 1) * gn])
    out = jnp.concatenate(outs, axis=1).astype(BF16)
    y_ref[0] = out[0:rows_in]

    @pl.when(ci == n_chunks - 1)
    def _():
        fin_ref[0] = state_scr[...]


def _dot_x3_left(sel, x):
    hi, mid, lo = _split3(x)
    return _dot(sel, hi) + _dot(sel, mid) + _dot(sel, lo)


def _dot_x2_left(sel, x):
    hi, lo = _split2(x)
    return _dot(sel, hi) + _dot(sel, lo)


def _ssd(src, col_blocks, dt_raw, halo, init_state, prm, rows_in, n_valid):
    nb, seq = src.shape[0], src.shape[1]
    n_chunks = max(1, seq // SSD_CHUNK)
    bc_w = N_GROUPS * D_STATE
    dtt = jnp.swapaxes(dt_raw[:, :, :N_SSM_HEADS], 1, 2)
    tri = np.tril(np.ones((SSD_CHUNK, SSD_CHUNK)))
    expand = np.zeros((LANES, D_SSM))
    expand[np.arange(D_SSM) // SSM_HEAD_DIM, np.arange(D_SSM)] = 1.0
    sel = expand[:N_SSM_HEADS].T
    cx, cbk, cck, cz = col_blocks

    def const(shape):
        return pl.BlockSpec(shape, lambda b, c: (0,) * len(shape))

    return pl.pallas_call(
        functools.partial(_ssd_kernel, rows_in=rows_in, n_valid=n_valid),
        grid=(nb, n_chunks),
        in_specs=[
            pl.BlockSpec((1, rows_in, D_SSM), lambda b, c: (b, c, cx)),
            pl.BlockSpec((1, rows_in, bc_w), lambda b, c: (b, c, cbk)),
            pl.BlockSpec((1, rows_in, bc_w), lambda b, c: (b, c, cck)),
            pl.BlockSpec((1, rows_in, D_SSM), lambda b, c: (b, c, cz)),
            pl.BlockSpec((1, rows_in, LANES), lambda b, c: (b, c, 0)),
            pl.BlockSpec((1, N_SSM_HEADS, rows_in), lambda b, c: (b, 0, c)),
            pl.BlockSpec((1, HALO, CONV_DIM), lambda b, c: (b, 0, 0)),
            pl.BlockSpec((1, D_SSM, D_STATE), lambda b, c: (b, 0, 0)),
            const((CONV_WIDTH, CONV_DIM)),
            const((1, CONV_DIM)),
            const((1, LANES)),
            const((N_SSM_HEADS, 1)),
            const((1, LANES)),
            const((N_SSM_HEADS, 1)),
            const((1, D_SSM)),
            const((1, D_SSM)),
            const((SSD_CHUNK, SSD_CHUNK)),
            const((SSD_CHUNK, SSD_CHUNK)),
            const((LANES, D_SSM)),
            const((D_SSM, N_SSM_HEADS)),
        ],
        out_specs=[
            pl.BlockSpec((1, rows_in, D_SSM), lambda b, c: (b, c, 0)),
            pl.BlockSpec((1, D_SSM, D_STATE), lambda b, c: (b, 0, 0)),
        ],
        out_shape=[
            jax.ShapeDtypeStruct((nb, seq, D_SSM), BF16),
            jax.ShapeDtypeStruct((nb, D_SSM, D_STATE), F32),
        ],
        scratch_shapes=[
            pltpu.VMEM((HALO + SSD_CHUNK, CONV_DIM), F32),
            pltpu.VMEM((D_SSM, D_STATE), F32),
        ],
        compiler_params=_cparams(("arbitrary", "arbitrary")),
        name="ssd",
    )(src, src, src, src, dt_raw, dtt, halo, init_state,
      prm["conv_w"], prm["conv_b"], prm["dt_bias"], prm["dt_bias_t"], prm["a"], prm["a_t"],
      prm["d_skip"], prm["ssm_norm_w"],
      jnp.asarray(tri, BF16), jnp.asarray(tri.T, BF16), jnp.asarray(expand, BF16),
      jnp.asarray(sel, BF16))


def _merge_kernel(o_ref, s_ref, wa_ref, ws_ref, ga_ref, gs_ref, out_ref):
    a = _dot(o_ref[...], wa_ref[...])
    s = _dot(s_ref[...], ws_ref[...])
    out_ref[...] = (_sigmoid(ga_ref[...]) * a + _sigmoid(gs_ref[...]) * s).astype(BF16)


def _merge(o, s, wa, ws, proj, tm, tn):
    t = o.shape[0]
    ga0, gs0 = COL_GA // tn, COL_GS // tn
    return pl.pallas_call(
        _merge_kernel,
        grid=(t // tm, D_MODEL // tn),
        in_specs=[
            pl.BlockSpec((tm, ATT_WIDTH), lambda i, j: (i, 0)),
            pl.BlockSpec((tm, D_SSM), lambda i, j: (i, 0)),
            pl.BlockSpec((ATT_WIDTH, tn), lambda i, j: (0, j)),
            pl.BlockSpec((D_SSM, tn), lambda i, j: (0, j)),
            pl.BlockSpec((tm, tn), lambda i, j: (i, ga0 + j)),
            pl.BlockSpec((tm, tn), lambda i, j: (i, gs0 + j)),
        ],
        out_specs=pl.BlockSpec((tm, tn), lambda i, j: (i, j)),
        out_shape=jax.ShapeDtypeStruct((t, D_MODEL), BF16),
        compiler_params=_cparams(("arbitrary", "arbitrary")),
        name="merge",
    )(o, s, wa, ws, proj, proj)


ROUTE_E1, ROUTE_E2, ROUTE_W1, ROUTE_W2 = 0, 1, 2, 3


def _resid_kernel(x_ref, m_ref, wo_ref, nw_ref, wrh_ref, wrl_ref, br_ref, *rest, n_real):
    h_ref, u_ref, route_ref = rest[-3:]

    @pl.when(pl.program_id(0) >= n_real)
    def _():
        h_ref[...] = jnp.zeros(h_ref.shape, F32)
        u_ref[...] = jnp.zeros(u_ref.shape, jnp.int32)
        route_ref[...] = jnp.zeros(route_ref.shape, F32)

    @pl.when(pl.program_id(0) < n_real)
    def _():
        _resid_tile(x_ref, m_ref, wo_ref, nw_ref, wrh_ref, wrl_ref, br_ref,
                    h_ref, u_ref, route_ref)


def _resid_tile(x_ref, m_ref, wo_ref, nw_ref, wrh_ref, wrl_ref, br_ref, h_ref, u_ref, route_ref):
    h = x_ref[...] + _dot(m_ref[...], wo_ref[...])
    h_ref[...] = h
    ms = jnp.mean(h * h, axis=-1, keepdims=True)
    u = h * lax.rsqrt(ms + EPS) * nw_ref[...]
    u_hi, u_lo = _split2(u)
    u_ref[...] = _pack_bf16_pair(u[:, 0:D_HALF], u[:, D_HALF:D_MODEL])
    logits = (_dot(u_hi, wrh_ref[...]) + _dot(u_lo, wrh_ref[...])
              + _dot(u_hi, wrl_ref[...]) + br_ref[...])
    lane = lax.broadcasted_iota(jnp.int32, logits.shape, 1)
    lane_f = lane.astype(F32)
    far = float(2 * LANES)

    def first_max(vals):
        top = jnp.max(vals, axis=1, keepdims=True)
        idx = jnp.min(jnp.where(vals == top, lane_f, far), axis=1, keepdims=True)
        return top, idx

    is_group = (lane >= N_EXPERTS) & (lane < N_EXPERTS + N_EXPERT_GROUPS)
    gl = jnp.where(is_group, logits, NEG_BIG)
    g_top, g_idx = first_max(gl)
    g_p = 1.0 / jnp.sum(jnp.exp(gl - g_top), axis=1, keepdims=True)
    lo_lane = (g_idx - N_EXPERTS) * EXPERTS_PER_GROUP
    in_group = (lane_f >= lo_lane) & (lane_f < lo_lane + EXPERTS_PER_GROUP)
    el = jnp.where(in_group, logits, NEG_BIG)
    m1, i1 = first_max(el)
    el2 = jnp.where(lane_f == i1, NEG_BIG, el)
    m2, i2 = first_max(el2)
    e = jnp.exp(m2 - m1)
    w1 = 1.0 / (1.0 + e)
    w2 = e / (1.0 + e)
    route = jnp.where(lane == ROUTE_E1, i1, 0.0)
    route = jnp.where(lane == ROUTE_E2, i2, route)
    route = jnp.where(lane == ROUTE_W1, g_p * w1, route)
    route_ref[...] = jnp.where(lane == ROUTE_W2, g_p * w2, route)


def _resid(x, merged, wo, norm_w, wr_hi, wr_lo, br, tm, t_all, row_off, bufs):
    t = x.shape[0]
    blk_off = row_off // tm
    n_real = t // tm
    n_fill = pl.cdiv(t_all - t, tm) if bufs is None else 0

    def const(shape):
        return pl.BlockSpec(shape, lambda i: (0,) * len(shape))

    in_specs = [
        pl.BlockSpec((tm, D_MODEL), lambda i: (jnp.minimum(i, n_real - 1), 0)),
        pl.BlockSpec((tm, D_MODEL), lambda i: (jnp.minimum(i, n_real - 1), 0)),
        const((D_MODEL, D_MODEL)),
        const((1, D_MODEL)),
        const((D_MODEL, LANES)),
        const((D_MODEL, LANES)),
        const((1, LANES)),
    ]
    args = [x, merged, wo, norm_w, wr_hi, wr_lo, br]
    aliases = {}
    if bufs is not None:
        aliases = {len(args) + k: k for k in range(len(bufs))}
        in_specs += [pl.BlockSpec(memory_space=pl.ANY)] * len(bufs)
        args += list(bufs)
    return pl.pallas_call(
        functools.partial(_resid_kernel, n_real=n_real),
        grid=(n_real + n_fill,),
        in_specs=in_specs,
        out_specs=[
            pl.BlockSpec((tm, D_MODEL), lambda i: (i + blk_off, 0)),
            pl.BlockSpec((tm, D_HALF), lambda i: (i + blk_off, 0)),
            pl.BlockSpec((tm, LANES), lambda i: (i + blk_off, 0)),
        ],
        out_shape=[
            jax.ShapeDtypeStruct((t_all, D_MODEL), F32),
            jax.ShapeDtypeStruct((t_all, D_HALF), jnp.int32),
            jax.ShapeDtypeStruct((t_all, LANES), F32),
        ],
        input_output_aliases=aliases,
        compiler_params=_cparams(("arbitrary",)),
        name="resid",
    )(*args)


MOE_TILE = 256
COMB_TILE = 128
DMA_UNROLL = 8


def _route_plan(route, n_tiles):
    n_pairs = 2 * route.shape[0]
    pair_e = route[:, ROUTE_E1:ROUTE_E2 + 1].astype(jnp.int32).reshape(-1)
    onehot = (pair_e[:, None] == jnp.arange(N_EXPERTS, dtype=jnp.int32)[None, :]).astype(jnp.int32)
    csum = jnp.cumsum(onehot, axis=0)
    rank = jnp.sum((csum - onehot) * onehot, axis=1)
    tiles_e = (csum[-1] + MOE_TILE - 1) // MOE_TILE
    tile_end = jnp.cumsum(tiles_e)
    first_row = (tile_end - tiles_e) * MOE_TILE
    slot = jnp.sum(onehot * first_row[None, :], axis=1) + rank
    tok_of_slot = jnp.zeros(((n_tiles + 1) * MOE_TILE,), jnp.int32).at[slot].set(
        jnp.arange(n_pairs, dtype=jnp.int32) // 2)
    tile_expert = jnp.sum(
        jnp.arange(n_tiles + 1, dtype=jnp.int32)[:, None] >= tile_end[None, :], axis=1)
    tile_expert = jnp.minimum(tile_expert, N_EXPERTS - 1).astype(jnp.int32)
    n_used = tile_end[-1:].astype(jnp.int32)
    slot_tab = slot.reshape(-1, COMB_TILE, 2).transpose(0, 2, 1).reshape(-1).astype(jnp.int32)
    return tile_expert, n_used, tok_of_slot, slot_tab


def _gmm_kernel(te_ref, nu_ref, tok_ref, u_hbm, wg_ref, wu_ref, wd_ref, o_ref,
                xbuf, sem, wgb, wub, wdb):
    t = pl.program_id(0)
    n_used = nu_ref[0]
    cur = lax.rem(t, 2)

    @pl.when(jnp.logical_or(t == 0, te_ref[t] != te_ref[jnp.maximum(t - 1, 0)]))
    def _():
        wgb[...] = wg_ref[0].astype(BF16)
        wub[...] = wu_ref[0].astype(BF16)
        wdb[...] = wd_ref[0].astype(BF16)

    def row_copy(tile, r, buf):
        tok = tok_ref[tile * MOE_TILE + r]
        return pltpu.make_async_copy(
            u_hbm.at[pl.ds(tok, 1), :], xbuf.at[buf, pl.ds(r, 1), :], sem.at[buf])

    def issue(tile, buf):
        def body(r, carry):
            row_copy(tile, r, buf).start()
            return carry
        lax.fori_loop(0, MOE_TILE, body, 0, unroll=DMA_UNROLL)

    def wait(tile, buf):
        def body(r, carry):
            row_copy(tile, r, buf).wait()
            return carry
        lax.fori_loop(0, MOE_TILE, body, 0, unroll=DMA_UNROLL)

    @pl.when(t == 0)
    def _():
        issue(0, 0)

    @pl.when(t < n_used)
    def _():
        wait(t, cur)
        x_lo, x_hi = _unpack_bf16_pair(xbuf[cur])
        x_lo, x_hi = x_lo.astype(BF16), x_hi.astype(BF16)
        for r in range(MOE_TILE):
            row_copy(t + 1, r, 1 - cur).start()
        hg = _dot(x_lo, wgb[0:D_HALF, :]) + _dot(x_hi, wgb[D_HALF:D_MODEL, :])
        hu = _dot(x_lo, wub[0:D_HALF, :]) + _dot(x_hi, wub[D_HALF:D_MODEL, :])
        out = _dot((_silu(hg) * hu).astype(BF16), wdb[...])
        o_ref[...] = _pack_bf16_pair(out[:, 0:D_HALF], out[:, D_HALF:D_MODEL])

    @pl.when(t == n_used)
    def _():
        wait(t, cur)

    @pl.when(t >= n_used)
    def _():
        o_ref[...] = jnp.zeros(o_ref.shape, jnp.int32)


def _gmm(u_all, plan, wg, wu, wd, n_tiles):
    tile_expert, n_used, tok_of_slot, _ = plan
    grid_spec = pltpu.PrefetchScalarGridSpec(
        num_scalar_prefetch=3,
        grid=(n_tiles + 1,),
        in_specs=[
            pl.BlockSpec(memory_space=pl.ANY),
            pl.BlockSpec((1, D_MODEL, D_EXPERT), lambda t, te, nu, tok: (te[t], 0, 0)),
            pl.BlockSpec((1, D_MODEL, D_EXPERT), lambda t, te, nu, tok: (te[t], 0, 0)),
            pl.BlockSpec((1, D_EXPERT, D_MODEL), lambda t, te, nu, tok: (te[t], 0, 0)),
        ],
        out_specs=pl.BlockSpec((MOE_TILE, D_HALF), lambda t, te, nu, tok: (t, 0)),
        scratch_shapes=[
            pltpu.VMEM((2, MOE_TILE, D_HALF), jnp.int32),
            pltpu.SemaphoreType.DMA((2,)),
            pltpu.VMEM((D_MODEL, D_EXPERT), BF16),
            pltpu.VMEM((D_MODEL, D_EXPERT), BF16),
            pltpu.VMEM((D_EXPERT, D_MODEL), BF16),
        ],
    )
    return pl.pallas_call(
        _gmm_kernel,
        grid_spec=grid_spec,
        out_shape=jax.ShapeDtypeStruct(((n_tiles + 1) * MOE_TILE, D_HALF), jnp.int32),
        compiler_params=_cparams(("arbitrary",)),
        name="gmm",
    )(tile_expert, n_used, tok_of_slot, u_all, wg, wu, wd)


def _combine_kernel(slot_ref, route_ref, h_ref, o_hbm, yp_ref, ys_ref, gbuf, sem, *, n_prompt):
    i = pl.program_id(0)
    n = pl.num_programs(0)
    cur = lax.rem(i, 2)
    rows = 2 * COMB_TILE

    def row_copy(tile, j, buf):
        slot = slot_ref[tile * rows + j]
        return pltpu.make_async_copy(
            o_hbm.at[pl.ds(slot, 1), :], gbuf.at[buf, pl.ds(j, 1), :], sem.at[buf])

    def issue(tile, buf):
        def body(j, carry):
            row_copy(tile, j, buf).start()
            return carry
        lax.fori_loop(0, rows, body, 0, unroll=DMA_UNROLL)

    def wait(tile, buf):
        def body(j, carry):
            row_copy(tile, j, buf).wait()
            return carry
        lax.fori_loop(0, rows, body, 0, unroll=DMA_UNROLL)

    @pl.when(i == 0)
    def _():
        issue(0, 0)

    @pl.when(i + 1 < n)
    def _():
        issue(i + 1, 1 - cur)

    wait(i, cur)
    w1 = route_ref[:, ROUTE_W1:ROUTE_W1 + 1]
    w2 = route_ref[:, ROUTE_W2:ROUTE_W2 + 1]
    g_lo, g_hi = _unpack_bf16_pair(gbuf[cur])
    y = jnp.concatenate(
        [h_ref[:, 0:D_HALF] + w1 * g_lo[0:COMB_TILE] + w2 * g_lo[COMB_TILE:rows],
         h_ref[:, D_HALF:D_MODEL] + w1 * g_hi[0:COMB_TILE] + w2 * g_hi[COMB_TILE:rows]], axis=1)

    @pl.when(i < n_prompt)
    def _():
        yp_ref[...] = y

    @pl.when(i >= n_prompt)
    def _():
        ys_ref[...] = y


def _combine(route, h_all, o_sorted, plan, t_prompt):
    t_all = h_all.shape[0]
    n_prompt = t_prompt // COMB_TILE
    slot_tab = plan[3]
    grid_spec = pltpu.PrefetchScalarGridSpec(
        num_scalar_prefetch=1,
        grid=(t_all // COMB_TILE,),
        in_specs=[
            pl.BlockSpec((COMB_TILE, LANES), lambda i, st: (i, 0)),
            pl.BlockSpec((COMB_TILE, D_MODEL), lambda i, st: (i, 0)),
            pl.BlockSpec(memory_space=pl.ANY),
        ],
        out_specs=[
            pl.BlockSpec((COMB_TILE, D_MODEL), lambda i, st: (jnp.minimum(i, n_prompt - 1), 0)),
            pl.BlockSpec((COMB_TILE, D_MODEL), lambda i, st: (0, 0)),
        ],
        scratch_shapes=[
            pltpu.VMEM((2, 2 * COMB_TILE, D_HALF), jnp.int32),
            pltpu.SemaphoreType.DMA((2,)),
        ],
    )
    return pl.pallas_call(
        functools.partial(_combine_kernel, n_prompt=n_prompt),
        grid_spec=grid_spec,
        out_shape=[
            jax.ShapeDtypeStruct((t_prompt, D_MODEL), F32),
            jax.ShapeDtypeStruct((t_all - t_prompt, D_MODEL), F32),
        ],
        compiler_params=_cparams(("arbitrary",)),
        name="combine",
    )(slot_tab, route, h_all, o_sorted)


def _layer_tokens(x2d, w, tm_proj, tm_small):
    proj, dt_raw = _proj(x2d, w["norm_attn_w"], w["w_a"], w["w_b"], w["w_dt"], tm_proj, 1024)
    qn, kn, kb, vb = _qk_norm(proj, w["q_norm_w"], w["k_norm_w"], tm_small)
    return proj, dt_raw, qn, kn, kb, vb


def _branch_merge(x2d, o, s, proj, w, tm, tm_resid, t_all, row_off, bufs):
    merged = _merge(o, s, w["w_att_out"], w["w_ssm_out"], proj, tm, 512)
    return _resid(x2d, merged, w["w_o"], w["norm_ffn_w"], w["wr_hi"], w["wr_lo"], w["br"],
                  tm_resid, t_all, row_off, bufs)


def kernel(x_prompt, x_sample, cache_k, cache_v, state_ssm, state_conv, page_table, norm_attn_w, w_in, q_norm_w, k_norm_w, lambda_q1, lambda_k1, lambda_q2, lambda_k2, subln_w, w_att_out, conv_w, conv_b, dt_bias, a_log, d_skip, ssm_norm_w, w_ssm_out, w_o, norm_ffn_w, w_group_router, b_group_router, w_expert_router, b_expert_router, w_gate, w_up, w_down):
    layer = 0
    nb, seq, _ = x_prompt.shape
    db, dec_seq, _ = x_sample.shape

    w_in_l = w_in[layer]
    c_dt = Q_WIDTH + K_WIDTH + V_WIDTH + D_SSM + CONV_DIM
    w_a = w_in_l[:, :c_dt].astype(BF16)
    w_b = w_in_l[:, c_dt + N_SSM_HEADS:].astype(BF16)
    w_dt = jnp.pad(w_in_l[:, c_dt:c_dt + N_SSM_HEADS], ((0, 0), (0, LANES - N_SSM_HEADS))).astype(BF16)
    wr = jnp.concatenate([w_expert_router[layer], w_group_router[layer]], axis=1)
    wr = jnp.pad(wr, ((0, 0), (0, LANES - wr.shape[1])))
    wr_hi = wr.astype(BF16)
    wr_lo = (wr - wr_hi.astype(F32)).astype(BF16)
    br = jnp.concatenate([b_expert_router[layer], b_group_router[layer]])
    br = jnp.pad(br, (0, LANES - br.shape[0])).reshape(1, LANES)
    pad_h = (0, LANES - N_SSM_HEADS)
    w = dict(
        norm_attn_w=norm_attn_w[layer].reshape(1, D_MODEL), w_a=w_a, w_b=w_b, w_dt=w_dt,
        q_norm_w=q_norm_w[layer], k_norm_w=k_norm_w[layer],
        w_att_out=w_att_out[layer].astype(BF16), w_ssm_out=w_ssm_out[layer].astype(BF16),
        w_o=w_o[layer].astype(BF16), norm_ffn_w=norm_ffn_w[layer].reshape(1, D_MODEL),
        wr_hi=wr_hi, wr_lo=wr_lo, br=br,
        w_gate=w_gate[layer], w_up=w_up[layer], w_down=w_down[layer],
    )
    ssm_prm = dict(
        conv_w=conv_w[layer], conv_b=conv_b[layer].reshape(1, CONV_DIM),
        dt_bias=jnp.pad(dt_bias[layer], pad_h).reshape(1, LANES),
        dt_bias_t=dt_bias[layer].reshape(N_SSM_HEADS, 1),
        a=jnp.pad(-jnp.exp(a_log[layer]), pad_h).reshape(1, LANES),
        a_t=(-jnp.exp(a_log[layer])).reshape(N_SSM_HEADS, 1),
        d_skip=jnp.repeat(d_skip[layer], SSM_HEAD_DIM).reshape(1, D_SSM),
        ssm_norm_w=ssm_norm_w[layer].reshape(1, D_SSM),
    )
    lam_vecs = jnp.stack([lambda_q1[layer], lambda_k1[layer], lambda_q2[layer], lambda_k2[layer]])
    sw = subln_w[layer]
    ssd_cols = (COL_X // D_SSM, COL_B // (N_GROUPS * D_STATE), COL_C // (N_GROUPS * D_STATE),
                COL_Z // D_SSM)

    xp = x_prompt.reshape(nb * seq, D_MODEL)
    n_tok = db * dec_seq
    t_prompt = nb * seq
    t_all = t_prompt + n_tok
    proj_p, dt_p, qn_p, kn_p, kb_p, vb_p = _layer_tokens(xp, w, 1024, 512)
    o_p = _attn_prompt(qn_p, kb_p, vb_p, lam_vecs, sw, 512)
    s_p, ssm_p = _ssd(
        proj_p.reshape(nb, seq, PROJ_WIDTH), ssd_cols, dt_p.reshape(nb, seq, LANES),
        jnp.zeros((nb, HALO, CONV_DIM), F32), jnp.zeros((nb, D_SSM, D_STATE), F32),
        ssm_prm, SSD_CHUNK, SSD_CHUNK)
    bufs = _branch_merge(xp, o_p, s_p.reshape(t_prompt, D_SSM), proj_p, w, 1024, 256,
                         t_all, 0, None)
    keep = CONV_WIDTH - 1
    conv_p = proj_p.reshape(nb, seq, PROJ_WIDTH)[:, seq - keep:, COL_X:COL_X + CONV_DIM]

    xs = x_sample.reshape(db * dec_seq, D_MODEL)
    proj_s, dt_s, qn_s, kn_s, _, _ = _layer_tokens(xs, w, n_tok, n_tok)
    v_s = proj_s[:, COL_V:COL_V + V_WIDTH]
    o_s = _attn_sample(qn_s, kn_s, v_s, cache_k[layer], cache_v[layer], page_table, lam_vecs, sw)
    rows_s = SUBLANES
    pad_rows = ((0, 0), (0, rows_s - dec_seq), (0, 0))
    src_s = jnp.pad(proj_s[:, COL_Z:COL_GA].reshape(db, dec_seq, COL_GA - COL_Z), pad_rows)
    halo_s = jnp.pad(state_conv[layer], ((0, 0), (HALO - (CONV_WIDTH - 1), 0), (0, 0)))
    cols_s = ((COL_X - COL_Z) // D_SSM, (COL_B - COL_Z) // (N_GROUPS * D_STATE),
              (COL_C - COL_Z) // (N_GROUPS * D_STATE), 0)
    s_s, ssm_s = _ssd(
        src_s, cols_s, jnp.pad(dt_s.reshape(db, dec_seq, LANES), pad_rows), halo_s,
        state_ssm[layer].reshape(db, D_SSM, D_STATE), ssm_prm, rows_s, dec_seq)
    s_s = s_s[:, :dec_seq].reshape(n_tok, D_SSM)
    h_all, u_all, route = _branch_merge(xs, o_s, s_s, proj_s, w, n_tok, n_tok,
                                        t_all, t_prompt, bufs)

    n_tiles = 2 * t_all // MOE_TILE + N_EXPERTS
    plan = _route_plan(route, n_tiles)
    o_sorted = _gmm(u_all, plan, w["w_gate"], w["w_up"], w["w_down"], n_tiles)
    y_p, y_s = _combine(route, h_all, o_sorted, plan, t_prompt)
    conv_s =proj_s.reshape(db, dec_seq, PROJ_WIDTH)[:, dec_seq - keep:, COL_X:COL_X + CONV_DIM]

    return (
        y_p.reshape(nb, seq, D_MODEL),
        y_s.reshape(db, dec_seq, D_MODEL),
        kn_p.reshape(1, nb, seq, N_KV_HEADS, 2 * HEAD_DIM),
        proj_p[:, COL_V:COL_V + V_WIDTH].reshape(1, nb, seq, N_KV_HEADS, V_DIM),
        ssm_p.reshape(1, nb, N_SSM_HEADS, SSM_HEAD_DIM, D_STATE),
        conv_p[None],
        kn_s.reshape(1, db, dec_seq, N_KV_HEADS, 2 * HEAD_DIM),
        v_s.reshape(1, db, dec_seq, N_KV_HEADS, V_DIM),
        ssm_s.reshape(1, db, N_SSM_HEADS, SSM_HEAD_DIM, D_STATE),
        conv_s[None],
    )
```

```python
import functools
import math

import jax
import jax.numpy as jnp
import ml_dtypes
import numpy as np
from jax import lax
from jax.experimental import pallas as pl
from jax.experimental.pallas import tpu as pltpu

F32 = jnp.float32
BF16 = jnp.bfloat16

D_MODEL = 2048
N_HEADS = 8
N_KV_HEADS = 4
GQA_REP = N_HEADS // N_KV_HEADS
HEAD_DIM = 64
V_DIM = 2 * HEAD_DIM
Q_WIDTH = N_HEADS * 2 * HEAD_DIM
K_WIDTH = N_KV_HEADS * 2 * HEAD_DIM
V_WIDTH = N_KV_HEADS * V_DIM
ATT_WIDTH = N_HEADS * V_DIM
D_SSM = D_MODEL
SSM_HEAD_DIM = 64
N_SSM_HEADS = D_SSM // SSM_HEAD_DIM
N_GROUPS = 4
HEADS_PER_GROUP = N_SSM_HEADS // N_GROUPS
D_STATE = 128
CONV_WIDTH = 4
CONV_DIM = D_SSM + 2 * N_GROUPS * D_STATE
SSD_CHUNK = 128
N_EXPERT_GROUPS = 4
EXPERTS_PER_GROUP = 8
N_EXPERTS = N_EXPERT_GROUPS * EXPERTS_PER_GROUP
D_EXPERT = D_MODEL // 4
PAGE_SIZE = 128
EPS = 1e-6
LAM_INIT = 0.8 - 0.6 * math.exp(-0.3 * 0)

LANES = 128
SUBLANES = 8
NEG_BIG = -1e30
VMEM_LIMIT = 56 * 1024 * 1024

COL_Q = 0
COL_K = COL_Q + Q_WIDTH
COL_V = COL_K + K_WIDTH
COL_Z = COL_V + V_WIDTH
COL_X = COL_Z + D_SSM
COL_B = COL_X + D_SSM
COL_C = COL_B + N_GROUPS * D_STATE
COL_GA = COL_C + N_GROUPS * D_STATE
COL_GS = COL_GA + D_MODEL
PROJ_WIDTH = COL_GS + D_MODEL

ALIBI_SLOPES = [2.0 ** (-8.0 * (h + 1) / N_HEADS) for h in range(N_HEADS)]


def _cparams(sem):
    return pltpu.CompilerParams(dimension_semantics=sem, vmem_limit_bytes=VMEM_LIMIT)


def _dot(a, b):
    return jnp.dot(a, b, preferred_element_type=F32)


def _dot_nt(a, b):
    return lax.dot_general(a, b, (((1,), (1,)), ((), ())), preferred_element_type=F32)


def _dot_tn(a, b):
    return lax.dot_general(a, b, (((0,), (0,)), ((), ())), preferred_element_type=F32)


def _split2(x):
    hi = x.astype(BF16)
    lo = (x - hi.astype(F32)).astype(BF16)
    return hi, lo


def _split3(x):
    hi = x.astype(BF16)
    r = x - hi.astype(F32)
    mid = r.astype(BF16)
    lo = (r - mid.astype(F32)).astype(BF16)
    return hi, mid, lo


def _dot_x2(x, sel):
    hi, lo = _split2(x)
    return _dot(hi, sel) + _dot(lo, sel)


def _dot_x3(x, sel):
    hi, mid, lo = _split3(x)
    return _dot(hi, sel) + _dot(mid, sel) + _dot(lo, sel)


def _sigmoid(x):
    return 1.0 / (1.0 + jnp.exp(-x))


def _silu(x):
    return x * _sigmoid(x)


def _softplus(x):
    return jnp.maximum(x, 0.0) + jnp.log1p(jnp.exp(-jnp.abs(x)))


NORM_ROWS = 256


def _proj_kernel(x_ref, nw_ref, wa_ref, wb_ref, wdt_ref, o_ref, dt_ref, u_scr, *, n_a):
    j = pl.program_id(1)

    @pl.when(j == 0)
    def _():
        tm = x_ref.shape[0]
        for lo in range(0, tm, min(tm, NORM_ROWS)):
            hi = lo + min(tm, NORM_ROWS)
            x = x_ref[lo:hi, :]
            ms = jnp.mean(x * x, axis=-1, keepdims=True)
            u_scr[lo:hi, :] = (x * lax.rsqrt(ms + EPS) * nw_ref[...]).astype(BF16)
        dt_ref[...] = _dot(u_scr[...], wdt_ref[...])

    @pl.when(j < n_a)
    def _():
        o_ref[...] = _dot(u_scr[...], wa_ref[...])

    @pl.when(j >= n_a)
    def _():
        o_ref[...] = _dot(u_scr[...], wb_ref[...])


def _proj(x, norm_w, w_a, w_b, w_dt, tm, tn):
    t = x.shape[0]
    n_a = w_a.shape[1] // tn
    return pl.pallas_call(
        functools.partial(_proj_kernel, n_a=n_a),
        grid=(t // tm, PROJ_WIDTH // tn),
        in_specs=[
            pl.BlockSpec((tm, D_MODEL), lambda i, j: (i, 0)),
            pl.BlockSpec((1, D_MODEL), lambda i, j: (0, 0)),
            pl.BlockSpec((D_MODEL, tn), lambda i, j: (0, jnp.minimum(j, n_a - 1))),
            pl.BlockSpec((D_MODEL, tn), lambda i, j: (0, jnp.maximum(j - n_a, 0))),
            pl.BlockSpec((D_MODEL, LANES), lambda i, j: (0, 0)),
        ],
        out_specs=[
            pl.BlockSpec((tm, tn), lambda i, j: (i, j)),
            pl.BlockSpec((tm, LANES), lambda i, j: (i, 0)),
        ],
        out_shape=[
            jax.ShapeDtypeStruct((t, PROJ_WIDTH), F32),
            jax.ShapeDtypeStruct((t, LANES), F32),
        ],
        scratch_shapes=[pltpu.VMEM((tm, D_MODEL), BF16)],
        compiler_params=_cparams(("arbitrary", "arbitrary")),
        name="proj",
    )(x, norm_w, w_a, w_b, w_dt)


LOG2E = math.log2(math.e)
Q_SCALE = LOG2E * HEAD_DIM ** -0.5


def _qknorm_kernel(p_ref, qw_ref, kw_ref, g_ref, qn_ref, kn_ref, kb_ref, vb_ref):
    gsum = g_ref[...]
    n_q = Q_WIDTH // LANES
    for c in range((Q_WIDTH + K_WIDTH) // LANES):
        x = p_ref[:, c * LANES:(c + 1) * LANES]
        ss = _dot_x2(x * x, gsum)
        y = x * lax.rsqrt(ss * (1.0 / HEAD_DIM) + EPS)
        if c < n_q:
            qn_ref[:, c * LANES:(c + 1) * LANES] = (y * qw_ref[...] * Q_SCALE).astype(BF16)
        else:
            kn = y * kw_ref[...]
            kn_ref[:, (c - n_q) * LANES:(c - n_q + 1) * LANES] = kn
            kb_ref[:, (c - n_q) * LANES:(c - n_q + 1) * LANES] = kn.astype(BF16)
    vb_ref[...] = p_ref[:, COL_V:COL_V + V_WIDTH].astype(BF16)


def _qk_norm(proj, q_norm_w, k_norm_w, tm):
    t = proj.shape[0]
    group = np.kron(np.eye(LANES // HEAD_DIM), np.ones((HEAD_DIM, HEAD_DIM)))
    qw = jnp.tile(q_norm_w, LANES // HEAD_DIM).reshape(1, LANES)
    kw = jnp.tile(k_norm_w, LANES // HEAD_DIM).reshape(1, LANES)
    return pl.pallas_call(
        _qknorm_kernel,
        grid=(t // tm,),
        in_specs=[
            pl.BlockSpec((tm, Q_WIDTH + K_WIDTH + V_WIDTH), lambda i: (i, 0)),
            pl.BlockSpec((1, LANES), lambda i: (0, 0)),
            pl.BlockSpec((1, LANES), lambda i: (0, 0)),
            pl.BlockSpec((LANES, LANES), lambda i: (0, 0)),
        ],
        out_specs=[
            pl.BlockSpec((tm, Q_WIDTH), lambda i: (i, 0)),
            pl.BlockSpec((tm, K_WIDTH), lambda i: (i, 0)),
            pl.BlockSpec((tm, K_WIDTH), lambda i: (i, 0)),
            pl.BlockSpec((tm, V_WIDTH), lambda i: (i, 0)),
        ],
        out_shape=[
            jax.ShapeDtypeStruct((t, Q_WIDTH), BF16),
            jax.ShapeDtypeStruct((t, K_WIDTH), F32),
            jax.ShapeDtypeStruct((t, K_WIDTH), BF16),
            jax.ShapeDtypeStruct((t, V_WIDTH), BF16),
        ],
        compiler_params=_cparams(("arbitrary",)),
        name="qk_norm",
    )(proj, qw, kw, jnp.asarray(group, BF16))


def _diff_lambda(lam_ref):
    lamv = lam_ref[...]
    s1 = jnp.sum(lamv[0:1] * lamv[1:2], axis=1, keepdims=True)
    s2 = jnp.sum(lamv[2:3] * lamv[3:4], axis=1, keepdims=True)
    return jnp.exp(s1) - jnp.exp(s2) + LAM_INIT


def _subln(o, w):
    ms = jnp.mean(o * o, axis=-1, keepdims=True)
    return o * lax.rsqrt(ms + EPS) * w * (1.0 - LAM_INIT)


N_SLOPE_PARTS = 3


def _bf16_parts(x, n):
    parts, rem = [], np.float32(x)
    for _ in range(n):
        p = np.float32(rem.astype(ml_dtypes.bfloat16))
        parts.append(float(p))
        rem = np.float32(rem - p)
    return parts


def _alibi_tables(tk):
    qcols = np.zeros((N_HEADS, 16, LANES), np.float32)
    csum = np.zeros((N_HEADS,), np.float32)
    for h, slope in enumerate(ALIBI_SLOPES):
        parts = _bf16_parts(slope * LOG2E, N_SLOPE_PARTS)
        csum[h] = np.float32(sum(np.float32(p) for p in parts))
        for i, p in enumerate(parts):
            qcols[h, :, i] = p * LANES
            qcols[h, :, N_SLOPE_PARTS + i] = p
    pos = np.arange(tk)
    kcols = np.zeros((tk, LANES), np.float32)
    kcols[:, 0:N_SLOPE_PARTS] = (pos // LANES)[:, None]
    kcols[:, N_SLOPE_PARTS:2 * N_SLOPE_PARTS] = (pos % LANES)[:, None]
    return jnp.asarray(qcols, BF16), jnp.asarray(kcols, BF16), jnp.asarray(csum, F32)


def _attn_p_kernel(qi_ref, ki_ref, cf_ref, q_ref, k_ref, v_ref, qc_ref, kc_ref, lam_ref, sw_ref,
                   o_ref, qa_scr, m_scr, l_scr, acc_scr, *, tq):
    g = pl.program_id(0)
    t = pl.program_id(1)
    qi = qi_ref[t]
    ki = ki_ref[t]
    n_sub = GQA_REP * 2
    rows = n_sub * tq
    n_chunk = tq // LANES

    @pl.when(ki == 0)
    def _():
        lane = lax.broadcasted_iota(jnp.int32, (tq, LANES), 1)
        for r in range(GQA_REP):
            qq = q_ref[:, r * LANES:(r + 1) * LANES]
            qc = jnp.broadcast_to(qc_ref[r, 0:1, :], (tq, LANES))
            for c in range(2):
                idx = 2 * r + c
                keep = (lane < HEAD_DIM) if c == 0 else (lane >= HEAD_DIM)
                qa_scr[idx * tq:(idx + 1) * tq, 0:LANES] = jnp.where(keep, qq, jnp.zeros_like(qq))
                qa_scr[idx * tq:(idx + 1) * tq, LANES:2 * LANES] = qc
        m_scr[...] = jnp.full(m_scr.shape, NEG_BIG, F32)
        l_scr[...] = jnp.zeros(l_scr.shape, F32)
        acc_scr[...] = jnp.zeros(acc_scr.shape, F32)

    def step(diag):
        k_aug = jnp.concatenate([k_ref[...], kc_ref[...]], axis=1)
        s_all = _dot_nt(qa_scr[...], k_aug)
        if diag:
            row_in = lax.broadcasted_iota(jnp.int32, (rows, tq), 0) & (tq - 1)
            col = lax.broadcasted_iota(jnp.int32, (rows, tq), 1)
            s_all = jnp.where(col <= row_in, s_all, NEG_BIG)
        block_dist = ((qi - ki) * tq).astype(F32)
        ps, alphas = [], []
        for r in range(GQA_REP):
            off = -cf_ref[g * GQA_REP + r] * block_dist
            lo, hi = r * 2 * tq, (r + 1) * 2 * tq
            chunks = [s_all[lo:hi, j * LANES:(j + 1) * LANES] for j in range(n_chunk)]
            m_prev = m_scr[lo:hi]
            m_blk = jnp.max(functools.reduce(jnp.maximum, chunks), axis=1, keepdims=True) + off
            m_new = jnp.maximum(m_prev, m_blk)
            alpha = jnp.exp2(m_prev - m_new)
            m_sub = m_new - off
            pj = [jnp.exp2(ch - m_sub) for ch in chunks]
            l_scr[lo:hi] = alpha * l_scr[lo:hi] + functools.reduce(jnp.add, pj)
            m_scr[lo:hi] = m_new
            ps.append(jnp.concatenate(pj, axis=1).astype(BF16))
            alphas.append(alpha)
        pv = _dot(jnp.concatenate(ps, axis=0), v_ref[...])
        acc_scr[...] = jnp.concatenate(alphas, axis=0) * acc_scr[...] + pv

    @pl.when(ki < qi)
    def _():
        step(False)

    @pl.when(ki == qi)
    def _():
        step(True)
        lam = _diff_lambda(lam_ref)
        for r in range(GQA_REP):
            i1, i2 = 2 * r * tq, (2 * r + 1) * tq
            l1 = jnp.sum(l_scr[i1:i1 + tq], axis=1, keepdims=True)
            l2 = jnp.sum(l_scr[i2:i2 + tq], axis=1, keepdims=True)
            o = acc_scr[i1:i1 + tq] / l1 - lam * (acc_scr[i2:i2 + tq] / l2)
            o_ref[:, r * LANES:(r + 1) * LANES] = _subln(o, sw_ref[...]).astype(BF16)


def _attn_prompt(qn, kb, vb, lam_vecs, subln_w, tq):
    t = qn.shape[0]
    nq = t // tq
    pairs = [(i, j) for i in range(nq) for j in range(i + 1)]
    qi_tab = jnp.asarray([p[0] for p in pairs], jnp.int32)
    ki_tab = jnp.asarray([p[1] for p in pairs], jnp.int32)
    qcols, kcols, csum = _alibi_tables(tq)
    n_sub = 2 * GQA_REP
    grid_spec = pltpu.PrefetchScalarGridSpec(
        num_scalar_prefetch=3,
        grid=(N_KV_HEADS, len(pairs)),
        in_specs=[
            pl.BlockSpec((tq, GQA_REP * LANES), lambda g, t, qi, ki, cf: (qi[t], g)),
            pl.BlockSpec((tq, LANES), lambda g, t, qi, ki, cf: (ki[t], g)),
            pl.BlockSpec((tq, V_DIM), lambda g, t, qi, ki, cf: (ki[t], g)),
            pl.BlockSpec((GQA_REP, 16, LANES), lambda g, t, qi, ki, cf: (g, 0, 0)),
            pl.BlockSpec((tq, LANES), lambda g, t, qi, ki, cf: (0, 0)),
            pl.BlockSpec((4, HEAD_DIM), lambda g, t, qi, ki, cf: (0, 0)),
            pl.BlockSpec((1, V_DIM), lambda g, t, qi, ki, cf: (0, 0)),
        ],
        out_specs=pl.BlockSpec((tq, GQA_REP * V_DIM), lambda g, t, qi, ki, cf: (qi[t], g)),
        scratch_shapes=[
            pltpu.VMEM((n_sub * tq, 2 * LANES), BF16),
            pltpu.VMEM((n_sub * tq, LANES), F32),
            pltpu.VMEM((n_sub * tq, LANES), F32),
            pltpu.VMEM((n_sub * tq, V_DIM), F32),
        ],
    )
    return pl.pallas_call(
        functools.partial(_attn_p_kernel, tq=tq),
        grid_spec=grid_spec,
        out_shape=jax.ShapeDtypeStruct((t, ATT_WIDTH), BF16),
        compiler_params=_cparams(("arbitrary", "arbitrary")),
        name="attn_p",
    )(qi_tab, ki_tab, csum, qn, kb, vb, qcols, kcols, lam_vecs, subln_w.reshape(1, V_DIM))


PAGES_PER_STEP = 16
PAGE_GROUP = 4
ROWS_S = 2 * 4 * N_HEADS


def _attn_s_kernel(pt_ref, q_ref, d0_ref, mask_ref, sl_ref, bn_ref, kn_ref, vn_ref, lam_ref,
                   sw_ref, *rest, n_steps):
    k_refs = [r.at[0] for r in rest[:PAGES_PER_STEP]]
    v_refs = [r.at[0] for r in rest[PAGES_PER_STEP:2 * PAGES_PER_STEP]]
    o_ref = rest[2 * PAGES_PER_STEP]
    m_scr, l_scr, acc_scr = rest[2 * PAGES_PER_STEP + 1:]
    s_id = pl.program_id(1)

    @pl.when(s_id == 0)
    def _():
        m_scr[...] = jnp.full(m_scr.shape, NEG_BIG, F32)
        l_scr[...] = jnp.zeros(l_scr.shape, F32)
        acc_scr[...] = jnp.zeros(acc_scr.shape, F32)

    q = q_ref[0]

    def update(scores, values):
        m_prev = m_scr[...]
        m_new = m_prev
        for sc in scores:
            m_new = jnp.maximum(m_new, jnp.max(sc, axis=1, keepdims=True))
        alpha = jnp.exp2(m_prev - m_new)
        l_new = alpha * l_scr[...]
        acc = alpha * acc_scr[...]
        for sc, vv in zip(scores, values):
            p = jnp.exp2(sc - m_new)
            l_new = l_new + jnp.sum(p, axis=1, keepdims=True)
            acc = acc + _dot(p.astype(BF16), vv)
        m_scr[...] = m_new
        l_scr[...] = l_new
        acc_scr[...] = acc

    for first in range(0, PAGES_PER_STEP, PAGE_GROUP):
        scores, values = [], []
        for i in range(first, first + PAGE_GROUP):
            page_start = ((s_id * PAGES_PER_STEP + i) * PAGE_SIZE).astype(F32)
            bias = sl_ref[...] * (d0_ref[...] - page_start) + mask_ref[...]
            scores.append(_dot_nt(q, k_refs[i][...].astype(BF16)) + bias)
            values.append(v_refs[i][...].astype(BF16))
        update(scores, values)

    @pl.when(s_id == n_steps - 1)
    def _():
        sc = _dot_nt(q, kn_ref[0].astype(BF16)) + bn_ref[...]
        update([sc], [vn_ref[0].astype(BF16)])
        lam = _diff_lambda(lam_ref)
        half = ROWS_S // 2
        o1 = acc_scr[0:half] / l_scr[0:half]
        o2 = acc_scr[half:ROWS_S] / l_scr[half:ROWS_S]
        o_ref[0] = _subln(o1 - lam * o2, sw_ref[...]).astype(BF16)


def _attn_sample(qn_s, kn_s, v_s, cache_k, cache_v, page_table, lam_vecs, subln_w):
    db, n_pages = page_table.shape
    dec_seq = qn_s.shape[0] // db
    past = n_pages * PAGE_SIZE
    n_steps = n_pages // PAGES_PER_STEP
    page_rows = PAGE_SIZE * N_KV_HEADS
    n_phys = cache_k.shape[0]
    ck = cache_k.reshape(n_phys, page_rows, 2 * HEAD_DIM)
    cv = cache_v.reshape(n_phys, page_rows, V_DIM)

    q5 = qn_s.reshape(db, dec_seq, N_HEADS, 2, HEAD_DIM)
    zeros = jnp.zeros_like(q5[:, :, :, 0])
    q_all = jnp.stack([jnp.concatenate([q5[:, :, :, 0], zeros], axis=-1),
                       jnp.concatenate([zeros, q5[:, :, :, 1]], axis=-1)], axis=1)
    q_all = q_all.reshape(db, ROWS_S, LANES)

    r = np.arange(ROWS_S)
    tok_r = (r % (dec_seq * N_HEADS)) // N_HEADS
    head_r = r % N_HEADS
    slope_r = np.asarray(ALIBI_SLOPES)[head_r] * LOG2E
    c = np.arange(page_rows)
    key_c, grp_c = c // N_KV_HEADS, c % N_KV_HEADS
    same = (head_r[:, None] // GQA_REP) == grp_c[None, :]
    d0 = np.broadcast_to(past + tok_r[:, None] - key_c[None, :], (ROWS_S, page_rows))
    mask = np.where(same, 0.0, NEG_BIG)
    sl = np.broadcast_to(-slope_r[:, None], (ROWS_S, 1))
    cn = np.arange(LANES)
    tok_c, grp_n = cn // N_KV_HEADS, cn % N_KV_HEADS
    ok = ((head_r[:, None] // GQA_REP) == grp_n[None, :]) & (tok_c[None, :] <= tok_r[:, None])
    bn = np.where(ok, -slope_r[:, None] * (tok_r[:, None] - tok_c[None, :]), NEG_BIG)

    new_rows = dec_seq * N_KV_HEADS
    kn_pad = jnp.pad(kn_s.reshape(db, new_rows, LANES), ((0, 0), (0, LANES - new_rows), (0, 0)))
    vn_pad = jnp.pad(v_s.reshape(db, new_rows, LANES), ((0, 0), (0, LANES - new_rows), (0, 0)))

    def const(shape):
        return pl.BlockSpec(shape, lambda b, s, pt: (0,) * len(shape))

    def page_spec(i):
        return pl.BlockSpec(
            (1, page_rows, LANES),
            lambda b, s, pt: (pt[b * n_pages + s * PAGES_PER_STEP + i], 0, 0))

    grid_spec = pltpu.PrefetchScalarGridSpec(
        num_scalar_prefetch=1,
        grid=(db, n_steps),
        in_specs=[
            pl.BlockSpec((1, ROWS_S, LANES), lambda b, s, pt: (b, 0, 0)),
            const((ROWS_S, page_rows)),
            const((ROWS_S, page_rows)),
            const((ROWS_S, 1)),
            const((ROWS_S, LANES)),
            pl.BlockSpec((1, LANES, LANES), lambda b, s, pt: (b, 0, 0)),
            pl.BlockSpec((1, LANES, LANES), lambda b, s, pt: (b, 0, 0)),
            const((4, HEAD_DIM)),
            const((1, V_DIM)),
        ] + [page_spec(i) for i in range(PAGES_PER_STEP)] * 2,
        out_specs=pl.BlockSpec((1, ROWS_S // 2, V_DIM), lambda b, s, pt: (b, 0, 0)),
        scratch_shapes=[
            pltpu.VMEM((ROWS_S, 1), F32),
            pltpu.VMEM((ROWS_S, 1), F32),
            pltpu.VMEM((ROWS_S, V_DIM), F32),
        ],
    )
    o = pl.pallas_call(
        functools.partial(_attn_s_kernel, n_steps=n_steps),
        grid_spec=grid_spec,
        out_shape=jax.ShapeDtypeStruct((db, ROWS_S // 2, V_DIM), BF16),
        compiler_params=_cparams(("arbitrary", "arbitrary")),
        name="attn_s",
    )(page_table.reshape(-1), q_all, jnp.asarray(d0, F32), jnp.asarray(mask, F32),
      jnp.asarray(sl, F32), jnp.asarray(bn, F32), kn_pad, vn_pad, lam_vecs,
      subln_w.reshape(1, V_DIM),
      *([ck] * PAGES_PER_STEP), *([cv] * PAGES_PER_STEP))
    return o.reshape(db * dec_seq, ATT_WIDTH)


HALO = SUBLANES


def _ssd_kernel(xs_ref, b_ref, c_ref, z_ref, dt_ref, dtt_ref, halo_ref, init_ref,
                cw_ref, cb_ref, dtb_ref, dtbt_ref, a_ref, at_ref, dsk_ref, nw_ref,
                tri_ref, trit_ref, exp_ref, sel_ref,
                y_ref, fin_ref, win_scr, state_scr, *, rows_in, n_valid):
    ci = pl.program_id(1)
    n_chunks = pl.num_programs(1)
    lc = SSD_CHUNK
    bc_w = N_GROUPS * D_STATE

    @pl.when(ci == 0)
    def _():
        state_scr[...] = init_ref[0].reshape(D_SSM, D_STATE)
        win_scr[0:HALO, :] = halo_ref[0]

    if rows_in < lc:
        win_scr[HALO:HALO + lc, :] = jnp.zeros((lc, CONV_DIM), F32)
    win_scr[HALO:HALO + rows_in, 0:D_SSM] = xs_ref[0]
    win_scr[HALO:HALO + rows_in, D_SSM:D_SSM + bc_w] = b_ref[0]
    win_scr[HALO:HALO + rows_in, D_SSM + bc_w:CONV_DIM] = c_ref[0]

    acc = cb_ref[...]
    for tap in range(CONV_WIDTH):
        off = HALO - (CONV_WIDTH - 1) + tap
        acc = acc + win_scr[off:off + lc, :] * cw_ref[tap:tap + 1, :]
    conv = _silu(acc)
    win_scr[0:HALO, :] = win_scr[lc:lc + HALO, :]
    xs = conv[:, 0:D_SSM]
    bm = conv[:, D_SSM:D_SSM + bc_w].astype(BF16)
    cm = conv[:, D_SSM + bc_w:CONV_DIM].astype(BF16)

    if rows_in < lc:
        dt_in = jnp.concatenate([dt_ref[0], jnp.zeros((lc - rows_in, LANES), F32)], axis=0)
        dtt_in = jnp.concatenate(
            [dtt_ref[0], jnp.zeros((N_SSM_HEADS, lc - rows_in), F32)], axis=1)
    else:
        dt_in, dtt_in = dt_ref[0], dtt_ref[0]
    rowi = lax.broadcasted_iota(jnp.int32, (lc, LANES), 0)
    coli = lax.broadcasted_iota(jnp.int32, (N_SSM_HEADS, lc), 1)
    dt = jnp.where(rowi < n_valid, _softplus(dt_in + dtb_ref[...]), 0.0)
    dtt = jnp.where(coli < n_valid, _softplus(dtt_in + dtbt_ref[...]), 0.0)
    a_cs = _dot_x3_left(tri_ref[...], dt * a_ref[...])
    a_cst = _dot_x3(dtt * at_ref[...], trit_ref[...])
    a_last = a_cs[lc - 1:lc, :]
    exp_cs = jnp.exp(a_cs)
    exp_rest = jnp.exp(a_last - a_cs)
    expand = exp_ref[...]
    dtx = _dot_x2(dt, expand)
    ecx = _dot_x2(exp_cs, expand)
    erx = _dot_x2(exp_rest, expand)
    xc = xs * dtx
    xcb = xc.astype(BF16)
    xcd = (xc * erx).astype(BF16)

    last_t = jnp.exp(a_cst[:, lc - 1:lc])
    rdec = _dot_x2_left(sel_ref[...], jnp.broadcast_to(last_t, (N_SSM_HEADS, D_STATE)))

    tril = (lax.broadcasted_iota(jnp.int32, (lc, lc), 0)
            >= lax.broadcasted_iota(jnp.int32, (lc, lc), 1))
    lane = lax.broadcasted_iota(jnp.int32, (lc, LANES), 1)
    gw = HEADS_PER_GROUP * SSM_HEAD_DIM
    y_parts = []
    for g in range(N_GROUPS):
        bg = bm[:, g * D_STATE:(g + 1) * D_STATE]
        cg = cm[:, g * D_STATE:(g + 1) * D_STATE]
        cb = _dot_nt(cg, bg)
        st = state_scr[g * gw:(g + 1) * gw, :]
        y_off = _dot_nt(cg, st.astype(BF16)) * ecx[:, g * gw:(g + 1) * gw]
        new_st = _dot_tn(xcd[:, g * gw:(g + 1) * gw], bg)
        state_scr[g * gw:(g + 1) * gw, :] = st * rdec[g * gw:(g + 1) * gw, :] + new_st
        for j in range(HEADS_PER_GROUP // 2):
            pair = g * (HEADS_PER_GROUP // 2) + j
            blk = xcb[:, pair * LANES:(pair + 1) * LANES]
            y_pair = None
            for half in range(2):
                h = 2 * pair + half
                seg = a_cs[:, h:h + 1] - a_cst[h:h + 1, :]
                decay = jnp.exp(jnp.where(tril, seg, NEG_BIG))
                mh = (cb * decay).astype(BF16)
                keep = (lane < SSM_HEAD_DIM) if half == 0 else (lane >= SSM_HEAD_DIM)
                part = _dot(mh, jnp.where(keep, blk, jnp.zeros_like(blk)))
                y_pair = part if y_pair is None else y_pair + part
            y_parts.append(y_pair + y_off[:, (pair % (HEADS_PER_GROUP // 2)) * LANES:
                                          (pair % (HEADS_PER_GROUP // 2) + 1) * LANES])
    y = jnp.concatenate(y_parts, axis=1)
    y = y + dsk_ref[...] * xs
    if rows_in < lc:
        z = jnp.concatenate([z_ref[0], jnp.zeros((lc - rows_in, D_SSM), F32)], axis=0)
    else:
        z = z_ref[0]
    y = y * _silu(z)
    gn = D_SSM // N_GROUPS
    outs = []
    for g in range(N_GROUPS):
        yg = y[:, g * gn:(g + 1) * gn]
        ms = jnp.mean(yg * yg, axis=-1, keepdims=True)
        outs.append(yg * lax.rsqrt(ms + EPS) * nw_ref[:, g * gn:(g + 1) * gn])
    out = jnp.concatenate(outs, axis=1).astype(BF16)
    y_ref[0] = out[0:rows_in]

    @pl.when(ci == n_chunks - 1)
    def _():
        fin_ref[0] = state_scr[...].reshape(N_SSM_HEADS, SSM_HEAD_DIM, D_STATE)


def _dot_x3_left(sel, x):
    hi, mid, lo = _split3(x)
    return _dot(sel, hi) + _dot(sel, mid) + _dot(sel, lo)


def _dot_x2_left(sel, x):
    hi, lo = _split2(x)
    return _dot(sel, hi) + _dot(sel, lo)


def _ssd(src, col_blocks, dt_raw, halo, init_state, prm, rows_in, n_valid):
    nb, seq = src.shape[0], src.shape[1]
    n_chunks = max(1, seq // SSD_CHUNK)
    bc_w = N_GROUPS * D_STATE
    dtt = jnp.swapaxes(dt_raw[:, :, :N_SSM_HEADS], 1, 2)
    tri = np.tril(np.ones((SSD_CHUNK, SSD_CHUNK)))
    expand = np.zeros((LANES, D_SSM))
    expand[np.arange(D_SSM) // SSM_HEAD_DIM, np.arange(D_SSM)] = 1.0
    sel = expand[:N_SSM_HEADS].T
    cx, cbk, cck, cz = col_blocks

    def const(shape):
        return pl.BlockSpec(shape, lambda b, c: (0,) * len(shape))

    return pl.pallas_call(
        functools.partial(_ssd_kernel, rows_in=rows_in, n_valid=n_valid),
        grid=(nb, n_chunks),
        in_specs=[
            pl.BlockSpec((1, rows_in, D_SSM), lambda b, c: (b, c, cx)),
            pl.BlockSpec((1, rows_in, bc_w), lambda b, c: (b, c, cbk)),
            pl.BlockSpec((1, rows_in, bc_w), lambda b, c: (b, c, cck)),
            pl.BlockSpec((1, rows_in, D_SSM), lambda b, c: (b, c, cz)),
            pl.BlockSpec((1, rows_in, LANES), lambda b, c: (b, c, 0)),
            pl.BlockSpec((1, N_SSM_HEADS, rows_in), lambda b, c: (b, 0, c)),
            pl.BlockSpec((1, HALO, CONV_DIM), lambda b, c: (b, 0, 0)),
            pl.BlockSpec((1, N_SSM_HEADS, SSM_HEAD_DIM, D_STATE), lambda b, c: (b, 0, 0, 0)),
            const((CONV_WIDTH, CONV_DIM)),
            const((1, CONV_DIM)),
            const((1, LANES)),
            const((N_SSM_HEADS, 1)),
            const((1, LANES)),
            const((N_SSM_HEADS, 1)),
            const((1, D_SSM)),
            const((1, D_SSM)),
            const((SSD_CHUNK, SSD_CHUNK)),
            const((SSD_CHUNK, SSD_CHUNK)),
            const((LANES, D_SSM)),
            const((D_SSM, N_SSM_HEADS)),
        ],
        out_specs=[
            pl.BlockSpec((1, rows_in, D_SSM), lambda b, c: (b, c, 0)),
            pl.BlockSpec((1, N_SSM_HEADS, SSM_HEAD_DIM, D_STATE), lambda b, c: (b, 0, 0, 0)),
        ],
        out_shape=[
            jax.ShapeDtypeStruct((nb, seq, D_SSM), BF16),
            jax.ShapeDtypeStruct((nb, N_SSM_HEADS, SSM_HEAD_DIM, D_STATE), F32),
        ],
        scratch_shapes=[
            pltpu.VMEM((HALO + SSD_CHUNK, CONV_DIM), F32),
            pltpu.VMEM((D_SSM, D_STATE), F32),
        ],
        compiler_params=_cparams(("arbitrary", "arbitrary")),
        name="ssd",
    )(src, src, src, src, dt_raw, dtt, halo, init_state,
      prm["conv_w"], prm["conv_b"], prm["dt_bias"], prm["dt_bias_t"], prm["a"], prm["a_t"],
      prm["d_skip"], prm["ssm_norm_w"],
      jnp.asarray(tri, BF16), jnp.asarray(tri.T, BF16), jnp.asarray(expand, BF16),
      jnp.asarray(sel, BF16))


def _merge_kernel(o_ref, s_ref, wa_ref, ws_ref, ga_ref, gs_ref, out_ref):
    a = _dot(o_ref[...], wa_ref[...])
    s = _dot(s_ref[...], ws_ref[...])
    out_ref[...] = (_sigmoid(ga_ref[...]) * a + _sigmoid(gs_ref[...]) * s).astype(BF16)


def _merge(o, s, wa, ws, proj, tm, tn):
    t = o.shape[0]
    ga0, gs0 = COL_GA // tn, COL_GS // tn
    return pl.pallas_call(
        _merge_kernel,
        grid=(t // tm, D_MODEL // tn),
        in_specs=[
            pl.BlockSpec((tm, ATT_WIDTH), lambda i, j: (i, 0)),
            pl.BlockSpec((tm, D_SSM), lambda i, j: (i, 0)),
            pl.BlockSpec((ATT_WIDTH, tn), lambda i, j: (0, j)),
            pl.BlockSpec((D_SSM, tn), lambda i, j: (0, j)),
            pl.BlockSpec((tm, tn), lambda i, j: (i, ga0 + j)),
            pl.BlockSpec((tm, tn), lambda i, j: (i, gs0 + j)),
        ],
        out_specs=pl.BlockSpec((tm, tn), lambda i, j: (i, j)),
        out_shape=jax.ShapeDtypeStruct((t, D_MODEL), BF16),
        compiler_params=_cparams(("arbitrary", "arbitrary")),
        name="merge",
    )(o, s, wa, ws, proj, proj)


ROUTE_E1, ROUTE_E2, ROUTE_W1, ROUTE_W2 = 0, 1, 2, 3


def _resid_kernel(x_ref, m_ref, wo_ref, nw_ref, wrh_ref, wrl_ref, br_ref, *rest, n_real):
    h_ref, u_ref, route_ref = rest[-3:]

    @pl.when(pl.program_id(0) >= n_real)
    def _():
        h_ref[...] = jnp.zeros(h_ref.shape, F32)
        u_ref[...] = jnp.zeros(u_ref.shape, F32)
        route_ref[...] = jnp.zeros(route_ref.shape, F32)

    @pl.when(pl.program_id(0) < n_real)
    def _():
        _resid_tile(x_ref, m_ref, wo_ref, nw_ref, wrh_ref, wrl_ref, br_ref,
                    h_ref, u_ref, route_ref)


def _resid_tile(x_ref, m_ref, wo_ref, nw_ref, wrh_ref, wrl_ref, br_ref, h_ref, u_ref, route_ref):
    h = x_ref[...] + _dot(m_ref[...], wo_ref[...])
    h_ref[...] = h
    ms = jnp.mean(h * h, axis=-1, keepdims=True)
    u = h * lax.rsqrt(ms + EPS) * nw_ref[...]
    u_hi, u_lo = _split2(u)
    u_ref[...] = u
    logits = (_dot(u_hi, wrh_ref[...]) + _dot(u_lo, wrh_ref[...])
              + _dot(u_hi, wrl_ref[...]) + br_ref[...])
    lane = lax.broadcasted_iota(jnp.int32, logits.shape, 1)
    lane_f = lane.astype(F32)
    far = float(2 * LANES)

    def first_max(vals):
        top = jnp.max(vals, axis=1, keepdims=True)
        idx = jnp.min(jnp.where(vals == top, lane_f, far), axis=1, keepdims=True)
        return top, idx

    is_group = (lane >= N_EXPERTS) & (lane < N_EXPERTS + N_EXPERT_GROUPS)
    gl = jnp.where(is_group, logits, NEG_BIG)
    g_top, g_idx = first_max(gl)
    g_p = 1.0 / jnp.sum(jnp.exp(gl - g_top), axis=1, keepdims=True)
    lo_lane = (g_idx - N_EXPERTS) * EXPERTS_PER_GROUP
    in_group = (lane_f >= lo_lane) & (lane_f < lo_lane + EXPERTS_PER_GROUP)
    el = jnp.where(in_group, logits, NEG_BIG)
    m1, i1 = first_max(el)
    el2 = jnp.where(lane_f == i1, NEG_BIG, el)
    m2, i2 = first_max(el2)
    e = jnp.exp(m2 - m1)
    w1 = 1.0 / (1.0 + e)
    w2 = e / (1.0 + e)
    route = jnp.where(lane == ROUTE_E1, i1, 0.0)
    route = jnp.where(lane == ROUTE_E2, i2, route)
    route = jnp.where(lane == ROUTE_W1, g_p * w1, route)
    route_ref[...] = jnp.where(lane == ROUTE_W2, g_p * w2, route)


def _resid(x, merged, wo, norm_w, wr_hi, wr_lo, br, tm, t_all, row_off, bufs):
    t = x.shape[0]
    blk_off = row_off // tm
    n_real = t // tm
    n_fill = pl.cdiv(t_all - t, tm) if bufs is None else 0

    def const(shape):
        return pl.BlockSpec(shape, lambda i: (0,) * len(shape))

    in_specs = [
        pl.BlockSpec((tm, D_MODEL), lambda i: (jnp.minimum(i, n_real - 1), 0)),
        pl.BlockSpec((tm, D_MODEL), lambda i: (jnp.minimum(i, n_real - 1), 0)),
        const((D_MODEL, D_MODEL)),
        const((1, D_MODEL)),
        const((D_MODEL, LANES)),
        const((D_MODEL, LANES)),
        const((1, LANES)),
    ]
    args = [x, merged, wo, norm_w, wr_hi, wr_lo, br]
    aliases = {}
    if bufs is not None:
        aliases = {len(args) + k: k for k in range(len(bufs))}
        in_specs += [pl.BlockSpec(memory_space=pl.ANY)] * len(bufs)
        args += list(bufs)
    return pl.pallas_call(
        functools.partial(_resid_kernel, n_real=n_real),
        grid=(n_real + n_fill,),
        in_specs=in_specs,
        out_specs=[
            pl.BlockSpec((tm, D_MODEL), lambda i: (i + blk_off, 0)),
            pl.BlockSpec((tm, D_MODEL), lambda i: (i + blk_off, 0)),
            pl.BlockSpec((tm, LANES), lambda i: (i + blk_off, 0)),
        ],
        out_shape=[
            jax.ShapeDtypeStruct((t_all, D_MODEL), F32),
            jax.ShapeDtypeStruct((t_all, D_MODEL), F32),
            jax.ShapeDtypeStruct((t_all, LANES), F32),
        ],
        input_output_aliases=aliases,
        compiler_params=_cparams(("arbitrary",)),
        name="resid",
    )(*args)


MOE_TILE = 256
COMB_TILE = 128
DMA_UNROLL = 8


def _route_plan(route, n_tiles):
    n_pairs = 2 * route.shape[0]
    pair_e = route[:, ROUTE_E1:ROUTE_E2 + 1].astype(jnp.int32).reshape(-1)
    onehot = (pair_e[:, None] == jnp.arange(N_EXPERTS, dtype=jnp.int32)[None, :]).astype(jnp.int32)
    csum = jnp.cumsum(onehot, axis=0)
    rank = jnp.sum((csum - onehot) * onehot, axis=1)
    tiles_e = (csum[-1] + MOE_TILE - 1) // MOE_TILE
    tile_end = jnp.cumsum(tiles_e)
    first_row = (tile_end - tiles_e) * MOE_TILE
    slot = jnp.sum(onehot * first_row[None, :], axis=1) + rank
    tok_of_slot = jnp.zeros(((n_tiles + 1) * MOE_TILE,), jnp.int32).at[slot].set(
        jnp.arange(n_pairs, dtype=jnp.int32) // 2)
    tile_start = (tile_end - tiles_e).astype(jnp.int32)
    slot_tab = slot.reshape(-1, COMB_TILE, 2).transpose(0, 2, 1).reshape(-1).astype(jnp.int32)
    return tile_start, tiles_e.astype(jnp.int32), tok_of_slot, slot_tab


def _gmm_kernel(ts_ref, ne_ref, tok_ref, u_hbm, wg_ref, wu_ref, wd_ref, o_hbm,
                xbuf, obuf, gsem, osem, wgb, wub, wdb, *, n_tiles):
    e = pl.program_id(0)
    last = pl.num_programs(0) - 1
    n_used = ts_ref[last] + ne_ref[last]

    def row_copy(tile, r, buf):
        tok = tok_ref[tile * MOE_TILE + r]
        return pltpu.make_async_copy(
            u_hbm.at[pl.ds(tok, 1), :], xbuf.at[buf, pl.ds(r, 1), :], gsem.at[buf])

    def out_copy(tile, buf):
        return pltpu.make_async_copy(
            obuf.at[buf], o_hbm.at[pl.ds(tile * MOE_TILE, MOE_TILE), :], osem.at[buf])

    def gather_wait(tile, buf):
        def body(r, carry):
            row_copy(tile, r, buf).wait()
            return carry
        lax.fori_loop(0, MOE_TILE, body, 0, unroll=DMA_UNROLL)

    @pl.when(e == 0)
    def _():
        def body(r, carry):
            row_copy(0, r, 0).start()
            return carry
        lax.fori_loop(0, MOE_TILE, body, 0, unroll=DMA_UNROLL)

    wgb[...] = wg_ref[0].astype(BF16)
    wub[...] = wu_ref[0].astype(BF16)
    wdb[...] = wd_ref[0].astype(BF16)

    def tile_body(j, carry):
        t = ts_ref[e] + j
        cur = lax.rem(t, 2)

        @pl.when(t >= 2)
        def _():
            out_copy(t - 2, cur).wait()

        gather_wait(t, cur)
        x = xbuf[cur].astype(BF16)
        for r in range(MOE_TILE):
            row_copy(t + 1, r, 1 - cur).start()
        hid = _silu(_dot(x, wgb[...])) * _dot(x, wub[...])
        obuf[cur] = _dot(hid.astype(BF16), wdb[...])
        out_copy(t, cur).start()
        return carry

    lax.fori_loop(0, ne_ref[e], tile_body, 0)

    @pl.when(e == last)
    def _():
        gather_wait(n_used, lax.rem(n_used, 2))
        out_copy(n_used - 2, lax.rem(n_used, 2)).wait()
        out_copy(n_used - 1, lax.rem(n_used - 1, 2)).wait()
        obuf[0] = jnp.zeros(obuf.shape[1:], F32)

        def fill(t, carry):
            cp = out_copy(t, 0)
            cp.start()
            cp.wait()
            return carry
        lax.fori_loop(n_used, n_tiles + 1, fill, 0)


def _gmm(u_all, plan, wg, wu, wd, n_tiles):
    tile_start, tiles_e, tok_of_slot, _ = plan
    assert 2 * u_all.shape[0] >= 2 * MOE_TILE
    grid_spec = pltpu.PrefetchScalarGridSpec(
        num_scalar_prefetch=3,
        grid=(N_EXPERTS,),
        in_specs=[
            pl.BlockSpec(memory_space=pl.ANY),
            pl.BlockSpec((1, D_MODEL, D_EXPERT), lambda e, ts, ne, tok: (e, 0, 0)),
            pl.BlockSpec((1, D_MODEL, D_EXPERT), lambda e, ts, ne, tok: (e, 0, 0)),
            pl.BlockSpec((1, D_EXPERT, D_MODEL), lambda e, ts, ne, tok: (e, 0, 0)),
        ],
        out_specs=pl.BlockSpec(memory_space=pl.ANY),
        scratch_shapes=[
            pltpu.VMEM((2, MOE_TILE, D_MODEL), F32),
            pltpu.VMEM((2, MOE_TILE, D_MODEL), F32),
            pltpu.SemaphoreType.DMA((2,)),
            pltpu.SemaphoreType.DMA((2,)),
            pltpu.VMEM((D_MODEL, D_EXPERT), BF16),
            pltpu.VMEM((D_MODEL, D_EXPERT), BF16),
            pltpu.VMEM((D_EXPERT, D_MODEL), BF16),
        ],
    )
    return pl.pallas_call(
        functools.partial(_gmm_kernel, n_tiles=n_tiles),
        grid_spec=grid_spec,
        out_shape=jax.ShapeDtypeStruct(((n_tiles + 1) * MOE_TILE, D_MODEL), F32),
        compiler_params=_cparams(("arbitrary",)),
        name="gmm",
    )(tile_start, tiles_e, tok_of_slot, u_all, wg, wu, wd)


def _combine_kernel(slot_ref, route_ref, h_ref, o_hbm, yp_ref, ys_ref, gbuf, sem, *, n_prompt):
    i = pl.program_id(0)
    n = pl.num_programs(0)
    cur = lax.rem(i, 2)
    rows = 2 * COMB_TILE

    def row_copy(tile, j, buf):
        slot = slot_ref[tile * rows + j]
        return pltpu.make_async_copy(
            o_hbm.at[pl.ds(slot, 1), :], gbuf.at[buf, pl.ds(j, 1), :], sem.at[buf])

    def issue(tile, buf):
        def body(j, carry):
            row_copy(tile, j, buf).start()
            return carry
        lax.fori_loop(0, rows, body, 0, unroll=DMA_UNROLL)

    def wait(tile, buf):
        def body(j, carry):
            row_copy(tile, j, buf).wait()
            return carry
        lax.fori_loop(0, rows, body, 0, unroll=DMA_UNROLL)

    @pl.when(i == 0)
    def _():
        issue(0, 0)

    @pl.when(i + 1 < n)
    def _():
        issue(i + 1, 1 - cur)

    wait(i, cur)
    w1 = route_ref[:, ROUTE_W1:ROUTE_W1 + 1]
    w2 = route_ref[:, ROUTE_W2:ROUTE_W2 + 1]
    y = h_ref[...] + w1 * gbuf[cur, 0:COMB_TILE, :] + w2 * gbuf[cur, COMB_TILE:rows, :]

    @pl.when(i < n_prompt)
    def _():
        yp_ref[...] = y

    @pl.when(i >= n_prompt)
    def _():
        ys_ref[...] = y


def _combine(route, h_all, o_sorted, plan, t_prompt):
    t_all = h_all.shape[0]
    n_prompt = t_prompt // COMB_TILE
    slot_tab = plan[3]
    grid_spec = pltpu.PrefetchScalarGridSpec(
        num_scalar_prefetch=1,
        grid=(t_all // COMB_TILE,),
        in_specs=[
            pl.BlockSpec((COMB_TILE, LANES), lambda i, st: (i, 0)),
            pl.BlockSpec((COMB_TILE, D_MODEL), lambda i, st: (i, 0)),
            pl.BlockSpec(memory_space=pl.ANY),
        ],
        out_specs=[
            pl.BlockSpec((COMB_TILE, D_MODEL), lambda i, st: (jnp.minimum(i, n_prompt - 1), 0)),
            pl.BlockSpec((COMB_TILE, D_MODEL), lambda i, st: (0, 0)),
        ],
        scratch_shapes=[
            pltpu.VMEM((2, 2 * COMB_TILE, D_MODEL), F32),
            pltpu.SemaphoreType.DMA((2,)),
        ],
    )
    return pl.pallas_call(
        functools.partial(_combine_kernel, n_prompt=n_prompt),
        grid_spec=grid_spec,
        out_shape=[
            jax.ShapeDtypeStruct((t_prompt, D_MODEL), F32),
            jax.ShapeDtypeStruct((t_all - t_prompt, D_MODEL), F32),
        ],
        compiler_params=_cparams(("arbitrary",)),
        name="combine",
    )(slot_tab, route, h_all, o_sorted)


def _layer_tokens(x2d, w, tm_proj, tm_small):
    proj, dt_raw = _proj(x2d, w["norm_attn_w"], w["w_a"], w["w_b"], w["w_dt"], tm_proj, 1024)
    qn, kn, kb, vb = _qk_norm(proj, w["q_norm_w"], w["k_norm_w"], tm_small)
    return proj, dt_raw, qn, kn, kb, vb


def _branch_merge(x2d, o, s, proj, w, tm, tm_resid, t_all, row_off, bufs):
    merged = _merge(o, s, w["w_att_out"], w["w_ssm_out"], proj, tm, 512)
    return _resid(x2d, merged, w["w_o"], w["norm_ffn_w"], w["wr_hi"], w["wr_lo"], w["br"],
                  tm_resid, t_all, row_off, bufs)


def kernel(x_prompt, x_sample, cache_k, cache_v, state_ssm, state_conv, page_table, norm_attn_w, w_in, q_norm_w, k_norm_w, lambda_q1, lambda_k1, lambda_q2, lambda_k2, subln_w, w_att_out, conv_w, conv_b, dt_bias, a_log, d_skip, ssm_norm_w, w_ssm_out, w_o, norm_ffn_w, w_group_router, b_group_router, w_expert_router, b_expert_router, w_gate, w_up, w_down):
    layer = 0
    nb, seq, _ = x_prompt.shape
    db, dec_seq, _ = x_sample.shape

    w_in_l = w_in[layer]
    c_dt = Q_WIDTH + K_WIDTH + V_WIDTH + D_SSM + CONV_DIM
    w_a = w_in_l[:, :c_dt].astype(BF16)
    w_b = w_in_l[:, c_dt + N_SSM_HEADS:].astype(BF16)
    w_dt = jnp.pad(w_in_l[:, c_dt:c_dt + N_SSM_HEADS], ((0, 0), (0, LANES - N_SSM_HEADS))).astype(BF16)
    wr = jnp.concatenate([w_expert_router[layer], w_group_router[layer]], axis=1)
    wr = jnp.pad(wr, ((0, 0), (0, LANES - wr.shape[1])))
    wr_hi = wr.astype(BF16)
    wr_lo = (wr - wr_hi.astype(F32)).astype(BF16)
    br = jnp.concatenate([b_expert_router[layer], b_group_router[layer]])
    br = jnp.pad(br, (0, LANES - br.shape[0])).reshape(1, LANES)
    pad_h = (0, LANES - N_SSM_HEADS)
    w = dict(
        norm_attn_w=norm_attn_w[layer].reshape(1, D_MODEL), w_a=w_a, w_b=w_b, w_dt=w_dt,
        q_norm_w=q_norm_w[layer], k_norm_w=k_norm_w[layer],
        w_att_out=w_att_out[layer].astype(BF16), w_ssm_out=w_ssm_out[layer].astype(BF16),
        w_o=w_o[layer].astype(BF16), norm_ffn_w=norm_ffn_w[layer].reshape(1, D_MODEL),
        wr_hi=wr_hi, wr_lo=wr_lo, br=br,
        w_gate=w_gate[layer], w_up=w_up[layer], w_down=w_down[layer],
    )
    ssm_prm = dict(
        conv_w=conv_w[layer], conv_b=conv_b[layer].reshape(1, CONV_DIM),
        dt_bias=jnp.pad(dt_bias[layer], pad_h).reshape(1, LANES),
        dt_bias_t=dt_bias[layer].reshape(N_SSM_HEADS, 1),
        a=jnp.pad(-jnp.exp(a_log[layer]), pad_h).reshape(1, LANES),
        a_t=(-jnp.exp(a_log[layer])).reshape(N_SSM_HEADS, 1),
        d_skip=jnp.repeat(d_skip[layer], SSM_HEAD_DIM).reshape(1, D_SSM),
        ssm_norm_w=ssm_norm_w[layer].reshape(1, D_SSM),
    )
    lam_vecs = jnp.stack([lambda_q1[layer], lambda_k1[layer], lambda_q2[layer], lambda_k2[layer]])
    sw = subln_w[layer]
    ssd_cols = (COL_X // D_SSM, COL_B // (N_GROUPS * D_STATE), COL_C // (N_GROUPS * D_STATE),
                COL_Z // D_SSM)

    xp = x_prompt.reshape(nb * seq, D_MODEL)
    n_tok = db * dec_seq
    t_prompt = nb * seq
    t_all = t_prompt + n_tok
    proj_p, dt_p, qn_p, kn_p, kb_p, vb_p = _layer_tokens(xp, w, 1024, 512)
    o_p = _attn_prompt(qn_p, kb_p, vb_p, lam_vecs, sw, 512)
    s_p, ssm_p = _ssd(
        proj_p.reshape(nb, seq, PROJ_WIDTH), ssd_cols, dt_p.reshape(nb, seq, LANES),
        jnp.zeros((nb, HALO, CONV_DIM), F32), jnp.zeros((nb, N_SSM_HEADS, SSM_HEAD_DIM, D_STATE), F32),
        ssm_prm, SSD_CHUNK, SSD_CHUNK)
    bufs = _branch_merge(xp, o_p, s_p.reshape(t_prompt, D_SSM), proj_p, w, 1024, 256,
                         t_all, 0, None)
    keep = CONV_WIDTH - 1
    conv_p = proj_p.reshape(nb, seq, PROJ_WIDTH)[:, seq - keep:, COL_X:COL_X + CONV_DIM]

    xs = x_sample.reshape(db * dec_seq, D_MODEL)
    proj_s, dt_s, qn_s, kn_s, _, _ = _layer_tokens(xs, w, n_tok, n_tok)
    v_s = proj_s[:, COL_V:COL_V + V_WIDTH]
    o_s = _attn_sample(qn_s, kn_s, v_s, cache_k[layer], cache_v[layer], page_table, lam_vecs, sw)
    rows_s = SUBLANES
    pad_rows = ((0, 0), (0, rows_s - dec_seq), (0, 0))
    src_s = jnp.pad(proj_s[:, COL_Z:COL_GA].reshape(db, dec_seq, COL_GA - COL_Z), pad_rows)
    halo_s = jnp.pad(state_conv[layer], ((0, 0), (HALO - (CONV_WIDTH - 1), 0), (0, 0)))
    cols_s = ((COL_X - COL_Z) // D_SSM, (COL_B - COL_Z) // (N_GROUPS * D_STATE),
              (COL_C - COL_Z) // (N_GROUPS * D_STATE), 0)
    s_s, ssm_s = _ssd(
        src_s, cols_s, jnp.pad(dt_s.reshape(db, dec_seq, LANES), pad_rows), halo_s,
        state_ssm[layer], ssm_prm, rows_s, dec_seq)
    s_s = s_s[:, :dec_seq].reshape(n_tok, D_SSM)
    h_all, u_all, route = _branch_merge(xs, o_s, s_s, proj_s, w, n_tok, n_tok,
                                        t_all, t_prompt, bufs)

    n_tiles = 2 * t_all // MOE_TILE + N_EXPERTS
    plan = _route_plan(route, n_tiles)
    o_sorted = _gmm(u_all, plan, w["w_gate"], w["w_up"], w["w_down"], n_tiles)
    y_p, y_s = _combine(route, h_all, o_sorted, plan, t_prompt)
    conv_s =proj_s.reshape(db, dec_seq, PROJ_WIDTH)[:, dec_seq - keep:, COL_X:COL_X + CONV_DIM]

    return (
        y_p.reshape(nb, seq, D_MODEL),
        y_s.reshape(db, dec_seq, D_MODEL),
        kn_p.reshape(1, nb, seq, N_KV_HEADS, 2 * HEAD_DIM),
        proj_p[:, COL_V:COL_V + V_WIDTH].reshape(1, nb, seq, N_KV_HEADS, V_DIM),
        ssm_p.reshape(1, nb, N_SSM_HEADS, SSM_HEAD_DIM, D_STATE),
        conv_p[None],
        kn_s.reshape(1, db, dec_seq, N_KV_HEADS, 2 * HEAD_DIM),
        v_s.reshape(1, db, dec_seq, N_KV_HEADS, V_DIM),
        ssm_s.reshape(1, db, N_SSM_HEADS, SSM_HEAD_DIM, D_STATE),
        conv_s[None],
    )
```

```python
import functools
import math

import jax
import jax.numpy as jnp
import ml_dtypes
import numpy as np
from jax import lax
from jax.experimental import pallas as pl
from jax.experimental.pallas import tpu as pltpu

F32 = jnp.float32
BF16 = jnp.bfloat16

D_MODEL = 2048
N_HEADS = 8
N_KV_HEADS = 4
GQA_REP = N_HEADS // N_KV_HEADS
HEAD_DIM = 64
V_DIM = 2 * HEAD_DIM
Q_WIDTH = N_HEADS * 2 * HEAD_DIM
K_WIDTH = N_KV_HEADS * 2 * HEAD_DIM
V_WIDTH = N_KV_HEADS * V_DIM
ATT_WIDTH = N_HEADS * V_DIM
D_SSM = D_MODEL
SSM_HEAD_DIM = 64
N_SSM_HEADS = D_SSM // SSM_HEAD_DIM
N_GROUPS = 4
HEADS_PER_GROUP = N_SSM_HEADS // N_GROUPS
D_STATE = 128
CONV_WIDTH = 4
CONV_DIM = D_SSM + 2 * N_GROUPS * D_STATE
SSD_CHUNK = 128
N_EXPERT_GROUPS = 4
EXPERTS_PER_GROUP = 8
N_EXPERTS = N_EXPERT_GROUPS * EXPERTS_PER_GROUP
D_EXPERT = D_MODEL // 4
PAGE_SIZE = 128
EPS = 1e-6
LAM_INIT = 0.8 - 0.6 * math.exp(-0.3 * 0)

LANES = 128
SUBLANES = 8
NEG_BIG = -1e30
VMEM_LIMIT = 56 * 1024 * 1024

COL_Q = 0
COL_K = COL_Q + Q_WIDTH
COL_V = COL_K + K_WIDTH
COL_Z = COL_V + V_WIDTH
COL_X = COL_Z + D_SSM
COL_B = COL_X + D_SSM
COL_C = COL_B + N_GROUPS * D_STATE
COL_GA = COL_C + N_GROUPS * D_STATE
COL_GS = COL_GA + D_MODEL
PROJ_WIDTH = COL_GS + D_MODEL

ALIBI_SLOPES = [2.0 ** (-8.0 * (h + 1) / N_HEADS) for h in range(N_HEADS)]


def _cparams(sem):
    return pltpu.CompilerParams(dimension_semantics=sem, vmem_limit_bytes=VMEM_LIMIT)


def _dot(a, b):
    return jnp.dot(a, b, preferred_element_type=F32)


def _dot_nt(a, b):
    return lax.dot_general(a, b, (((1,), (1,)), ((), ())), preferred_element_type=F32)


def _dot_tn(a, b):
    return lax.dot_general(a, b, (((0,), (0,)), ((), ())), preferred_element_type=F32)


def _split2(x):
    hi = x.astype(BF16)
    lo = (x - hi.astype(F32)).astype(BF16)
    return hi, lo


def _split3(x):
    hi = x.astype(BF16)
    r = x - hi.astype(F32)
    mid = r.astype(BF16)
    lo = (r - mid.astype(F32)).astype(BF16)
    return hi, mid, lo


def _dot_x2(x, sel):
    hi, lo = _split2(x)
    return _dot(hi, sel) + _dot(lo, sel)


def _dot_x3(x, sel):
    hi, mid, lo = _split3(x)
    return _dot(hi, sel) + _dot(mid, sel) + _dot(lo, sel)


def _sigmoid(x):
    return 1.0 / (1.0 + jnp.exp(-x))


def _silu(x):
    return x * _sigmoid(x)


def _softplus(x):
    return jnp.maximum(x, 0.0) + jnp.log1p(jnp.exp(-jnp.abs(x)))


NORM_ROWS = 256


def _proj_kernel(x_ref, nw_ref, wa_ref, wb_ref, wdt_ref, o_ref, dt_ref, u_scr, *, n_a):
    j = pl.program_id(1)

    @pl.when(j == 0)
    def _():
        tm = x_ref.shape[0]
        for lo in range(0, tm, min(tm, NORM_ROWS)):
            hi = lo + min(tm, NORM_ROWS)
            x = x_ref[lo:hi, :]
            ms = jnp.mean(x * x, axis=-1, keepdims=True)
            u_scr[lo:hi, :] = (x * lax.rsqrt(ms + EPS) * nw_ref[...]).astype(BF16)
        dt_ref[...] = _dot(u_scr[...], wdt_ref[...])

    @pl.when(j < n_a)
    def _():
        o_ref[...] = _dot(u_scr[...], wa_ref[...])

    @pl.when(j >= n_a)
    def _():
        o_ref[...] = _dot(u_scr[...], wb_ref[...])


def _proj(x, norm_w, w_a, w_b, w_dt, tm, tn):
    t = x.shape[0]
    n_a = w_a.shape[1] // tn
    return pl.pallas_call(
        functools.partial(_proj_kernel, n_a=n_a),
        grid=(t // tm, PROJ_WIDTH // tn),
        in_specs=[
            pl.BlockSpec((tm, D_MODEL), lambda i, j: (i, 0)),
            pl.BlockSpec((1, D_MODEL), lambda i, j: (0, 0)),
            pl.BlockSpec((D_MODEL, tn), lambda i, j: (0, jnp.minimum(j, n_a - 1))),
            pl.BlockSpec((D_MODEL, tn), lambda i, j: (0, jnp.maximum(j - n_a, 0))),
            pl.BlockSpec((D_MODEL, LANES), lambda i, j: (0, 0)),
        ],
        out_specs=[
            pl.BlockSpec((tm, tn), lambda i, j: (i, j)),
            pl.BlockSpec((tm, LANES), lambda i, j: (i, 0)),
        ],
        out_shape=[
            jax.ShapeDtypeStruct((t, PROJ_WIDTH), F32),
            jax.ShapeDtypeStruct((t, LANES), F32),
        ],
        scratch_shapes=[pltpu.VMEM((tm, D_MODEL), BF16)],
        compiler_params=_cparams(("arbitrary", "arbitrary")),
        name="proj",
    )(x, norm_w, w_a, w_b, w_dt)


LOG2E = math.log2(math.e)
Q_SCALE = LOG2E * HEAD_DIM ** -0.5


def _qknorm_kernel(p_ref, qw_ref, kw_ref, g_ref, qn_ref, kn_ref, kb_ref, vb_ref):
    gsum = g_ref[...]
    n_q = Q_WIDTH // LANES
    for c in range((Q_WIDTH + K_WIDTH) // LANES):
        x = p_ref[:, c * LANES:(c + 1) * LANES]
        ss = _dot_x2(x * x, gsum)
        y = x * lax.rsqrt(ss * (1.0 / HEAD_DIM) + EPS)
        if c < n_q:
            qn_ref[:, c * LANES:(c + 1) * LANES] = (y * qw_ref[...] * Q_SCALE).astype(BF16)
        else:
            kn = y * kw_ref[...]
            kn_ref[:, (c - n_q) * LANES:(c - n_q + 1) * LANES] = kn
            kb_ref[:, (c - n_q) * LANES:(c - n_q + 1) * LANES] = kn.astype(BF16)
    vb_ref[...] = p_ref[:, COL_V:COL_V + V_WIDTH].astype(BF16)


def _qk_norm(proj, q_norm_w, k_norm_w, tm):
    t = proj.shape[0]
    group = np.kron(np.eye(LANES // HEAD_DIM), np.ones((HEAD_DIM, HEAD_DIM)))
    qw = jnp.tile(q_norm_w, LANES // HEAD_DIM).reshape(1, LANES)
    kw = jnp.tile(k_norm_w, LANES // HEAD_DIM).reshape(1, LANES)
    return pl.pallas_call(
        _qknorm_kernel,
        grid=(t // tm,),
        in_specs=[
            pl.BlockSpec((tm, Q_WIDTH + K_WIDTH + V_WIDTH), lambda i: (i, 0)),
            pl.BlockSpec((1, LANES), lambda i: (0, 0)),
            pl.BlockSpec((1, LANES), lambda i: (0, 0)),
            pl.BlockSpec((LANES, LANES), lambda i: (0, 0)),
        ],
        out_specs=[
            pl.BlockSpec((tm, Q_WIDTH), lambda i: (i, 0)),
            pl.BlockSpec((tm, K_WIDTH), lambda i: (i, 0)),
            pl.BlockSpec((tm, K_WIDTH), lambda i: (i, 0)),
            pl.BlockSpec((tm, V_WIDTH), lambda i: (i, 0)),
        ],
        out_shape=[
            jax.ShapeDtypeStruct((t, Q_WIDTH), BF16),
            jax.ShapeDtypeStruct((t, K_WIDTH), F32),
            jax.ShapeDtypeStruct((t, K_WIDTH), BF16),
            jax.ShapeDtypeStruct((t, V_WIDTH), BF16),
        ],
        compiler_params=_cparams(("arbitrary",)),
        name="qk_norm",
    )(proj, qw, kw, jnp.asarray(group, BF16))


def _diff_lambda(lam_ref):
    lamv = lam_ref[...]
    s1 = jnp.sum(lamv[0:1] * lamv[1:2], axis=1, keepdims=True)
    s2 = jnp.sum(lamv[2:3] * lamv[3:4], axis=1, keepdims=True)
    return jnp.exp(s1) - jnp.exp(s2) + LAM_INIT


def _subln(o, w):
    ms = jnp.mean(o * o, axis=-1, keepdims=True)
    return o * lax.rsqrt(ms + EPS) * w * (1.0 - LAM_INIT)


N_SLOPE_PARTS = 3


def _bf16_parts(x, n):
    parts, rem = [], np.float32(x)
    for _ in range(n):
        p = np.float32(rem.astype(ml_dtypes.bfloat16))
        parts.append(float(p))
        rem = np.float32(rem - p)
    return parts


def _alibi_tables(tk):
    qcols = np.zeros((N_HEADS, 16, LANES), np.float32)
    csum = np.zeros((N_HEADS,), np.float32)
    for h, slope in enumerate(ALIBI_SLOPES):
        parts = _bf16_parts(slope * LOG2E, N_SLOPE_PARTS)
        csum[h] = np.float32(sum(np.float32(p) for p in parts))
        for i, p in enumerate(parts):
            qcols[h, :, i] = p * LANES
            qcols[h, :, N_SLOPE_PARTS + i] = p
    pos = np.arange(tk)
    kcols = np.zeros((tk, LANES), np.float32)
    kcols[:, 0:N_SLOPE_PARTS] = (pos // LANES)[:, None]
    kcols[:, N_SLOPE_PARTS:2 * N_SLOPE_PARTS] = (pos % LANES)[:, None]
    return jnp.asarray(qcols, BF16), jnp.asarray(kcols, BF16), jnp.asarray(csum, F32)


def _attn_p_kernel(qi_ref, ki_ref, cf_ref, q_ref, k_ref, v_ref, qc_ref, kc_ref, lam_ref, sw_ref,
                   o_ref, qa_scr, m_scr, l_scr, acc_scr, *, tq):
    g = pl.program_id(0)
    t = pl.program_id(1)
    qi = qi_ref[t]
    ki = ki_ref[t]
    n_sub = GQA_REP * 2
    rows = n_sub * tq
    n_chunk = tq // LANES

    @pl.when(ki == 0)
    def _():
        lane = lax.broadcasted_iota(jnp.int32, (tq, LANES), 1)
        for r in range(GQA_REP):
            qq = q_ref[:, r * LANES:(r + 1) * LANES]
            qc = jnp.broadcast_to(qc_ref[r, 0:1, :], (tq, LANES))
            for c in range(2):
                idx = 2 * r + c
                keep = (lane < HEAD_DIM) if c == 0 else (lane >= HEAD_DIM)
                qa_scr[idx * tq:(idx + 1) * tq, 0:LANES] = jnp.where(keep, qq, jnp.zeros_like(qq))
                qa_scr[idx * tq:(idx + 1) * tq, LANES:2 * LANES] = qc
        m_scr[...] = jnp.full(m_scr.shape, NEG_BIG, F32)
        l_scr[...] = jnp.zeros(l_scr.shape, F32)
        acc_scr[...] = jnp.zeros(acc_scr.shape, F32)

    def step(diag):
        k_aug = jnp.concatenate([k_ref[...], kc_ref[...]], axis=1)
        s_all = _dot_nt(qa_scr[...], k_aug)
        if diag:
            row_in = lax.broadcasted_iota(jnp.int32, (rows, tq), 0) & (tq - 1)
            col = lax.broadcasted_iota(jnp.int32, (rows, tq), 1)
            s_all = jnp.where(col <= row_in, s_all, NEG_BIG)
        block_dist = ((qi - ki) * tq).astype(F32)
        ps, alphas = [], []
        for r in range(GQA_REP):
            off = -cf_ref[g * GQA_REP + r] * block_dist
            lo, hi = r * 2 * tq, (r + 1) * 2 * tq
            chunks = [s_all[lo:hi, j * LANES:(j + 1) * LANES] for j in range(n_chunk)]
            m_prev = m_scr[lo:hi]
            m_blk = jnp.max(functools.reduce(jnp.maximum, chunks), axis=1, keepdims=True) + off
            m_new = jnp.maximum(m_prev, m_blk)
            alpha = jnp.exp2(m_prev - m_new)
            m_sub = m_new - off
            pj = [jnp.exp2(ch - m_sub) for ch in chunks]
            l_scr[lo:hi] = alpha * l_scr[lo:hi] + functools.reduce(jnp.add, pj)
            m_scr[lo:hi] = m_new
            ps.append(jnp.concatenate(pj, axis=1).astype(BF16))
            alphas.append(alpha)
        pv = _dot(jnp.concatenate(ps, axis=0), v_ref[...])
        acc_scr[...] = jnp.concatenate(alphas, axis=0) * acc_scr[...] + pv

    @pl.when(ki < qi)
    def _():
        step(False)

    @pl.when(ki == qi)
    def _():
        step(True)
        lam = _diff_lambda(lam_ref)
        for r in range(GQA_REP):
            i1, i2 = 2 * r * tq, (2 * r + 1) * tq
            l1 = jnp.sum(l_scr[i1:i1 + tq], axis=1, keepdims=True)
            l2 = jnp.sum(l_scr[i2:i2 + tq], axis=1, keepdims=True)
            o = acc_scr[i1:i1 + tq] / l1 - lam * (acc_scr[i2:i2 + tq] / l2)
            o_ref[:, r * LANES:(r + 1) * LANES] = _subln(o, sw_ref[...]).astype(BF16)


def _attn_prompt(qn, kb, vb, lam_vecs, subln_w, tq):
    t = qn.shape[0]
    nq = t // tq
    pairs = [(i, j) for i in range(nq) for j in range(i + 1)]
    qi_tab = jnp.asarray([p[0] for p in pairs], jnp.int32)
    ki_tab = jnp.asarray([p[1] for p in pairs], jnp.int32)
    qcols, kcols, csum = _alibi_tables(tq)
    n_sub = 2 * GQA_REP
    grid_spec = pltpu.PrefetchScalarGridSpec(
        num_scalar_prefetch=3,
        grid=(N_KV_HEADS, len(pairs)),
        in_specs=[
            pl.BlockSpec((tq, GQA_REP * LANES), lambda g, t, qi, ki, cf: (qi[t], g)),
            pl.BlockSpec((tq, LANES), lambda g, t, qi, ki, cf: (ki[t], g)),
            pl.BlockSpec((tq, V_DIM), lambda g, t, qi, ki, cf: (ki[t], g)),
            pl.BlockSpec((GQA_REP, 16, LANES), lambda g, t, qi, ki, cf: (g, 0, 0)),
            pl.BlockSpec((tq, LANES), lambda g, t, qi, ki, cf: (0, 0)),
            pl.BlockSpec((4, HEAD_DIM), lambda g, t, qi, ki, cf: (0, 0)),
            pl.BlockSpec((1, V_DIM), lambda g, t, qi, ki, cf: (0, 0)),
        ],
        out_specs=pl.BlockSpec((tq, GQA_REP * V_DIM), lambda g, t, qi, ki, cf: (qi[t], g)),
        scratch_shapes=[
            pltpu.VMEM((n_sub * tq, 2 * LANES), BF16),
            pltpu.VMEM((n_sub * tq, LANES), F32),
            pltpu.VMEM((n_sub * tq, LANES), F32),
            pltpu.VMEM((n_sub * tq, V_DIM), F32),
        ],
    )
    return pl.pallas_call(
        functools.partial(_attn_p_kernel, tq=tq),
        grid_spec=grid_spec,
        out_shape=jax.ShapeDtypeStruct((t, ATT_WIDTH), BF16),
        compiler_params=_cparams(("arbitrary", "arbitrary")),
        name="attn_p",
    )(qi_tab, ki_tab, csum, qn, kb, vb, qcols, kcols, lam_vecs, subln_w.reshape(1, V_DIM))


PAGES_PER_STEP = 16
PAGE_GROUP = PAGES_PER_STEP
ROWS_S = 2 * 4 * N_HEADS


def _attn_s_kernel(pt_ref, q_ref, d0_ref, mask_ref, sl_ref, bn_ref, kn_ref, vn_ref, lam_ref,
                   sw_ref, *rest, n_steps):
    k_refs = [r.at[0] for r in rest[:PAGES_PER_STEP]]
    v_refs = [r.at[0] for r in rest[PAGES_PER_STEP:2 * PAGES_PER_STEP]]
    o_ref = rest[2 * PAGES_PER_STEP]
    m_scr, l_scr, acc_scr = rest[2 * PAGES_PER_STEP + 1:]
    s_id = pl.program_id(1)

    @pl.when(s_id == 0)
    def _():
        m_scr[...] = jnp.full(m_scr.shape, NEG_BIG, F32)
        l_scr[...] = jnp.zeros(l_scr.shape, F32)
        acc_scr[...] = jnp.zeros(acc_scr.shape, F32)

    q = q_ref[0]

    def update(scores, values):
        m_prev = m_scr[...]
        m_new = m_prev
        for sc in scores:
            m_new = jnp.maximum(m_new, jnp.max(sc, axis=1, keepdims=True))
        alpha = jnp.exp2(m_prev - m_new)
        l_new = alpha * l_scr[...]
        acc = alpha * acc_scr[...]
        for sc, vv in zip(scores, values):
            p = jnp.exp2(sc - m_new)
            l_new = l_new + jnp.sum(p, axis=1, keepdims=True)
            acc = acc + _dot(p.astype(BF16), vv)
        m_scr[...] = m_new
        l_scr[...] = l_new
        acc_scr[...] = acc

    for first in range(0, PAGES_PER_STEP, PAGE_GROUP):
        scores, values = [], []
        for i in range(first, first + PAGE_GROUP):
            page_start = ((s_id * PAGES_PER_STEP + i) * PAGE_SIZE).astype(F32)
            bias = sl_ref[...] * (d0_ref[...] - page_start) + mask_ref[...]
            scores.append(_dot_nt(q, k_refs[i][...].astype(BF16)) + bias)
            values.append(v_refs[i][...].astype(BF16))
        update(scores, values)

    @pl.when(s_id == n_steps - 1)
    def _():
        sc = _dot_nt(q, kn_ref[0].astype(BF16)) + bn_ref[...]
        update([sc], [vn_ref[0].astype(BF16)])
        lam = _diff_lambda(lam_ref)
        half = ROWS_S // 2
        o1 = acc_scr[0:half] / l_scr[0:half]
        o2 = acc_scr[half:ROWS_S] / l_scr[half:ROWS_S]
        o_ref[0] = _subln(o1 - lam * o2, sw_ref[...]).astype(BF16)


def _attn_sample(qn_s, kn_s, v_s, cache_k, cache_v, page_table, lam_vecs, subln_w):
    db, n_pages = page_table.shape
    dec_seq = qn_s.shape[0] // db
    past = n_pages * PAGE_SIZE
    n_steps = n_pages // PAGES_PER_STEP
    page_rows = PAGE_SIZE * N_KV_HEADS
    n_phys = cache_k.shape[0]
    ck = cache_k.reshape(n_phys, page_rows, 2 * HEAD_DIM)
    cv = cache_v.reshape(n_phys, page_rows, V_DIM)

    q5 = qn_s.reshape(db, dec_seq, N_HEADS, 2, HEAD_DIM)
    zeros = jnp.zeros_like(q5[:, :, :, 0])
    q_all = jnp.stack([jnp.concatenate([q5[:, :, :, 0], zeros], axis=-1),
                       jnp.concatenate([zeros, q5[:, :, :, 1]], axis=-1)], axis=1)
    q_all = q_all.reshape(db, ROWS_S, LANES)

    r = np.arange(ROWS_S)
    tok_r = (r % (dec_seq * N_HEADS)) // N_HEADS
    head_r = r % N_HEADS
    slope_r = np.asarray(ALIBI_SLOPES)[head_r] * LOG2E
    c = np.arange(page_rows)
    key_c, grp_c = c // N_KV_HEADS, c % N_KV_HEADS
    same = (head_r[:, None] // GQA_REP) == grp_c[None, :]
    d0 = np.broadcast_to(past + tok_r[:, None] - key_c[None, :], (ROWS_S, page_rows))
    mask = np.where(same, 0.0, NEG_BIG)
    sl = np.broadcast_to(-slope_r[:, None], (ROWS_S, 1))
    cn = np.arange(LANES)
    tok_c, grp_n = cn // N_KV_HEADS, cn % N_KV_HEADS
    ok = ((head_r[:, None] // GQA_REP) == grp_n[None, :]) & (tok_c[None, :] <= tok_r[:, None])
    bn = np.where(ok, -slope_r[:, None] * (tok_r[:, None] - tok_c[None, :]), NEG_BIG)

    new_rows = dec_seq * N_KV_HEADS
    kn_pad = jnp.pad(kn_s.reshape(db, new_rows, LANES), ((0, 0), (0, LANES - new_rows), (0, 0)))
    vn_pad = jnp.pad(v_s.reshape(db, new_rows, LANES), ((0, 0), (0, LANES - new_rows), (0, 0)))

    def const(shape):
        return pl.BlockSpec(shape, lambda b, s, pt: (0,) * len(shape))

    def page_spec(i):
        return pl.BlockSpec(
            (1, page_rows, LANES),
            lambda b, s, pt: (pt[b * n_pages + s * PAGES_PER_STEP + i], 0, 0))

    grid_spec = pltpu.PrefetchScalarGridSpec(
        num_scalar_prefetch=1,
        grid=(db, n_steps),
        in_specs=[
            pl.BlockSpec((1, ROWS_S, LANES), lambda b, s, pt: (b, 0, 0)),
            const((ROWS_S, page_rows)),
            const((ROWS_S, page_rows)),
            const((ROWS_S, 1)),
            const((ROWS_S, LANES)),
            pl.BlockSpec((1, LANES, LANES), lambda b, s, pt: (b, 0, 0)),
            pl.BlockSpec((1, LANES, LANES), lambda b, s, pt: (b, 0, 0)),
            const((4, HEAD_DIM)),
            const((1, V_DIM)),
        ] + [page_spec(i) for i in range(PAGES_PER_STEP)] * 2,
        out_specs=pl.BlockSpec((1, ROWS_S // 2, V_DIM), lambda b, s, pt: (b, 0, 0)),
        scratch_shapes=[
            pltpu.VMEM((ROWS_S, 1), F32),
            pltpu.VMEM((ROWS_S, 1), F32),
            pltpu.VMEM((ROWS_S, V_DIM), F32),
        ],
    )
    o = pl.pallas_call(
        functools.partial(_attn_s_kernel, n_steps=n_steps),
        grid_spec=grid_spec,
        out_shape=jax.ShapeDtypeStruct((db, ROWS_S // 2, V_DIM), BF16),
        compiler_params=_cparams(("arbitrary", "arbitrary")),
        name="attn_s",
    )(page_table.reshape(-1), q_all, jnp.asarray(d0, F32), jnp.asarray(mask, F32),
      jnp.asarray(sl, F32), jnp.asarray(bn, F32), kn_pad, vn_pad, lam_vecs,
      subln_w.reshape(1, V_DIM),
      *([ck] * PAGES_PER_STEP), *([cv] * PAGES_PER_STEP))
    return o.reshape(db * dec_seq, ATT_WIDTH)


HALO = SUBLANES


def _ssd_kernel(xs_ref, b_ref, c_ref, z_ref, dt_ref, dtt_ref, halo_ref, init_ref,
                cw_ref, cb_ref, dtb_ref, dtbt_ref, a_ref, at_ref, dsk_ref, nw_ref,
                tri_ref, trit_ref, exp_ref, sel_ref,
                y_ref, fin_ref, win_scr, state_scr, *, rows_in, n_valid):
    ci = pl.program_id(1)
    n_chunks = pl.num_programs(1)
    lc = SSD_CHUNK
    bc_w = N_GROUPS * D_STATE

    @pl.when(ci == 0)
    def _():
        state_scr[...] = init_ref[0].reshape(D_SSM, D_STATE)
        win_scr[0:HALO, :] = halo_ref[0]

    if rows_in < lc:
        win_scr[HALO:HALO + lc, :] = jnp.zeros((lc, CONV_DIM), F32)
    win_scr[HALO:HALO + rows_in, 0:D_SSM] = xs_ref[0]
    win_scr[HALO:HALO + rows_in, D_SSM:D_SSM + bc_w] = b_ref[0]
    win_scr[HALO:HALO + rows_in, D_SSM + bc_w:CONV_DIM] = c_ref[0]

    acc = cb_ref[...]
    for tap in range(CONV_WIDTH):
        off = HALO - (CONV_WIDTH - 1) + tap
        acc = acc + win_scr[off:off + lc, :] * cw_ref[tap:tap + 1, :]
    conv = _silu(acc)
    win_scr[0:HALO, :] = win_scr[lc:lc + HALO, :]
    xs = conv[:, 0:D_SSM]
    bm = conv[:, D_SSM:D_SSM + bc_w].astype(BF16)
    cm = conv[:, D_SSM + bc_w:CONV_DIM].astype(BF16)

    if rows_in < lc:
        dt_in = jnp.concatenate([dt_ref[0], jnp.zeros((lc - rows_in, LANES), F32)], axis=0)
        dtt_in = jnp.concatenate(
            [dtt_ref[0], jnp.zeros((N_SSM_HEADS, lc - rows_in), F32)], axis=1)
    else:
        dt_in, dtt_in = dt_ref[0], dtt_ref[0]
    rowi = lax.broadcasted_iota(jnp.int32, (lc, LANES), 0)
    coli = lax.broadcasted_iota(jnp.int32, (N_SSM_HEADS, lc), 1)
    dt = jnp.where(rowi < n_valid, _softplus(dt_in + dtb_ref[...]), 0.0)
    dtt = jnp.where(coli < n_valid, _softplus(dtt_in + dtbt_ref[...]), 0.0)
    a_cs = _dot_x3_left(tri_ref[...], dt * a_ref[...])
    a_cst = _dot_x3(dtt * at_ref[...], trit_ref[...])
    a_last = a_cs[lc - 1:lc, :]
    exp_cs = jnp.exp(a_cs)
    exp_rest = jnp.exp(a_last - a_cs)
    expand = exp_ref[...]
    dtx = _dot_x2(dt, expand)
    ecx = _dot_x2(exp_cs, expand)
    erx = _dot_x2(exp_rest, expand)
    xc = xs * dtx
    xcb = xc.astype(BF16)
    xcd = (xc * erx).astype(BF16)

    last_t = jnp.exp(a_cst[:, lc - 1:lc])
    rdec = _dot_x2_left(sel_ref[...], jnp.broadcast_to(last_t, (N_SSM_HEADS, D_STATE)))

    tril = (lax.broadcasted_iota(jnp.int32, (lc, lc), 0)
            >= lax.broadcasted_iota(jnp.int32, (lc, lc), 1))
    lane = lax.broadcasted_iota(jnp.int32, (lc, LANES), 1)
    gw = HEADS_PER_GROUP * SSM_HEAD_DIM
    y_parts = []
    for g in range(N_GROUPS):
        bg = bm[:, g * D_STATE:(g + 1) * D_STATE]
        cg = cm[:, g * D_STATE:(g + 1) * D_STATE]
        cb = _dot_nt(cg, bg)
        st = state_scr[g * gw:(g + 1) * gw, :]
        y_off = _dot_nt(cg, st.astype(BF16)) * ecx[:, g * gw:(g + 1) * gw]
        new_st = _dot_tn(xcd[:, g * gw:(g + 1) * gw], bg)
        state_scr[g * gw:(g + 1) * gw, :] = st * rdec[g * gw:(g + 1) * gw, :] + new_st
        for j in range(HEADS_PER_GROUP // 2):
            pair = g * (HEADS_PER_GROUP // 2) + j
            blk = xcb[:, pair * LANES:(pair + 1) * LANES]
            y_pair = None
            for half in range(2):
                h = 2 * pair + half
                seg = a_cs[:, h:h + 1] - a_cst[h:h + 1, :]
                decay = jnp.exp(jnp.where(tril, seg, NEG_BIG))
                mh = (cb * decay).astype(BF16)
                keep = (lane < SSM_HEAD_DIM) if half == 0 else (lane >= SSM_HEAD_DIM)
                part = _dot(mh, jnp.where(keep, blk, jnp.zeros_like(blk)))
                y_pair = part if y_pair is None else y_pair + part
            y_parts.append(y_pair + y_off[:, (pair % (HEADS_PER_GROUP // 2)) * LANES:
                                          (pair % (HEADS_PER_GROUP // 2) + 1) * LANES])
    y = jnp.concatenate(y_parts, axis=1)
    y = y + dsk_ref[...] * xs
    if rows_in < lc:
        z = jnp.concatenate([z_ref[0], jnp.zeros((lc - rows_in, D_SSM), F32)], axis=0)
    else:
        z = z_ref[0]
    y = y * _silu(z)
    gn = D_SSM // N_GROUPS
    outs = []
    for g in range(N_GROUPS):
        yg = y[:, g * gn:(g + 1) * gn]
        ms = jnp.mean(yg * yg, axis=-1, keepdims=True)
        outs.append(yg * lax.rsqrt(ms + EPS) * nw_ref[:, g * gn:(g + 1) * gn])
    out = jnp.concatenate(outs, axis=1).astype(BF16)
    y_ref[0] = out[0:rows_in]

    @pl.when(ci == n_chunks - 1)
    def _():
        fin_ref[0] = state_scr[...].reshape(N_SSM_HEADS, SSM_HEAD_DIM, D_STATE)


def _dot_x3_left(sel, x):
    hi, mid, lo = _split3(x)
    return _dot(sel, hi) + _dot(sel, mid) + _dot(sel, lo)


def _dot_x2_left(sel, x):
    hi, lo = _split2(x)
    return _dot(sel, hi) + _dot(sel, lo)


def _ssd(src, col_blocks, dt_raw, halo, init_state, prm, rows_in, n_valid):
    nb, seq = src.shape[0], src.shape[1]
    n_chunks = max(1, seq // SSD_CHUNK)
    bc_w = N_GROUPS * D_STATE
    dtt = jnp.swapaxes(dt_raw[:, :, :N_SSM_HEADS], 1, 2)
    tri = np.tril(np.ones((SSD_CHUNK, SSD_CHUNK)))
    expand = np.zeros((LANES, D_SSM))
    expand[np.arange(D_SSM) // SSM_HEAD_DIM, np.arange(D_SSM)] = 1.0
    sel = expand[:N_SSM_HEADS].T
    cx, cbk, cck, cz = col_blocks

    def const(shape):
        return pl.BlockSpec(shape, lambda b, c: (0,) * len(shape))

    return pl.pallas_call(
        functools.partial(_ssd_kernel, rows_in=rows_in, n_valid=n_valid),
        grid=(nb, n_chunks),
        in_specs=[
            pl.BlockSpec((1, rows_in, D_SSM), lambda b, c: (b, c, cx)),
            pl.BlockSpec((1, rows_in, bc_w), lambda b, c: (b, c, cbk)),
            pl.BlockSpec((1, rows_in, bc_w), lambda b, c: (b, c, cck)),
            pl.BlockSpec((1, rows_in, D_SSM), lambda b, c: (b, c, cz)),
            pl.BlockSpec((1, rows_in, LANES), lambda b, c: (b, c, 0)),
            pl.BlockSpec((1, N_SSM_HEADS, rows_in), lambda b, c: (b, 0, c)),
            pl.BlockSpec((1, HALO, CONV_DIM), lambda b, c: (b, 0, 0)),
            pl.BlockSpec((1, N_SSM_HEADS, SSM_HEAD_DIM, D_STATE), lambda b, c: (b, 0, 0, 0)),
            const((CONV_WIDTH, CONV_DIM)),
            const((1, CONV_DIM)),
            const((1, LANES)),
            const((N_SSM_HEADS, 1)),
            const((1, LANES)),
            const((N_SSM_HEADS, 1)),
            const((1, D_SSM)),
            const((1, D_SSM)),
            const((SSD_CHUNK, SSD_CHUNK)),
            const((SSD_CHUNK, SSD_CHUNK)),
            const((LANES, D_SSM)),
            const((D_SSM, N_SSM_HEADS)),
        ],
        out_specs=[
            pl.BlockSpec((1, rows_in, D_SSM), lambda b, c: (b, c, 0)),
            pl.BlockSpec((1, N_SSM_HEADS, SSM_HEAD_DIM, D_STATE), lambda b, c: (b, 0, 0, 0)),
        ],
        out_shape=[
            jax.ShapeDtypeStruct((nb, seq, D_SSM), BF16),
            jax.ShapeDtypeStruct((nb, N_SSM_HEADS, SSM_HEAD_DIM, D_STATE), F32),
        ],
        scratch_shapes=[
            pltpu.VMEM((HALO + SSD_CHUNK, CONV_DIM), F32),
            pltpu.VMEM((D_SSM, D_STATE), F32),
        ],
        compiler_params=_cparams(("arbitrary", "arbitrary")),
        name="ssd",
    )(src, src, src, src, dt_raw, dtt, halo, init_state,
      prm["conv_w"], prm["conv_b"], prm["dt_bias"], prm["dt_bias_t"], prm["a"], prm["a_t"],
      prm["d_skip"], prm["ssm_norm_w"],
      jnp.asarray(tri, BF16), jnp.asarray(tri.T, BF16), jnp.asarray(expand, BF16),
      jnp.asarray(sel, BF16))


def _merge_kernel(o_ref, s_ref, wa_ref, ws_ref, ga_ref, gs_ref, out_ref):
    a = _dot(o_ref[...], wa_ref[...])
    s = _dot(s_ref[...], ws_ref[...])
    out_ref[...] = (_sigmoid(ga_ref[...]) * a + _sigmoid(gs_ref[...]) * s).astype(BF16)


def _merge(o, s, wa, ws, proj, tm, tn):
    t = o.shape[0]
    ga0, gs0 = COL_GA // tn, COL_GS // tn
    return pl.pallas_call(
        _merge_kernel,
        grid=(t // tm, D_MODEL // tn),
        in_specs=[
            pl.BlockSpec((tm, ATT_WIDTH), lambda i, j: (i, 0)),
            pl.BlockSpec((tm, D_SSM), lambda i, j: (i, 0)),
            pl.BlockSpec((ATT_WIDTH, tn), lambda i, j: (0, j)),
            pl.BlockSpec((D_SSM, tn), lambda i, j: (0, j)),
            pl.BlockSpec((tm, tn), lambda i, j: (i, ga0 + j)),
            pl.BlockSpec((tm, tn), lambda i, j: (i, gs0 + j)),
        ],
        out_specs=pl.BlockSpec((tm, tn), lambda i, j: (i, j)),
        out_shape=jax.ShapeDtypeStruct((t, D_MODEL), BF16),
        compiler_params=_cparams(("arbitrary", "arbitrary")),
        name="merge",
    )(o, s, wa, ws, proj, proj)


ROUTE_E1, ROUTE_E2, ROUTE_W1, ROUTE_W2 = 0, 1, 2, 3
ROW_CHUNKS = D_MODEL // LANES


def _resid_kernel(x_ref, m_ref, wo_ref, nw_ref, wrh_ref, wrl_ref, br_ref, *rest, n_real):
    h_ref, u_ref, route_ref = rest[-3:]

    @pl.when(pl.program_id(0) >= n_real)
    def _():
        h_ref[...] = jnp.zeros(h_ref.shape, F32)
        u_ref[...] = jnp.zeros(u_ref.shape, F32)
        route_ref[...] = jnp.zeros(route_ref.shape, F32)

    @pl.when(pl.program_id(0) < n_real)
    def _():
        _resid_tile(x_ref, m_ref, wo_ref, nw_ref, wrh_ref, wrl_ref, br_ref,
                    h_ref, u_ref, route_ref)


def _resid_tile(x_ref, m_ref, wo_ref, nw_ref, wrh_ref, wrl_ref, br_ref, h_ref, u_ref, route_ref):
    h = x_ref[...] + _dot(m_ref[...], wo_ref[...])
    h_ref[...] = h
    ms = jnp.mean(h * h, axis=-1, keepdims=True)
    u = h * lax.rsqrt(ms + EPS) * nw_ref[...]
    u_hi, u_lo = _split2(u)
    u_ref[...] = u
    logits = (_dot(u_hi, wrh_ref[...]) + _dot(u_lo, wrh_ref[...])
              + _dot(u_hi, wrl_ref[...]) + br_ref[...])
    lane = lax.broadcasted_iota(jnp.int32, logits.shape, 1)
    lane_f = lane.astype(F32)
    far = float(2 * LANES)

    def first_max(vals):
        top = jnp.max(vals, axis=1, keepdims=True)
        idx = jnp.min(jnp.where(vals == top, lane_f, far), axis=1, keepdims=True)
        return top, idx

    is_group = (lane >= N_EXPERTS) & (lane < N_EXPERTS + N_EXPERT_GROUPS)
    gl = jnp.where(is_group, logits, NEG_BIG)
    g_top, g_idx = first_max(gl)
    g_p = 1.0 / jnp.sum(jnp.exp(gl - g_top), axis=1, keepdims=True)
    lo_lane = (g_idx - N_EXPERTS) * EXPERTS_PER_GROUP
    in_group = (lane_f >= lo_lane) & (lane_f < lo_lane + EXPERTS_PER_GROUP)
    el = jnp.where(in_group, logits, NEG_BIG)
    m1, i1 = first_max(el)
    el2 = jnp.where(lane_f == i1, NEG_BIG, el)
    m2, i2 = first_max(el2)
    e = jnp.exp(m2 - m1)
    w1 = 1.0 / (1.0 + e)
    w2 = e / (1.0 + e)
    route = jnp.where(lane == ROUTE_E1, i1, 0.0)
    route = jnp.where(lane == ROUTE_E2, i2, route)
    route = jnp.where(lane == ROUTE_W1, g_p * w1, route)
    route_ref[...] = jnp.where(lane == ROUTE_W2, g_p * w2, route)


def _resid(x, merged, wo, norm_w, wr_hi, wr_lo, br, tm, t_all, row_off, bufs):
    t = x.shape[0]
    blk_off = row_off // tm
    n_real = t // tm
    n_fill = pl.cdiv(t_all - t, tm) if bufs is None else 0

    def const(shape):
        return pl.BlockSpec(shape, lambda i: (0,) * len(shape))

    in_specs = [
        pl.BlockSpec((tm, D_MODEL), lambda i: (jnp.minimum(i, n_real - 1), 0)),
        pl.BlockSpec((tm, D_MODEL), lambda i: (jnp.minimum(i, n_real - 1), 0)),
        const((D_MODEL, D_MODEL)),
        const((1, D_MODEL)),
        const((D_MODEL, LANES)),
        const((D_MODEL, LANES)),
        const((1, LANES)),
    ]
    args = [x, merged, wo, norm_w, wr_hi, wr_lo, br]
    aliases = {}
    if bufs is not None:
        aliases = {len(args) + k: k for k in range(len(bufs))}
        in_specs += [pl.BlockSpec(memory_space=pl.ANY)] * len(bufs)
        args += list(bufs)
    return pl.pallas_call(
        functools.partial(_resid_kernel, n_real=n_real),
        grid=(n_real + n_fill,),
        in_specs=in_specs,
        out_specs=[
            pl.BlockSpec((tm, D_MODEL), lambda i: (i + blk_off, 0)),
            pl.BlockSpec((tm, D_MODEL), lambda i: (i + blk_off, 0)),
            pl.BlockSpec((tm, LANES), lambda i: (i + blk_off, 0)),
        ],
        out_shape=[
            jax.ShapeDtypeStruct((t_all, D_MODEL), F32),
            jax.ShapeDtypeStruct((t_all, D_MODEL), F32),
            jax.ShapeDtypeStruct((t_all, LANES), F32),
        ],
        input_output_aliases=aliases,
        compiler_params=_cparams(("arbitrary",)),
        name="resid",
    )(*args)


MOE_TILE = 256
COMB_TILE = 128
DMA_UNROLL = 8


def _route_plan(route, n_tiles):
    n_pairs = 2 * route.shape[0]
    pair_e = route[:, ROUTE_E1:ROUTE_E2 + 1].astype(jnp.int32).reshape(-1)
    onehot = (pair_e[:, None] == jnp.arange(N_EXPERTS, dtype=jnp.int32)[None, :]).astype(jnp.int32)
    csum = jnp.cumsum(onehot, axis=0)
    rank = jnp.sum((csum - onehot) * onehot, axis=1)
    tiles_e = (csum[-1] + MOE_TILE - 1) // MOE_TILE
    tile_end = jnp.cumsum(tiles_e)
    first_row = (tile_end - tiles_e) * MOE_TILE
    slot = jnp.sum(onehot * first_row[None, :], axis=1) + rank
    tok_of_slot = jnp.zeros(((n_tiles + 1) * MOE_TILE,), jnp.int32).at[slot].set(
        jnp.arange(n_pairs, dtype=jnp.int32) // 2)
    tile_start = (tile_end - tiles_e).astype(jnp.int32)
    slot_tab = slot.reshape(-1, COMB_TILE, 2).transpose(0, 2, 1).reshape(-1).astype(jnp.int32)
    return tile_start, tiles_e.astype(jnp.int32), tok_of_slot, slot_tab


def _gmm_kernel(ts_ref, ne_ref, tok_ref, u_hbm, wg_ref, wu_ref, wd_ref, o_hbm,
                xbuf, obuf, gsem, osem, wgb, wub, wdb, *, n_tiles):
    e = pl.program_id(0)
    last = pl.num_programs(0) - 1
    n_used = ts_ref[last] + ne_ref[last]

    def row_copy(tile, r, buf):
        tok = tok_ref[tile * MOE_TILE + r]
        return pltpu.make_async_copy(
            u_hbm.at[tok], xbuf.at[buf, r // SUBLANES, :, r % SUBLANES, :], gsem.at[buf])

    def out_copy(tile, buf):
        return pltpu.make_async_copy(
            obuf.at[buf], o_hbm.at[pl.ds(tile * MOE_TILE, MOE_TILE), :], osem.at[buf])

    def gather_wait(tile, buf):
        def body(r, carry):
            row_copy(tile, r, buf).wait()
            return carry
        lax.fori_loop(0, MOE_TILE, body, 0, unroll=DMA_UNROLL)

    @pl.when(e == 0)
    def _():
        def body(r, carry):
            row_copy(0, r, 0).start()
            return carry
        lax.fori_loop(0, MOE_TILE, body, 0, unroll=DMA_UNROLL)

    wgb[...] = wg_ref[0].astype(BF16)
    wub[...] = wu_ref[0].astype(BF16)
    wdb[...] = wd_ref[0].astype(BF16)

    def tile_body(j, carry):
        t = ts_ref[e] + j
        cur = lax.rem(t, 2)

        @pl.when(t >= 2)
        def _():
            out_copy(t - 2, cur).wait()

        gather_wait(t, cur)
        x = jnp.concatenate(
            [xbuf[cur, :, c].reshape(MOE_TILE, LANES) for c in range(ROW_CHUNKS)],
            axis=1).astype(BF16)
        for r in range(MOE_TILE):
            row_copy(t + 1, r, 1 - cur).start()
        hid = _silu(_dot(x, wgb[...])) * _dot(x, wub[...])
        obuf[cur] = _dot(hid.astype(BF16), wdb[...])
        out_copy(t, cur).start()
        return carry

    lax.fori_loop(0, ne_ref[e], tile_body, 0)

    @pl.when(e == last)
    def _():
        gather_wait(n_used, lax.rem(n_used, 2))
        out_copy(n_used - 2, lax.rem(n_used, 2)).wait()
        out_copy(n_used - 1, lax.rem(n_used - 1, 2)).wait()
        obuf[0] = jnp.zeros(obuf.shape[1:], F32)

        def fill(t, carry):
            cp = out_copy(t, 0)
            cp.start()
            cp.wait()
            return carry
        lax.fori_loop(n_used, n_tiles + 1, fill, 0)


def _gmm(u_all, plan, wg, wu, wd, n_tiles):
    tile_start, tiles_e, tok_of_slot, _ = plan
    assert 2 * u_all.shape[0] >= 2 * MOE_TILE
    grid_spec = pltpu.PrefetchScalarGridSpec(
        num_scalar_prefetch=3,
        grid=(N_EXPERTS,),
        in_specs=[
            pl.BlockSpec(memory_space=pl.ANY),
            pl.BlockSpec((1, D_MODEL, D_EXPERT), lambda e, ts, ne, tok: (e, 0, 0)),
            pl.BlockSpec((1, D_MODEL, D_EXPERT), lambda e, ts, ne, tok: (e, 0, 0)),
            pl.BlockSpec((1, D_EXPERT, D_MODEL), lambda e, ts, ne, tok: (e, 0, 0)),
        ],
        out_specs=pl.BlockSpec(memory_space=pl.ANY),
        scratch_shapes=[
            pltpu.VMEM((2, MOE_TILE // SUBLANES, ROW_CHUNKS, SUBLANES, LANES), F32),
            pltpu.VMEM((2, MOE_TILE, D_MODEL), F32),
            pltpu.SemaphoreType.DMA((2,)),
            pltpu.SemaphoreType.DMA((2,)),
            pltpu.VMEM((D_MODEL, D_EXPERT), BF16),
            pltpu.VMEM((D_MODEL, D_EXPERT), BF16),
            pltpu.VMEM((D_EXPERT, D_MODEL), BF16),
        ],
    )
    return pl.pallas_call(
        functools.partial(_gmm_kernel, n_tiles=n_tiles),
        grid_spec=grid_spec,
        out_shape=jax.ShapeDtypeStruct(((n_tiles + 1) * MOE_TILE, D_MODEL), F32),
        compiler_params=_cparams(("arbitrary",)),
        name="gmm",
    )(tile_start, tiles_e, tok_of_slot, u_all, wg, wu, wd)


def _combine_kernel(slot_ref, route_ref, h_ref, o_hbm, yp_ref, ys_ref, gbuf, sem, *, n_prompt):
    i = pl.program_id(0)
    n = pl.num_programs(0)
    cur = lax.rem(i, 2)
    rows = 2 * COMB_TILE

    def row_copy(tile, j, buf):
        slot = slot_ref[tile * rows + j]
        return pltpu.make_async_copy(
            o_hbm.at[pl.ds(slot, 1), :], gbuf.at[buf, pl.ds(j, 1), :], sem.at[buf])

    def issue(tile, buf):
        def body(j, carry):
            row_copy(tile, j, buf).start()
            return carry
        lax.fori_loop(0, rows, body, 0, unroll=DMA_UNROLL)

    def wait(tile, buf):
        def body(j, carry):
            row_copy(tile, j, buf).wait()
            return carry
        lax.fori_loop(0, rows, body, 0, unroll=DMA_UNROLL)

    @pl.when(i == 0)
    def _():
        issue(0, 0)

    @pl.when(i + 1 < n)
    def _():
        issue(i + 1, 1 - cur)

    wait(i, cur)
    w1 = route_ref[:, ROUTE_W1:ROUTE_W1 + 1]
    w2 = route_ref[:, ROUTE_W2:ROUTE_W2 + 1]
    y = h_ref[...] + w1 * gbuf[cur, 0:COMB_TILE, :] + w2 * gbuf[cur, COMB_TILE:rows, :]

    @pl.when(i < n_prompt)
    def _():
        yp_ref[...] = y

    @pl.when(i >= n_prompt)
    def _():
        ys_ref[...] = y


def _combine(route, h_all, o_sorted, plan, t_prompt):
    t_all = h_all.shape[0]
    n_prompt = t_prompt // COMB_TILE
    slot_tab = plan[3]
    grid_spec = pltpu.PrefetchScalarGridSpec(
        num_scalar_prefetch=1,
        grid=(t_all // COMB_TILE,),
        in_specs=[
            pl.BlockSpec((COMB_TILE, LANES), lambda i, st: (i, 0)),
            pl.BlockSpec((COMB_TILE, D_MODEL), lambda i, st: (i, 0)),
            pl.BlockSpec(memory_space=pl.ANY),
        ],
        out_specs=[
            pl.BlockSpec((COMB_TILE, D_MODEL), lambda i, st: (jnp.minimum(i, n_prompt - 1), 0)),
            pl.BlockSpec((COMB_TILE, D_MODEL), lambda i, st: (0, 0)),
        ],
        scratch_shapes=[
            pltpu.VMEM((2, 2 * COMB_TILE, D_MODEL), F32),
            pltpu.SemaphoreType.DMA((2,)),
        ],
    )
    return pl.pallas_call(
        functools.partial(_combine_kernel, n_prompt=n_prompt),
        grid_spec=grid_spec,
        out_shape=[
            jax.ShapeDtypeStruct((t_prompt, D_MODEL), F32),
            jax.ShapeDtypeStruct((t_all - t_prompt, D_MODEL), F32),
        ],
        compiler_params=_cparams(("arbitrary",)),
        name="combine",
    )(slot_tab, route, h_all, o_sorted)


def _layer_tokens(x2d, w, tm_proj, tm_small):
    proj, dt_raw = _proj(x2d, w["norm_attn_w"], w["w_a"], w["w_b"], w["w_dt"], tm_proj, 1024)
    qn, kn, kb, vb = _qk_norm(proj, w["q_norm_w"], w["k_norm_w"], tm_small)
    return proj, dt_raw, qn, kn, kb, vb


def _branch_merge(x2d, o, s, proj, w, tm, tm_resid, t_all, row_off, bufs):
    merged = _merge(o, s, w["w_att_out"], w["w_ssm_out"], proj, tm, 512)
    return _resid(x2d, merged, w["w_o"], w["norm_ffn_w"], w["wr_hi"], w["wr_lo"], w["br"],
                  tm_resid, t_all, row_off, bufs)


def kernel(x_prompt, x_sample, cache_k, cache_v, state_ssm, state_conv, page_table, norm_attn_w, w_in, q_norm_w, k_norm_w, lambda_q1, lambda_k1, lambda_q2, lambda_k2, subln_w, w_att_out, conv_w, conv_b, dt_bias, a_log, d_skip, ssm_norm_w, w_ssm_out, w_o, norm_ffn_w, w_group_router, b_group_router, w_expert_router, b_expert_router, w_gate, w_up, w_down):
    layer = 0
    nb, seq, _ = x_prompt.shape
    db, dec_seq, _ = x_sample.shape

    w_in_l = w_in[layer]
    c_dt = Q_WIDTH + K_WIDTH + V_WIDTH + D_SSM + CONV_DIM
    w_a = w_in_l[:, :c_dt].astype(BF16)
    w_b = w_in_l[:, c_dt + N_SSM_HEADS:].astype(BF16)
    w_dt = jnp.pad(w_in_l[:, c_dt:c_dt + N_SSM_HEADS], ((0, 0), (0, LANES - N_SSM_HEADS))).astype(BF16)
    wr = jnp.concatenate([w_expert_router[layer], w_group_router[layer]], axis=1)
    wr = jnp.pad(wr, ((0, 0), (0, LANES - wr.shape[1])))
    wr_hi = wr.astype(BF16)
    wr_lo = (wr - wr_hi.astype(F32)).astype(BF16)
    br = jnp.concatenate([b_expert_router[layer], b_group_router[layer]])
    br = jnp.pad(br, (0, LANES - br.shape[0])).reshape(1, LANES)
    pad_h = (0, LANES - N_SSM_HEADS)
    w = dict(
        norm_attn_w=norm_attn_w[layer].reshape(1, D_MODEL), w_a=w_a, w_b=w_b, w_dt=w_dt,
        q_norm_w=q_norm_w[layer], k_norm_w=k_norm_w[layer],
        w_att_out=w_att_out[layer].astype(BF16), w_ssm_out=w_ssm_out[layer].astype(BF16),
        w_o=w_o[layer].astype(BF16), norm_ffn_w=norm_ffn_w[layer].reshape(1, D_MODEL),
        wr_hi=wr_hi, wr_lo=wr_lo, br=br,
        w_gate=w_gate[layer], w_up=w_up[layer], w_down=w_down[layer],
    )
    ssm_prm = dict(
        conv_w=conv_w[layer], conv_b=conv_b[layer].reshape(1, CONV_DIM),
        dt_bias=jnp.pad(dt_bias[layer], pad_h).reshape(1, LANES),
        dt_bias_t=dt_bias[layer].reshape(N_SSM_HEADS, 1),
        a=jnp.pad(-jnp.exp(a_log[layer]), pad_h).reshape(1, LANES),
        a_t=(-jnp.exp(a_log[layer])).reshape(N_SSM_HEADS, 1),
        d_skip=jnp.repeat(d_skip[layer], SSM_HEAD_DIM).reshape(1, D_SSM),
        ssm_norm_w=ssm_norm_w[layer].reshape(1, D_SSM),
    )
    lam_vecs = jnp.stack([lambda_q1[layer], lambda_k1[layer], lambda_q2[layer], lambda_k2[layer]])
    sw = subln_w[layer]
    ssd_cols = (COL_X // D_SSM, COL_B // (N_GROUPS * D_STATE), COL_C // (N_GROUPS * D_STATE),
                COL_Z // D_SSM)

    xp = x_prompt.reshape(nb * seq, D_MODEL)
    n_tok = db * dec_seq
    t_prompt = nb * seq
    t_all = t_prompt + n_tok
    proj_p, dt_p, qn_p, kn_p, kb_p, vb_p = _layer_tokens(xp, w, 1024, 512)
    o_p = _attn_prompt(qn_p, kb_p, vb_p, lam_vecs, sw, 512)
    s_p, ssm_p = _ssd(
        proj_p.reshape(nb, seq, PROJ_WIDTH), ssd_cols, dt_p.reshape(nb, seq, LANES),
        jnp.zeros((nb, HALO, CONV_DIM), F32), jnp.zeros((nb, N_SSM_HEADS, SSM_HEAD_DIM, D_STATE), F32),
        ssm_prm, SSD_CHUNK, SSD_CHUNK)
    bufs = _branch_merge(xp, o_p, s_p.reshape(t_prompt, D_SSM), proj_p, w, 1024, 256,
                         t_all, 0, None)
    keep = CONV_WIDTH - 1
    conv_p = proj_p.reshape(nb, seq, PROJ_WIDTH)[:, seq - keep:, COL_X:COL_X + CONV_DIM]

    xs = x_sample.reshape(db * dec_seq, D_MODEL)
    proj_s, dt_s, qn_s, kn_s, _, _ = _layer_tokens(xs, w, n_tok, n_tok)
    v_s = proj_s[:, COL_V:COL_V + V_WIDTH]
    o_s = _attn_sample(qn_s, kn_s, v_s, cache_k[layer], cache_v[layer], page_table, lam_vecs, sw)
    rows_s = SUBLANES
    pad_rows = ((0, 0), (0, rows_s - dec_seq), (0, 0))
    src_s = jnp.pad(proj_s[:, COL_Z:COL_GA].reshape(db, dec_seq, COL_GA - COL_Z), pad_rows)
    halo_s = jnp.pad(state_conv[layer], ((0, 0), (HALO - (CONV_WIDTH - 1), 0), (0, 0)))
    cols_s = ((COL_X - COL_Z) // D_SSM, (COL_B - COL_Z) // (N_GROUPS * D_STATE),
              (COL_C - COL_Z) // (N_GROUPS * D_STATE), 0)
    s_s, ssm_s = _ssd(
        src_s, cols_s, jnp.pad(dt_s.reshape(db, dec_seq, LANES), pad_rows), halo_s,
        state_ssm[layer], ssm_prm, rows_s, dec_seq)
    s_s = s_s[:, :dec_seq].reshape(n_tok, D_SSM)
    h_all, u_all, route = _branch_merge(xs, o_s, s_s, proj_s, w, n_tok, n_tok,
                                        t_all, t_prompt, bufs)

    n_tiles = 2 * t_all // MOE_TILE + N_EXPERTS
    plan = _route_plan(route, n_tiles)
    o_sorted = _gmm(u_all.reshape(t_all, ROW_CHUNKS, LANES), plan, w["w_gate"], w["w_up"], w["w_down"], n_tiles)
    y_p, y_s = _combine(route, h_all, o_sorted, plan, t_prompt)
    conv_s =proj_s.reshape(db, dec_seq, PROJ_WIDTH)[:, dec_seq - keep:, COL_X:COL_X + CONV_DIM]

    return (
        y_p.reshape(nb, seq, D_MODEL),
        y_s.reshape(db, dec_seq, D_MODEL),
        kn_p.reshape(1, nb, seq, N_KV_HEADS, 2 * HEAD_DIM),
        proj_p[:, COL_V:COL_V + V_WIDTH].reshape(1, nb, seq, N_KV_HEADS, V_DIM),
        ssm_p.reshape(1, nb, N_SSM_HEADS, SSM_HEAD_DIM, D_STATE),
        conv_p[None],
        kn_s.reshape(1, db, dec_seq, N_KV_HEADS, 2 * HEAD_DIM),
        v_s.reshape(1, db, dec_seq, N_KV_HEADS, V_DIM),
        ssm_s.reshape(1, db, N_SSM_HEADS, SSM_HEAD_DIM, D_STATE),
        conv_s[None],
    )
```

```python
import functools
import math

import jax
import jax.numpy as jnp
import ml_dtypes
import numpy as np
from jax import lax
from jax.experimental import pallas as pl
from jax.experimental.pallas import tpu as pltpu

F32 = jnp.float32
BF16 = jnp.bfloat16

D_MODEL = 2048
N_HEADS = 8
N_KV_HEADS = 4
GQA_REP = N_HEADS // N_KV_HEADS
HEAD_DIM = 64
V_DIM = 2 * HEAD_DIM
Q_WIDTH = N_HEADS * 2 * HEAD_DIM
K_WIDTH = N_KV_HEADS * 2 * HEAD_DIM
V_WIDTH = N_KV_HEADS * V_DIM
ATT_WIDTH = N_HEADS * V_DIM
D_SSM = D_MODEL
SSM_HEAD_DIM = 64
N_SSM_HEADS = D_SSM // SSM_HEAD_DIM
N_GROUPS = 4
HEADS_PER_GROUP = N_SSM_HEADS // N_GROUPS
D_STATE = 128
CONV_WIDTH = 4
CONV_DIM = D_SSM + 2 * N_GROUPS * D_STATE
SSD_CHUNK = 128
N_EXPERT_GROUPS = 4
EXPERTS_PER_GROUP = 8
N_EXPERTS = N_EXPERT_GROUPS * EXPERTS_PER_GROUP
D_EXPERT = D_MODEL // 4
PAGE_SIZE = 128
EPS = 1e-6
LAM_INIT = 0.8 - 0.6 * math.exp(-0.3 * 0)

LANES = 128
SUBLANES = 8
NEG_BIG = -1e30
VMEM_LIMIT = 56 * 1024 * 1024

COL_Q = 0
COL_K = COL_Q + Q_WIDTH
COL_V = COL_K + K_WIDTH
COL_Z = COL_V + V_WIDTH
COL_X = COL_Z + D_SSM
COL_B = COL_X + D_SSM
COL_C = COL_B + N_GROUPS * D_STATE
COL_GA = COL_C + N_GROUPS * D_STATE
COL_GS = COL_GA + D_MODEL
PROJ_WIDTH = COL_GS + D_MODEL

ALIBI_SLOPES = [2.0 ** (-8.0 * (h + 1) / N_HEADS) for h in range(N_HEADS)]


def _cparams(sem):
    return pltpu.CompilerParams(dimension_semantics=sem, vmem_limit_bytes=VMEM_LIMIT)


def _dot(a, b):
    return jnp.dot(a, b, preferred_element_type=F32)


def _dot_nt(a, b):
    return lax.dot_general(a, b, (((1,), (1,)), ((), ())), preferred_element_type=F32)


def _dot_tn(a, b):
    return lax.dot_general(a, b, (((0,), (0,)), ((), ())), preferred_element_type=F32)


def _split2(x):
    hi = x.astype(BF16)
    lo = (x - hi.astype(F32)).astype(BF16)
    return hi, lo


def _split3(x):
    hi = x.astype(BF16)
    r = x - hi.astype(F32)
    mid = r.astype(BF16)
    lo = (r - mid.astype(F32)).astype(BF16)
    return hi, mid, lo


def _dot_x2(x, sel):
    hi, lo = _split2(x)
    return _dot(hi, sel) + _dot(lo, sel)


def _dot_x3(x, sel):
    hi, mid, lo = _split3(x)
    return _dot(hi, sel) + _dot(mid, sel) + _dot(lo, sel)


def _sigmoid(x):
    return 1.0 / (1.0 + jnp.exp(-x))


def _silu(x):
    return x * _sigmoid(x)


def _softplus(x):
    return jnp.maximum(x, 0.0) + jnp.log1p(jnp.exp(-jnp.abs(x)))


NORM_ROWS = 256


def _proj_kernel(x_ref, nw_ref, wa_ref, wb_ref, wdt_ref, o_ref, dt_ref, u_scr, *, n_a):
    j = pl.program_id(1)

    @pl.when(j == 0)
    def _():
        tm = x_ref.shape[0]
        for lo in range(0, tm, min(tm, NORM_ROWS)):
            hi = lo + min(tm, NORM_ROWS)
            x = x_ref[lo:hi, :]
            ms = jnp.mean(x * x, axis=-1, keepdims=True)
            u_scr[lo:hi, :] = (x * lax.rsqrt(ms + EPS) * nw_ref[...]).astype(BF16)
        dt_ref[...] = _dot(u_scr[...], wdt_ref[...])

    @pl.when(j < n_a)
    def _():
        o_ref[...] = _dot(u_scr[...], wa_ref[...])

    @pl.when(j >= n_a)
    def _():
        o_ref[...] = _dot(u_scr[...], wb_ref[...])


def _proj(x, norm_w, w_a, w_b, w_dt, tm, tn):
    t = x.shape[0]
    n_a = w_a.shape[1] // tn
    return pl.pallas_call(
        functools.partial(_proj_kernel, n_a=n_a),
        grid=(t // tm, PROJ_WIDTH // tn),
        in_specs=[
            pl.BlockSpec((tm, D_MODEL), lambda i, j: (i, 0)),
            pl.BlockSpec((1, D_MODEL), lambda i, j: (0, 0)),
            pl.BlockSpec((D_MODEL, tn), lambda i, j: (0, jnp.minimum(j, n_a - 1))),
            pl.BlockSpec((D_MODEL, tn), lambda i, j: (0, jnp.maximum(j - n_a, 0))),
            pl.BlockSpec((D_MODEL, LANES), lambda i, j: (0, 0)),
        ],
        out_specs=[
            pl.BlockSpec((tm, tn), lambda i, j: (i, j)),
            pl.BlockSpec((tm, LANES), lambda i, j: (i, 0)),
        ],
        out_shape=[
            jax.ShapeDtypeStruct((t, PROJ_WIDTH), F32),
            jax.ShapeDtypeStruct((t, LANES), F32),
        ],
        scratch_shapes=[pltpu.VMEM((tm, D_MODEL), BF16)],
        compiler_params=_cparams(("arbitrary", "arbitrary")),
        name="proj",
    )(x, norm_w, w_a, w_b, w_dt)


LOG2E = math.log2(math.e)
Q_SCALE = LOG2E * HEAD_DIM ** -0.5


def _qknorm_kernel(p_ref, qw_ref, kw_ref, g_ref, qn_ref, kn_ref, kb_ref, vb_ref):
    gsum = g_ref[...]
    n_q = Q_WIDTH // LANES
    for c in range((Q_WIDTH + K_WIDTH) // LANES):
        x = p_ref[:, c * LANES:(c + 1) * LANES]
        ss = _dot_x2(x * x, gsum)
        y = x * lax.rsqrt(ss * (1.0 / HEAD_DIM) + EPS)
        if c < n_q:
            qn_ref[:, c * LANES:(c + 1) * LANES] = (y * qw_ref[...] * Q_SCALE).astype(BF16)
        else:
            kn = y * kw_ref[...]
            kn_ref[:, (c - n_q) * LANES:(c - n_q + 1) * LANES] = kn
            kb_ref[:, (c - n_q) * LANES:(c - n_q + 1) * LANES] = kn.astype(BF16)
    vb_ref[...] = p_ref[:, COL_V:COL_V + V_WIDTH].astype(BF16)


def _qk_norm(proj, q_norm_w, k_norm_w, tm):
    t = proj.shape[0]
    group = np.kron(np.eye(LANES // HEAD_DIM), np.ones((HEAD_DIM, HEAD_DIM)))
    qw = jnp.tile(q_norm_w, LANES // HEAD_DIM).reshape(1, LANES)
    kw = jnp.tile(k_norm_w, LANES // HEAD_DIM).reshape(1, LANES)
    return pl.pallas_call(
        _qknorm_kernel,
        grid=(t // tm,),
        in_specs=[
            pl.BlockSpec((tm, Q_WIDTH + K_WIDTH + V_WIDTH), lambda i: (i, 0)),
            pl.BlockSpec((1, LANES), lambda i: (0, 0)),
            pl.BlockSpec((1, LANES), lambda i: (0, 0)),
            pl.BlockSpec((LANES, LANES), lambda i: (0, 0)),
        ],
        out_specs=[
            pl.BlockSpec((tm, Q_WIDTH), lambda i: (i, 0)),
            pl.BlockSpec((tm, K_WIDTH), lambda i: (i, 0)),
            pl.BlockSpec((tm, K_WIDTH), lambda i: (i, 0)),
            pl.BlockSpec((tm, V_WIDTH), lambda i: (i, 0)),
        ],
        out_shape=[
            jax.ShapeDtypeStruct((t, Q_WIDTH), BF16),
            jax.ShapeDtypeStruct((t, K_WIDTH), F32),
            jax.ShapeDtypeStruct((t, K_WIDTH), BF16),
            jax.ShapeDtypeStruct((t, V_WIDTH), BF16),
        ],
        compiler_params=_cparams(("arbitrary",)),
        name="qk_norm",
    )(proj, qw, kw, jnp.asarray(group, BF16))


def _diff_lambda(lam_ref):
    lamv = lam_ref[...]
    s1 = jnp.sum(lamv[0:1] * lamv[1:2], axis=1, keepdims=True)
    s2 = jnp.sum(lamv[2:3] * lamv[3:4], axis=1, keepdims=True)
    return jnp.exp(s1) - jnp.exp(s2) + LAM_INIT


def _subln(o, w):
    ms = jnp.mean(o * o, axis=-1, keepdims=True)
    return o * lax.rsqrt(ms + EPS) * w * (1.0 - LAM_INIT)


N_SLOPE_PARTS = 3


def _bf16_parts(x, n):
    parts, rem = [], np.float32(x)
    for _ in range(n):
        p = np.float32(rem.astype(ml_dtypes.bfloat16))
        parts.append(float(p))
        rem = np.float32(rem - p)
    return parts


def _alibi_tables(tk):
    qcols = np.zeros((N_HEADS, 16, LANES), np.float32)
    csum = np.zeros((N_HEADS,), np.float32)
    for h, slope in enumerate(ALIBI_SLOPES):
        parts = _bf16_parts(slope * LOG2E, N_SLOPE_PARTS)
        csum[h] = np.float32(sum(np.float32(p) for p in parts))
        for i, p in enumerate(parts):
            qcols[h, :, i] = p * LANES
            qcols[h, :, N_SLOPE_PARTS + i] = p
    pos = np.arange(tk)
    kcols = np.zeros((tk, LANES), np.float32)
    kcols[:, 0:N_SLOPE_PARTS] = (pos // LANES)[:, None]
    kcols[:, N_SLOPE_PARTS:2 * N_SLOPE_PARTS] = (pos % LANES)[:, None]
    return jnp.asarray(qcols, BF16), jnp.asarray(kcols, BF16), jnp.asarray(csum, F32)


def _attn_p_kernel(qi_ref, ki_ref, cf_ref, q_ref, k_ref, v_ref, qc_ref, kc_ref, lam_ref, sw_ref,
                   o_ref, qa_scr, m_scr, l_scr, acc_scr, *, tq):
    g = pl.program_id(0)
    t = pl.program_id(1)
    qi = qi_ref[t]
    ki = ki_ref[t]
    n_sub = GQA_REP * 2
    rows = n_sub * tq
    n_chunk = tq // LANES

    @pl.when(ki == 0)
    def _():
        lane = lax.broadcasted_iota(jnp.int32, (tq, LANES), 1)
        for r in range(GQA_REP):
            qq = q_ref[:, r * LANES:(r + 1) * LANES]
            qc = jnp.broadcast_to(qc_ref[r, 0:1, :], (tq, LANES))
            for c in range(2):
                idx = 2 * r + c
                keep = (lane < HEAD_DIM) if c == 0 else (lane >= HEAD_DIM)
                qa_scr[idx * tq:(idx + 1) * tq, 0:LANES] = jnp.where(keep, qq, jnp.zeros_like(qq))
                qa_scr[idx * tq:(idx + 1) * tq, LANES:2 * LANES] = qc
        m_scr[...] = jnp.full(m_scr.shape, NEG_BIG, F32)
        l_scr[...] = jnp.zeros(l_scr.shape, F32)
        acc_scr[...] = jnp.zeros(acc_scr.shape, F32)

    def step(diag):
        k_aug = jnp.concatenate([k_ref[...], kc_ref[...]], axis=1)
        s_all = _dot_nt(qa_scr[...], k_aug)
        if diag:
            row_in = lax.broadcasted_iota(jnp.int32, (rows, tq), 0) & (tq - 1)
            col = lax.broadcasted_iota(jnp.int32, (rows, tq), 1)
            s_all = jnp.where(col <= row_in, s_all, NEG_BIG)
        block_dist = ((qi - ki) * tq).astype(F32)
        ps, alphas = [], []
        for r in range(GQA_REP):
            off = -cf_ref[g * GQA_REP + r] * block_dist
            lo, hi = r * 2 * tq, (r + 1) * 2 * tq
            chunks = [s_all[lo:hi, j * LANES:(j + 1) * LANES] for j in range(n_chunk)]
            m_prev = m_scr[lo:hi]
            m_blk = jnp.max(functools.reduce(jnp.maximum, chunks), axis=1, keepdims=True) + off
            m_new = jnp.maximum(m_prev, m_blk)
            alpha = jnp.exp2(m_prev - m_new)
            m_sub = m_new - off
            pj = [jnp.exp2(ch - m_sub) for ch in chunks]
            l_scr[lo:hi] = alpha * l_scr[lo:hi] + functools.reduce(jnp.add, pj)
            m_scr[lo:hi] = m_new
            ps.append(jnp.concatenate(pj, axis=1).astype(BF16))
            alphas.append(alpha)
        pv = _dot(jnp.concatenate(ps, axis=0), v_ref[...])
        acc_scr[...] = jnp.concatenate(alphas, axis=0) * acc_scr[...] + pv

    @pl.when(ki < qi)
    def _():
        step(False)

    @pl.when(ki == qi)
    def _():
        step(True)
        lam = _diff_lambda(lam_ref)
        for r in range(GQA_REP):
            i1, i2 = 2 * r * tq, (2 * r + 1) * tq
            l1 = jnp.sum(l_scr[i1:i1 + tq], axis=1, keepdims=True)
            l2 = jnp.sum(l_scr[i2:i2 + tq], axis=1, keepdims=True)
            o = acc_scr[i1:i1 + tq] / l1 - lam * (acc_scr[i2:i2 + tq] / l2)
            o_ref[:, r * LANES:(r + 1) * LANES] = _subln(o, sw_ref[...]).astype(BF16)


def _attn_prompt(qn, kb, vb, lam_vecs, subln_w, tq):
    t = qn.shape[0]
    nq = t // tq
    pairs = [(i, j) for i in range(nq) for j in range(i + 1)]
    qi_tab = jnp.asarray([p[0] for p in pairs], jnp.int32)
    ki_tab = jnp.asarray([p[1] for p in pairs], jnp.int32)
    qcols, kcols, csum = _alibi_tables(tq)
    n_sub = 2 * GQA_REP
    grid_spec = pltpu.PrefetchScalarGridSpec(
        num_scalar_prefetch=3,
        grid=(N_KV_HEADS, len(pairs)),
        in_specs=[
            pl.BlockSpec((tq, GQA_REP * LANES), lambda g, t, qi, ki, cf: (qi[t], g)),
            pl.BlockSpec((tq, LANES), lambda g, t, qi, ki, cf: (ki[t], g)),
            pl.BlockSpec((tq, V_DIM), lambda g, t, qi, ki, cf: (ki[t], g)),
            pl.BlockSpec((GQA_REP, 16, LANES), lambda g, t, qi, ki, cf: (g, 0, 0)),
            pl.BlockSpec((tq, LANES), lambda g, t, qi, ki, cf: (0, 0)),
            pl.BlockSpec((4, HEAD_DIM), lambda g, t, qi, ki, cf: (0, 0)),
            pl.BlockSpec((1, V_DIM), lambda g, t, qi, ki, cf: (0, 0)),
        ],
        out_specs=pl.BlockSpec((tq, GQA_REP * V_DIM), lambda g, t, qi, ki, cf: (qi[t], g)),
        scratch_shapes=[
            pltpu.VMEM((n_sub * tq, 2 * LANES), BF16),
            pltpu.VMEM((n_sub * tq, LANES), F32),
            pltpu.VMEM((n_sub * tq, LANES), F32),
            pltpu.VMEM((n_sub * tq, V_DIM), F32),
        ],
    )
    return pl.pallas_call(
        functools.partial(_attn_p_kernel, tq=tq),
        grid_spec=grid_spec,
        out_shape=jax.ShapeDtypeStruct((t, ATT_WIDTH), BF16),
        compiler_params=_cparams(("arbitrary", "arbitrary")),
        name="attn_p",
    )(qi_tab, ki_tab, csum, qn, kb, vb, qcols, kcols, lam_vecs, subln_w.reshape(1, V_DIM))


PAGES_PER_STEP = 16
PAGE_GROUP = PAGES_PER_STEP
ROWS_S = 2 * 4 * N_HEADS


def _attn_s_kernel(pt_ref, q_ref, d0_ref, mask_ref, sl_ref, bn_ref, kn_ref, vn_ref, lam_ref,
                   sw_ref, *rest, n_steps):
    k_refs = [r.at[0] for r in rest[:PAGES_PER_STEP]]
    v_refs = [r.at[0] for r in rest[PAGES_PER_STEP:2 * PAGES_PER_STEP]]
    o_ref = rest[2 * PAGES_PER_STEP]
    m_scr, l_scr, acc_scr = rest[2 * PAGES_PER_STEP + 1:]
    s_id = pl.program_id(1)

    @pl.when(s_id == 0)
    def _():
        m_scr[...] = jnp.full(m_scr.shape, NEG_BIG, F32)
        l_scr[...] = jnp.zeros(l_scr.shape, F32)
        acc_scr[...] = jnp.zeros(acc_scr.shape, F32)

    q = q_ref[0]

    def update(scores, values):
        m_prev = m_scr[...]
        m_new = m_prev
        for sc in scores:
            m_new = jnp.maximum(m_new, jnp.max(sc, axis=1, keepdims=True))
        alpha = jnp.exp2(m_prev - m_new)
        l_new = alpha * l_scr[...]
        acc = alpha * acc_scr[...]
        for sc, vv in zip(scores, values):
            p = jnp.exp2(sc - m_new)
            l_new = l_new + jnp.sum(p, axis=1, keepdims=True)
            acc = acc + _dot(p.astype(BF16), vv)
        m_scr[...] = m_new
        l_scr[...] = l_new
        acc_scr[...] = acc

    for first in range(0, PAGES_PER_STEP, PAGE_GROUP):
        scores, values = [], []
        for i in range(first, first + PAGE_GROUP):
            page_start = ((s_id * PAGES_PER_STEP + i) * PAGE_SIZE).astype(F32)
            bias = sl_ref[...] * (d0_ref[...] - page_start) + mask_ref[...]
            scores.append(_dot_nt(q, k_refs[i][...].astype(BF16)) + bias)
            values.append(v_refs[i][...].astype(BF16))
        update(scores, values)

    @pl.when(s_id == n_steps - 1)
    def _():
        sc = _dot_nt(q, kn_ref[0].astype(BF16)) + bn_ref[...]
        update([sc], [vn_ref[0].astype(BF16)])
        lam = _diff_lambda(lam_ref)
        half = ROWS_S // 2
        o1 = acc_scr[0:half] / l_scr[0:half]
        o2 = acc_scr[half:ROWS_S] / l_scr[half:ROWS_S]
        o_ref[0] = _subln(o1 - lam * o2, sw_ref[...]).astype(BF16)


def _attn_sample(qn_s, kn_s, v_s, cache_k, cache_v, page_table, lam_vecs, subln_w):
    db, n_pages = page_table.shape
    dec_seq = qn_s.shape[0] // db
    past = n_pages * PAGE_SIZE
    n_steps = n_pages // PAGES_PER_STEP
    page_rows = PAGE_SIZE * N_KV_HEADS
    n_phys = cache_k.shape[0]
    ck = cache_k.reshape(n_phys, page_rows, 2 * HEAD_DIM)
    cv = cache_v.reshape(n_phys, page_rows, V_DIM)

    q5 = qn_s.reshape(db, dec_seq, N_HEADS, 2, HEAD_DIM)
    zeros = jnp.zeros_like(q5[:, :, :, 0])
    q_all = jnp.stack([jnp.concatenate([q5[:, :, :, 0], zeros], axis=-1),
                       jnp.concatenate([zeros, q5[:, :, :, 1]], axis=-1)], axis=1)
    q_all = q_all.reshape(db, ROWS_S, LANES)

    r = np.arange(ROWS_S)
    tok_r = (r % (dec_seq * N_HEADS)) // N_HEADS
    head_r = r % N_HEADS
    slope_r = np.asarray(ALIBI_SLOPES)[head_r] * LOG2E
    c = np.arange(page_rows)
    key_c, grp_c = c // N_KV_HEADS, c % N_KV_HEADS
    same = (head_r[:, None] // GQA_REP) == grp_c[None, :]
    d0 = np.broadcast_to(past + tok_r[:, None] - key_c[None, :], (ROWS_S, page_rows))
    mask = np.where(same, 0.0, NEG_BIG)
    sl = np.broadcast_to(-slope_r[:, None], (ROWS_S, 1))
    cn = np.arange(LANES)
    tok_c, grp_n = cn // N_KV_HEADS, cn % N_KV_HEADS
    ok = ((head_r[:, None] // GQA_REP) == grp_n[None, :]) & (tok_c[None, :] <= tok_r[:, None])
    bn = np.where(ok, -slope_r[:, None] * (tok_r[:, None] - tok_c[None, :]), NEG_BIG)

    new_rows = dec_seq * N_KV_HEADS
    kn_pad = jnp.pad(kn_s.reshape(db, new_rows, LANES), ((0, 0), (0, LANES - new_rows), (0, 0)))
    vn_pad = jnp.pad(v_s.reshape(db, new_rows, LANES), ((0, 0), (0, LANES - new_rows), (0, 0)))

    def const(shape):
        return pl.BlockSpec(shape, lambda b, s, pt: (0,) * len(shape))

    def page_spec(i):
        return pl.BlockSpec(
            (1, page_rows, LANES),
            lambda b, s, pt: (pt[b * n_pages + s * PAGES_PER_STEP + i], 0, 0))

    grid_spec = pltpu.PrefetchScalarGridSpec(
        num_scalar_prefetch=1,
        grid=(db, n_steps),
        in_specs=[
            pl.BlockSpec((1, ROWS_S, LANES), lambda b, s, pt: (b, 0, 0)),
            const((ROWS_S, page_rows)),
            const((ROWS_S, page_rows)),
            const((ROWS_S, 1)),
            const((ROWS_S, LANES)),
            pl.BlockSpec((1, LANES, LANES), lambda b, s, pt: (b, 0, 0)),
            pl.BlockSpec((1, LANES, LANES), lambda b, s, pt: (b, 0, 0)),
            const((4, HEAD_DIM)),
            const((1, V_DIM)),
        ] + [page_spec(i) for i in range(PAGES_PER_STEP)] * 2,
        out_specs=pl.BlockSpec((1, ROWS_S // 2, V_DIM), lambda b, s, pt: (b, 0, 0)),
        scratch_shapes=[
            pltpu.VMEM((ROWS_S, 1), F32),
            pltpu.VMEM((ROWS_S, 1), F32),
            pltpu.VMEM((ROWS_S, V_DIM), F32),
        ],
    )
    o = pl.pallas_call(
        functools.partial(_attn_s_kernel, n_steps=n_steps),
        grid_spec=grid_spec,
        out_shape=jax.ShapeDtypeStruct((db, ROWS_S // 2, V_DIM), BF16),
        compiler_params=_cparams(("arbitrary", "arbitrary")),
        name="attn_s",
    )(page_table.reshape(-1), q_all, jnp.asarray(d0, F32), jnp.asarray(mask, F32),
      jnp.asarray(sl, F32), jnp.asarray(bn, F32), kn_pad, vn_pad, lam_vecs,
      subln_w.reshape(1, V_DIM),
      *([ck] * PAGES_PER_STEP), *([cv] * PAGES_PER_STEP))
    return o.reshape(db * dec_seq, ATT_WIDTH)


HALO = SUBLANES


def _ssd_kernel(xs_ref, b_ref, c_ref, z_ref, dt_ref, dtt_ref, halo_ref, init_ref,
                cw_ref, cb_ref, dtb_ref, dtbt_ref, a_ref, at_ref, dsk_ref, nw_ref,
                tri_ref, trit_ref, exp_ref, sel_ref,
                y_ref, fin_ref, win_scr, state_scr, *, rows_in, n_valid):
    ci = pl.program_id(1)
    n_chunks = pl.num_programs(1)
    lc = SSD_CHUNK
    bc_w = N_GROUPS * D_STATE

    @pl.when(ci == 0)
    def _():
        state_scr[...] = init_ref[0].reshape(D_SSM, D_STATE)
        win_scr[0:HALO, :] = halo_ref[0]

    if rows_in < lc:
        win_scr[HALO:HALO + lc, :] = jnp.zeros((lc, CONV_DIM), F32)
    win_scr[HALO:HALO + rows_in, 0:D_SSM] = xs_ref[0]
    win_scr[HALO:HALO + rows_in, D_SSM:D_SSM + bc_w] = b_ref[0]
    win_scr[HALO:HALO + rows_in, D_SSM + bc_w:CONV_DIM] = c_ref[0]

    acc = cb_ref[...]
    for tap in range(CONV_WIDTH):
        off = HALO - (CONV_WIDTH - 1) + tap
        acc = acc + win_scr[off:off + lc, :] * cw_ref[tap:tap + 1, :]
    conv = _silu(acc)
    win_scr[0:HALO, :] = win_scr[lc:lc + HALO, :]
    xs = conv[:, 0:D_SSM]
    bm = conv[:, D_SSM:D_SSM + bc_w].astype(BF16)
    cm = conv[:, D_SSM + bc_w:CONV_DIM].astype(BF16)

    if rows_in < lc:
        dt_in = jnp.concatenate([dt_ref[0], jnp.zeros((lc - rows_in, LANES), F32)], axis=0)
        dtt_in = jnp.concatenate(
            [dtt_ref[0], jnp.zeros((N_SSM_HEADS, lc - rows_in), F32)], axis=1)
    else:
        dt_in, dtt_in = dt_ref[0], dtt_ref[0]
    rowi = lax.broadcasted_iota(jnp.int32, (lc, LANES), 0)
    coli = lax.broadcasted_iota(jnp.int32, (N_SSM_HEADS, lc), 1)
    dt = jnp.where(rowi < n_valid, _softplus(dt_in + dtb_ref[...]), 0.0)
    dtt = jnp.where(coli < n_valid, _softplus(dtt_in + dtbt_ref[...]), 0.0)
    a_cs = _dot_x3_left(tri_ref[...], dt * a_ref[...])
    a_cst = _dot_x3(dtt * at_ref[...], trit_ref[...])
    a_last = a_cs[lc - 1:lc, :]
    exp_cs = jnp.exp(a_cs)
    exp_rest = jnp.exp(a_last - a_cs)
    expand = exp_ref[...]
    dtx = _dot_x2(dt, expand)
    ecx = _dot_x2(exp_cs, expand)
    erx = _dot_x2(exp_rest, expand)
    xc = xs * dtx
    xcb = xc.astype(BF16)
    xcd = (xc * erx).astype(BF16)

    last_t = jnp.exp(a_cst[:, lc - 1:lc])
    rdec = _dot_x2_left(sel_ref[...], jnp.broadcast_to(last_t, (N_SSM_HEADS, D_STATE)))

    tril = (lax.broadcasted_iota(jnp.int32, (lc, lc), 0)
            >= lax.broadcasted_iota(jnp.int32, (lc, lc), 1))
    lane = lax.broadcasted_iota(jnp.int32, (lc, LANES), 1)
    gw = HEADS_PER_GROUP * SSM_HEAD_DIM
    y_parts = []
    for g in range(N_GROUPS):
        bg = bm[:, g * D_STATE:(g + 1) * D_STATE]
        cg = cm[:, g * D_STATE:(g + 1) * D_STATE]
        cb = _dot_nt(cg, bg)
        st = state_scr[g * gw:(g + 1) * gw, :]
        y_off = _dot_nt(cg, st.astype(BF16)) * ecx[:, g * gw:(g + 1) * gw]
        new_st = _dot_tn(xcd[:, g * gw:(g + 1) * gw], bg)
        state_scr[g * gw:(g + 1) * gw, :] = st * rdec[g * gw:(g + 1) * gw, :] + new_st
        for j in range(HEADS_PER_GROUP // 2):
            pair = g * (HEADS_PER_GROUP // 2) + j
            blk = xcb[:, pair * LANES:(pair + 1) * LANES]
            y_pair = None
            for half in range(2):
                h = 2 * pair + half
                seg = a_cs[:, h:h + 1] - a_cst[h:h + 1, :]
                decay = jnp.exp(jnp.where(tril, seg, NEG_BIG))
                mh = (cb * decay).astype(BF16)
                keep = (lane < SSM_HEAD_DIM) if half == 0 else (lane >= SSM_HEAD_DIM)
                part = _dot(mh, jnp.where(keep, blk, jnp.zeros_like(blk)))
                y_pair = part if y_pair is None else y_pair + part
            y_parts.append(y_pair + y_off[:, (pair % (HEADS_PER_GROUP // 2)) * LANES:
                                          (pair % (HEADS_PER_GROUP // 2) + 1) * LANES])
    y = jnp.concatenate(y_parts, axis=1)
    y = y + dsk_ref[...] * xs
    if rows_in < lc:
        z = jnp.concatenate([z_ref[0], jnp.zeros((lc - rows_in, D_SSM), F32)], axis=0)
    else:
        z = z_ref[0]
    y = y * _silu(z)
    gn = D_SSM // N_GROUPS
    outs = []
    for g in range(N_GROUPS):
        yg = y[:, g * gn:(g + 1) * gn]
        ms = jnp.mean(yg * yg, axis=-1, keepdims=True)
        outs.append(yg * lax.rsqrt(ms + EPS) * nw_ref[:, g * gn:(g + 1) * gn])
    out = jnp.concatenate(outs, axis=1).astype(BF16)
    y_ref[0] = out[0:rows_in]

    @pl.when(ci == n_chunks - 1)
    def _():
        fin_ref[0] = state_scr[...].reshape(N_SSM_HEADS, SSM_HEAD_DIM, D_STATE)


def _dot_x3_left(sel, x):
    hi, mid, lo = _split3(x)
    return _dot(sel, hi) + _dot(sel, mid) + _dot(sel, lo)


def _dot_x2_left(sel, x):
    hi, lo = _split2(x)
    return _dot(sel, hi) + _dot(sel, lo)


def _ssd(src, col_blocks, dt_raw, halo, init_state, prm, rows_in, n_valid):
    nb, seq = src.shape[0], src.shape[1]
    n_chunks = max(1, seq // SSD_CHUNK)
    bc_w = N_GROUPS * D_STATE
    dtt = jnp.swapaxes(dt_raw[:, :, :N_SSM_HEADS], 1, 2)
    tri = np.tril(np.ones((SSD_CHUNK, SSD_CHUNK)))
    expand = np.zeros((LANES, D_SSM))
    expand[np.arange(D_SSM) // SSM_HEAD_DIM, np.arange(D_SSM)] = 1.0
    sel = expand[:N_SSM_HEADS].T
    cx, cbk, cck, cz = col_blocks

    def const(shape):
        return pl.BlockSpec(shape, lambda b, c: (0,) * len(shape))

    return pl.pallas_call(
        functools.partial(_ssd_kernel, rows_in=rows_in, n_valid=n_valid),
        grid=(nb, n_chunks),
        in_specs=[
            pl.BlockSpec((1, rows_in, D_SSM), lambda b, c: (b, c, cx)),
            pl.BlockSpec((1, rows_in, bc_w), lambda b, c: (b, c, cbk)),
            pl.BlockSpec((1, rows_in, bc_w), lambda b, c: (b, c, cck)),
            pl.BlockSpec((1, rows_in, D_SSM), lambda b, c: (b, c, cz)),
            pl.BlockSpec((1, rows_in, LANES), lambda b, c: (b, c, 0)),
            pl.BlockSpec((1, N_SSM_HEADS, rows_in), lambda b, c: (b, 0, c)),
            pl.BlockSpec((1, HALO, CONV_DIM), lambda b, c: (b, 0, 0)),
            pl.BlockSpec((1, N_SSM_HEADS, SSM_HEAD_DIM, D_STATE), lambda b, c: (b, 0, 0, 0)),
            const((CONV_WIDTH, CONV_DIM)),
            const((1, CONV_DIM)),
            const((1, LANES)),
            const((N_SSM_HEADS, 1)),
            const((1, LANES)),
            const((N_SSM_HEADS, 1)),
            const((1, D_SSM)),
            const((1, D_SSM)),
            const((SSD_CHUNK, SSD_CHUNK)),
            const((SSD_CHUNK, SSD_CHUNK)),
            const((LANES, D_SSM)),
            const((D_SSM, N_SSM_HEADS)),
        ],
        out_specs=[
            pl.BlockSpec((1, rows_in, D_SSM), lambda b, c: (b, c, 0)),
            pl.BlockSpec((1, N_SSM_HEADS, SSM_HEAD_DIM, D_STATE), lambda b, c: (b, 0, 0, 0)),
        ],
        out_shape=[
            jax.ShapeDtypeStruct((nb, seq, D_SSM), BF16),
            jax.ShapeDtypeStruct((nb, N_SSM_HEADS, SSM_HEAD_DIM, D_STATE), F32),
        ],
        scratch_shapes=[
            pltpu.VMEM((HALO + SSD_CHUNK, CONV_DIM), F32),
            pltpu.VMEM((D_SSM, D_STATE), F32),
        ],
        compiler_params=_cparams(("arbitrary", "arbitrary")),
        name="ssd",
    )(src, src, src, src, dt_raw, dtt, halo, init_state,
      prm["conv_w"], prm["conv_b"], prm["dt_bias"], prm["dt_bias_t"], prm["a"], prm["a_t"],
      prm["d_skip"], prm["ssm_norm_w"],
      jnp.asarray(tri, BF16), jnp.asarray(tri.T, BF16), jnp.asarray(expand, BF16),
      jnp.asarray(sel, BF16))


def _merge_kernel(o_ref, s_ref, wa_ref, ws_ref, ga_ref, gs_ref, out_ref):
    a = _dot(o_ref[...], wa_ref[...])
    s = _dot(s_ref[...], ws_ref[...])
    out_ref[...] = (_sigmoid(ga_ref[...]) * a + _sigmoid(gs_ref[...]) * s).astype(BF16)


def _merge(o, s, wa, ws, proj, tm, tn):
    t = o.shape[0]
    ga0, gs0 = COL_GA // tn, COL_GS // tn
    return pl.pallas_call(
        _merge_kernel,
        grid=(t // tm, D_MODEL // tn),
        in_specs=[
            pl.BlockSpec((tm, ATT_WIDTH), lambda i, j: (i, 0)),
            pl.BlockSpec((tm, D_SSM), lambda i, j: (i, 0)),
            pl.BlockSpec((ATT_WIDTH, tn), lambda i, j: (0, j)),
            pl.BlockSpec((D_SSM, tn), lambda i, j: (0, j)),
            pl.BlockSpec((tm, tn), lambda i, j: (i, ga0 + j)),
            pl.BlockSpec((tm, tn), lambda i, j: (i, gs0 + j)),
        ],
        out_specs=pl.BlockSpec((tm, tn), lambda i, j: (i, j)),
        out_shape=jax.ShapeDtypeStruct((t, D_MODEL), BF16),
        compiler_params=_cparams(("arbitrary", "arbitrary")),
        name="merge",
    )(o, s, wa, ws, proj, proj)


ROUTE_E1, ROUTE_E2, ROUTE_W1, ROUTE_W2 = 0, 1, 2, 3
ROW_CHUNKS = D_MODEL // LANES


def _resid_kernel(x_ref, m_ref, wo_ref, nw_ref, wrh_ref, wrl_ref, br_ref, *rest, n_real):
    h_ref, u_ref, route_ref = rest[-3:]

    @pl.when(pl.program_id(0) >= n_real)
    def _():
        h_ref[...] = jnp.zeros(h_ref.shape, F32)
        u_ref[...] = jnp.zeros(u_ref.shape, F32)
        route_ref[...] = jnp.zeros(route_ref.shape, F32)

    @pl.when(pl.program_id(0) < n_real)
    def _():
        _resid_tile(x_ref, m_ref, wo_ref, nw_ref, wrh_ref, wrl_ref, br_ref,
                    h_ref, u_ref, route_ref)


def _resid_tile(x_ref, m_ref, wo_ref, nw_ref, wrh_ref, wrl_ref, br_ref, h_ref, u_ref, route_ref):
    h = x_ref[...] + _dot(m_ref[...], wo_ref[...])
    h_ref[...] = h
    ms = jnp.mean(h * h, axis=-1, keepdims=True)
    u = h * lax.rsqrt(ms + EPS) * nw_ref[...]
    u_hi, u_lo = _split2(u)
    u_ref[...] = u
    logits = (_dot(u_hi, wrh_ref[...]) + _dot(u_lo, wrh_ref[...])
              + _dot(u_hi, wrl_ref[...]) + br_ref[...])
    lane = lax.broadcasted_iota(jnp.int32, logits.shape, 1)
    lane_f = lane.astype(F32)
    far = float(2 * LANES)

    def first_max(vals):
        top = jnp.max(vals, axis=1, keepdims=True)
        idx = jnp.min(jnp.where(vals == top, lane_f, far), axis=1, keepdims=True)
        return top, idx

    is_group = (lane >= N_EXPERTS) & (lane < N_EXPERTS + N_EXPERT_GROUPS)
    gl = jnp.where(is_group, logits, NEG_BIG)
    g_top, g_idx = first_max(gl)
    g_p = 1.0 / jnp.sum(jnp.exp(gl - g_top), axis=1, keepdims=True)
    lo_lane = (g_idx - N_EXPERTS) * EXPERTS_PER_GROUP
    in_group = (lane_f >= lo_lane) & (lane_f < lo_lane + EXPERTS_PER_GROUP)
    el = jnp.where(in_group, logits, NEG_BIG)
    m1, i1 = first_max(el)
    el2 = jnp.where(lane_f == i1, NEG_BIG, el)
    m2, i2 = first_max(el2)
    e = jnp.exp(m2 - m1)
    w1 = 1.0 / (1.0 + e)
    w2 = e / (1.0 + e)
    route = jnp.where(lane == ROUTE_E1, i1, 0.0)
    route = jnp.where(lane == ROUTE_E2, i2, route)
    route = jnp.where(lane == ROUTE_W1, g_p * w1, route)
    route_ref[...] = jnp.where(lane == ROUTE_W2, g_p * w2, route)


def _resid(x, merged, wo, norm_w, wr_hi, wr_lo, br, tm, t_all, row_off, bufs):
    t = x.shape[0]
    blk_off = row_off // tm
    n_real = t // tm
    n_fill = pl.cdiv(t_all - t, tm) if bufs is None else 0

    def const(shape):
        return pl.BlockSpec(shape, lambda i: (0,) * len(shape))

    in_specs = [
        pl.BlockSpec((tm, D_MODEL), lambda i: (jnp.minimum(i, n_real - 1), 0)),
        pl.BlockSpec((tm, D_MODEL), lambda i: (jnp.minimum(i, n_real - 1), 0)),
        const((D_MODEL, D_MODEL)),
        const((1, D_MODEL)),
        const((D_MODEL, LANES)),
        const((D_MODEL, LANES)),
        const((1, LANES)),
    ]
    args = [x, merged, wo, norm_w, wr_hi, wr_lo, br]
    aliases = {}
    if bufs is not None:
        aliases = {len(args) + k: k for k in range(len(bufs))}
        in_specs += [pl.BlockSpec(memory_space=pl.ANY)] * len(bufs)
        args += list(bufs)
    return pl.pallas_call(
        functools.partial(_resid_kernel, n_real=n_real),
        grid=(n_real + n_fill,),
        in_specs=in_specs,
        out_specs=[
            pl.BlockSpec((tm, D_MODEL), lambda i: (i + blk_off, 0)),
            pl.BlockSpec((tm, D_MODEL), lambda i: (i + blk_off, 0)),
            pl.BlockSpec((tm, LANES), lambda i: (i + blk_off, 0)),
        ],
        out_shape=[
            jax.ShapeDtypeStruct((t_all, D_MODEL), F32),
            jax.ShapeDtypeStruct((t_all, D_MODEL), F32),
            jax.ShapeDtypeStruct((t_all, LANES), F32),
        ],
        input_output_aliases=aliases,
        compiler_params=_cparams(("arbitrary",)),
        name="resid",
    )(*args)


MOE_TILE = 256
COMB_TILE = 128
DMA_UNROLL = 8


def _route_plan(route, n_tiles):
    n_pairs = 2 * route.shape[0]
    pair_e = route[:, ROUTE_E1:ROUTE_E2 + 1].astype(jnp.int32).reshape(-1)
    onehot = (pair_e[:, None] == jnp.arange(N_EXPERTS, dtype=jnp.int32)[None, :]).astype(jnp.int32)
    csum = jnp.cumsum(onehot, axis=0)
    rank = jnp.sum((csum - onehot) * onehot, axis=1)
    tiles_e = (csum[-1] + MOE_TILE - 1) // MOE_TILE
    tile_end = jnp.cumsum(tiles_e)
    first_row = (tile_end - tiles_e) * MOE_TILE
    slot = jnp.sum(onehot * first_row[None, :], axis=1) + rank
    tok_of_slot = jnp.zeros(((n_tiles + 1) * MOE_TILE,), jnp.int32).at[slot].set(
        jnp.arange(n_pairs, dtype=jnp.int32) // 2)
    tile_start = (tile_end - tiles_e).astype(jnp.int32)
    slot_tab = slot.reshape(-1, COMB_TILE, 2).transpose(0, 2, 1).reshape(-1).astype(jnp.int32)
    return tile_start, tiles_e.astype(jnp.int32), tok_of_slot, slot_tab


ROW_DMA_PRIORITY = 0
BULK_DMA_PRIORITY = 1


def _gmm_kernel(ts_ref, ne_ref, tok_ref, u_hbm, wg_hbm, wu_hbm, wd_hbm, o_hbm,
                xbuf, obuf, gsem, osem, wgb, wub, wdb, wg_buf, wu_buf, wd_buf, wsem,
                *, n_tiles):
    e = pl.program_id(0)
    last = pl.num_programs(0) - 1
    n_used = ts_ref[last] + ne_ref[last]

    def row_copy(tile, r, buf):
        tok = tok_ref[tile * MOE_TILE + r]
        return pltpu.make_async_copy(
            u_hbm.at[tok], xbuf.at[buf, r // SUBLANES, :, r % SUBLANES, :], gsem.at[buf])

    def out_copy(tile, buf):
        return pltpu.make_async_copy(
            obuf.at[buf], o_hbm.at[pl.ds(tile * MOE_TILE, MOE_TILE), :], osem.at[buf])

    def gather_wait(tile, buf):
        def body(r, carry):
            row_copy(tile, r, buf).wait()
            return carry
        lax.fori_loop(0, MOE_TILE, body, 0, unroll=DMA_UNROLL)

    @pl.when(e == 0)
    def _():
        def body(r, carry):
            row_copy(0, r, 0).start(priority=ROW_DMA_PRIORITY)
            return carry
        lax.fori_loop(0, MOE_TILE, body, 0, unroll=DMA_UNROLL)

    def weight_copies(expert, buf):
        return (pltpu.make_async_copy(wg_hbm.at[expert], wg_buf.at[buf], wsem.at[buf]),
                pltpu.make_async_copy(wu_hbm.at[expert], wu_buf.at[buf], wsem.at[buf]),
                pltpu.make_async_copy(wd_hbm.at[expert], wd_buf.at[buf], wsem.at[buf]))

    wcur = lax.rem(e, 2)

    @pl.when(e == 0)
    def _():
        for cp in weight_copies(0, 0):
            cp.start(priority=BULK_DMA_PRIORITY)

    @pl.when(e < last)
    def _():
        for cp in weight_copies(e + 1, 1 - wcur):
            cp.start(priority=BULK_DMA_PRIORITY)

    for cp in weight_copies(e, wcur):
        cp.wait()
    wgb[...] = wg_buf[wcur].astype(BF16)
    wub[...] = wu_buf[wcur].astype(BF16)
    wdb[...] = wd_buf[wcur].astype(BF16)

    def tile_body(j, carry):
        t = ts_ref[e] + j
        cur = lax.rem(t, 2)

        @pl.when(t >= 2)
        def _():
            out_copy(t - 2, cur).wait()

        gather_wait(t, cur)
        x = jnp.concatenate(
            [xbuf[cur, :, c].reshape(MOE_TILE, LANES) for c in range(ROW_CHUNKS)],
            axis=1).astype(BF16)
        for r in range(MOE_TILE):
            row_copy(t + 1, r, 1 - cur).start(priority=ROW_DMA_PRIORITY)
        hid = _silu(_dot(x, wgb[...])) * _dot(x, wub[...])
        obuf[cur] = _dot(hid.astype(BF16), wdb[...])
        out_copy(t, cur).start(priority=BULK_DMA_PRIORITY)
        return carry

    lax.fori_loop(0, ne_ref[e], tile_body, 0)

    @pl.when(e == last)
    def _():
        gather_wait(n_used, lax.rem(n_used, 2))
        out_copy(n_used - 2, lax.rem(n_used, 2)).wait()
        out_copy(n_used - 1, lax.rem(n_used - 1, 2)).wait()
        obuf[0] = jnp.zeros(obuf.shape[1:], F32)

        def fill(t, carry):
            cp = out_copy(t, 0)
            cp.start()
            cp.wait()
            return carry
        lax.fori_loop(n_used, n_tiles + 1, fill, 0)


def _gmm(u_all, plan, wg, wu, wd, n_tiles):
    tile_start, tiles_e, tok_of_slot, _ = plan
    assert 2 * u_all.shape[0] >= 2 * MOE_TILE
    grid_spec = pltpu.PrefetchScalarGridSpec(
        num_scalar_prefetch=3,
        grid=(N_EXPERTS,),
        in_specs=[
            pl.BlockSpec(memory_space=pl.ANY),
            pl.BlockSpec(memory_space=pl.ANY),
            pl.BlockSpec(memory_space=pl.ANY),
            pl.BlockSpec(memory_space=pl.ANY),
        ],
        out_specs=pl.BlockSpec(memory_space=pl.ANY),
        scratch_shapes=[
            pltpu.VMEM((2, MOE_TILE // SUBLANES, ROW_CHUNKS, SUBLANES, LANES), F32),
            pltpu.VMEM((2, MOE_TILE, D_MODEL), F32),
            pltpu.SemaphoreType.DMA((2,)),
            pltpu.SemaphoreType.DMA((2,)),
            pltpu.VMEM((D_MODEL, D_EXPERT), BF16),
            pltpu.VMEM((D_MODEL, D_EXPERT), BF16),
            pltpu.VMEM((D_EXPERT, D_MODEL), BF16),
            pltpu.VMEM((2, D_MODEL, D_EXPERT), F32),
            pltpu.VMEM((2, D_MODEL, D_EXPERT), F32),
            pltpu.VMEM((2, D_EXPERT, D_MODEL), F32),
            pltpu.SemaphoreType.DMA((2,)),
        ],
    )
    return pl.pallas_call(
        functools.partial(_gmm_kernel, n_tiles=n_tiles),
        grid_spec=grid_spec,
        out_shape=jax.ShapeDtypeStruct(((n_tiles + 1) * MOE_TILE, D_MODEL), F32),
        compiler_params=_cparams(("arbitrary",)),
        name="gmm",
    )(tile_start, tiles_e, tok_of_slot, u_all, wg, wu, wd)


def _combine_kernel(slot_ref, route_ref, h_ref, o_hbm, yp_ref, ys_ref, gbuf, sem, *, n_prompt):
    i = pl.program_id(0)
    n = pl.num_programs(0)
    cur = lax.rem(i, 2)
    rows = 2 * COMB_TILE

    def row_copy(tile, j, buf):
        slot = slot_ref[tile * rows + j]
        return pltpu.make_async_copy(
            o_hbm.at[pl.ds(slot, 1), :], gbuf.at[buf, pl.ds(j, 1), :], sem.at[buf])

    def issue(tile, buf):
        def body(j, carry):
            row_copy(tile, j, buf).start()
            return carry
        lax.fori_loop(0, rows, body, 0, unroll=DMA_UNROLL)

    def wait(tile, buf):
        def body(j, carry):
            row_copy(tile, j, buf).wait()
            return carry
        lax.fori_loop(0, rows, body, 0, unroll=DMA_UNROLL)

    @pl.when(i == 0)
    def _():
        issue(0, 0)

    @pl.when(i + 1 < n)
    def _():
        issue(i + 1, 1 - cur)

    wait(i, cur)
    w1 = route_ref[:, ROUTE_W1:ROUTE_W1 + 1]
    w2 = route_ref[:, ROUTE_W2:ROUTE_W2 + 1]
    y = h_ref[...] + w1 * gbuf[cur, 0:COMB_TILE, :] + w2 * gbuf[cur, COMB_TILE:rows, :]

    @pl.when(i < n_prompt)
    def _():
        yp_ref[...] = y

    @pl.when(i >= n_prompt)
    def _():
        ys_ref[...] = y


def _combine(route, h_all, o_sorted, plan, t_prompt):
    t_all = h_all.shape[0]
    n_prompt = t_prompt // COMB_TILE
    slot_tab = plan[3]
    grid_spec = pltpu.PrefetchScalarGridSpec(
        num_scalar_prefetch=1,
        grid=(t_all // COMB_TILE,),
        in_specs=[
            pl.BlockSpec((COMB_TILE, LANES), lambda i, st: (i, 0)),
            pl.BlockSpec((COMB_TILE, D_MODEL), lambda i, st: (i, 0)),
            pl.BlockSpec(memory_space=pl.ANY),
        ],
        out_specs=[
            pl.BlockSpec((COMB_TILE, D_MODEL), lambda i, st: (jnp.minimum(i, n_prompt - 1), 0)),
            pl.BlockSpec((COMB_TILE, D_MODEL), lambda i, st: (0, 0)),
        ],
        scratch_shapes=[
            pltpu.VMEM((2, 2 * COMB_TILE, D_MODEL), F32),
            pltpu.SemaphoreType.DMA((2,)),
        ],
    )
    return pl.pallas_call(
        functools.partial(_combine_kernel, n_prompt=n_prompt),
        grid_spec=grid_spec,
        out_shape=[
            jax.ShapeDtypeStruct((t_prompt, D_MODEL), F32),
            jax.ShapeDtypeStruct((t_all - t_prompt, D_MODEL), F32),
        ],
        compiler_params=_cparams(("arbitrary",)),
        name="combine",
    )(slot_tab, route, h_all, o_sorted)


def _layer_tokens(x2d, w, tm_proj, tm_small):
    proj, dt_raw = _proj(x2d, w["norm_attn_w"], w["w_a"], w["w_b"], w["w_dt"], tm_proj, 1024)
    qn, kn, kb, vb = _qk_norm(proj, w["q_norm_w"], w["k_norm_w"], tm_small)
    return proj, dt_raw, qn, kn, kb, vb


def _branch_merge(x2d, o, s, proj, w, tm, tm_resid, t_all, row_off, bufs):
    merged = _merge(o, s, w["w_att_out"], w["w_ssm_out"], proj, tm, 512)
    return _resid(x2d, merged, w["w_o"], w["norm_ffn_w"], w["wr_hi"], w["wr_lo"], w["br"],
                  tm_resid, t_all, row_off, bufs)


def kernel(x_prompt, x_sample, cache_k, cache_v, state_ssm, state_conv, page_table, norm_attn_w, w_in, q_norm_w, k_norm_w, lambda_q1, lambda_k1, lambda_q2, lambda_k2, subln_w, w_att_out, conv_w, conv_b, dt_bias, a_log, d_skip, ssm_norm_w, w_ssm_out, w_o, norm_ffn_w, w_group_router, b_group_router, w_expert_router, b_expert_router, w_gate, w_up, w_down):
    layer = 0
    nb, seq, _ = x_prompt.shape
    db, dec_seq, _ = x_sample.shape

    w_in_l = w_in[layer]
    c_dt = Q_WIDTH + K_WIDTH + V_WIDTH + D_SSM + CONV_DIM
    w_a = w_in_l[:, :c_dt].astype(BF16)
    w_b = w_in_l[:, c_dt + N_SSM_HEADS:].astype(BF16)
    w_dt = jnp.pad(w_in_l[:, c_dt:c_dt + N_SSM_HEADS], ((0, 0), (0, LANES - N_SSM_HEADS))).astype(BF16)
    wr = jnp.concatenate([w_expert_router[layer], w_group_router[layer]], axis=1)
    wr = jnp.pad(wr, ((0, 0), (0, LANES - wr.shape[1])))
    wr_hi = wr.astype(BF16)
    wr_lo = (wr - wr_hi.astype(F32)).astype(BF16)
    br = jnp.concatenate([b_expert_router[layer], b_group_router[layer]])
    br = jnp.pad(br, (0, LANES - br.shape[0])).reshape(1, LANES)
    pad_h = (0, LANES - N_SSM_HEADS)
    w = dict(
        norm_attn_w=norm_attn_w[layer].reshape(1, D_MODEL), w_a=w_a, w_b=w_b, w_dt=w_dt,
        q_norm_w=q_norm_w[layer], k_norm_w=k_norm_w[layer],
        w_att_out=w_att_out[layer].astype(BF16), w_ssm_out=w_ssm_out[layer].astype(BF16),
        w_o=w_o[layer].astype(BF16), norm_ffn_w=norm_ffn_w[layer].reshape(1, D_MODEL),
        wr_hi=wr_hi, wr_lo=wr_lo, br=br,
        w_gate=w_gate[layer], w_up=w_up[layer], w_down=w_down[layer],
    )
    ssm_prm = dict(
        conv_w=conv_w[layer], conv_b=conv_b[layer].reshape(1, CONV_DIM),
        dt_bias=jnp.pad(dt_bias[layer], pad_h).reshape(1, LANES),
        dt_bias_t=dt_bias[layer].reshape(N_SSM_HEADS, 1),
        a=jnp.pad(-jnp.exp(a_log[layer]), pad_h).reshape(1, LANES),
        a_t=(-jnp.exp(a_log[layer])).reshape(N_SSM_HEADS, 1),
        d_skip=jnp.repeat(d_skip[layer], SSM_HEAD_DIM).reshape(1, D_SSM),
        ssm_norm_w=ssm_norm_w[layer].reshape(1, D_SSM),
    )
    lam_vecs = jnp.stack([lambda_q1[layer], lambda_k1[layer], lambda_q2[layer], lambda_k2[layer]])
    sw = subln_w[layer]
    ssd_cols = (COL_X // D_SSM, COL_B // (N_GROUPS * D_STATE), COL_C // (N_GROUPS * D_STATE),
                COL_Z // D_SSM)

    xp = x_prompt.reshape(nb * seq, D_MODEL)
    n_tok = db * dec_seq
    t_prompt = nb * seq
    t_all = t_prompt + n_tok
    proj_p, dt_p, qn_p, kn_p, kb_p, vb_p = _layer_tokens(xp, w, 1024, 512)
    o_p = _attn_prompt(qn_p, kb_p, vb_p, lam_vecs, sw, 512)
    s_p, ssm_p = _ssd(
        proj_p.reshape(nb, seq, PROJ_WIDTH), ssd_cols, dt_p.reshape(nb, seq, LANES),
        jnp.zeros((nb, HALO, CONV_DIM), F32), jnp.zeros((nb, N_SSM_HEADS, SSM_HEAD_DIM, D_STATE), F32),
        ssm_prm, SSD_CHUNK, SSD_CHUNK)
    bufs = _branch_merge(xp, o_p, s_p.reshape(t_prompt, D_SSM), proj_p, w, 1024, 256,
                         t_all, 0, None)
    keep = CONV_WIDTH - 1
    conv_p = proj_p.reshape(nb, seq, PROJ_WIDTH)[:, seq - keep:, COL_X:COL_X + CONV_DIM]

    xs = x_sample.reshape(db * dec_seq, D_MODEL)
    proj_s, dt_s, qn_s, kn_s, _, _ = _layer_tokens(xs, w, n_tok, n_tok)
    v_s = proj_s[:, COL_V:COL_V + V_WIDTH]
    o_s = _attn_sample(qn_s, kn_s, v_s, cache_k[layer], cache_v[layer], page_table, lam_vecs, sw)
    rows_s = SUBLANES
    pad_rows = ((0, 0), (0, rows_s - dec_seq), (0, 0))
    src_s = jnp.pad(proj_s[:, COL_Z:COL_GA].reshape(db, dec_seq, COL_GA - COL_Z), pad_rows)
    halo_s = jnp.pad(state_conv[layer], ((0, 0), (HALO - (CONV_WIDTH - 1), 0), (0, 0)))
    cols_s = ((COL_X - COL_Z) // D_SSM, (COL_B - COL_Z) // (N_GROUPS * D_STATE),
              (COL_C - COL_Z) // (N_GROUPS * D_STATE), 0)
    s_s, ssm_s = _ssd(
        src_s, cols_s, jnp.pad(dt_s.reshape(db, dec_seq, LANES), pad_rows), halo_s,
        state_ssm[layer], ssm_prm, rows_s, dec_seq)
    s_s = s_s[:, :dec_seq].reshape(n_tok, D_SSM)
    h_all, u_all, route = _branch_merge(xs, o_s, s_s, proj_s, w, n_tok, n_tok,
                                        t_all, t_prompt, bufs)

    n_tiles = 2 * t_all // MOE_TILE + N_EXPERTS
    plan = _route_plan(route, n_tiles)
    o_sorted = _gmm(u_all.reshape(t_all, ROW_CHUNKS, LANES), plan, w["w_gate"], w["w_up"], w["w_down"], n_tiles)
    y_p, y_s = _combine(route, h_all, o_sorted, plan, t_prompt)
    conv_s =proj_s.reshape(db, dec_seq, PROJ_WIDTH)[:, dec_seq - keep:, COL_X:COL_X + CONV_DIM]

    return (
        y_p.reshape(nb, seq, D_MODEL),
        y_s.reshape(db, dec_seq, D_MODEL),
        kn_p.reshape(1, nb, seq, N_KV_HEADS, 2 * HEAD_DIM),
        proj_p[:, COL_V:COL_V + V_WIDTH].reshape(1, nb, seq, N_KV_HEADS, V_DIM),
        ssm_p.reshape(1, nb, N_SSM_HEADS, SSM_HEAD_DIM, D_STATE),
        conv_p[None],
        kn_s.reshape(1, db, dec_seq, N_KV_HEADS, 2 * HEAD_DIM),
        v_s.reshape(1, db, dec_seq, N_KV_HEADS, V_DIM),
        ssm_s.reshape(1, db, N_SSM_HEADS, SSM_HEAD_DIM, D_STATE),
        conv_s[None],
    )
```

```python
import functools
import math

import jax
import jax.numpy as jnp
import ml_dtypes
import numpy as np
from jax import lax
from jax.experimental import pallas as pl
from jax.experimental.pallas import tpu as pltpu

F32 = jnp.float32
BF16 = jnp.bfloat16

D_MODEL = 2048
N_HEADS = 8
N_KV_HEADS = 4
GQA_REP = N_HEADS // N_KV_HEADS
HEAD_DIM = 64
V_DIM = 2 * HEAD_DIM
Q_WIDTH = N_HEADS * 2 * HEAD_DIM
K_WIDTH = N_KV_HEADS * 2 * HEAD_DIM
V_WIDTH = N_KV_HEADS * V_DIM
ATT_WIDTH = N_HEADS * V_DIM
D_SSM = D_MODEL
SSM_HEAD_DIM = 64
N_SSM_HEADS = D_SSM // SSM_HEAD_DIM
N_GROUPS = 4
HEADS_PER_GROUP = N_SSM_HEADS // N_GROUPS
D_STATE = 128
CONV_WIDTH = 4
CONV_DIM = D_SSM + 2 * N_GROUPS * D_STATE
SSD_CHUNK = 128
N_EXPERT_GROUPS = 4
EXPERTS_PER_GROUP = 8
N_EXPERTS = N_EXPERT_GROUPS * EXPERTS_PER_GROUP
D_EXPERT = D_MODEL // 4
PAGE_SIZE = 128
EPS = 1e-6
LAM_INIT = 0.8 - 0.6 * math.exp(-0.3 * 0)

LANES = 128
SUBLANES = 8
NEG_BIG = -1e30
VMEM_LIMIT = 56 * 1024 * 1024

COL_Q = 0
COL_K = COL_Q + Q_WIDTH
COL_V = COL_K + K_WIDTH
COL_Z = COL_V + V_WIDTH
COL_X = COL_Z + D_SSM
COL_B = COL_X + D_SSM
COL_C = COL_B + N_GROUPS * D_STATE
COL_GA = COL_C + N_GROUPS * D_STATE
COL_GS = COL_GA + D_MODEL
PROJ_WIDTH = COL_GS + D_MODEL

ALIBI_SLOPES = [2.0 ** (-8.0 * (h + 1) / N_HEADS) for h in range(N_HEADS)]


def _cparams(sem):
    return pltpu.CompilerParams(dimension_semantics=sem, vmem_limit_bytes=VMEM_LIMIT)


def _dot(a, b):
    return jnp.dot(a, b, preferred_element_type=F32)


def _dot_nt(a, b):
    return lax.dot_general(a, b, (((1,), (1,)), ((), ())), preferred_element_type=F32)


def _dot_tn(a, b):
    return lax.dot_general(a, b, (((0,), (0,)), ((), ())), preferred_element_type=F32)


def _split2(x):
    hi = x.astype(BF16)
    lo = (x - hi.astype(F32)).astype(BF16)
    return hi, lo


def _split3(x):
    hi = x.astype(BF16)
    r = x - hi.astype(F32)
    mid = r.astype(BF16)
    lo = (r - mid.astype(F32)).astype(BF16)
    return hi, mid, lo


def _dot_x2(x, sel):
    hi, lo = _split2(x)
    return _dot(hi, sel) + _dot(lo, sel)


def _dot_x3(x, sel):
    hi, mid, lo = _split3(x)
    return _dot(hi, sel) + _dot(mid, sel) + _dot(lo, sel)


def _sigmoid(x):
    return 1.0 / (1.0 + jnp.exp(-x))


def _silu(x):
    return x * _sigmoid(x)


def _softplus(x):
    return jnp.maximum(x, 0.0) + jnp.log1p(jnp.exp(-jnp.abs(x)))


NORM_ROWS = 256


def _proj_kernel(x_ref, nw_ref, wa_ref, wb_ref, wdt_ref, o_ref, dt_ref, u_scr, *, n_a):
    j = pl.program_id(1)

    @pl.when(j == 0)
    def _():
        tm = x_ref.shape[0]
        for lo in range(0, tm, min(tm, NORM_ROWS)):
            hi = lo + min(tm, NORM_ROWS)
            x = x_ref[lo:hi, :]
            ms = jnp.mean(x * x, axis=-1, keepdims=True)
            u_scr[lo:hi, :] = (x * lax.rsqrt(ms + EPS) * nw_ref[...]).astype(BF16)
        dt_ref[...] = _dot(u_scr[...], wdt_ref[...])

    @pl.when(j < n_a)
    def _():
        o_ref[...] = _dot(u_scr[...], wa_ref[...])

    @pl.when(j >= n_a)
    def _():
        o_ref[...] = _dot(u_scr[...], wb_ref[...])


def _proj(x, norm_w, w_a, w_b, w_dt, tm, tn):
    t = x.shape[0]
    n_a = w_a.shape[1] // tn
    return pl.pallas_call(
        functools.partial(_proj_kernel, n_a=n_a),
        grid=(t // tm, PROJ_WIDTH // tn),
        in_specs=[
            pl.BlockSpec((tm, D_MODEL), lambda i, j: (i, 0)),
            pl.BlockSpec((1, D_MODEL), lambda i, j: (0, 0)),
            pl.BlockSpec((D_MODEL, tn), lambda i, j: (0, jnp.minimum(j, n_a - 1))),
            pl.BlockSpec((D_MODEL, tn), lambda i, j: (0, jnp.maximum(j - n_a, 0))),
            pl.BlockSpec((D_MODEL, LANES), lambda i, j: (0, 0)),
        ],
        out_specs=[
            pl.BlockSpec((tm, tn), lambda i, j: (i, j)),
            pl.BlockSpec((tm, LANES), lambda i, j: (i, 0)),
        ],
        out_shape=[
            jax.ShapeDtypeStruct((t, PROJ_WIDTH), F32),
            jax.ShapeDtypeStruct((t, LANES), F32),
        ],
        scratch_shapes=[pltpu.VMEM((tm, D_MODEL), BF16)],
        compiler_params=_cparams(("arbitrary", "arbitrary")),
        name="proj",
    )(x, norm_w, w_a, w_b, w_dt)


LOG2E = math.log2(math.e)
Q_SCALE = LOG2E * HEAD_DIM ** -0.5


def _qknorm_kernel(p_ref, qw_ref, kw_ref, g_ref, qn_ref, kn_ref, kb_ref, vb_ref):
    gsum = g_ref[...]
    n_q = Q_WIDTH // LANES
    for c in range((Q_WIDTH + K_WIDTH) // LANES):
        x = p_ref[:, c * LANES:(c + 1) * LANES]
        ss = _dot_x2(x * x, gsum)
        y = x * lax.rsqrt(ss * (1.0 / HEAD_DIM) + EPS)
        if c < n_q:
            qn_ref[:, c * LANES:(c + 1) * LANES] = (y * qw_ref[...] * Q_SCALE).astype(BF16)
        else:
            kn = y * kw_ref[...]
            kn_ref[:, (c - n_q) * LANES:(c - n_q + 1) * LANES] = kn
            kb_ref[:, (c - n_q) * LANES:(c - n_q + 1) * LANES] = kn.astype(BF16)
    vb_ref[...] = p_ref[:, COL_V:COL_V + V_WIDTH].astype(BF16)


def _qk_norm(proj, q_norm_w, k_norm_w, tm):
    t = proj.shape[0]
    group = np.kron(np.eye(LANES // HEAD_DIM), np.ones((HEAD_DIM, HEAD_DIM)))
    qw = jnp.tile(q_norm_w, LANES // HEAD_DIM).reshape(1, LANES)
    kw = jnp.tile(k_norm_w, LANES // HEAD_DIM).reshape(1, LANES)
    return pl.pallas_call(
        _qknorm_kernel,
        grid=(t // tm,),
        in_specs=[
            pl.BlockSpec((tm, Q_WIDTH + K_WIDTH + V_WIDTH), lambda i: (i, 0)),
            pl.BlockSpec((1, LANES), lambda i: (0, 0)),
            pl.BlockSpec((1, LANES), lambda i: (0, 0)),
            pl.BlockSpec((LANES, LANES), lambda i: (0, 0)),
        ],
        out_specs=[
            pl.BlockSpec((tm, Q_WIDTH), lambda i: (i, 0)),
            pl.BlockSpec((tm, K_WIDTH), lambda i: (i, 0)),
            pl.BlockSpec((tm, K_WIDTH), lambda i: (i, 0)),
            pl.BlockSpec((tm, V_WIDTH), lambda i: (i, 0)),
        ],
        out_shape=[
            jax.ShapeDtypeStruct((t, Q_WIDTH), BF16),
            jax.ShapeDtypeStruct((t, K_WIDTH), F32),
            jax.ShapeDtypeStruct((t, K_WIDTH), BF16),
            jax.ShapeDtypeStruct((t, V_WIDTH), BF16),
        ],
        compiler_params=_cparams(("arbitrary",)),
        name="qk_norm",
    )(proj, qw, kw, jnp.asarray(group, BF16))


def _diff_lambda(lam_ref):
    lamv = lam_ref[...]
    s1 = jnp.sum(lamv[0:1] * lamv[1:2], axis=1, keepdims=True)
    s2 = jnp.sum(lamv[2:3] * lamv[3:4], axis=1, keepdims=True)
    return jnp.exp(s1) - jnp.exp(s2) + LAM_INIT


def _subln(o, w):
    ms = jnp.mean(o * o, axis=-1, keepdims=True)
    return o * lax.rsqrt(ms + EPS) * w * (1.0 - LAM_INIT)


N_SLOPE_PARTS = 3


def _bf16_parts(x, n):
    parts, rem = [], np.float32(x)
    for _ in range(n):
        p = np.float32(rem.astype(ml_dtypes.bfloat16))
        parts.append(float(p))
        rem = np.float32(rem - p)
    return parts


def _alibi_tables(tk):
    qcols = np.zeros((N_HEADS, 16, LANES), np.float32)
    csum = np.zeros((N_HEADS,), np.float32)
    for h, slope in enumerate(ALIBI_SLOPES):
        parts = _bf16_parts(slope * LOG2E, N_SLOPE_PARTS)
        csum[h] = np.float32(sum(np.float32(p) for p in parts))
        for i, p in enumerate(parts):
            qcols[h, :, i] = p * LANES
            qcols[h, :, N_SLOPE_PARTS + i] = p
    pos = np.arange(tk)
    kcols = np.zeros((tk, LANES), np.float32)
    kcols[:, 0:N_SLOPE_PARTS] = (pos // LANES)[:, None]
    kcols[:, N_SLOPE_PARTS:2 * N_SLOPE_PARTS] = (pos % LANES)[:, None]
    return jnp.asarray(qcols, BF16), jnp.asarray(kcols, BF16), jnp.asarray(csum, F32)


def _attn_p_kernel(qi_ref, ki_ref, cf_ref, q_ref, k_ref, v_ref, qc_ref, kc_ref, lam_ref, sw_ref,
                   o_ref, qa_scr, m_scr, l_scr, acc_scr, *, tq):
    g = pl.program_id(0)
    t = pl.program_id(1)
    qi = qi_ref[t]
    ki = ki_ref[t]
    n_sub = GQA_REP * 2
    rows = n_sub * tq
    n_chunk = tq // LANES

    @pl.when(ki == 0)
    def _():
        lane = lax.broadcasted_iota(jnp.int32, (tq, LANES), 1)
        for r in range(GQA_REP):
            qq = q_ref[:, r * LANES:(r + 1) * LANES]
            qc = jnp.broadcast_to(qc_ref[r, 0:1, :], (tq, LANES))
            for c in range(2):
                idx = 2 * r + c
                keep = (lane < HEAD_DIM) if c == 0 else (lane >= HEAD_DIM)
                qa_scr[idx * tq:(idx + 1) * tq, 0:LANES] = jnp.where(keep, qq, jnp.zeros_like(qq))
                qa_scr[idx * tq:(idx + 1) * tq, LANES:2 * LANES] = qc
        m_scr[...] = jnp.full(m_scr.shape, NEG_BIG, F32)
        l_scr[...] = jnp.zeros(l_scr.shape, F32)
        acc_scr[...] = jnp.zeros(acc_scr.shape, F32)

    def step(diag):
        k_aug = jnp.concatenate([k_ref[...], kc_ref[...]], axis=1)
        s_all = _dot_nt(qa_scr[...], k_aug)
        if diag:
            row_in = lax.broadcasted_iota(jnp.int32, (rows, tq), 0) & (tq - 1)
            col = lax.broadcasted_iota(jnp.int32, (rows, tq), 1)
            s_all = jnp.where(col <= row_in, s_all, NEG_BIG)
        block_dist = ((qi - ki) * tq).astype(F32)
        ps, alphas = [], []
        for r in range(GQA_REP):
            off = -cf_ref[g * GQA_REP + r] * block_dist
            lo, hi = r * 2 * tq, (r + 1) * 2 * tq
            chunks = [s_all[lo:hi, j * LANES:(j + 1) * LANES] for j in range(n_chunk)]
            m_prev = m_scr[lo:hi]
            m_blk = jnp.max(functools.reduce(jnp.maximum, chunks), axis=1, keepdims=True) + off
            m_new = jnp.maximum(m_prev, m_blk)
            alpha = jnp.exp2(m_prev - m_new)
            m_sub = m_new - off
            pj = [jnp.exp2(ch - m_sub) for ch in chunks]
            l_scr[lo:hi] = alpha * l_scr[lo:hi] + functools.reduce(jnp.add, pj)
            m_scr[lo:hi] = m_new
            ps.append(jnp.concatenate(pj, axis=1).astype(BF16))
            alphas.append(alpha)
        pv = _dot(jnp.concatenate(ps, axis=0), v_ref[...])
        acc_scr[...] = jnp.concatenate(alphas, axis=0) * acc_scr[...] + pv

    @pl.when(ki < qi)
    def _():
        step(False)

    @pl.when(ki == qi)
    def _():
        step(True)
        lam = _diff_lambda(lam_ref)
        for r in range(GQA_REP):
            i1, i2 = 2 * r * tq, (2 * r + 1) * tq
            l1 = jnp.sum(l_scr[i1:i1 + tq], axis=1, keepdims=True)
            l2 = jnp.sum(l_scr[i2:i2 + tq], axis=1, keepdims=True)
            o = acc_scr[i1:i1 + tq] / l1 - lam * (acc_scr[i2:i2 + tq] / l2)
            o_ref[:, r * LANES:(r + 1) * LANES] = _subln(o, sw_ref[...]).astype(BF16)


def _attn_prompt(qn, kb, vb, lam_vecs, subln_w, tq):
    t = qn.shape[0]
    nq = t // tq
    pairs = [(i, j) for i in range(nq) for j in range(i + 1)]
    qi_tab = jnp.asarray([p[0] for p in pairs], jnp.int32)
    ki_tab = jnp.asarray([p[1] for p in pairs], jnp.int32)
    qcols, kcols, csum = _alibi_tables(tq)
    n_sub = 2 * GQA_REP
    grid_spec = pltpu.PrefetchScalarGridSpec(
        num_scalar_prefetch=3,
        grid=(N_KV_HEADS, len(pairs)),
        in_specs=[
            pl.BlockSpec((tq, GQA_REP * LANES), lambda g, t, qi, ki, cf: (qi[t], g)),
            pl.BlockSpec((tq, LANES), lambda g, t, qi, ki, cf: (ki[t], g)),
            pl.BlockSpec((tq, V_DIM), lambda g, t, qi, ki, cf: (ki[t], g)),
            pl.BlockSpec((GQA_REP, 16, LANES), lambda g, t, qi, ki, cf: (g, 0, 0)),
            pl.BlockSpec((tq, LANES), lambda g, t, qi, ki, cf: (0, 0)),
            pl.BlockSpec((4, HEAD_DIM), lambda g, t, qi, ki, cf: (0, 0)),
            pl.BlockSpec((1, V_DIM), lambda g, t, qi, ki, cf: (0, 0)),
        ],
        out_specs=pl.BlockSpec((tq, GQA_REP * V_DIM), lambda g, t, qi, ki, cf: (qi[t], g)),
        scratch_shapes=[
            pltpu.VMEM((n_sub * tq, 2 * LANES), BF16),
            pltpu.VMEM((n_sub * tq, LANES), F32),
            pltpu.VMEM((n_sub * tq, LANES), F32),
            pltpu.VMEM((n_sub * tq, V_DIM), F32),
        ],
    )
    return pl.pallas_call(
        functools.partial(_attn_p_kernel, tq=tq),
        grid_spec=grid_spec,
        out_shape=jax.ShapeDtypeStruct((t, ATT_WIDTH), BF16),
        compiler_params=_cparams(("arbitrary", "arbitrary")),
        name="attn_p",
    )(qi_tab, ki_tab, csum, qn, kb, vb, qcols, kcols, lam_vecs, subln_w.reshape(1, V_DIM))


PAGES_PER_STEP = 16
PAGE_GROUP = PAGES_PER_STEP
ROWS_S = 2 * 4 * N_HEADS


def _attn_s_kernel(pt_ref, q_ref, d0_ref, mask_ref, sl_ref, bn_ref, kn_ref, vn_ref, lam_ref,
                   sw_ref, *rest, n_steps):
    k_refs = [r.at[0] for r in rest[:PAGES_PER_STEP]]
    v_refs = [r.at[0] for r in rest[PAGES_PER_STEP:2 * PAGES_PER_STEP]]
    o_ref = rest[2 * PAGES_PER_STEP]
    m_scr, l_scr, acc_scr = rest[2 * PAGES_PER_STEP + 1:]
    s_id = pl.program_id(1)

    @pl.when(s_id == 0)
    def _():
        m_scr[...] = jnp.full(m_scr.shape, NEG_BIG, F32)
        l_scr[...] = jnp.zeros(l_scr.shape, F32)
        acc_scr[...] = jnp.zeros(acc_scr.shape, F32)

    q = q_ref[0]

    def update(scores, values):
        m_prev = m_scr[...]
        m_new = m_prev
        for sc in scores:
            m_new = jnp.maximum(m_new, jnp.max(sc, axis=1, keepdims=True))
        alpha = jnp.exp2(m_prev - m_new)
        l_new = alpha * l_scr[...]
        acc = alpha * acc_scr[...]
        for sc, vv in zip(scores, values):
            p = jnp.exp2(sc - m_new)
            l_new = l_new + jnp.sum(p, axis=1, keepdims=True)
            acc = acc + _dot(p.astype(BF16), vv)
        m_scr[...] = m_new
        l_scr[...] = l_new
        acc_scr[...] = acc

    for first in range(0, PAGES_PER_STEP, PAGE_GROUP):
        scores, values = [], []
        for i in range(first, first + PAGE_GROUP):
            page_start = ((s_id * PAGES_PER_STEP + i) * PAGE_SIZE).astype(F32)
            bias = sl_ref[...] * (d0_ref[...] - page_start) + mask_ref[...]
            scores.append(_dot_nt(q, k_refs[i][...].astype(BF16)) + bias)
            values.append(v_refs[i][...].astype(BF16))
        update(scores, values)

    @pl.when(s_id == n_steps - 1)
    def _():
        sc = _dot_nt(q, kn_ref[0].astype(BF16)) + bn_ref[...]
        update([sc], [vn_ref[0].astype(BF16)])
        lam = _diff_lambda(lam_ref)
        half = ROWS_S // 2
        o1 = acc_scr[0:half] / l_scr[0:half]
        o2 = acc_scr[half:ROWS_S] / l_scr[half:ROWS_S]
        o_ref[0] = _subln(o1 - lam * o2, sw_ref[...]).astype(BF16)


def _attn_sample(qn_s, kn_s, v_s, cache_k, cache_v, page_table, lam_vecs, subln_w):
    db, n_pages = page_table.shape
    dec_seq = qn_s.shape[0] // db
    past = n_pages * PAGE_SIZE
    n_steps = n_pages // PAGES_PER_STEP
    page_rows = PAGE_SIZE * N_KV_HEADS
    n_phys = cache_k.shape[0]
    ck = cache_k.reshape(n_phys, page_rows, 2 * HEAD_DIM)
    cv = cache_v.reshape(n_phys, page_rows, V_DIM)

    q5 = qn_s.reshape(db, dec_seq, N_HEADS, 2, HEAD_DIM)
    zeros = jnp.zeros_like(q5[:, :, :, 0])
    q_all = jnp.stack([jnp.concatenate([q5[:, :, :, 0], zeros], axis=-1),
                       jnp.concatenate([zeros, q5[:, :, :, 1]], axis=-1)], axis=1)
    q_all = q_all.reshape(db, ROWS_S, LANES)

    r = np.arange(ROWS_S)
    tok_r = (r % (dec_seq * N_HEADS)) // N_HEADS
    head_r = r % N_HEADS
    slope_r = np.asarray(ALIBI_SLOPES)[head_r] * LOG2E
    c = np.arange(page_rows)
    key_c, grp_c = c // N_KV_HEADS, c % N_KV_HEADS
    same = (head_r[:, None] // GQA_REP) == grp_c[None, :]
    d0 = np.broadcast_to(past + tok_r[:, None] - key_c[None, :], (ROWS_S, page_rows))
    mask = np.where(same, 0.0, NEG_BIG)
    sl = np.broadcast_to(-slope_r[:, None], (ROWS_S, 1))
    cn = np.arange(LANES)
    tok_c, grp_n = cn // N_KV_HEADS, cn % N_KV_HEADS
    ok = ((head_r[:, None] // GQA_REP) == grp_n[None, :]) & (tok_c[None, :] <= tok_r[:, None])
    bn = np.where(ok, -slope_r[:, None] * (tok_r[:, None] - tok_c[None, :]), NEG_BIG)

    new_rows = dec_seq * N_KV_HEADS
    kn_pad = jnp.pad(kn_s.reshape(db, new_rows, LANES), ((0, 0), (0, LANES - new_rows), (0, 0)))
    vn_pad = jnp.pad(v_s.reshape(db, new_rows, LANES), ((0, 0), (0, LANES - new_rows), (0, 0)))

    def const(shape):
        return pl.BlockSpec(shape, lambda b, s, pt: (0,) * len(shape))

    def page_spec(i):
        return pl.BlockSpec(
            (1, page_rows, LANES),
            lambda b, s, pt: (pt[b * n_pages + s * PAGES_PER_STEP + i], 0, 0))

    grid_spec = pltpu.PrefetchScalarGridSpec(
        num_scalar_prefetch=1,
        grid=(db, n_steps),
        in_specs=[
            pl.BlockSpec((1, ROWS_S, LANES), lambda b, s, pt: (b, 0, 0)),
            const((ROWS_S, page_rows)),
            const((ROWS_S, page_rows)),
            const((ROWS_S, 1)),
            const((ROWS_S, LANES)),
            pl.BlockSpec((1, LANES, LANES), lambda b, s, pt: (b, 0, 0)),
            pl.BlockSpec((1, LANES, LANES), lambda b, s, pt: (b, 0, 0)),
            const((4, HEAD_DIM)),
            const((1, V_DIM)),
        ] + [page_spec(i) for i in range(PAGES_PER_STEP)] * 2,
        out_specs=pl.BlockSpec((1, ROWS_S // 2, V_DIM), lambda b, s, pt: (b, 0, 0)),
        scratch_shapes=[
            pltpu.VMEM((ROWS_S, 1), F32),
            pltpu.VMEM((ROWS_S, 1), F32),
            pltpu.VMEM((ROWS_S, V_DIM), F32),
        ],
    )
    o = pl.pallas_call(
        functools.partial(_attn_s_kernel, n_steps=n_steps),
        grid_spec=grid_spec,
        out_shape=jax.ShapeDtypeStruct((db, ROWS_S // 2, V_DIM), BF16),
        compiler_params=_cparams(("arbitrary", "arbitrary")),
        name="attn_s",
    )(page_table.reshape(-1), q_all, jnp.asarray(d0, F32), jnp.asarray(mask, F32),
      jnp.asarray(sl, F32), jnp.asarray(bn, F32), kn_pad, vn_pad, lam_vecs,
      subln_w.reshape(1, V_DIM),
      *([ck] * PAGES_PER_STEP), *([cv] * PAGES_PER_STEP))
    return o.reshape(db * dec_seq, ATT_WIDTH)


HALO = SUBLANES


def _ssd_kernel(xs_ref, b_ref, c_ref, z_ref, dt_ref, dtt_ref, halo_ref, init_ref,
                cw_ref, cb_ref, dtb_ref, dtbt_ref, a_ref, at_ref, dsk_ref, nw_ref,
                tri_ref, trit_ref, exp_ref, sel_ref,
                y_ref, fin_ref, win_scr, state_scr, *, rows_in, n_valid):
    ci = pl.program_id(1)
    n_chunks = pl.num_programs(1)
    lc = SSD_CHUNK
    bc_w = N_GROUPS * D_STATE

    @pl.when(ci == 0)
    def _():
        state_scr[...] = init_ref[0].reshape(D_SSM, D_STATE)
        win_scr[0:HALO, :] = halo_ref[0]

    if rows_in < lc:
        win_scr[HALO:HALO + lc, :] = jnp.zeros((lc, CONV_DIM), F32)
    win_scr[HALO:HALO + rows_in, 0:D_SSM] = xs_ref[0]
    win_scr[HALO:HALO + rows_in, D_SSM:D_SSM + bc_w] = b_ref[0]
    win_scr[HALO:HALO + rows_in, D_SSM + bc_w:CONV_DIM] = c_ref[0]

    acc = cb_ref[...]
    for tap in range(CONV_WIDTH):
        off = HALO - (CONV_WIDTH - 1) + tap
        acc = acc + win_scr[off:off + lc, :] * cw_ref[tap:tap + 1, :]
    conv = _silu(acc)
    win_scr[0:HALO, :] = win_scr[lc:lc + HALO, :]
    xs = conv[:, 0:D_SSM]
    bm = conv[:, D_SSM:D_SSM + bc_w].astype(BF16)
    cm = conv[:, D_SSM + bc_w:CONV_DIM].astype(BF16)

    if rows_in < lc:
        dt_in = jnp.concatenate([dt_ref[0], jnp.zeros((lc - rows_in, LANES), F32)], axis=0)
        dtt_in = jnp.concatenate(
            [dtt_ref[0], jnp.zeros((N_SSM_HEADS, lc - rows_in), F32)], axis=1)
    else:
        dt_in, dtt_in = dt_ref[0], dtt_ref[0]
    rowi = lax.broadcasted_iota(jnp.int32, (lc, LANES), 0)
    coli = lax.broadcasted_iota(jnp.int32, (N_SSM_HEADS, lc), 1)
    dt = jnp.where(rowi < n_valid, _softplus(dt_in + dtb_ref[...]), 0.0)
    dtt = jnp.where(coli < n_valid, _softplus(dtt_in + dtbt_ref[...]), 0.0)
    a_cs = _dot_x3_left(tri_ref[...], dt * a_ref[...])
    a_cst = _dot_x3(dtt * at_ref[...], trit_ref[...])
    a_last = a_cs[lc - 1:lc, :]
    exp_cs = jnp.exp(a_cs)
    exp_rest = jnp.exp(a_last - a_cs)
    expand = exp_ref[...]
    dtx = _dot_x2(dt, expand)
    ecx = _dot_x2(exp_cs, expand)
    erx = _dot_x2(exp_rest, expand)
    xc = xs * dtx
    xcb = xc.astype(BF16)
    xcd = (xc * erx).astype(BF16)

    last_t = jnp.exp(a_cst[:, lc - 1:lc])
    rdec = _dot_x2_left(sel_ref[...], jnp.broadcast_to(last_t, (N_SSM_HEADS, D_STATE)))

    tril = (lax.broadcasted_iota(jnp.int32, (lc, lc), 0)
            >= lax.broadcasted_iota(jnp.int32, (lc, lc), 1))
    lane = lax.broadcasted_iota(jnp.int32, (lc, LANES), 1)
    gw = HEADS_PER_GROUP * SSM_HEAD_DIM
    y_parts = []
    for g in range(N_GROUPS):
        bg = bm[:, g * D_STATE:(g + 1) * D_STATE]
        cg = cm[:, g * D_STATE:(g + 1) * D_STATE]
        cb = _dot_nt(cg, bg)
        st = state_scr[g * gw:(g + 1) * gw, :]
        y_off = _dot_nt(cg, st.astype(BF16)) * ecx[:, g * gw:(g + 1) * gw]
        new_st = _dot_tn(xcd[:, g * gw:(g + 1) * gw], bg)
        state_scr[g * gw:(g + 1) * gw, :] = st * rdec[g * gw:(g + 1) * gw, :] + new_st
        for j in range(HEADS_PER_GROUP // 2):
            pair = g * (HEADS_PER_GROUP // 2) + j
            blk = xcb[:, pair * LANES:(pair + 1) * LANES]
            y_pair = None
            for half in range(2):
                h = 2 * pair + half
                seg = a_cs[:, h:h + 1] - a_cst[h:h + 1, :]
                decay = jnp.exp(jnp.where(tril, seg, NEG_BIG))
                mh = (cb * decay).astype(BF16)
                keep = (lane < SSM_HEAD_DIM) if half == 0 else (lane >= SSM_HEAD_DIM)
                part = _dot(mh, jnp.where(keep, blk, jnp.zeros_like(blk)))
                y_pair = part if y_pair is None else y_pair + part
            y_parts.append(y_pair + y_off[:, (pair % (HEADS_PER_GROUP // 2)) * LANES:
                                          (pair % (HEADS_PER_GROUP // 2) + 1) * LANES])
    y = jnp.concatenate(y_parts, axis=1)
    y = y + dsk_ref[...] * xs
    if rows_in < lc:
        z = jnp.concatenate([z_ref[0], jnp.zeros((lc - rows_in, D_SSM), F32)], axis=0)
    else:
        z = z_ref[0]
    y = y * _silu(z)
    gn = D_SSM // N_GROUPS
    outs = []
    for g in range(N_GROUPS):
        yg = y[:, g * gn:(g + 1) * gn]
        ms = jnp.mean(yg * yg, axis=-1, keepdims=True)
        outs.append(yg * lax.rsqrt(ms + EPS) * nw_ref[:, g * gn:(g + 1) * gn])
    out = jnp.concatenate(outs, axis=1).astype(BF16)
    y_ref[0] = out[0:rows_in]

    @pl.when(ci == n_chunks - 1)
    def _():
        fin_ref[0] = state_scr[...].reshape(N_SSM_HEADS, SSM_HEAD_DIM, D_STATE)


def _dot_x3_left(sel, x):
    hi, mid, lo = _split3(x)
    return _dot(sel, hi) + _dot(sel, mid) + _dot(sel, lo)


def _dot_x2_left(sel, x):
    hi, lo = _split2(x)
    return _dot(sel, hi) + _dot(sel, lo)


def _ssd(src, col_blocks, dt_raw, halo, init_state, prm, rows_in, n_valid):
    nb, seq = src.shape[0], src.shape[1]
    n_chunks = max(1, seq // SSD_CHUNK)
    bc_w = N_GROUPS * D_STATE
    dtt = jnp.swapaxes(dt_raw[:, :, :N_SSM_HEADS], 1, 2)
    tri = np.tril(np.ones((SSD_CHUNK, SSD_CHUNK)))
    expand = np.zeros((LANES, D_SSM))
    expand[np.arange(D_SSM) // SSM_HEAD_DIM, np.arange(D_SSM)] = 1.0
    sel = expand[:N_SSM_HEADS].T
    cx, cbk, cck, cz = col_blocks

    def const(shape):
        return pl.BlockSpec(shape, lambda b, c: (0,) * len(shape))

    return pl.pallas_call(
        functools.partial(_ssd_kernel, rows_in=rows_in, n_valid=n_valid),
        grid=(nb, n_chunks),
        in_specs=[
            pl.BlockSpec((1, rows_in, D_SSM), lambda b, c: (b, c, cx)),
            pl.BlockSpec((1, rows_in, bc_w), lambda b, c: (b, c, cbk)),
            pl.BlockSpec((1, rows_in, bc_w), lambda b, c: (b, c, cck)),
            pl.BlockSpec((1, rows_in, D_SSM), lambda b, c: (b, c, cz)),
            pl.BlockSpec((1, rows_in, LANES), lambda b, c: (b, c, 0)),
            pl.BlockSpec((1, N_SSM_HEADS, rows_in), lambda b, c: (b, 0, c)),
            pl.BlockSpec((1, HALO, CONV_DIM), lambda b, c: (b, 0, 0)),
            pl.BlockSpec((1, N_SSM_HEADS, SSM_HEAD_DIM, D_STATE), lambda b, c: (b, 0, 0, 0)),
            const((CONV_WIDTH, CONV_DIM)),
            const((1, CONV_DIM)),
            const((1, LANES)),
            const((N_SSM_HEADS, 1)),
            const((1, LANES)),
            const((N_SSM_HEADS, 1)),
            const((1, D_SSM)),
            const((1, D_SSM)),
            const((SSD_CHUNK, SSD_CHUNK)),
            const((SSD_CHUNK, SSD_CHUNK)),
            const((LANES, D_SSM)),
            const((D_SSM, N_SSM_HEADS)),
        ],
        out_specs=[
            pl.BlockSpec((1, rows_in, D_SSM), lambda b, c: (b, c, 0)),
            pl.BlockSpec((1, N_SSM_HEADS, SSM_HEAD_DIM, D_STATE), lambda b, c: (b, 0, 0, 0)),
        ],
        out_shape=[
            jax.ShapeDtypeStruct((nb, seq, D_SSM), BF16),
            jax.ShapeDtypeStruct((nb, N_SSM_HEADS, SSM_HEAD_DIM, D_STATE), F32),
        ],
        scratch_shapes=[
            pltpu.VMEM((HALO + SSD_CHUNK, CONV_DIM), F32),
            pltpu.VMEM((D_SSM, D_STATE), F32),
        ],
        compiler_params=_cparams(("arbitrary", "arbitrary")),
        name="ssd",
    )(src, src, src, src, dt_raw, dtt, halo, init_state,
      prm["conv_w"], prm["conv_b"], prm["dt_bias"], prm["dt_bias_t"], prm["a"], prm["a_t"],
      prm["d_skip"], prm["ssm_norm_w"],
      jnp.asarray(tri, BF16), jnp.asarray(tri.T, BF16), jnp.asarray(expand, BF16),
      jnp.asarray(sel, BF16))


def _merge_kernel(o_ref, s_ref, wa_ref, ws_ref, ga_ref, gs_ref, out_ref):
    a = _dot(o_ref[...], wa_ref[...])
    s = _dot(s_ref[...], ws_ref[...])
    out_ref[...] = (_sigmoid(ga_ref[...]) * a + _sigmoid(gs_ref[...]) * s).astype(BF16)


def _merge(o, s, wa, ws, proj, tm, tn):
    t = o.shape[0]
    ga0, gs0 = COL_GA // tn, COL_GS // tn
    return pl.pallas_call(
        _merge_kernel,
        grid=(t // tm, D_MODEL // tn),
        in_specs=[
            pl.BlockSpec((tm, ATT_WIDTH), lambda i, j: (i, 0)),
            pl.BlockSpec((tm, D_SSM), lambda i, j: (i, 0)),
            pl.BlockSpec((ATT_WIDTH, tn), lambda i, j: (0, j)),
            pl.BlockSpec((D_SSM, tn), lambda i, j: (0, j)),
            pl.BlockSpec((tm, tn), lambda i, j: (i, ga0 + j)),
            pl.BlockSpec((tm, tn), lambda i, j: (i, gs0 + j)),
        ],
        out_specs=pl.BlockSpec((tm, tn), lambda i, j: (i, j)),
        out_shape=jax.ShapeDtypeStruct((t, D_MODEL), BF16),
        compiler_params=_cparams(("arbitrary", "arbitrary")),
        name="merge",
    )(o, s, wa, ws, proj, proj)


ROUTE_E1, ROUTE_E2, ROUTE_W1, ROUTE_W2 = 0, 1, 2, 3
ROW_CHUNKS = D_MODEL // LANES


def _resid_kernel(x_ref, m_ref, wo_ref, nw_ref, wrh_ref, wrl_ref, br_ref, *rest, n_real):
    h_ref, u_ref, route_ref = rest[-3:]

    @pl.when(pl.program_id(0) >= n_real)
    def _():
        h_ref[...] = jnp.zeros(h_ref.shape, F32)
        u_ref[...] = jnp.zeros(u_ref.shape, F32)
        route_ref[...] = jnp.zeros(route_ref.shape, F32)

    @pl.when(pl.program_id(0) < n_real)
    def _():
        _resid_tile(x_ref, m_ref, wo_ref, nw_ref, wrh_ref, wrl_ref, br_ref,
                    h_ref, u_ref, route_ref)


def _resid_tile(x_ref, m_ref, wo_ref, nw_ref, wrh_ref, wrl_ref, br_ref, h_ref, u_ref, route_ref):
    h = x_ref[...] + _dot(m_ref[...], wo_ref[...])
    h_ref[...] = h
    ms = jnp.mean(h * h, axis=-1, keepdims=True)
    u = h * lax.rsqrt(ms + EPS) * nw_ref[...]
    u_hi, u_lo = _split2(u)
    u_ref[...] = u
    logits = (_dot(u_hi, wrh_ref[...]) + _dot(u_lo, wrh_ref[...])
              + _dot(u_hi, wrl_ref[...]) + br_ref[...])
    lane = lax.broadcasted_iota(jnp.int32, logits.shape, 1)
    lane_f = lane.astype(F32)
    far = float(2 * LANES)

    def first_max(vals):
        top = jnp.max(vals, axis=1, keepdims=True)
        idx = jnp.min(jnp.where(vals == top, lane_f, far), axis=1, keepdims=True)
        return top, idx

    is_group = (lane >= N_EXPERTS) & (lane < N_EXPERTS + N_EXPERT_GROUPS)
    gl = jnp.where(is_group, logits, NEG_BIG)
    g_top, g_idx = first_max(gl)
    g_p = 1.0 / jnp.sum(jnp.exp(gl - g_top), axis=1, keepdims=True)
    lo_lane = (g_idx - N_EXPERTS) * EXPERTS_PER_GROUP
    in_group = (lane_f >= lo_lane) & (lane_f < lo_lane + EXPERTS_PER_GROUP)
    el = jnp.where(in_group, logits, NEG_BIG)
    m1, i1 = first_max(el)
    el2 = jnp.where(lane_f == i1, NEG_BIG, el)
    m2, i2 = first_max(el2)
    e = jnp.exp(m2 - m1)
    w1 = 1.0 / (1.0 + e)
    w2 = e / (1.0 + e)
    route = jnp.where(lane == ROUTE_E1, i1, 0.0)
    route = jnp.where(lane == ROUTE_E2, i2, route)
    route = jnp.where(lane == ROUTE_W1, g_p * w1, route)
    route_ref[...] = jnp.where(lane == ROUTE_W2, g_p * w2, route)


def _resid(x, merged, wo, norm_w, wr_hi, wr_lo, br, tm, t_all, row_off, bufs):
    t = x.shape[0]
    blk_off = row_off // tm
    n_real = t // tm
    n_fill = pl.cdiv(t_all - t, tm) if bufs is None else 0

    def const(shape):
        return pl.BlockSpec(shape, lambda i: (0,) * len(shape))

    in_specs = [
        pl.BlockSpec((tm, D_MODEL), lambda i: (jnp.minimum(i, n_real - 1), 0)),
        pl.BlockSpec((tm, D_MODEL), lambda i: (jnp.minimum(i, n_real - 1), 0)),
        const((D_MODEL, D_MODEL)),
        const((1, D_MODEL)),
        const((D_MODEL, LANES)),
        const((D_MODEL, LANES)),
        const((1, LANES)),
    ]
    args = [x, merged, wo, norm_w, wr_hi, wr_lo, br]
    aliases = {}
    if bufs is not None:
        aliases = {len(args) + k: k for k in range(len(bufs))}
        in_specs += [pl.BlockSpec(memory_space=pl.ANY)] * len(bufs)
        args += list(bufs)
    return pl.pallas_call(
        functools.partial(_resid_kernel, n_real=n_real),
        grid=(n_real + n_fill,),
        in_specs=in_specs,
        out_specs=[
            pl.BlockSpec((tm, D_MODEL), lambda i: (i + blk_off, 0)),
            pl.BlockSpec((tm, D_MODEL), lambda i: (i + blk_off, 0)),
            pl.BlockSpec((tm, LANES), lambda i: (i + blk_off, 0)),
        ],
        out_shape=[
            jax.ShapeDtypeStruct((t_all, D_MODEL), F32),
            jax.ShapeDtypeStruct((t_all, D_MODEL), F32),
            jax.ShapeDtypeStruct((t_all, LANES), F32),
        ],
        input_output_aliases=aliases,
        compiler_params=_cparams(("arbitrary",)),
        name="resid",
    )(*args)


MOE_TILE = 256
COMB_TILE = 128
DMA_UNROLL = 8


def _route_plan(route, n_tiles):
    n_pairs = 2 * route.shape[0]
    pair_e = route[:, ROUTE_E1:ROUTE_E2 + 1].astype(jnp.int32).reshape(-1)
    onehot = (pair_e[:, None] == jnp.arange(N_EXPERTS, dtype=jnp.int32)[None, :]).astype(jnp.int32)
    csum = jnp.cumsum(onehot, axis=0)
    rank = jnp.sum((csum - onehot) * onehot, axis=1)
    tiles_e = (csum[-1] + MOE_TILE - 1) // MOE_TILE
    tile_end = jnp.cumsum(tiles_e)
    first_row = (tile_end - tiles_e) * MOE_TILE
    slot = jnp.sum(onehot * first_row[None, :], axis=1) + rank
    tok_of_slot = jnp.zeros(((n_tiles + 1) * MOE_TILE,), jnp.int32).at[slot].set(
        jnp.arange(n_pairs, dtype=jnp.int32) // 2)
    tile_start = (tile_end - tiles_e).astype(jnp.int32)
    slot_tab = slot.reshape(-1, COMB_TILE, 2).transpose(0, 2, 1).reshape(-1).astype(jnp.int32)
    return tile_start, tiles_e.astype(jnp.int32), tok_of_slot, slot_tab


ROW_DMA_PRIORITY = 0
BULK_DMA_PRIORITY = 1


def _gmm_kernel(ts_ref, ne_ref, tok_ref, u_hbm, wg_hbm, wu_hbm, wd_hbm, o_hbm,
                xbuf, obuf, gsem, osem, wgb, wub, wdb, wg_buf, wu_buf, wd_buf, wsem,
                *, n_tiles):
    e = pl.program_id(0)
    last = pl.num_programs(0) - 1
    n_used = ts_ref[last] + ne_ref[last]

    def row_copy(tile, r, buf):
        tok = tok_ref[tile * MOE_TILE + r]
        return pltpu.make_async_copy(
            u_hbm.at[tok], xbuf.at[buf, r // SUBLANES, :, r % SUBLANES, :], gsem.at[buf])

    def out_copy(tile, buf):
        return pltpu.make_async_copy(
            obuf.at[buf], o_hbm.at[pl.ds(tile * MOE_TILE, MOE_TILE), :], osem.at[buf])

    def gather_wait(tile, buf):
        del tile
        pltpu.make_async_copy(obuf.at[0], obuf.at[1], gsem.at[buf]).wait()

    @pl.when(e == 0)
    def _():
        def body(r, carry):
            row_copy(0, r, 0).start(priority=ROW_DMA_PRIORITY)
            return carry
        lax.fori_loop(0, MOE_TILE, body, 0, unroll=DMA_UNROLL)

    def weight_copies(expert, buf):
        return (pltpu.make_async_copy(wg_hbm.at[expert], wg_buf.at[buf], wsem.at[buf]),
                pltpu.make_async_copy(wu_hbm.at[expert], wu_buf.at[buf], wsem.at[buf]),
                pltpu.make_async_copy(wd_hbm.at[expert], wd_buf.at[buf], wsem.at[buf]))

    wcur = lax.rem(e, 2)

    @pl.when(e == 0)
    def _():
        for cp in weight_copies(0, 0):
            cp.start(priority=BULK_DMA_PRIORITY)

    @pl.when(e < last)
    def _():
        for cp in weight_copies(e + 1, 1 - wcur):
            cp.start(priority=BULK_DMA_PRIORITY)

    for cp in weight_copies(e, wcur):
        cp.wait()
    wgb[...] = wg_buf[wcur].astype(BF16)
    wub[...] = wu_buf[wcur].astype(BF16)
    wdb[...] = wd_buf[wcur].astype(BF16)

    def tile_body(j, carry):
        t = ts_ref[e] + j
        cur = lax.rem(t, 2)

        @pl.when(t >= 2)
        def _():
            out_copy(t - 2, cur).wait()

        gather_wait(t, cur)
        x = jnp.concatenate(
            [xbuf[cur, :, c].reshape(MOE_TILE, LANES) for c in range(ROW_CHUNKS)],
            axis=1).astype(BF16)
        for r in range(MOE_TILE):
            row_copy(t + 1, r, 1 - cur).start(priority=ROW_DMA_PRIORITY)
        hid = _silu(_dot(x, wgb[...])) * _dot(x, wub[...])
        obuf[cur] = _dot(hid.astype(BF16), wdb[...])
        out_copy(t, cur).start(priority=BULK_DMA_PRIORITY)
        return carry

    lax.fori_loop(0, ne_ref[e], tile_body, 0)

    @pl.when(e == last)
    def _():
        gather_wait(n_used, lax.rem(n_used, 2))
        out_copy(n_used - 2, lax.rem(n_used, 2)).wait()
        out_copy(n_used - 1, lax.rem(n_used - 1, 2)).wait()
        obuf[0] = jnp.zeros(obuf.shape[1:], F32)

        def fill(t, carry):
            cp = out_copy(t, 0)
            cp.start()
            cp.wait()
            return carry
        lax.fori_loop(n_used, n_tiles + 1, fill, 0)


def _gmm(u_all, plan, wg, wu, wd, n_tiles):
    tile_start, tiles_e, tok_of_slot, _ = plan
    assert 2 * u_all.shape[0] >= 2 * MOE_TILE
    grid_spec = pltpu.PrefetchScalarGridSpec(
        num_scalar_prefetch=3,
        grid=(N_EXPERTS,),
        in_specs=[
            pl.BlockSpec(memory_space=pl.ANY),
            pl.BlockSpec(memory_space=pl.ANY),
            pl.BlockSpec(memory_space=pl.ANY),
            pl.BlockSpec(memory_space=pl.ANY),
        ],
        out_specs=pl.BlockSpec(memory_space=pl.ANY),
        scratch_shapes=[
            pltpu.VMEM((2, MOE_TILE // SUBLANES, ROW_CHUNKS, SUBLANES, LANES), F32),
            pltpu.VMEM((2, MOE_TILE, D_MODEL), F32),
            pltpu.SemaphoreType.DMA((2,)),
            pltpu.SemaphoreType.DMA((2,)),
            pltpu.VMEM((D_MODEL, D_EXPERT), BF16),
            pltpu.VMEM((D_MODEL, D_EXPERT), BF16),
            pltpu.VMEM((D_EXPERT, D_MODEL), BF16),
            pltpu.VMEM((2, D_MODEL, D_EXPERT), F32),
            pltpu.VMEM((2, D_MODEL, D_EXPERT), F32),
            pltpu.VMEM((2, D_EXPERT, D_MODEL), F32),
            pltpu.SemaphoreType.DMA((2,)),
        ],
    )
    return pl.pallas_call(
        functools.partial(_gmm_kernel, n_tiles=n_tiles),
        grid_spec=grid_spec,
        out_shape=jax.ShapeDtypeStruct(((n_tiles + 1) * MOE_TILE, D_MODEL), F32),
        compiler_params=_cparams(("arbitrary",)),
        name="gmm",
    )(tile_start, tiles_e, tok_of_slot, u_all, wg, wu, wd)


def _combine_kernel(slot_ref, route_ref, h_ref, o_hbm, yp_ref, ys_ref, gbuf, sem, *, n_prompt):
    i = pl.program_id(0)
    n = pl.num_programs(0)
    cur = lax.rem(i, 2)
    rows = 2 * COMB_TILE

    def row_copy(tile, j, buf):
        slot = slot_ref[tile * rows + j]
        return pltpu.make_async_copy(
            o_hbm.at[pl.ds(slot, 1), :], gbuf.at[buf, pl.ds(j, 1), :], sem.at[buf])

    def issue(tile, buf):
        def body(j, carry):
            row_copy(tile, j, buf).start()
            return carry
        lax.fori_loop(0, rows, body, 0, unroll=DMA_UNROLL)

    def wait(tile, buf):
        del tile
        pltpu.make_async_copy(o_hbm.at[pl.ds(0, rows), :], gbuf.at[buf], sem.at[buf]).wait()

    @pl.when(i == 0)
    def _():
        issue(0, 0)

    @pl.when(i + 1 < n)
    def _():
        issue(i + 1, 1 - cur)

    wait(i, cur)
    w1 = route_ref[:, ROUTE_W1:ROUTE_W1 + 1]
    w2 = route_ref[:, ROUTE_W2:ROUTE_W2 + 1]
    y = h_ref[...] + w1 * gbuf[cur, 0:COMB_TILE, :] + w2 * gbuf[cur, COMB_TILE:rows, :]

    @pl.when(i < n_prompt)
    def _():
        yp_ref[...] = y

    @pl.when(i >= n_prompt)
    def _():
        ys_ref[...] = y


def _combine(route, h_all, o_sorted, plan, t_prompt):
    t_all = h_all.shape[0]
    n_prompt = t_prompt // COMB_TILE
    slot_tab = plan[3]
    grid_spec = pltpu.PrefetchScalarGridSpec(
        num_scalar_prefetch=1,
        grid=(t_all // COMB_TILE,),
        in_specs=[
            pl.BlockSpec((COMB_TILE, LANES), lambda i, st: (i, 0)),
            pl.BlockSpec((COMB_TILE, D_MODEL), lambda i, st: (i, 0)),
            pl.BlockSpec(memory_space=pl.ANY),
        ],
        out_specs=[
            pl.BlockSpec((COMB_TILE, D_MODEL), lambda i, st: (jnp.minimum(i, n_prompt - 1), 0)),
            pl.BlockSpec((COMB_TILE, D_MODEL), lambda i, st: (0, 0)),
        ],
        scratch_shapes=[
            pltpu.VMEM((2, 2 * COMB_TILE, D_MODEL), F32),
            pltpu.SemaphoreType.DMA((2,)),
        ],
    )
    return pl.pallas_call(
        functools.partial(_combine_kernel, n_prompt=n_prompt),
        grid_spec=grid_spec,
        out_shape=[
            jax.ShapeDtypeStruct((t_prompt, D_MODEL), F32),
            jax.ShapeDtypeStruct((t_all - t_prompt, D_MODEL), F32),
        ],
        compiler_params=_cparams(("arbitrary",)),
        name="combine",
    )(slot_tab, route, h_all, o_sorted)


def _layer_tokens(x2d, w, tm_proj, tm_small):
    proj, dt_raw = _proj(x2d, w["norm_attn_w"], w["w_a"], w["w_b"], w["w_dt"], tm_proj, 1024)
    qn, kn, kb, vb = _qk_norm(proj, w["q_norm_w"], w["k_norm_w"], tm_small)
    return proj, dt_raw, qn, kn, kb, vb


def _branch_merge(x2d, o, s, proj, w, tm, tm_resid, t_all, row_off, bufs):
    merged = _merge(o, s, w["w_att_out"], w["w_ssm_out"], proj, tm, 512)
    return _resid(x2d, merged, w["w_o"], w["norm_ffn_w"], w["wr_hi"], w["wr_lo"], w["br"],
                  tm_resid, t_all, row_off, bufs)


def kernel(x_prompt, x_sample, cache_k, cache_v, state_ssm, state_conv, page_table, norm_attn_w, w_in, q_norm_w, k_norm_w, lambda_q1, lambda_k1, lambda_q2, lambda_k2, subln_w, w_att_out, conv_w, conv_b, dt_bias, a_log, d_skip, ssm_norm_w, w_ssm_out, w_o, norm_ffn_w, w_group_router, b_group_router, w_expert_router, b_expert_router, w_gate, w_up, w_down):
    layer = 0
    nb, seq, _ = x_prompt.shape
    db, dec_seq, _ = x_sample.shape

    w_in_l = w_in[layer]
    c_dt = Q_WIDTH + K_WIDTH + V_WIDTH + D_SSM + CONV_DIM
    w_a = w_in_l[:, :c_dt].astype(BF16)
    w_b = w_in_l[:, c_dt + N_SSM_HEADS:].astype(BF16)
    w_dt = jnp.pad(w_in_l[:, c_dt:c_dt + N_SSM_HEADS], ((0, 0), (0, LANES - N_SSM_HEADS))).astype(BF16)
    wr = jnp.concatenate([w_expert_router[layer], w_group_router[layer]], axis=1)
    wr = jnp.pad(wr, ((0, 0), (0, LANES - wr.shape[1])))
    wr_hi = wr.astype(BF16)
    wr_lo = (wr - wr_hi.astype(F32)).astype(BF16)
    br = jnp.concatenate([b_expert_router[layer], b_group_router[layer]])
    br = jnp.pad(br, (0, LANES - br.shape[0])).reshape(1, LANES)
    pad_h = (0, LANES - N_SSM_HEADS)
    w = dict(
        norm_attn_w=norm_attn_w[layer].reshape(1, D_MODEL), w_a=w_a, w_b=w_b, w_dt=w_dt,
        q_norm_w=q_norm_w[layer], k_norm_w=k_norm_w[layer],
        w_att_out=w_att_out[layer].astype(BF16), w_ssm_out=w_ssm_out[layer].astype(BF16),
        w_o=w_o[layer].astype(BF16), norm_ffn_w=norm_ffn_w[layer].reshape(1, D_MODEL),
        wr_hi=wr_hi, wr_lo=wr_lo, br=br,
        w_gate=w_gate[layer], w_up=w_up[layer], w_down=w_down[layer],
    )
    ssm_prm = dict(
        conv_w=conv_w[layer], conv_b=conv_b[layer].reshape(1, CONV_DIM),
        dt_bias=jnp.pad(dt_bias[layer], pad_h).reshape(1, LANES),
        dt_bias_t=dt_bias[layer].reshape(N_SSM_HEADS, 1),
        a=jnp.pad(-jnp.exp(a_log[layer]), pad_h).reshape(1, LANES),
        a_t=(-jnp.exp(a_log[layer])).reshape(N_SSM_HEADS, 1),
        d_skip=jnp.repeat(d_skip[layer], SSM_HEAD_DIM).reshape(1, D_SSM),
        ssm_norm_w=ssm_norm_w[layer].reshape(1, D_SSM),
    )
    lam_vecs = jnp.stack([lambda_q1[layer], lambda_k1[layer], lambda_q2[layer], lambda_k2[layer]])
    sw = subln_w[layer]
    ssd_cols = (COL_X // D_SSM, COL_B // (N_GROUPS * D_STATE), COL_C // (N_GROUPS * D_STATE),
                COL_Z // D_SSM)

    xp = x_prompt.reshape(nb * seq, D_MODEL)
    n_tok = db * dec_seq
    t_prompt = nb * seq
    t_all = t_prompt + n_tok
    proj_p, dt_p, qn_p, kn_p, kb_p, vb_p = _layer_tokens(xp, w, 1024, 512)
    o_p = _attn_prompt(qn_p, kb_p, vb_p, lam_vecs, sw, 512)
    s_p, ssm_p = _ssd(
        proj_p.reshape(nb, seq, PROJ_WIDTH), ssd_cols, dt_p.reshape(nb, seq, LANES),
        jnp.zeros((nb, HALO, CONV_DIM), F32), jnp.zeros((nb, N_SSM_HEADS, SSM_HEAD_DIM, D_STATE), F32),
        ssm_prm, SSD_CHUNK, SSD_CHUNK)
    bufs = _branch_merge(xp, o_p, s_p.reshape(t_prompt, D_SSM), proj_p, w, 1024, 256,
                         t_all, 0, None)
    keep = CONV_WIDTH - 1
    conv_p = proj_p.reshape(nb, seq, PROJ_WIDTH)[:, seq - keep:, COL_X:COL_X + CONV_DIM]

    xs = x_sample.reshape(db * dec_seq, D_MODEL)
    proj_s, dt_s, qn_s, kn_s, _, _ = _layer_tokens(xs, w, n_tok, n_tok)
    v_s = proj_s[:, COL_V:COL_V + V_WIDTH]
    o_s = _attn_sample(qn_s, kn_s, v_s, cache_k[layer], cache_v[layer], page_table, lam_vecs, sw)
    rows_s = SUBLANES
    pad_rows = ((0, 0), (0, rows_s - dec_seq), (0, 0))
    src_s = jnp.pad(proj_s[:, COL_Z:COL_GA].reshape(db, dec_seq, COL_GA - COL_Z), pad_rows)
    halo_s = jnp.pad(state_conv[layer], ((0, 0), (HALO - (CONV_WIDTH - 1), 0), (0, 0)))
    cols_s = ((COL_X - COL_Z) // D_SSM, (COL_B - COL_Z) // (N_GROUPS * D_STATE),
              (COL_C - COL_Z) // (N_GROUPS * D_STATE), 0)
    s_s, ssm_s = _ssd(
        src_s, cols_s, jnp.pad(dt_s.reshape(db, dec_seq, LANES), pad_rows), halo_s,
        state_ssm[layer], ssm_prm, rows_s, dec_seq)
    s_s = s_s[:, :dec_seq].reshape(n_tok, D_SSM)
    h_all, u_all, route = _branch_merge(xs, o_s, s_s, proj_s, w, n_tok, n_tok,
                                        t_all, t_prompt, bufs)

    n_tiles = 2 * t_all // MOE_TILE + N_EXPERTS
    plan = _route_plan(route, n_tiles)
    o_sorted = _gmm(u_all.reshape(t_all, ROW_CHUNKS, LANES), plan, w["w_gate"], w["w_up"], w["w_down"], n_tiles)
    y_p, y_s = _combine(route, h_all, o_sorted, plan, t_prompt)
    conv_s =proj_s.reshape(db, dec_seq, PROJ_WIDTH)[:, dec_seq - keep:, COL_X:COL_X + CONV_DIM]

    return (
        y_p.reshape(nb, seq, D_MODEL),
        y_s.reshape(db, dec_seq, D_MODEL),
        kn_p.reshape(1, nb, seq, N_KV_HEADS, 2 * HEAD_DIM),
        proj_p[:, COL_V:COL_V + V_WIDTH].reshape(1, nb, seq, N_KV_HEADS, V_DIM),
        ssm_p.reshape(1, nb, N_SSM_HEADS, SSM_HEAD_DIM, D_STATE),
        conv_p[None],
        kn_s.reshape(1, db, dec_seq, N_KV_HEADS, 2 * HEAD_DIM),
        v_s.reshape(1, db, dec_seq, N_KV_HEADS, V_DIM),
        ssm_s.reshape(1, db, N_SSM_HEADS, SSM_HEAD_DIM, D_STATE),
        conv_s[None],
    )
```

```python
import functools
import math

import jax
import jax.numpy as jnp
import ml_dtypes
import numpy as np
from jax import lax
from jax.experimental import pallas as pl
from jax.experimental.pallas import tpu as pltpu

F32 = jnp.float32
BF16 = jnp.bfloat16

D_MODEL = 2048
N_HEADS = 8
N_KV_HEADS = 4
GQA_REP = N_HEADS // N_KV_HEADS
HEAD_DIM = 64
V_DIM = 2 * HEAD_DIM
Q_WIDTH = N_HEADS * 2 * HEAD_DIM
K_WIDTH = N_KV_HEADS * 2 * HEAD_DIM
V_WIDTH = N_KV_HEADS * V_DIM
ATT_WIDTH = N_HEADS * V_DIM
D_SSM = D_MODEL
SSM_HEAD_DIM = 64
N_SSM_HEADS = D_SSM // SSM_HEAD_DIM
N_GROUPS = 4
HEADS_PER_GROUP = N_SSM_HEADS // N_GROUPS
D_STATE = 128
CONV_WIDTH = 4
CONV_DIM = D_SSM + 2 * N_GROUPS * D_STATE
SSD_CHUNK = 128
N_EXPERT_GROUPS = 4
EXPERTS_PER_GROUP = 8
N_EXPERTS = N_EXPERT_GROUPS * EXPERTS_PER_GROUP
D_EXPERT = D_MODEL // 4
PAGE_SIZE = 128
EPS = 1e-6
LAM_INIT = 0.8 - 0.6 * math.exp(-0.3 * 0)

LANES = 128
SUBLANES = 8
NEG_BIG = -1e30
VMEM_LIMIT = 56 * 1024 * 1024

COL_Q = 0
COL_K = COL_Q + Q_WIDTH
COL_V = COL_K + K_WIDTH
COL_Z = COL_V + V_WIDTH
COL_X = COL_Z + D_SSM
COL_B = COL_X + D_SSM
COL_C = COL_B + N_GROUPS * D_STATE
COL_GA = COL_C + N_GROUPS * D_STATE
COL_GS = COL_GA + D_MODEL
PROJ_WIDTH = COL_GS + D_MODEL

ALIBI_SLOPES = [2.0 ** (-8.0 * (h + 1) / N_HEADS) for h in range(N_HEADS)]


def _cparams(sem):
    return pltpu.CompilerParams(dimension_semantics=sem, vmem_limit_bytes=VMEM_LIMIT)


def _dot(a, b):
    return jnp.dot(a, b, preferred_element_type=F32)


def _dot_nt(a, b):
    return lax.dot_general(a, b, (((1,), (1,)), ((), ())), preferred_element_type=F32)


def _dot_tn(a, b):
    return lax.dot_general(a, b, (((0,), (0,)), ((), ())), preferred_element_type=F32)


def _split2(x):
    hi = x.astype(BF16)
    lo = (x - hi.astype(F32)).astype(BF16)
    return hi, lo


def _split3(x):
    hi = x.astype(BF16)
    r = x - hi.astype(F32)
    mid = r.astype(BF16)
    lo = (r - mid.astype(F32)).astype(BF16)
    return hi, mid, lo


def _dot_x2(x, sel):
    hi, lo = _split2(x)
    return _dot(hi, sel) + _dot(lo, sel)


def _dot_x3(x, sel):
    hi, mid, lo = _split3(x)
    return _dot(hi, sel) + _dot(mid, sel) + _dot(lo, sel)


def _sigmoid(x):
    return 1.0 / (1.0 + jnp.exp(-x))


def _silu(x):
    return x * _sigmoid(x)


def _softplus(x):
    return jnp.maximum(x, 0.0) + jnp.log1p(jnp.exp(-jnp.abs(x)))


NORM_ROWS = 256


def _proj_kernel(x_ref, nw_ref, wa_ref, wb_ref, wdt_ref, o_ref, dt_ref, u_scr, *, n_a):
    j = pl.program_id(1)

    @pl.when(j == 0)
    def _():
        tm = x_ref.shape[0]
        for lo in range(0, tm, min(tm, NORM_ROWS)):
            hi = lo + min(tm, NORM_ROWS)
            x = x_ref[lo:hi, :]
            ms = jnp.mean(x * x, axis=-1, keepdims=True)
            u_scr[lo:hi, :] = (x * lax.rsqrt(ms + EPS) * nw_ref[...]).astype(BF16)
        dt_ref[...] = _dot(u_scr[...], wdt_ref[...])

    @pl.when(j < n_a)
    def _():
        o_ref[...] = _dot(u_scr[...], wa_ref[...])

    @pl.when(j >= n_a)
    def _():
        o_ref[...] = _dot(u_scr[...], wb_ref[...])


def _proj(x, norm_w, w_a, w_b, w_dt, tm, tn):
    t = x.shape[0]
    n_a = w_a.shape[1] // tn
    return pl.pallas_call(
        functools.partial(_proj_kernel, n_a=n_a),
        grid=(t // tm, PROJ_WIDTH // tn),
        in_specs=[
            pl.BlockSpec((tm, D_MODEL), lambda i, j: (i, 0)),
            pl.BlockSpec((1, D_MODEL), lambda i, j: (0, 0)),
            pl.BlockSpec((D_MODEL, tn), lambda i, j: (0, jnp.minimum(j, n_a - 1))),
            pl.BlockSpec((D_MODEL, tn), lambda i, j: (0, jnp.maximum(j - n_a, 0))),
            pl.BlockSpec((D_MODEL, LANES), lambda i, j: (0, 0)),
        ],
        out_specs=[
            pl.BlockSpec((tm, tn), lambda i, j: (i, j)),
            pl.BlockSpec((tm, LANES), lambda i, j: (i, 0)),
        ],
        out_shape=[
            jax.ShapeDtypeStruct((t, PROJ_WIDTH), F32),
            jax.ShapeDtypeStruct((t, LANES), F32),
        ],
        scratch_shapes=[pltpu.VMEM((tm, D_MODEL), BF16)],
        compiler_params=_cparams(("arbitrary", "arbitrary")),
        name="proj",
    )(x, norm_w, w_a, w_b, w_dt)


LOG2E = math.log2(math.e)
Q_SCALE = LOG2E * HEAD_DIM ** -0.5


def _qknorm_kernel(p_ref, qw_ref, kw_ref, g_ref, qn_ref, kn_ref, kb_ref, vb_ref):
    gsum = g_ref[...]
    n_q = Q_WIDTH // LANES
    for c in range((Q_WIDTH + K_WIDTH) // LANES):
        x = p_ref[:, c * LANES:(c + 1) * LANES]
        ss = _dot_x2(x * x, gsum)
        y = x * lax.rsqrt(ss * (1.0 / HEAD_DIM) + EPS)
        if c < n_q:
            qn_ref[:, c * LANES:(c + 1) * LANES] = (y * qw_ref[...] * Q_SCALE).astype(BF16)
        else:
            kn = y * kw_ref[...]
            kn_ref[:, (c - n_q) * LANES:(c - n_q + 1) * LANES] = kn
            kb_ref[:, (c - n_q) * LANES:(c - n_q + 1) * LANES] = kn.astype(BF16)
    vb_ref[...] = p_ref[:, COL_V:COL_V + V_WIDTH].astype(BF16)


def _qk_norm(proj, q_norm_w, k_norm_w, tm):
    t = proj.shape[0]
    group = np.kron(np.eye(LANES // HEAD_DIM), np.ones((HEAD_DIM, HEAD_DIM)))
    qw = jnp.tile(q_norm_w, LANES // HEAD_DIM).reshape(1, LANES)
    kw = jnp.tile(k_norm_w, LANES // HEAD_DIM).reshape(1, LANES)
    return pl.pallas_call(
        _qknorm_kernel,
        grid=(t // tm,),
        in_specs=[
            pl.BlockSpec((tm, Q_WIDTH + K_WIDTH + V_WIDTH), lambda i: (i, 0)),
            pl.BlockSpec((1, LANES), lambda i: (0, 0)),
            pl.BlockSpec((1, LANES), lambda i: (0, 0)),
            pl.BlockSpec((LANES, LANES), lambda i: (0, 0)),
        ],
        out_specs=[
            pl.BlockSpec((tm, Q_WIDTH), lambda i: (i, 0)),
            pl.BlockSpec((tm, K_WIDTH), lambda i: (i, 0)),
            pl.BlockSpec((tm, K_WIDTH), lambda i: (i, 0)),
            pl.BlockSpec((tm, V_WIDTH), lambda i: (i, 0)),
        ],
        out_shape=[
            jax.ShapeDtypeStruct((t, Q_WIDTH), BF16),
            jax.ShapeDtypeStruct((t, K_WIDTH), F32),
            jax.ShapeDtypeStruct((t, K_WIDTH), BF16),
            jax.ShapeDtypeStruct((t, V_WIDTH), BF16),
        ],
        compiler_params=_cparams(("arbitrary",)),
        name="qk_norm",
    )(proj, qw, kw, jnp.asarray(group, BF16))


def _diff_lambda(lam_ref):
    lamv = lam_ref[...]
    s1 = jnp.sum(lamv[0:1] * lamv[1:2], axis=1, keepdims=True)
    s2 = jnp.sum(lamv[2:3] * lamv[3:4], axis=1, keepdims=True)
    return jnp.exp(s1) - jnp.exp(s2) + LAM_INIT


def _subln(o, w):
    ms = jnp.mean(o * o, axis=-1, keepdims=True)
    return o * lax.rsqrt(ms + EPS) * w * (1.0 - LAM_INIT)


N_SLOPE_PARTS = 3


def _bf16_parts(x, n):
    parts, rem = [], np.float32(x)
    for _ in range(n):
        p = np.float32(rem.astype(ml_dtypes.bfloat16))
        parts.append(float(p))
        rem = np.float32(rem - p)
    return parts


def _alibi_tables(tk):
    qcols = np.zeros((N_HEADS, 16, LANES), np.float32)
    csum = np.zeros((N_HEADS,), np.float32)
    for h, slope in enumerate(ALIBI_SLOPES):
        parts = _bf16_parts(slope * LOG2E, N_SLOPE_PARTS)
        csum[h] = np.float32(sum(np.float32(p) for p in parts))
        for i, p in enumerate(parts):
            qcols[h, :, i] = p * LANES
            qcols[h, :, N_SLOPE_PARTS + i] = p
    pos = np.arange(tk)
    kcols = np.zeros((tk, LANES), np.float32)
    kcols[:, 0:N_SLOPE_PARTS] = (pos // LANES)[:, None]
    kcols[:, N_SLOPE_PARTS:2 * N_SLOPE_PARTS] = (pos % LANES)[:, None]
    return jnp.asarray(qcols, BF16), jnp.asarray(kcols, BF16), jnp.asarray(csum, F32)


def _attn_p_kernel(qi_ref, ki_ref, cf_ref, q_ref, k_ref, v_ref, qc_ref, kc_ref, lam_ref, sw_ref,
                   o_ref, qa_scr, m_scr, l_scr, acc_scr, *, tq):
    g = pl.program_id(0)
    t = pl.program_id(1)
    qi = qi_ref[t]
    ki = ki_ref[t]
    n_chunk = tq // LANES

    @pl.when(ki == 0)
    def _():
        lane = lax.broadcasted_iota(jnp.int32, (tq, LANES), 1)
        for r in range(GQA_REP):
            qq = q_ref[:, r * LANES:(r + 1) * LANES]
            qc = jnp.broadcast_to(qc_ref[r, 0:1, :], (tq, LANES))
            for c in range(2):
                idx = 2 * r + c
                keep = (lane < HEAD_DIM) if c == 0 else (lane >= HEAD_DIM)
                qa_scr[idx * tq:(idx + 1) * tq, 0:LANES] = jnp.where(keep, qq, jnp.zeros_like(qq))
                qa_scr[idx * tq:(idx + 1) * tq, LANES:2 * LANES] = qc
        m_scr[...] = jnp.full(m_scr.shape, NEG_BIG, F32)
        l_scr[...] = jnp.zeros(l_scr.shape, F32)
        acc_scr[...] = jnp.zeros(acc_scr.shape, F32)

    def step(diag):
        k_aug = jnp.concatenate([k_ref[...], kc_ref[...]], axis=1)
        block_dist = ((qi - ki) * tq).astype(F32)
        for r in range(GQA_REP):
            off = -cf_ref[g * GQA_REP + r] * block_dist
            lo, hi = r * 2 * tq, (r + 1) * 2 * tq
            s = _dot_nt(qa_scr[lo:hi, :], k_aug)
            if diag:
                row_in = lax.broadcasted_iota(jnp.int32, (2 * tq, tq), 0) & (tq - 1)
                col = lax.broadcasted_iota(jnp.int32, (2 * tq, tq), 1)
                s = jnp.where(col <= row_in, s, NEG_BIG)
            chunks = [s[:, j * LANES:(j + 1) * LANES] for j in range(n_chunk)]
            m_prev = m_scr[lo:hi]
            m_blk = jnp.max(functools.reduce(jnp.maximum, chunks), axis=1, keepdims=True) + off
            m_new = jnp.maximum(m_prev, m_blk)
            alpha = jnp.exp2(m_prev - m_new)
            m_sub = m_new - off
            pj = [jnp.exp2(ch - m_sub) for ch in chunks]
            l_scr[lo:hi] = alpha * l_scr[lo:hi] + functools.reduce(jnp.add, pj)
            m_scr[lo:hi] = m_new
            pv = _dot(jnp.concatenate(pj, axis=1).astype(BF16), v_ref[...])
            acc_scr[lo:hi] = alpha * acc_scr[lo:hi] + pv

    @pl.when(ki < qi)
    def _():
        step(False)

    @pl.when(ki == qi)
    def _():
        step(True)
        lam = _diff_lambda(lam_ref)
        for r in range(GQA_REP):
            i1, i2 = 2 * r * tq, (2 * r + 1) * tq
            l1 = jnp.sum(l_scr[i1:i1 + tq], axis=1, keepdims=True)
            l2 = jnp.sum(l_scr[i2:i2 + tq], axis=1, keepdims=True)
            o = acc_scr[i1:i1 + tq] / l1 - lam * (acc_scr[i2:i2 + tq] / l2)
            o_ref[:, r * LANES:(r + 1) * LANES] = _subln(o, sw_ref[...]).astype(BF16)


def _attn_prompt(qn, kb, vb, lam_vecs, subln_w, tq):
    t = qn.shape[0]
    nq = t // tq
    pairs = [(i, j) for i in range(nq) for j in range(i + 1)]
    qi_tab = jnp.asarray([p[0] for p in pairs], jnp.int32)
    ki_tab = jnp.asarray([p[1] for p in pairs], jnp.int32)
    qcols, kcols, csum = _alibi_tables(tq)
    n_sub = 2 * GQA_REP
    grid_spec = pltpu.PrefetchScalarGridSpec(
        num_scalar_prefetch=3,
        grid=(N_KV_HEADS, len(pairs)),
        in_specs=[
            pl.BlockSpec((tq, GQA_REP * LANES), lambda g, t, qi, ki, cf: (qi[t], g)),
            pl.BlockSpec((tq, LANES), lambda g, t, qi, ki, cf: (ki[t], g)),
            pl.BlockSpec((tq, V_DIM), lambda g, t, qi, ki, cf: (ki[t], g)),
            pl.BlockSpec((GQA_REP, 16, LANES), lambda g, t, qi, ki, cf: (g, 0, 0)),
            pl.BlockSpec((tq, LANES), lambda g, t, qi, ki, cf: (0, 0)),
            pl.BlockSpec((4, HEAD_DIM), lambda g, t, qi, ki, cf: (0, 0)),
            pl.BlockSpec((1, V_DIM), lambda g, t, qi, ki, cf: (0, 0)),
        ],
        out_specs=pl.BlockSpec((tq, GQA_REP * V_DIM), lambda g, t, qi, ki, cf: (qi[t], g)),
        scratch_shapes=[
            pltpu.VMEM((n_sub * tq, 2 * LANES), BF16),
            pltpu.VMEM((n_sub * tq, LANES), F32),
            pltpu.VMEM((n_sub * tq, LANES), F32),
            pltpu.VMEM((n_sub * tq, V_DIM), F32),
        ],
    )
    return pl.pallas_call(
        functools.partial(_attn_p_kernel, tq=tq),
        grid_spec=grid_spec,
        out_shape=jax.ShapeDtypeStruct((t, ATT_WIDTH), BF16),
        compiler_params=_cparams(("arbitrary", "arbitrary")),
        name="attn_p",
    )(qi_tab, ki_tab, csum, qn, kb, vb, qcols, kcols, lam_vecs, subln_w.reshape(1, V_DIM))


PAGES_PER_STEP = 16
PAGE_GROUP = PAGES_PER_STEP
ROWS_S = 2 * 4 * N_HEADS


def _attn_s_kernel(pt_ref, q_ref, d0_ref, mask_ref, sl_ref, bn_ref, kn_ref, vn_ref, lam_ref,
                   sw_ref, *rest, n_steps):
    k_refs = [r.at[0] for r in rest[:PAGES_PER_STEP]]
    v_refs = [r.at[0] for r in rest[PAGES_PER_STEP:2 * PAGES_PER_STEP]]
    o_ref = rest[2 * PAGES_PER_STEP]
    m_scr, l_scr, acc_scr = rest[2 * PAGES_PER_STEP + 1:]
    s_id = pl.program_id(1)

    @pl.when(s_id == 0)
    def _():
        m_scr[...] = jnp.full(m_scr.shape, NEG_BIG, F32)
        l_scr[...] = jnp.zeros(l_scr.shape, F32)
        acc_scr[...] = jnp.zeros(acc_scr.shape, F32)

    q = q_ref[0]

    def update(scores, values):
        m_prev = m_scr[...]
        m_new = m_prev
        for sc in scores:
            m_new = jnp.maximum(m_new, jnp.max(sc, axis=1, keepdims=True))
        alpha = jnp.exp2(m_prev - m_new)
        l_new = alpha * l_scr[...]
        acc = alpha * acc_scr[...]
        for sc, vv in zip(scores, values):
            p = jnp.exp2(sc - m_new)
            l_new = l_new + jnp.sum(p, axis=1, keepdims=True)
            acc = acc + _dot(p.astype(BF16), vv)
        m_scr[...] = m_new
        l_scr[...] = l_new
        acc_scr[...] = acc

    for first in range(0, PAGES_PER_STEP, PAGE_GROUP):
        scores, values = [], []
        for i in range(first, first + PAGE_GROUP):
            page_start = ((s_id * PAGES_PER_STEP + i) * PAGE_SIZE).astype(F32)
            bias = sl_ref[...] * (d0_ref[...] - page_start) + mask_ref[...]
            scores.append(_dot_nt(q, k_refs[i][...].astype(BF16)) + bias)
            values.append(v_refs[i][...].astype(BF16))
        update(scores, values)

    @pl.when(s_id == n_steps - 1)
    def _():
        sc = _dot_nt(q, kn_ref[0].astype(BF16)) + bn_ref[...]
        update([sc], [vn_ref[0].astype(BF16)])
        lam = _diff_lambda(lam_ref)
        half = ROWS_S // 2
        o1 = acc_scr[0:half] / l_scr[0:half]
        o2 = acc_scr[half:ROWS_S] / l_scr[half:ROWS_S]
        o_ref[0] = _subln(o1 - lam * o2, sw_ref[...]).astype(BF16)


def _attn_sample(qn_s, kn_s, v_s, cache_k, cache_v, page_table, lam_vecs, subln_w):
    db, n_pages = page_table.shape
    dec_seq = qn_s.shape[0] // db
    past = n_pages * PAGE_SIZE
    n_steps = n_pages // PAGES_PER_STEP
    page_rows = PAGE_SIZE * N_KV_HEADS
    n_phys = cache_k.shape[0]
    ck = cache_k.reshape(n_phys, page_rows, 2 * HEAD_DIM)
    cv = cache_v.reshape(n_phys, page_rows, V_DIM)

    q5 = qn_s.reshape(db, dec_seq, N_HEADS, 2, HEAD_DIM)
    zeros = jnp.zeros_like(q5[:, :, :, 0])
    q_all = jnp.stack([jnp.concatenate([q5[:, :, :, 0], zeros], axis=-1),
                       jnp.concatenate([zeros, q5[:, :, :, 1]], axis=-1)], axis=1)
    q_all = q_all.reshape(db, ROWS_S, LANES)

    r = np.arange(ROWS_S)
    tok_r = (r % (dec_seq * N_HEADS)) // N_HEADS
    head_r = r % N_HEADS
    slope_r = np.asarray(ALIBI_SLOPES)[head_r] * LOG2E
    c = np.arange(page_rows)
    key_c, grp_c = c // N_KV_HEADS, c % N_KV_HEADS
    same = (head_r[:, None] // GQA_REP) == grp_c[None, :]
    d0 = np.broadcast_to(past + tok_r[:, None] - key_c[None, :], (ROWS_S, page_rows))
    mask = np.where(same, 0.0, NEG_BIG)
    sl = np.broadcast_to(-slope_r[:, None], (ROWS_S, 1))
    cn = np.arange(LANES)
    tok_c, grp_n = cn // N_KV_HEADS, cn % N_KV_HEADS
    ok = ((head_r[:, None] // GQA_REP) == grp_n[None, :]) & (tok_c[None, :] <= tok_r[:, None])
    bn = np.where(ok, -slope_r[:, None] * (tok_r[:, None] - tok_c[None, :]), NEG_BIG)

    new_rows = dec_seq * N_KV_HEADS
    kn_pad = jnp.pad(kn_s.reshape(db, new_rows, LANES), ((0, 0), (0, LANES - new_rows), (0, 0)))
    vn_pad = jnp.pad(v_s.reshape(db, new_rows, LANES), ((0, 0), (0, LANES - new_rows), (0, 0)))

    def const(shape):
        return pl.BlockSpec(shape, lambda b, s, pt: (0,) * len(shape))

    def page_spec(i):
        return pl.BlockSpec(
            (1, page_rows, LANES),
            lambda b, s, pt: (pt[b * n_pages + s * PAGES_PER_STEP + i], 0, 0))

    grid_spec = pltpu.PrefetchScalarGridSpec(
        num_scalar_prefetch=1,
        grid=(db, n_steps),
        in_specs=[
            pl.BlockSpec((1, ROWS_S, LANES), lambda b, s, pt: (b, 0, 0)),
            const((ROWS_S, page_rows)),
            const((ROWS_S, page_rows)),
            const((ROWS_S, 1)),
            const((ROWS_S, LANES)),
            pl.BlockSpec((1, LANES, LANES), lambda b, s, pt: (b, 0, 0)),
            pl.BlockSpec((1, LANES, LANES), lambda b, s, pt: (b, 0, 0)),
            const((4, HEAD_DIM)),
            const((1, V_DIM)),
        ] + [page_spec(i) for i in range(PAGES_PER_STEP)] * 2,
        out_specs=pl.BlockSpec((1, ROWS_S // 2, V_DIM), lambda b, s, pt: (b, 0, 0)),
        scratch_shapes=[
            pltpu.VMEM((ROWS_S, 1), F32),
            pltpu.VMEM((ROWS_S, 1), F32),
            pltpu.VMEM((ROWS_S, V_DIM), F32),
        ],
    )
    o = pl.pallas_call(
        functools.partial(_attn_s_kernel, n_steps=n_steps),
        grid_spec=grid_spec,
        out_shape=jax.ShapeDtypeStruct((db, ROWS_S // 2, V_DIM), BF16),
        compiler_params=_cparams(("arbitrary", "arbitrary")),
        name="attn_s",
    )(page_table.reshape(-1), q_all, jnp.asarray(d0, F32), jnp.asarray(mask, F32),
      jnp.asarray(sl, F32), jnp.asarray(bn, F32), kn_pad, vn_pad, lam_vecs,
      subln_w.reshape(1, V_DIM),
      *([ck] * PAGES_PER_STEP), *([cv] * PAGES_PER_STEP))
    return o.reshape(db * dec_seq, ATT_WIDTH)


HALO = SUBLANES


def _ssd_kernel(xs_ref, b_ref, c_ref, z_ref, dt_ref, dtt_ref, halo_ref, init_ref,
                cw_ref, cb_ref, dtb_ref, dtbt_ref, a_ref, at_ref, dsk_ref, nw_ref,
                tri_ref, trit_ref, exp_ref, sel_ref,
                y_ref, fin_ref, win_scr, state_scr, *, rows_in, n_valid):
    ci = pl.program_id(1)
    n_chunks = pl.num_programs(1)
    lc = SSD_CHUNK
    bc_w = N_GROUPS * D_STATE

    @pl.when(ci == 0)
    def _():
        state_scr[...] = init_ref[0].reshape(D_SSM, D_STATE)
        win_scr[0:HALO, :] = halo_ref[0]

    if rows_in < lc:
        win_scr[HALO:HALO + lc, :] = jnp.zeros((lc, CONV_DIM), F32)
    win_scr[HALO:HALO + rows_in, 0:D_SSM] = xs_ref[0]
    win_scr[HALO:HALO + rows_in, D_SSM:D_SSM + bc_w] = b_ref[0]
    win_scr[HALO:HALO + rows_in, D_SSM + bc_w:CONV_DIM] = c_ref[0]

    acc = cb_ref[...]
    for tap in range(CONV_WIDTH):
        off = HALO - (CONV_WIDTH - 1) + tap
        acc = acc + win_scr[off:off + lc, :] * cw_ref[tap:tap + 1, :]
    conv = _silu(acc)
    win_scr[0:HALO, :] = win_scr[lc:lc + HALO, :]
    xs = conv[:, 0:D_SSM]
    bm = conv[:, D_SSM:D_SSM + bc_w].astype(BF16)
    cm = conv[:, D_SSM + bc_w:CONV_DIM].astype(BF16)

    if rows_in < lc:
        dt_in = jnp.concatenate([dt_ref[0], jnp.zeros((lc - rows_in, LANES), F32)], axis=0)
        dtt_in = jnp.concatenate(
            [dtt_ref[0], jnp.zeros((N_SSM_HEADS, lc - rows_in), F32)], axis=1)
    else:
        dt_in, dtt_in = dt_ref[0], dtt_ref[0]
    rowi = lax.broadcasted_iota(jnp.int32, (lc, LANES), 0)
    coli = lax.broadcasted_iota(jnp.int32, (N_SSM_HEADS, lc), 1)
    dt = jnp.where(rowi < n_valid, _softplus(dt_in + dtb_ref[...]), 0.0)
    dtt = jnp.where(coli < n_valid, _softplus(dtt_in + dtbt_ref[...]), 0.0)
    a_cs = _dot_x3_left(tri_ref[...], dt * a_ref[...])
    a_cst = _dot_x3(dtt * at_ref[...], trit_ref[...])
    a_last = a_cs[lc - 1:lc, :]
    exp_cs = jnp.exp(a_cs)
    exp_rest = jnp.exp(a_last - a_cs)
    expand = exp_ref[...]
    dtx = _dot_x2(dt, expand)
    ecx = _dot_x2(exp_cs, expand)
    erx = _dot_x2(exp_rest, expand)
    xc = xs * dtx
    xcb = xc.astype(BF16)
    xcd = (xc * erx).astype(BF16)

    last_t = jnp.exp(a_cst[:, lc - 1:lc])
    rdec = _dot_x2_left(sel_ref[...], jnp.broadcast_to(last_t, (N_SSM_HEADS, D_STATE)))

    tril = (lax.broadcasted_iota(jnp.int32, (lc, lc), 0)
            >= lax.broadcasted_iota(jnp.int32, (lc, lc), 1))
    lane = lax.broadcasted_iota(jnp.int32, (lc, LANES), 1)
    gw = HEADS_PER_GROUP * SSM_HEAD_DIM
    y_parts = []
    for g in range(N_GROUPS):
        bg = bm[:, g * D_STATE:(g + 1) * D_STATE]
        cg = cm[:, g * D_STATE:(g + 1) * D_STATE]
        cb = _dot_nt(cg, bg)
        st = state_scr[g * gw:(g + 1) * gw, :]
        y_off = _dot_nt(cg, st.astype(BF16)) * ecx[:, g * gw:(g + 1) * gw]
        new_st = _dot_tn(xcd[:, g * gw:(g + 1) * gw], bg)
        state_scr[g * gw:(g + 1) * gw, :] = st * rdec[g * gw:(g + 1) * gw, :] + new_st
        for j in range(HEADS_PER_GROUP // 2):
            pair = g * (HEADS_PER_GROUP // 2) + j
            blk = xcb[:, pair * LANES:(pair + 1) * LANES]
            y_pair = None
            for half in range(2):
                h = 2 * pair + half
                seg = a_cs[:, h:h + 1] - a_cst[h:h + 1, :]
                decay = jnp.exp(jnp.where(tril, seg, NEG_BIG))
                mh = (cb * decay).astype(BF16)
                keep = (lane < SSM_HEAD_DIM) if half == 0 else (lane >= SSM_HEAD_DIM)
                part = _dot(mh, jnp.where(keep, blk, jnp.zeros_like(blk)))
                y_pair = part if y_pair is None else y_pair + part
            y_parts.append(y_pair + y_off[:, (pair % (HEADS_PER_GROUP // 2)) * LANES:
                                          (pair % (HEADS_PER_GROUP // 2) + 1) * LANES])
    y = jnp.concatenate(y_parts, axis=1)
    y = y + dsk_ref[...] * xs
    if rows_in < lc:
        z = jnp.concatenate([z_ref[0], jnp.zeros((lc - rows_in, D_SSM), F32)], axis=0)
    else:
        z = z_ref[0]
    y = y * _silu(z)
    gn = D_SSM // N_GROUPS
    outs = []
    for g in range(N_GROUPS):
        yg = y[:, g * gn:(g + 1) * gn]
        ms = jnp.mean(yg * yg, axis=-1, keepdims=True)
        outs.append(yg * lax.rsqrt(ms + EPS) * nw_ref[:, g * gn:(g + 1) * gn])
    out = jnp.concatenate(outs, axis=1).astype(BF16)
    y_ref[0] = out[0:rows_in]

    @pl.when(ci == n_chunks - 1)
    def _():
        fin_ref[0] = state_scr[...].reshape(N_SSM_HEADS, SSM_HEAD_DIM, D_STATE)


def _dot_x3_left(sel, x):
    hi, mid, lo = _split3(x)
    return _dot(sel, hi) + _dot(sel, mid) + _dot(sel, lo)


def _dot_x2_left(sel, x):
    hi, lo = _split2(x)
    return _dot(sel, hi) + _dot(sel, lo)


def _ssd(src, col_blocks, dt_raw, halo, init_state, prm, rows_in, n_valid):
    nb, seq = src.shape[0], src.shape[1]
    n_chunks = max(1, seq // SSD_CHUNK)
    bc_w = N_GROUPS * D_STATE
    dtt = jnp.swapaxes(dt_raw[:, :, :N_SSM_HEADS], 1, 2)
    tri = np.tril(np.ones((SSD_CHUNK, SSD_CHUNK)))
    expand = np.zeros((LANES, D_SSM))
    expand[np.arange(D_SSM) // SSM_HEAD_DIM, np.arange(D_SSM)] = 1.0
    sel = expand[:N_SSM_HEADS].T
    cx, cbk, cck, cz = col_blocks

    def const(shape):
        return pl.BlockSpec(shape, lambda b, c: (0,) * len(shape))

    return pl.pallas_call(
        functools.partial(_ssd_kernel, rows_in=rows_in, n_valid=n_valid),
        grid=(nb, n_chunks),
        in_specs=[
            pl.BlockSpec((1, rows_in, D_SSM), lambda b, c: (b, c, cx)),
            pl.BlockSpec((1, rows_in, bc_w), lambda b, c: (b, c, cbk)),
            pl.BlockSpec((1, rows_in, bc_w), lambda b, c: (b, c, cck)),
            pl.BlockSpec((1, rows_in, D_SSM), lambda b, c: (b, c, cz)),
            pl.BlockSpec((1, rows_in, LANES), lambda b, c: (b, c, 0)),
            pl.BlockSpec((1, N_SSM_HEADS, rows_in), lambda b, c: (b, 0, c)),
            pl.BlockSpec((1, HALO, CONV_DIM), lambda b, c: (b, 0, 0)),
            pl.BlockSpec((1, N_SSM_HEADS, SSM_HEAD_DIM, D_STATE), lambda b, c: (b, 0, 0, 0)),
            const((CONV_WIDTH, CONV_DIM)),
            const((1, CONV_DIM)),
            const((1, LANES)),
            const((N_SSM_HEADS, 1)),
            const((1, LANES)),
            const((N_SSM_HEADS, 1)),
            const((1, D_SSM)),
            const((1, D_SSM)),
            const((SSD_CHUNK, SSD_CHUNK)),
            const((SSD_CHUNK, SSD_CHUNK)),
            const((LANES, D_SSM)),
            const((D_SSM, N_SSM_HEADS)),
        ],
        out_specs=[
            pl.BlockSpec((1, rows_in, D_SSM), lambda b, c: (b, c, 0)),
            pl.BlockSpec((1, N_SSM_HEADS, SSM_HEAD_DIM, D_STATE), lambda b, c: (b, 0, 0, 0)),
        ],
        out_shape=[
            jax.ShapeDtypeStruct((nb, seq, D_SSM), BF16),
            jax.ShapeDtypeStruct((nb, N_SSM_HEADS, SSM_HEAD_DIM, D_STATE), F32),
        ],
        scratch_shapes=[
            pltpu.VMEM((HALO + SSD_CHUNK, CONV_DIM), F32),
            pltpu.VMEM((D_SSM, D_STATE), F32),
        ],
        compiler_params=_cparams(("arbitrary", "arbitrary")),
        name="ssd",
    )(src, src, src, src, dt_raw, dtt, halo, init_state,
      prm["conv_w"], prm["conv_b"], prm["dt_bias"], prm["dt_bias_t"], prm["a"], prm["a_t"],
      prm["d_skip"], prm["ssm_norm_w"],
      jnp.asarray(tri, BF16), jnp.asarray(tri.T, BF16), jnp.asarray(expand, BF16),
      jnp.asarray(sel, BF16))


def _merge_kernel(o_ref, s_ref, wa_ref, ws_ref, ga_ref, gs_ref, out_ref):
    a = _dot(o_ref[...], wa_ref[...])
    s = _dot(s_ref[...], ws_ref[...])
    out_ref[...] = (_sigmoid(ga_ref[...]) * a + _sigmoid(gs_ref[...]) * s).astype(BF16)


def _merge(o, s, wa, ws, proj, tm, tn):
    t = o.shape[0]
    ga0, gs0 = COL_GA // tn, COL_GS // tn
    return pl.pallas_call(
        _merge_kernel,
        grid=(t // tm, D_MODEL // tn),
        in_specs=[
            pl.BlockSpec((tm, ATT_WIDTH), lambda i, j: (i, 0)),
            pl.BlockSpec((tm, D_SSM), lambda i, j: (i, 0)),
            pl.BlockSpec((ATT_WIDTH, tn), lambda i, j: (0, j)),
            pl.BlockSpec((D_SSM, tn), lambda i, j: (0, j)),
            pl.BlockSpec((tm, tn), lambda i, j: (i, ga0 + j)),
            pl.BlockSpec((tm, tn), lambda i, j: (i, gs0 + j)),
        ],
        out_specs=pl.BlockSpec((tm, tn), lambda i, j: (i, j)),
        out_shape=jax.ShapeDtypeStruct((t, D_MODEL), BF16),
        compiler_params=_cparams(("arbitrary", "arbitrary")),
        name="merge",
    )(o, s, wa, ws, proj, proj)


ROUTE_E1, ROUTE_E2, ROUTE_W1, ROUTE_W2 = 0, 1, 2, 3
ROW_CHUNKS = D_MODEL // LANES


def _resid_kernel(x_ref, m_ref, wo_ref, nw_ref, wrh_ref, wrl_ref, br_ref, *rest, n_real):
    h_ref, u_ref, route_ref = rest[-3:]

    @pl.when(pl.program_id(0) >= n_real)
    def _():
        h_ref[...] = jnp.zeros(h_ref.shape, F32)
        u_ref[...] = jnp.zeros(u_ref.shape, F32)
        route_ref[...] = jnp.zeros(route_ref.shape, F32)

    @pl.when(pl.program_id(0) < n_real)
    def _():
        _resid_tile(x_ref, m_ref, wo_ref, nw_ref, wrh_ref, wrl_ref, br_ref,
                    h_ref, u_ref, route_ref)


def _resid_tile(x_ref, m_ref, wo_ref, nw_ref, wrh_ref, wrl_ref, br_ref, h_ref, u_ref, route_ref):
    h = x_ref[...] + _dot(m_ref[...], wo_ref[...])
    h_ref[...] = h
    ms = jnp.mean(h * h, axis=-1, keepdims=True)
    u = h * lax.rsqrt(ms + EPS) * nw_ref[...]
    u_hi, u_lo = _split2(u)
    u_ref[...] = u
    logits = (_dot(u_hi, wrh_ref[...]) + _dot(u_lo, wrh_ref[...])
              + _dot(u_hi, wrl_ref[...]) + br_ref[...])
    lane = lax.broadcasted_iota(jnp.int32, logits.shape, 1)
    lane_f = lane.astype(F32)
    far = float(2 * LANES)

    def first_max(vals):
        top = jnp.max(vals, axis=1, keepdims=True)
        idx = jnp.min(jnp.where(vals == top, lane_f, far), axis=1, keepdims=True)
        return top, idx

    is_group = (lane >= N_EXPERTS) & (lane < N_EXPERTS + N_EXPERT_GROUPS)
    gl = jnp.where(is_group, logits, NEG_BIG)
    g_top, g_idx = first_max(gl)
    g_p = 1.0 / jnp.sum(jnp.exp(gl - g_top), axis=1, keepdims=True)
    lo_lane = (g_idx - N_EXPERTS) * EXPERTS_PER_GROUP
    in_group = (lane_f >= lo_lane) & (lane_f < lo_lane + EXPERTS_PER_GROUP)
    el = jnp.where(in_group, logits, NEG_BIG)
    m1, i1 = first_max(el)
    el2 = jnp.where(lane_f == i1, NEG_BIG, el)
    m2, i2 = first_max(el2)
    e = jnp.exp(m2 - m1)
    w1 = 1.0 / (1.0 + e)
    w2 = e / (1.0 + e)
    route = jnp.where(lane == ROUTE_E1, i1, 0.0)
    route = jnp.where(lane == ROUTE_E2, i2, route)
    route = jnp.where(lane == ROUTE_W1, g_p * w1, route)
    route_ref[...] = jnp.where(lane == ROUTE_W2, g_p * w2, route)


def _resid(x, merged, wo, norm_w, wr_hi, wr_lo, br, tm, t_all, row_off, bufs):
    t = x.shape[0]
    blk_off = row_off // tm
    n_real = t // tm
    n_fill = pl.cdiv(t_all - t, tm) if bufs is None else 0

    def const(shape):
        return pl.BlockSpec(shape, lambda i: (0,) * len(shape))

    in_specs = [
        pl.BlockSpec((tm, D_MODEL), lambda i: (jnp.minimum(i, n_real - 1), 0)),
        pl.BlockSpec((tm, D_MODEL), lambda i: (jnp.minimum(i, n_real - 1), 0)),
        const((D_MODEL, D_MODEL)),
        const((1, D_MODEL)),
        const((D_MODEL, LANES)),
        const((D_MODEL, LANES)),
        const((1, LANES)),
    ]
    args = [x, merged, wo, norm_w, wr_hi, wr_lo, br]
    aliases = {}
    if bufs is not None:
        aliases = {len(args) + k: k for k in range(len(bufs))}
        in_specs += [pl.BlockSpec(memory_space=pl.ANY)] * len(bufs)
        args += list(bufs)
    return pl.pallas_call(
        functools.partial(_resid_kernel, n_real=n_real),
        grid=(n_real + n_fill,),
        in_specs=in_specs,
        out_specs=[
            pl.BlockSpec((tm, D_MODEL), lambda i: (i + blk_off, 0)),
            pl.BlockSpec((tm, D_MODEL), lambda i: (i + blk_off, 0)),
            pl.BlockSpec((tm, LANES), lambda i: (i + blk_off, 0)),
        ],
        out_shape=[
            jax.ShapeDtypeStruct((t_all, D_MODEL), F32),
            jax.ShapeDtypeStruct((t_all, D_MODEL), F32),
            jax.ShapeDtypeStruct((t_all, LANES), F32),
        ],
        input_output_aliases=aliases,
        compiler_params=_cparams(("arbitrary",)),
        name="resid",
    )(*args)


MOE_TILE = 256
COMB_TILE = 128
DMA_UNROLL = 8


def _route_plan(route, n_tiles):
    n_pairs = 2 * route.shape[0]
    pair_e = route[:, ROUTE_E1:ROUTE_E2 + 1].astype(jnp.int32).reshape(-1)
    onehot = (pair_e[:, None] == jnp.arange(N_EXPERTS, dtype=jnp.int32)[None, :]).astype(jnp.int32)
    csum = jnp.cumsum(onehot, axis=0)
    rank = jnp.sum((csum - onehot) * onehot, axis=1)
    tiles_e = (csum[-1] + MOE_TILE - 1) // MOE_TILE
    tile_end = jnp.cumsum(tiles_e)
    first_row = (tile_end - tiles_e) * MOE_TILE
    slot = jnp.sum(onehot * first_row[None, :], axis=1) + rank
    tok_of_slot = jnp.zeros(((n_tiles + 1) * MOE_TILE,), jnp.int32).at[slot].set(
        jnp.arange(n_pairs, dtype=jnp.int32) // 2)
    tile_start = (tile_end - tiles_e).astype(jnp.int32)
    slot_tab = slot.reshape(-1, COMB_TILE, 2).transpose(0, 2, 1).reshape(-1).astype(jnp.int32)
    return tile_start, tiles_e.astype(jnp.int32), tok_of_slot, slot_tab


ROW_DMA_PRIORITY = 0
BULK_DMA_PRIORITY = 1


def _gmm_kernel(ts_ref, ne_ref, tok_ref, u_hbm, wg_hbm, wu_hbm, wd_hbm, o_hbm,
                xbuf, obuf, gsem, osem, wgb, wub, wdb, wg_buf, wu_buf, wd_buf, wsem,
                *, n_tiles):
    e = pl.program_id(0)
    last = pl.num_programs(0) - 1
    n_used = ts_ref[last] + ne_ref[last]

    def row_copy(tile, r, buf):
        tok = tok_ref[tile * MOE_TILE + r]
        return pltpu.make_async_copy(
            u_hbm.at[tok], xbuf.at[buf, r // SUBLANES, :, r % SUBLANES, :], gsem.at[buf])

    def out_copy(tile, buf):
        return pltpu.make_async_copy(
            obuf.at[buf], o_hbm.at[pl.ds(tile * MOE_TILE, MOE_TILE), :], osem.at[buf])

    def gather_wait(tile, buf):
        del tile
        pltpu.make_async_copy(obuf.at[0], obuf.at[1], gsem.at[buf]).wait()

    @pl.when(e == 0)
    def _():
        def body(r, carry):
            row_copy(0, r, 0).start(priority=ROW_DMA_PRIORITY)
            return carry
        lax.fori_loop(0, MOE_TILE, body, 0, unroll=DMA_UNROLL)

    def weight_copies(expert, buf):
        return (pltpu.make_async_copy(wg_hbm.at[expert], wg_buf.at[buf], wsem.at[buf]),
                pltpu.make_async_copy(wu_hbm.at[expert], wu_buf.at[buf], wsem.at[buf]),
                pltpu.make_async_copy(wd_hbm.at[expert], wd_buf.at[buf], wsem.at[buf]))

    wcur = lax.rem(e, 2)

    @pl.when(e == 0)
    def _():
        for cp in weight_copies(0, 0):
            cp.start(priority=BULK_DMA_PRIORITY)

    @pl.when(e < last)
    def _():
        for cp in weight_copies(e + 1, 1 - wcur):
            cp.start(priority=BULK_DMA_PRIORITY)

    for cp in weight_copies(e, wcur):
        cp.wait()
    wgb[...] = wg_buf[wcur].astype(BF16)
    wub[...] = wu_buf[wcur].astype(BF16)
    wdb[...] = wd_buf[wcur].astype(BF16)

    def tile_body(j, carry):
        t = ts_ref[e] + j
        cur = lax.rem(t, 2)

        @pl.when(t >= 2)
        def _():
            out_copy(t - 2, cur).wait()

        gather_wait(t, cur)
        x = jnp.concatenate(
            [xbuf[cur, :, c].reshape(MOE_TILE, LANES) for c in range(ROW_CHUNKS)],
            axis=1).astype(BF16)
        for r in range(MOE_TILE):
            row_copy(t + 1, r, 1 - cur).start(priority=ROW_DMA_PRIORITY)
        hid = _silu(_dot(x, wgb[...])) * _dot(x, wub[...])
        obuf[cur] = _dot(hid.astype(BF16), wdb[...])
        out_copy(t, cur).start(priority=BULK_DMA_PRIORITY)
        return carry

    lax.fori_loop(0, ne_ref[e], tile_body, 0)

    @pl.when(e == last)
    def _():
        gather_wait(n_used, lax.rem(n_used, 2))
        out_copy(n_used - 2, lax.rem(n_used, 2)).wait()
        out_copy(n_used - 1, lax.rem(n_used - 1, 2)).wait()
        obuf[0] = jnp.zeros(obuf.shape[1:], F32)

        def fill(t, carry):
            cp = out_copy(t, 0)
            cp.start()
            cp.wait()
            return carry
        lax.fori_loop(n_used, n_tiles + 1, fill, 0)


def _gmm(u_all, plan, wg, wu, wd, n_tiles):
    tile_start, tiles_e, tok_of_slot, _ = plan
    assert 2 * u_all.shape[0] >= 2 * MOE_TILE
    grid_spec = pltpu.PrefetchScalarGridSpec(
        num_scalar_prefetch=3,
        grid=(N_EXPERTS,),
        in_specs=[
            pl.BlockSpec(memory_space=pl.ANY),
            pl.BlockSpec(memory_space=pl.ANY),
            pl.BlockSpec(memory_space=pl.ANY),
            pl.BlockSpec(memory_space=pl.ANY),
        ],
        out_specs=pl.BlockSpec(memory_space=pl.ANY),
        scratch_shapes=[
            pltpu.VMEM((2, MOE_TILE // SUBLANES, ROW_CHUNKS, SUBLANES, LANES), F32),
            pltpu.VMEM((2, MOE_TILE, D_MODEL), F32),
            pltpu.SemaphoreType.DMA((2,)),
            pltpu.SemaphoreType.DMA((2,)),
            pltpu.VMEM((D_MODEL, D_EXPERT), BF16),
            pltpu.VMEM((D_MODEL, D_EXPERT), BF16),
            pltpu.VMEM((D_EXPERT, D_MODEL), BF16),
            pltpu.VMEM((2, D_MODEL, D_EXPERT), F32),
            pltpu.VMEM((2, D_MODEL, D_EXPERT), F32),
            pltpu.VMEM((2, D_EXPERT, D_MODEL), F32),
            pltpu.SemaphoreType.DMA((2,)),
        ],
    )
    return pl.pallas_call(
        functools.partial(_gmm_kernel, n_tiles=n_tiles),
        grid_spec=grid_spec,
        out_shape=jax.ShapeDtypeStruct(((n_tiles + 1) * MOE_TILE, D_MODEL), F32),
        compiler_params=_cparams(("arbitrary",)),
        name="gmm",
    )(tile_start, tiles_e, tok_of_slot, u_all, wg, wu, wd)


def _combine_kernel(slot_ref, route_ref, h_ref, o_hbm, yp_ref, ys_ref, gbuf, sem, *, n_prompt):
    i = pl.program_id(0)
    n = pl.num_programs(0)
    cur = lax.rem(i, 2)
    rows = 2 * COMB_TILE

    def row_copy(tile, j, buf):
        slot = slot_ref[tile * rows + j]
        return pltpu.make_async_copy(
            o_hbm.at[pl.ds(slot, 1), :], gbuf.at[buf, pl.ds(j, 1), :], sem.at[buf])

    def issue(tile, buf):
        def body(j, carry):
            row_copy(tile, j, buf).start()
            return carry
        lax.fori_loop(0, rows, body, 0, unroll=DMA_UNROLL)

    def wait(tile, buf):
        del tile
        pltpu.make_async_copy(o_hbm.at[pl.ds(0, rows), :], gbuf.at[buf], sem.at[buf]).wait()

    @pl.when(i == 0)
    def _():
        issue(0, 0)

    @pl.when(i + 1 < n)
    def _():
        issue(i + 1, 1 - cur)

    wait(i, cur)
    w1 = route_ref[:, ROUTE_W1:ROUTE_W1 + 1]
    w2 = route_ref[:, ROUTE_W2:ROUTE_W2 + 1]
    y = h_ref[...] + w1 * gbuf[cur, 0:COMB_TILE, :] + w2 * gbuf[cur, COMB_TILE:rows, :]

    @pl.when(i < n_prompt)
    def _():
        yp_ref[...] = y

    @pl.when(i >= n_prompt)
    def _():
        ys_ref[...] = y


def _combine(route, h_all, o_sorted, plan, t_prompt):
    t_all = h_all.shape[0]
    n_prompt = t_prompt // COMB_TILE
    slot_tab = plan[3]
    grid_spec = pltpu.PrefetchScalarGridSpec(
        num_scalar_prefetch=1,
        grid=(t_all // COMB_TILE,),
        in_specs=[
            pl.BlockSpec((COMB_TILE, LANES), lambda i, st: (i, 0)),
            pl.BlockSpec((COMB_TILE, D_MODEL), lambda i, st: (i, 0)),
            pl.BlockSpec(memory_space=pl.ANY),
        ],
        out_specs=[
            pl.BlockSpec((COMB_TILE, D_MODEL), lambda i, st: (jnp.minimum(i, n_prompt - 1), 0)),
            pl.BlockSpec((COMB_TILE, D_MODEL), lambda i, st: (0, 0)),
        ],
        scratch_shapes=[
            pltpu.VMEM((2, 2 * COMB_TILE, D_MODEL), F32),
            pltpu.SemaphoreType.DMA((2,)),
        ],
    )
    return pl.pallas_call(
        functools.partial(_combine_kernel, n_prompt=n_prompt),
        grid_spec=grid_spec,
        out_shape=[
            jax.ShapeDtypeStruct((t_prompt, D_MODEL), F32),
            jax.ShapeDtypeStruct((t_all - t_prompt, D_MODEL), F32),
        ],
        compiler_params=_cparams(("arbitrary",)),
        name="combine",
    )(slot_tab, route, h_all, o_sorted)


def _tiles(n_rows):
    return dict(
        proj_rows=min(n_rows, 1024), proj_cols=1024,
        qk_norm_rows=min(n_rows, 512),
        attn_block=512,
        merge_rows=min(n_rows, 1024), merge_cols=512,
        resid_rows=min(n_rows, 256),
    )


def _layer_tokens(x2d, w):
    tiles = _tiles(x2d.shape[0])
    proj, dt_raw = _proj(x2d, w["norm_attn_w"], w["w_a"], w["w_b"], w["w_dt"],
                         tiles["proj_rows"], tiles["proj_cols"])
    qn, kn, kb, vb = _qk_norm(proj, w["q_norm_w"], w["k_norm_w"], tiles["qk_norm_rows"])
    return proj, dt_raw, qn, kn, kb, vb


def _branch_merge(x2d, o, s, proj, w, t_all, row_off, bufs):
    tiles = _tiles(x2d.shape[0])
    merged = _merge(o, s, w["w_att_out"], w["w_ssm_out"], proj,
                    tiles["merge_rows"], tiles["merge_cols"])
    return _resid(x2d, merged, w["w_o"], w["norm_ffn_w"], w["wr_hi"], w["wr_lo"], w["br"],
                  tiles["resid_rows"], t_all, row_off, bufs)


def kernel(x_prompt, x_sample, cache_k, cache_v, state_ssm, state_conv, page_table, norm_attn_w, w_in, q_norm_w, k_norm_w, lambda_q1, lambda_k1, lambda_q2, lambda_k2, subln_w, w_att_out, conv_w, conv_b, dt_bias, a_log, d_skip, ssm_norm_w, w_ssm_out, w_o, norm_ffn_w, w_group_router, b_group_router, w_expert_router, b_expert_router, w_gate, w_up, w_down):
    layer = 0
    nb, seq, _ = x_prompt.shape
    db, dec_seq, _ = x_sample.shape

    w_in_l = w_in[layer]
    c_dt = Q_WIDTH + K_WIDTH + V_WIDTH + D_SSM + CONV_DIM
    w_a = w_in_l[:, :c_dt].astype(BF16)
    w_b = w_in_l[:, c_dt + N_SSM_HEADS:].astype(BF16)
    w_dt = jnp.pad(w_in_l[:, c_dt:c_dt + N_SSM_HEADS], ((0, 0), (0, LANES - N_SSM_HEADS))).astype(BF16)
    wr = jnp.concatenate([w_expert_router[layer], w_group_router[layer]], axis=1)
    wr = jnp.pad(wr, ((0, 0), (0, LANES - wr.shape[1])))
    wr_hi = wr.astype(BF16)
    wr_lo = (wr - wr_hi.astype(F32)).astype(BF16)
    br = jnp.concatenate([b_expert_router[layer], b_group_router[layer]])
    br = jnp.pad(br, (0, LANES - br.shape[0])).reshape(1, LANES)
    pad_h = (0, LANES - N_SSM_HEADS)
    w = dict(
        norm_attn_w=norm_attn_w[layer].reshape(1, D_MODEL), w_a=w_a, w_b=w_b, w_dt=w_dt,
        q_norm_w=q_norm_w[layer], k_norm_w=k_norm_w[layer],
        w_att_out=w_att_out[layer].astype(BF16), w_ssm_out=w_ssm_out[layer].astype(BF16),
        w_o=w_o[layer].astype(BF16), norm_ffn_w=norm_ffn_w[layer].reshape(1, D_MODEL),
        wr_hi=wr_hi, wr_lo=wr_lo, br=br,
        w_gate=w_gate[layer], w_up=w_up[layer], w_down=w_down[layer],
    )
    ssm_prm = dict(
        conv_w=conv_w[layer], conv_b=conv_b[layer].reshape(1, CONV_DIM),
        dt_bias=jnp.pad(dt_bias[layer], pad_h).reshape(1, LANES),
        dt_bias_t=dt_bias[layer].reshape(N_SSM_HEADS, 1),
        a=jnp.pad(-jnp.exp(a_log[layer]), pad_h).reshape(1, LANES),
        a_t=(-jnp.exp(a_log[layer])).reshape(N_SSM_HEADS, 1),
        d_skip=jnp.repeat(d_skip[layer], SSM_HEAD_DIM).reshape(1, D_SSM),
        ssm_norm_w=ssm_norm_w[layer].reshape(1, D_SSM),
    )
    lam_vecs = jnp.stack([lambda_q1[layer], lambda_k1[layer], lambda_q2[layer], lambda_k2[layer]])
    sw = subln_w[layer]
    ssd_cols = (COL_X // D_SSM, COL_B // (N_GROUPS * D_STATE), COL_C // (N_GROUPS * D_STATE),
                COL_Z // D_SSM)

    xp = x_prompt.reshape(nb * seq, D_MODEL)
    n_tok = db * dec_seq
    t_prompt = nb * seq
    t_all = t_prompt + n_tok
    proj_p, dt_p, qn_p, kn_p, kb_p, vb_p = _layer_tokens(xp, w)
    o_p = _attn_prompt(qn_p, kb_p, vb_p, lam_vecs, sw, _tiles(t_prompt)["attn_block"])
    s_p, ssm_p = _ssd(
        proj_p.reshape(nb, seq, PROJ_WIDTH), ssd_cols, dt_p.reshape(nb, seq, LANES),
        jnp.zeros((nb, HALO, CONV_DIM), F32), jnp.zeros((nb, N_SSM_HEADS, SSM_HEAD_DIM, D_STATE), F32),
        ssm_prm, SSD_CHUNK, SSD_CHUNK)
    bufs = _branch_merge(xp, o_p, s_p.reshape(t_prompt, D_SSM), proj_p, w, t_all, 0, None)
    keep = CONV_WIDTH - 1
    conv_p = proj_p.reshape(nb, seq, PROJ_WIDTH)[:, seq - keep:, COL_X:COL_X + CONV_DIM]

    xs = x_sample.reshape(db * dec_seq, D_MODEL)
    proj_s, dt_s, qn_s, kn_s, _, _ = _layer_tokens(xs, w)
    v_s = proj_s[:, COL_V:COL_V + V_WIDTH]
    o_s = _attn_sample(qn_s, kn_s, v_s, cache_k[layer], cache_v[layer], page_table, lam_vecs, sw)
    rows_s = SUBLANES
    pad_rows = ((0, 0), (0, rows_s - dec_seq), (0, 0))
    src_s = jnp.pad(proj_s[:, COL_Z:COL_GA].reshape(db, dec_seq, COL_GA - COL_Z), pad_rows)
    halo_s = jnp.pad(state_conv[layer], ((0, 0), (HALO - (CONV_WIDTH - 1), 0), (0, 0)))
    cols_s = ((COL_X - COL_Z) // D_SSM, (COL_B - COL_Z) // (N_GROUPS * D_STATE),
              (COL_C - COL_Z) // (N_GROUPS * D_STATE), 0)
    s_s, ssm_s = _ssd(
        src_s, cols_s, jnp.pad(dt_s.reshape(db, dec_seq, LANES), pad_rows), halo_s,
        state_ssm[layer], ssm_prm, rows_s, dec_seq)
    s_s = s_s[:, :dec_seq].reshape(n_tok, D_SSM)
    h_all, u_all, route = _branch_merge(xs, o_s, s_s, proj_s, w, t_all, t_prompt, bufs)

    n_tiles = 2 * t_all // MOE_TILE + N_EXPERTS
    plan = _route_plan(route, n_tiles)
    o_sorted = _gmm(u_all.reshape(t_all, ROW_CHUNKS, LANES), plan, w["w_gate"], w["w_up"], w["w_down"], n_tiles)
    y_p, y_s = _combine(route, h_all, o_sorted, plan, t_prompt)
    conv_s =proj_s.reshape(db, dec_seq, PROJ_WIDTH)[:, dec_seq - keep:, COL_X:COL_X + CONV_DIM]

    return (
        y_p.reshape(nb, seq, D_MODEL),
        y_s.reshape(db, dec_seq, D_MODEL),
        kn_p.reshape(1, nb, seq, N_KV_HEADS, 2 * HEAD_DIM),
        proj_p[:, COL_V:COL_V + V_WIDTH].reshape(1, nb, seq, N_KV_HEADS, V_DIM),
        ssm_p.reshape(1, nb, N_SSM_HEADS, SSM_HEAD_DIM, D_STATE),
        conv_p[None],
        kn_s.reshape(1, db, dec_seq, N_KV_HEADS, 2 * HEAD_DIM),
        v_s.reshape(1, db, dec_seq, N_KV_HEADS, V_DIM),
        ssm_s.reshape(1, db, N_SSM_HEADS, SSM_HEAD_DIM, D_STATE),
        conv_s[None],
    )
```

```python
import functools
import math

import jax
import jax.numpy as jnp
import ml_dtypes
import numpy as np
from jax import lax
from jax.experimental import pallas as pl
from jax.experimental.pallas import tpu as pltpu

F32 = jnp.float32
BF16 = jnp.bfloat16

D_MODEL = 2048
N_HEADS = 8
N_KV_HEADS = 4
GQA_REP = N_HEADS // N_KV_HEADS
HEAD_DIM = 64
V_DIM = 2 * HEAD_DIM
Q_WIDTH = N_HEADS * 2 * HEAD_DIM
K_WIDTH = N_KV_HEADS * 2 * HEAD_DIM
V_WIDTH = N_KV_HEADS * V_DIM
ATT_WIDTH = N_HEADS * V_DIM
D_SSM = D_MODEL
SSM_HEAD_DIM = 64
N_SSM_HEADS = D_SSM // SSM_HEAD_DIM
N_GROUPS = 4
HEADS_PER_GROUP = N_SSM_HEADS // N_GROUPS
D_STATE = 128
CONV_WIDTH = 4
CONV_DIM = D_SSM + 2 * N_GROUPS * D_STATE
SSD_CHUNK = 128
N_EXPERT_GROUPS = 4
EXPERTS_PER_GROUP = 8
N_EXPERTS = N_EXPERT_GROUPS * EXPERTS_PER_GROUP
D_EXPERT = D_MODEL // 4
PAGE_SIZE = 128
EPS = 1e-6
LAM_INIT = 0.8 - 0.6 * math.exp(-0.3 * 0)

LANES = 128
SUBLANES = 8
NEG_BIG = -1e30
VMEM_LIMIT = 56 * 1024 * 1024

COL_Q = 0
COL_K = COL_Q + Q_WIDTH
COL_V = COL_K + K_WIDTH
COL_Z = COL_V + V_WIDTH
COL_X = COL_Z + D_SSM
COL_B = COL_X + D_SSM
COL_C = COL_B + N_GROUPS * D_STATE
COL_GA = COL_C + N_GROUPS * D_STATE
COL_GS = COL_GA + D_MODEL
PROJ_WIDTH = COL_GS + D_MODEL

ALIBI_SLOPES = [2.0 ** (-8.0 * (h + 1) / N_HEADS) for h in range(N_HEADS)]


def _cparams(sem):
    return pltpu.CompilerParams(dimension_semantics=sem, vmem_limit_bytes=VMEM_LIMIT)


def _dot(a, b):
    return jnp.dot(a, b, preferred_element_type=F32)


def _dot_nt(a, b):
    return lax.dot_general(a, b, (((1,), (1,)), ((), ())), preferred_element_type=F32)


def _dot_tn(a, b):
    return lax.dot_general(a, b, (((0,), (0,)), ((), ())), preferred_element_type=F32)


def _split2(x):
    hi = x.astype(BF16)
    lo = (x - hi.astype(F32)).astype(BF16)
    return hi, lo


def _split3(x):
    hi = x.astype(BF16)
    r = x - hi.astype(F32)
    mid = r.astype(BF16)
    lo = (r - mid.astype(F32)).astype(BF16)
    return hi, mid, lo


def _dot_x2(x, sel):
    hi, lo = _split2(x)
    return _dot(hi, sel) + _dot(lo, sel)


def _dot_x3(x, sel):
    hi, mid, lo = _split3(x)
    return _dot(hi, sel) + _dot(mid, sel) + _dot(lo, sel)


def _sigmoid(x):
    return 1.0 / (1.0 + jnp.exp(-x))


def _silu(x):
    return x * _sigmoid(x)


def _softplus(x):
    return jnp.maximum(x, 0.0) + jnp.log1p(jnp.exp(-jnp.abs(x)))


NORM_ROWS = 256


def _proj_kernel(x_ref, nw_ref, wa_ref, wb_ref, wdt_ref, o_ref, dt_ref, u_scr, *, n_a):
    j = pl.program_id(1)

    @pl.when(j == 0)
    def _():
        tm = x_ref.shape[0]
        for lo in range(0, tm, min(tm, NORM_ROWS)):
            hi = lo + min(tm, NORM_ROWS)
            x = x_ref[lo:hi, :]
            ms = jnp.mean(x * x, axis=-1, keepdims=True)
            u_scr[lo:hi, :] = (x * lax.rsqrt(ms + EPS) * nw_ref[...]).astype(BF16)
        dt_ref[...] = _dot(u_scr[...], wdt_ref[...])

    @pl.when(j < n_a)
    def _():
        o_ref[...] = _dot(u_scr[...], wa_ref[...].astype(BF16))

    @pl.when(j >= n_a)
    def _():
        o_ref[...] = _dot(u_scr[...], wb_ref[...].astype(BF16))


def _proj(x, norm_w, w_a, w_b, w_dt, tm, tn):
    t = x.shape[0]
    n_a = COL_GA // tn
    return pl.pallas_call(
        functools.partial(_proj_kernel, n_a=n_a),
        grid=(t // tm, PROJ_WIDTH // tn),
        in_specs=[
            pl.BlockSpec((tm, D_MODEL), lambda i, j: (i, 0)),
            pl.BlockSpec((1, D_MODEL), lambda i, j: (0, 0)),
            pl.BlockSpec((D_MODEL, tn), lambda i, j: (0, jnp.minimum(j, n_a - 1))),
            pl.BlockSpec((D_MODEL, tn), lambda i, j: (0, jnp.maximum(j - n_a, 0))),
            pl.BlockSpec((D_MODEL, LANES), lambda i, j: (0, 0)),
        ],
        out_specs=[
            pl.BlockSpec((tm, tn), lambda i, j: (i, j)),
            pl.BlockSpec((tm, LANES), lambda i, j: (i, 0)),
        ],
        out_shape=[
            jax.ShapeDtypeStruct((t, PROJ_WIDTH), F32),
            jax.ShapeDtypeStruct((t, LANES), F32),
        ],
        scratch_shapes=[pltpu.VMEM((tm, D_MODEL), BF16)],
        compiler_params=_cparams(("arbitrary", "arbitrary")),
        name="proj",
    )(x, norm_w, w_a, w_b, w_dt)


LOG2E = math.log2(math.e)
Q_SCALE = LOG2E * HEAD_DIM ** -0.5


def _qknorm_kernel(p_ref, qw_ref, kw_ref, g_ref, qn_ref, kn_ref, kb_ref, vb_ref):
    gsum = g_ref[...]
    n_q = Q_WIDTH // LANES
    for c in range((Q_WIDTH + K_WIDTH) // LANES):
        x = p_ref[:, c * LANES:(c + 1) * LANES]
        ss = _dot_x2(x * x, gsum)
        y = x * lax.rsqrt(ss * (1.0 / HEAD_DIM) + EPS)
        if c < n_q:
            qn_ref[:, c * LANES:(c + 1) * LANES] = (y * qw_ref[...] * Q_SCALE).astype(BF16)
        else:
            kn = y * kw_ref[...]
            kn_ref[:, (c - n_q) * LANES:(c - n_q + 1) * LANES] = kn
            kb_ref[:, (c - n_q) * LANES:(c - n_q + 1) * LANES] = kn.astype(BF16)
    vb_ref[...] = p_ref[:, COL_V:COL_V + V_WIDTH].astype(BF16)


def _qk_norm(proj, q_norm_w, k_norm_w, tm):
    t = proj.shape[0]
    group = np.kron(np.eye(LANES // HEAD_DIM), np.ones((HEAD_DIM, HEAD_DIM)))
    qw = jnp.tile(q_norm_w, LANES // HEAD_DIM).reshape(1, LANES)
    kw = jnp.tile(k_norm_w, LANES // HEAD_DIM).reshape(1, LANES)
    return pl.pallas_call(
        _qknorm_kernel,
        grid=(t // tm,),
        in_specs=[
            pl.BlockSpec((tm, Q_WIDTH + K_WIDTH + V_WIDTH), lambda i: (i, 0)),
            pl.BlockSpec((1, LANES), lambda i: (0, 0)),
            pl.BlockSpec((1, LANES), lambda i: (0, 0)),
            pl.BlockSpec((LANES, LANES), lambda i: (0, 0)),
        ],
        out_specs=[
            pl.BlockSpec((tm, Q_WIDTH), lambda i: (i, 0)),
            pl.BlockSpec((tm, K_WIDTH), lambda i: (i, 0)),
            pl.BlockSpec((tm, K_WIDTH), lambda i: (i, 0)),
            pl.BlockSpec((tm, V_WIDTH), lambda i: (i, 0)),
        ],
        out_shape=[
            jax.ShapeDtypeStruct((t, Q_WIDTH), BF16),
            jax.ShapeDtypeStruct((t, K_WIDTH), F32),
            jax.ShapeDtypeStruct((t, K_WIDTH), BF16),
            jax.ShapeDtypeStruct((t, V_WIDTH), BF16),
        ],
        compiler_params=_cparams(("arbitrary",)),
        name="qk_norm",
    )(proj, qw, kw, jnp.asarray(group, BF16))


def _diff_lambda(lam_ref):
    lamv = lam_ref[...]
    s1 = jnp.sum(lamv[0:1] * lamv[1:2], axis=1, keepdims=True)
    s2 = jnp.sum(lamv[2:3] * lamv[3:4], axis=1, keepdims=True)
    return jnp.exp(s1) - jnp.exp(s2) + LAM_INIT


def _subln(o, w):
    ms = jnp.mean(o * o, axis=-1, keepdims=True)
    return o * lax.rsqrt(ms + EPS) * w * (1.0 - LAM_INIT)


N_SLOPE_PARTS = 3


def _bf16_parts(x, n):
    parts, rem = [], np.float32(x)
    for _ in range(n):
        p = np.float32(rem.astype(ml_dtypes.bfloat16))
        parts.append(float(p))
        rem = np.float32(rem - p)
    return parts


def _alibi_tables(tk):
    qcols = np.zeros((N_HEADS, 16, LANES), np.float32)
    csum = np.zeros((N_HEADS,), np.float32)
    for h, slope in enumerate(ALIBI_SLOPES):
        parts = _bf16_parts(slope * LOG2E, N_SLOPE_PARTS)
        csum[h] = np.float32(sum(np.float32(p) for p in parts))
        for i, p in enumerate(parts):
            qcols[h, :, i] = p * LANES
            qcols[h, :, N_SLOPE_PARTS + i] = p
    pos = np.arange(tk)
    kcols = np.zeros((tk, LANES), np.float32)
    kcols[:, 0:N_SLOPE_PARTS] = (pos // LANES)[:, None]
    kcols[:, N_SLOPE_PARTS:2 * N_SLOPE_PARTS] = (pos % LANES)[:, None]
    return jnp.asarray(qcols, BF16), jnp.asarray(kcols, BF16), jnp.asarray(csum, F32)


def _attn_p_kernel(qi_ref, ki_ref, cf_ref, q_ref, k_ref, v_ref, qc_ref, kc_ref, lam_ref, sw_ref,
                   o_ref, qa_scr, m_scr, l_scr, acc_scr, *, tq):
    g = pl.program_id(0)
    t = pl.program_id(1)
    qi = qi_ref[t]
    ki = ki_ref[t]
    n_chunk = tq // LANES

    @pl.when(ki == 0)
    def _():
        lane = lax.broadcasted_iota(jnp.int32, (tq, LANES), 1)
        for r in range(GQA_REP):
            qq = q_ref[:, r * LANES:(r + 1) * LANES]
            qc = jnp.broadcast_to(qc_ref[r, 0:1, :], (tq, LANES))
            for c in range(2):
                idx = 2 * r + c
                keep = (lane < HEAD_DIM) if c == 0 else (lane >= HEAD_DIM)
                qa_scr[idx * tq:(idx + 1) * tq, 0:LANES] = jnp.where(keep, qq, jnp.zeros_like(qq))
                qa_scr[idx * tq:(idx + 1) * tq, LANES:2 * LANES] = qc
        m_scr[...] = jnp.full(m_scr.shape, NEG_BIG, F32)
        l_scr[...] = jnp.zeros(l_scr.shape, F32)
        acc_scr[...] = jnp.zeros(acc_scr.shape, F32)

    def step(diag):
        k_aug = jnp.concatenate([k_ref[...], kc_ref[...]], axis=1)
        block_dist = ((qi - ki) * tq).astype(F32)
        for r in range(GQA_REP):
            off = -cf_ref[g * GQA_REP + r] * block_dist
            lo, hi = r * 2 * tq, (r + 1) * 2 * tq
            s = _dot_nt(qa_scr[lo:hi, :], k_aug)
            if diag:
                row_in = lax.broadcasted_iota(jnp.int32, (2 * tq, tq), 0) & (tq - 1)
                col = lax.broadcasted_iota(jnp.int32, (2 * tq, tq), 1)
                s = jnp.where(col <= row_in, s, NEG_BIG)
            chunks = [s[:, j * LANES:(j + 1) * LANES] for j in range(n_chunk)]
            m_prev = m_scr[lo:hi]
            m_blk = jnp.max(functools.reduce(jnp.maximum, chunks), axis=1, keepdims=True) + off
            m_new = jnp.maximum(m_prev, m_blk)
            alpha = jnp.exp2(m_prev - m_new)
            m_sub = m_new - off
            pj = [jnp.exp2(ch - m_sub) for ch in chunks]
            l_scr[lo:hi] = alpha * l_scr[lo:hi] + functools.reduce(jnp.add, pj)
            m_scr[lo:hi] = m_new
            pv = _dot(jnp.concatenate(pj, axis=1).astype(BF16), v_ref[...])
            acc_scr[lo:hi] = alpha * acc_scr[lo:hi] + pv

    @pl.when(ki < qi)
    def _():
        step(False)

    @pl.when(ki == qi)
    def _():
        step(True)
        lam = _diff_lambda(lam_ref)
        for r in range(GQA_REP):
            i1, i2 = 2 * r * tq, (2 * r + 1) * tq
            l1 = jnp.sum(l_scr[i1:i1 + tq], axis=1, keepdims=True)
            l2 = jnp.sum(l_scr[i2:i2 + tq], axis=1, keepdims=True)
            o = acc_scr[i1:i1 + tq] / l1 - lam * (acc_scr[i2:i2 + tq] / l2)
            o_ref[:, r * LANES:(r + 1) * LANES] = _subln(o, sw_ref[...]).astype(BF16)


def _attn_prompt(qn, kb, vb, lam_vecs, subln_w, tq):
    t = qn.shape[0]
    nq = t // tq
    pairs = [(i, j) for i in range(nq) for j in range(i + 1)]
    qi_tab = jnp.asarray([p[0] for p in pairs], jnp.int32)
    ki_tab = jnp.asarray([p[1] for p in pairs], jnp.int32)
    qcols, kcols, csum = _alibi_tables(tq)
    n_sub = 2 * GQA_REP
    grid_spec = pltpu.PrefetchScalarGridSpec(
        num_scalar_prefetch=3,
        grid=(N_KV_HEADS, len(pairs)),
        in_specs=[
            pl.BlockSpec((tq, GQA_REP * LANES), lambda g, t, qi, ki, cf: (qi[t], g)),
            pl.BlockSpec((tq, LANES), lambda g, t, qi, ki, cf: (ki[t], g)),
            pl.BlockSpec((tq, V_DIM), lambda g, t, qi, ki, cf: (ki[t], g)),
            pl.BlockSpec((GQA_REP, 16, LANES), lambda g, t, qi, ki, cf: (g, 0, 0)),
            pl.BlockSpec((tq, LANES), lambda g, t, qi, ki, cf: (0, 0)),
            pl.BlockSpec((4, HEAD_DIM), lambda g, t, qi, ki, cf: (0, 0)),
            pl.BlockSpec((1, V_DIM), lambda g, t, qi, ki, cf: (0, 0)),
        ],
        out_specs=pl.BlockSpec((tq, GQA_REP * V_DIM), lambda g, t, qi, ki, cf: (qi[t], g)),
        scratch_shapes=[
            pltpu.VMEM((n_sub * tq, 2 * LANES), BF16),
            pltpu.VMEM((n_sub * tq, LANES), F32),
            pltpu.VMEM((n_sub * tq, LANES), F32),
            pltpu.VMEM((n_sub * tq, V_DIM), F32),
        ],
    )
    return pl.pallas_call(
        functools.partial(_attn_p_kernel, tq=tq),
        grid_spec=grid_spec,
        out_shape=jax.ShapeDtypeStruct((t, ATT_WIDTH), BF16),
        compiler_params=_cparams(("arbitrary", "arbitrary")),
        name="attn_p",
    )(qi_tab, ki_tab, csum, qn, kb, vb, qcols, kcols, lam_vecs, subln_w.reshape(1, V_DIM))


PAGES_PER_STEP = 16
PAGE_GROUP = PAGES_PER_STEP
ROWS_S = 2 * 4 * N_HEADS


def _attn_s_kernel(pt_ref, q_ref, d0_ref, mask_ref, sl_ref, bn_ref, kn_ref, vn_ref, lam_ref,
                   sw_ref, *rest, n_steps):
    k_refs = [r.at[0] for r in rest[:PAGES_PER_STEP]]
    v_refs = [r.at[0] for r in rest[PAGES_PER_STEP:2 * PAGES_PER_STEP]]
    o_ref = rest[2 * PAGES_PER_STEP]
    m_scr, l_scr, acc_scr = rest[2 * PAGES_PER_STEP + 1:]
    s_id = pl.program_id(1)

    @pl.when(s_id == 0)
    def _():
        m_scr[...] = jnp.full(m_scr.shape, NEG_BIG, F32)
        l_scr[...] = jnp.zeros(l_scr.shape, F32)
        acc_scr[...] = jnp.zeros(acc_scr.shape, F32)

    q = q_ref[0]

    def update(scores, values):
        m_prev = m_scr[...]
        m_new = m_prev
        for sc in scores:
            m_new = jnp.maximum(m_new, jnp.max(sc, axis=1, keepdims=True))
        alpha = jnp.exp2(m_prev - m_new)
        l_new = alpha * l_scr[...]
        acc = alpha * acc_scr[...]
        for sc, vv in zip(scores, values):
            p = jnp.exp2(sc - m_new)
            l_new = l_new + jnp.sum(p, axis=1, keepdims=True)
            acc = acc + _dot(p.astype(BF16), vv)
        m_scr[...] = m_new
        l_scr[...] = l_new
        acc_scr[...] = acc

    for first in range(0, PAGES_PER_STEP, PAGE_GROUP):
        scores, values = [], []
        for i in range(first, first + PAGE_GROUP):
            page_start = ((s_id * PAGES_PER_STEP + i) * PAGE_SIZE).astype(F32)
            bias = sl_ref[...] * (d0_ref[...] - page_start) + mask_ref[...]
            scores.append(_dot_nt(q, k_refs[i][...].astype(BF16)) + bias)
            values.append(v_refs[i][...].astype(BF16))
        update(scores, values)

    @pl.when(s_id == n_steps - 1)
    def _():
        sc = _dot_nt(q, kn_ref[0].astype(BF16)) + bn_ref[...]
        update([sc], [vn_ref[0].astype(BF16)])
        lam = _diff_lambda(lam_ref)
        half = ROWS_S // 2
        o1 = acc_scr[0:half] / l_scr[0:half]
        o2 = acc_scr[half:ROWS_S] / l_scr[half:ROWS_S]
        o_ref[0] = _subln(o1 - lam * o2, sw_ref[...]).astype(BF16)


def _attn_sample(qn_s, kn_s, v_s, cache_k, cache_v, page_table, lam_vecs, subln_w):
    db, n_pages = page_table.shape
    dec_seq = qn_s.shape[0] // db
    past = n_pages * PAGE_SIZE
    n_steps = n_pages // PAGES_PER_STEP
    page_rows = PAGE_SIZE * N_KV_HEADS
    n_phys = cache_k.shape[0]
    ck = cache_k.reshape(n_phys, page_rows, 2 * HEAD_DIM)
    cv = cache_v.reshape(n_phys, page_rows, V_DIM)

    q5 = qn_s.reshape(db, dec_seq, N_HEADS, 2, HEAD_DIM)
    zeros = jnp.zeros_like(q5[:, :, :, 0])
    q_all = jnp.stack([jnp.concatenate([q5[:, :, :, 0], zeros], axis=-1),
                       jnp.concatenate([zeros, q5[:, :, :, 1]], axis=-1)], axis=1)
    q_all = q_all.reshape(db, ROWS_S, LANES)

    r = np.arange(ROWS_S)
    tok_r = (r % (dec_seq * N_HEADS)) // N_HEADS
    head_r = r % N_HEADS
    slope_r = np.asarray(ALIBI_SLOPES)[head_r] * LOG2E
    c = np.arange(page_rows)
    key_c, grp_c = c // N_KV_HEADS, c % N_KV_HEADS
    same = (head_r[:, None] // GQA_REP) == grp_c[None, :]
    d0 = np.broadcast_to(past + tok_r[:, None] - key_c[None, :], (ROWS_S, page_rows))
    mask = np.where(same, 0.0, NEG_BIG)
    sl = np.broadcast_to(-slope_r[:, None], (ROWS_S, 1))
    cn = np.arange(LANES)
    tok_c, grp_n = cn // N_KV_HEADS, cn % N_KV_HEADS
    ok = ((head_r[:, None] // GQA_REP) == grp_n[None, :]) & (tok_c[None, :] <= tok_r[:, None])
    bn = np.where(ok, -slope_r[:, None] * (tok_r[:, None] - tok_c[None, :]), NEG_BIG)

    new_rows = dec_seq * N_KV_HEADS
    kn_pad = jnp.pad(kn_s.reshape(db, new_rows, LANES), ((0, 0), (0, LANES - new_rows), (0, 0)))
    vn_pad = jnp.pad(v_s.reshape(db, new_rows, LANES), ((0, 0), (0, LANES - new_rows), (0, 0)))

    def const(shape):
        return pl.BlockSpec(shape, lambda b, s, pt: (0,) * len(shape))

    def page_spec(i):
        return pl.BlockSpec(
            (1, page_rows, LANES),
            lambda b, s, pt: (pt[b * n_pages + s * PAGES_PER_STEP + i], 0, 0))

    grid_spec = pltpu.PrefetchScalarGridSpec(
        num_scalar_prefetch=1,
        grid=(db, n_steps),
        in_specs=[
            pl.BlockSpec((1, ROWS_S, LANES), lambda b, s, pt: (b, 0, 0)),
            const((ROWS_S, page_rows)),
            const((ROWS_S, page_rows)),
            const((ROWS_S, 1)),
            const((ROWS_S, LANES)),
            pl.BlockSpec((1, LANES, LANES), lambda b, s, pt: (b, 0, 0)),
            pl.BlockSpec((1, LANES, LANES), lambda b, s, pt: (b, 0, 0)),
            const((4, HEAD_DIM)),
            const((1, V_DIM)),
        ] + [page_spec(i) for i in range(PAGES_PER_STEP)] * 2,
        out_specs=pl.BlockSpec((1, ROWS_S // 2, V_DIM), lambda b, s, pt: (b, 0, 0)),
        scratch_shapes=[
            pltpu.VMEM((ROWS_S, 1), F32),
            pltpu.VMEM((ROWS_S, 1), F32),
            pltpu.VMEM((ROWS_S, V_DIM), F32),
        ],
    )
    o = pl.pallas_call(
        functools.partial(_attn_s_kernel, n_steps=n_steps),
        grid_spec=grid_spec,
        out_shape=jax.ShapeDtypeStruct((db, ROWS_S // 2, V_DIM), BF16),
        compiler_params=_cparams(("arbitrary", "arbitrary")),
        name="attn_s",
    )(page_table.reshape(-1), q_all, jnp.asarray(d0, F32), jnp.asarray(mask, F32),
      jnp.asarray(sl, F32), jnp.asarray(bn, F32), kn_pad, vn_pad, lam_vecs,
      subln_w.reshape(1, V_DIM),
      *([ck] * PAGES_PER_STEP), *([cv] * PAGES_PER_STEP))
    return o.reshape(db * dec_seq, ATT_WIDTH)


HALO = SUBLANES


def _ssd_kernel(xs_ref, b_ref, c_ref, z_ref, dt_ref, dtt_ref, halo_ref, init_ref,
                cw_ref, cb_ref, dtb_ref, dtbt_ref, a_ref, at_ref, dsk_ref, nw_ref,
                tri_ref, trit_ref, exp_ref, sel_ref,
                y_ref, fin_ref, win_scr, state_scr, *, rows_in, n_valid):
    ci = pl.program_id(1)
    n_chunks = pl.num_programs(1)
    lc = SSD_CHUNK
    bc_w = N_GROUPS * D_STATE

    @pl.when(ci == 0)
    def _():
        state_scr[...] = init_ref[0].reshape(D_SSM, D_STATE)
        win_scr[0:HALO, :] = halo_ref[0]

    if rows_in < lc:
        win_scr[HALO:HALO + lc, :] = jnp.zeros((lc, CONV_DIM), F32)
    win_scr[HALO:HALO + rows_in, 0:D_SSM] = xs_ref[0]
    win_scr[HALO:HALO + rows_in, D_SSM:D_SSM + bc_w] = b_ref[0]
    win_scr[HALO:HALO + rows_in, D_SSM + bc_w:CONV_DIM] = c_ref[0]

    acc = cb_ref[...]
    for tap in range(CONV_WIDTH):
        off = HALO - (CONV_WIDTH - 1) + tap
        acc = acc + win_scr[off:off + lc, :] * cw_ref[tap:tap + 1, :]
    conv = _silu(acc)
    win_scr[0:HALO, :] = win_scr[lc:lc + HALO, :]
    xs = conv[:, 0:D_SSM]
    bm = conv[:, D_SSM:D_SSM + bc_w].astype(BF16)
    cm = conv[:, D_SSM + bc_w:CONV_DIM].astype(BF16)

    if rows_in < lc:
        dt_in = jnp.concatenate([dt_ref[0], jnp.zeros((lc - rows_in, LANES), F32)], axis=0)
        dtt_in = jnp.concatenate(
            [dtt_ref[0], jnp.zeros((N_SSM_HEADS, lc - rows_in), F32)], axis=1)
    else:
        dt_in, dtt_in = dt_ref[0], dtt_ref[0]
    rowi = lax.broadcasted_iota(jnp.int32, (lc, LANES), 0)
    coli = lax.broadcasted_iota(jnp.int32, (N_SSM_HEADS, lc), 1)
    dt = jnp.where(rowi < n_valid, _softplus(dt_in + dtb_ref[...]), 0.0)
    dtt = jnp.where(coli < n_valid, _softplus(dtt_in + dtbt_ref[...]), 0.0)
    a_cs = _dot_x3_left(tri_ref[...], dt * a_ref[...])
    a_cst = _dot_x3(dtt * at_ref[...], trit_ref[...])
    a_last = a_cs[lc - 1:lc, :]
    exp_cs = jnp.exp(a_cs)
    exp_rest = jnp.exp(a_last - a_cs)
    expand = exp_ref[...]
    dtx = _dot_x2(dt, expand)
    ecx = _dot_x2(exp_cs, expand)
    erx = _dot_x2(exp_rest, expand)
    xc = xs * dtx
    xcb = xc.astype(BF16)
    xcd = (xc * erx).astype(BF16)

    last_t = jnp.exp(a_cst[:, lc - 1:lc])
    rdec = _dot_x2_left(sel_ref[...], jnp.broadcast_to(last_t, (N_SSM_HEADS, D_STATE)))

    tril = (lax.broadcasted_iota(jnp.int32, (lc, lc), 0)
            >= lax.broadcasted_iota(jnp.int32, (lc, lc), 1))
    lane = lax.broadcasted_iota(jnp.int32, (lc, LANES), 1)
    gw = HEADS_PER_GROUP * SSM_HEAD_DIM
    y_parts = []
    for g in range(N_GROUPS):
        bg = bm[:, g * D_STATE:(g + 1) * D_STATE]
        cg = cm[:, g * D_STATE:(g + 1) * D_STATE]
        cb = _dot_nt(cg, bg)
        st = state_scr[g * gw:(g + 1) * gw, :]
        y_off = _dot_nt(cg, st.astype(BF16)) * ecx[:, g * gw:(g + 1) * gw]
        new_st = _dot_tn(xcd[:, g * gw:(g + 1) * gw], bg)
        state_scr[g * gw:(g + 1) * gw, :] = st * rdec[g * gw:(g + 1) * gw, :] + new_st
        for j in range(HEADS_PER_GROUP // 2):
            pair = g * (HEADS_PER_GROUP // 2) + j
            blk = xcb[:, pair * LANES:(pair + 1) * LANES]
            y_pair = None
            for half in range(2):
                h = 2 * pair + half
                seg = a_cs[:, h:h + 1] - a_cst[h:h + 1, :]
                decay = jnp.exp(jnp.where(tril, seg, NEG_BIG))
                mh = (cb * decay).astype(BF16)
                keep = (lane < SSM_HEAD_DIM) if half == 0 else (lane >= SSM_HEAD_DIM)
                part = _dot(mh, jnp.where(keep, blk, jnp.zeros_like(blk)))
                y_pair = part if y_pair is None else y_pair + part
            y_parts.append(y_pair + y_off[:, (pair % (HEADS_PER_GROUP // 2)) * LANES:
                                          (pair % (HEADS_PER_GROUP // 2) + 1) * LANES])
    y = jnp.concatenate(y_parts, axis=1)
    y = y + dsk_ref[...] * xs
    if rows_in < lc:
        z = jnp.concatenate([z_ref[0], jnp.zeros((lc - rows_in, D_SSM), F32)], axis=0)
    else:
        z = z_ref[0]
    y = y * _silu(z)
    gn = D_SSM // N_GROUPS
    outs = []
    for g in range(N_GROUPS):
        yg = y[:, g * gn:(g + 1) * gn]
        ms = jnp.mean(yg * yg, axis=-1, keepdims=True)
        outs.append(yg * lax.rsqrt(ms + EPS) * nw_ref[:, g * gn:(g + 1) * gn])
    out = jnp.concatenate(outs, axis=1).astype(BF16)
    y_ref[0] = out[0:rows_in]

    @pl.when(ci == n_chunks - 1)
    def _():
        fin_ref[0] = state_scr[...].reshape(N_SSM_HEADS, SSM_HEAD_DIM, D_STATE)


def _dot_x3_left(sel, x):
    hi, mid, lo = _split3(x)
    return _dot(sel, hi) + _dot(sel, mid) + _dot(sel, lo)


def _dot_x2_left(sel, x):
    hi, lo = _split2(x)
    return _dot(sel, hi) + _dot(sel, lo)


def _ssd(src, col_blocks, dt_raw, halo, init_state, prm, rows_in, n_valid):
    nb, seq = src.shape[0], src.shape[1]
    n_chunks = max(1, seq // SSD_CHUNK)
    bc_w = N_GROUPS * D_STATE
    dtt = jnp.swapaxes(dt_raw[:, :, :N_SSM_HEADS], 1, 2)
    tri = np.tril(np.ones((SSD_CHUNK, SSD_CHUNK)))
    expand = np.zeros((LANES, D_SSM))
    expand[np.arange(D_SSM) // SSM_HEAD_DIM, np.arange(D_SSM)] = 1.0
    sel = expand[:N_SSM_HEADS].T
    cx, cbk, cck, cz = col_blocks

    def const(shape):
        return pl.BlockSpec(shape, lambda b, c: (0,) * len(shape))

    return pl.pallas_call(
        functools.partial(_ssd_kernel, rows_in=rows_in, n_valid=n_valid),
        grid=(nb, n_chunks),
        in_specs=[
            pl.BlockSpec((1, rows_in, D_SSM), lambda b, c: (b, c, cx)),
            pl.BlockSpec((1, rows_in, bc_w), lambda b, c: (b, c, cbk)),
            pl.BlockSpec((1, rows_in, bc_w), lambda b, c: (b, c, cck)),
            pl.BlockSpec((1, rows_in, D_SSM), lambda b, c: (b, c, cz)),
            pl.BlockSpec((1, rows_in, LANES), lambda b, c: (b, c, 0)),
            pl.BlockSpec((1, N_SSM_HEADS, rows_in), lambda b, c: (b, 0, c)),
            pl.BlockSpec((1, HALO, CONV_DIM), lambda b, c: (b, 0, 0)),
            pl.BlockSpec((1, N_SSM_HEADS, SSM_HEAD_DIM, D_STATE), lambda b, c: (b, 0, 0, 0)),
            const((CONV_WIDTH, CONV_DIM)),
            const((1, CONV_DIM)),
            const((1, LANES)),
            const((N_SSM_HEADS, 1)),
            const((1, LANES)),
            const((N_SSM_HEADS, 1)),
            const((1, D_SSM)),
            const((1, D_SSM)),
            const((SSD_CHUNK, SSD_CHUNK)),
            const((SSD_CHUNK, SSD_CHUNK)),
            const((LANES, D_SSM)),
            const((D_SSM, N_SSM_HEADS)),
        ],
        out_specs=[
            pl.BlockSpec((1, rows_in, D_SSM), lambda b, c: (b, c, 0)),
            pl.BlockSpec((1, N_SSM_HEADS, SSM_HEAD_DIM, D_STATE), lambda b, c: (b, 0, 0, 0)),
        ],
        out_shape=[
            jax.ShapeDtypeStruct((nb, seq, D_SSM), BF16),
            jax.ShapeDtypeStruct((nb, N_SSM_HEADS, SSM_HEAD_DIM, D_STATE), F32),
        ],
        scratch_shapes=[
            pltpu.VMEM((HALO + SSD_CHUNK, CONV_DIM), F32),
            pltpu.VMEM((D_SSM, D_STATE), F32),
        ],
        compiler_params=_cparams(("arbitrary", "arbitrary")),
        name="ssd",
    )(src, src, src, src, dt_raw, dtt, halo, init_state,
      prm["conv_w"], prm["conv_b"], prm["dt_bias"], prm["dt_bias_t"], prm["a"], prm["a_t"],
      prm["d_skip"], prm["ssm_norm_w"],
      jnp.asarray(tri, BF16), jnp.asarray(tri.T, BF16), jnp.asarray(expand, BF16),
      jnp.asarray(sel, BF16))


def _merge_kernel(o_ref, s_ref, wa_ref, ws_ref, ga_ref, gs_ref, out_ref):
    a = _dot(o_ref[...], wa_ref[...])
    s = _dot(s_ref[...], ws_ref[...])
    out_ref[...] = (_sigmoid(ga_ref[...]) * a + _sigmoid(gs_ref[...]) * s).astype(BF16)


def _merge(o, s, wa, ws, proj, tm, tn):
    t = o.shape[0]
    ga0, gs0 = COL_GA // tn, COL_GS // tn
    return pl.pallas_call(
        _merge_kernel,
        grid=(t // tm, D_MODEL // tn),
        in_specs=[
            pl.BlockSpec((tm, ATT_WIDTH), lambda i, j: (i, 0)),
            pl.BlockSpec((tm, D_SSM), lambda i, j: (i, 0)),
            pl.BlockSpec((ATT_WIDTH, tn), lambda i, j: (0, j)),
            pl.BlockSpec((D_SSM, tn), lambda i, j: (0, j)),
            pl.BlockSpec((tm, tn), lambda i, j: (i, ga0 + j)),
            pl.BlockSpec((tm, tn), lambda i, j: (i, gs0 + j)),
        ],
        out_specs=pl.BlockSpec((tm, tn), lambda i, j: (i, j)),
        out_shape=jax.ShapeDtypeStruct((t, D_MODEL), BF16),
        compiler_params=_cparams(("arbitrary", "arbitrary")),
        name="merge",
    )(o, s, wa, ws, proj, proj)


ROUTE_E1, ROUTE_E2, ROUTE_W1, ROUTE_W2 = 0, 1, 2, 3
ROW_CHUNKS = D_MODEL // LANES


def _resid_kernel(x_ref, m_ref, wo_ref, nw_ref, wrh_ref, wrl_ref, br_ref, *rest, n_real):
    h_ref, u_ref, route_ref = rest[-3:]

    @pl.when(pl.program_id(0) >= n_real)
    def _():
        h_ref[...] = jnp.zeros(h_ref.shape, F32)
        u_ref[...] = jnp.zeros(u_ref.shape, F32)
        route_ref[...] = jnp.zeros(route_ref.shape, F32)

    @pl.when(pl.program_id(0) < n_real)
    def _():
        _resid_tile(x_ref, m_ref, wo_ref, nw_ref, wrh_ref, wrl_ref, br_ref,
                    h_ref, u_ref, route_ref)


def _resid_tile(x_ref, m_ref, wo_ref, nw_ref, wrh_ref, wrl_ref, br_ref, h_ref, u_ref, route_ref):
    h = x_ref[...] + _dot(m_ref[...], wo_ref[...])
    h_ref[...] = h
    ms = jnp.mean(h * h, axis=-1, keepdims=True)
    u = h * lax.rsqrt(ms + EPS) * nw_ref[...]
    u_hi, u_lo = _split2(u)
    u_ref[...] = u
    logits = (_dot(u_hi, wrh_ref[...]) + _dot(u_lo, wrh_ref[...])
              + _dot(u_hi, wrl_ref[...]) + br_ref[...])
    lane = lax.broadcasted_iota(jnp.int32, logits.shape, 1)
    lane_f = lane.astype(F32)
    far = float(2 * LANES)

    def first_max(vals):
        top = jnp.max(vals, axis=1, keepdims=True)
        idx = jnp.min(jnp.where(vals == top, lane_f, far), axis=1, keepdims=True)
        return top, idx

    is_group = (lane >= N_EXPERTS) & (lane < N_EXPERTS + N_EXPERT_GROUPS)
    gl = jnp.where(is_group, logits, NEG_BIG)
    g_top, g_idx = first_max(gl)
    g_p = 1.0 / jnp.sum(jnp.exp(gl - g_top), axis=1, keepdims=True)
    lo_lane = (g_idx - N_EXPERTS) * EXPERTS_PER_GROUP
    in_group = (lane_f >= lo_lane) & (lane_f < lo_lane + EXPERTS_PER_GROUP)
    el = jnp.where(in_group, logits, NEG_BIG)
    m1, i1 = first_max(el)
    el2 = jnp.where(lane_f == i1, NEG_BIG, el)
    m2, i2 = first_max(el2)
    e = jnp.exp(m2 - m1)
    w1 = 1.0 / (1.0 + e)
    w2 = e / (1.0 + e)
    route = jnp.where(lane == ROUTE_E1, i1, 0.0)
    route = jnp.where(lane == ROUTE_E2, i2, route)
    route = jnp.where(lane == ROUTE_W1, g_p * w1, route)
    route_ref[...] = jnp.where(lane == ROUTE_W2, g_p * w2, route)


def _resid(x, merged, wo, norm_w, wr_hi, wr_lo, br, tm, t_all, row_off, bufs):
    t = x.shape[0]
    blk_off = row_off // tm
    n_real = t // tm
    n_fill = pl.cdiv(t_all - t, tm) if bufs is None else 0

    def const(shape):
        return pl.BlockSpec(shape, lambda i: (0,) * len(shape))

    in_specs = [
        pl.BlockSpec((tm, D_MODEL), lambda i: (jnp.minimum(i, n_real - 1), 0)),
        pl.BlockSpec((tm, D_MODEL), lambda i: (jnp.minimum(i, n_real - 1), 0)),
        const((D_MODEL, D_MODEL)),
        const((1, D_MODEL)),
        const((D_MODEL, LANES)),
        const((D_MODEL, LANES)),
        const((1, LANES)),
    ]
    args = [x, merged, wo, norm_w, wr_hi, wr_lo, br]
    aliases = {}
    if bufs is not None:
        aliases = {len(args) + k: k for k in range(len(bufs))}
        in_specs += [pl.BlockSpec(memory_space=pl.ANY)] * len(bufs)
        args += list(bufs)
    return pl.pallas_call(
        functools.partial(_resid_kernel, n_real=n_real),
        grid=(n_real + n_fill,),
        in_specs=in_specs,
        out_specs=[
            pl.BlockSpec((tm, D_MODEL), lambda i: (i + blk_off, 0)),
            pl.BlockSpec((tm, D_MODEL), lambda i: (i + blk_off, 0)),
            pl.BlockSpec((tm, LANES), lambda i: (i + blk_off, 0)),
        ],
        out_shape=[
            jax.ShapeDtypeStruct((t_all, D_MODEL), F32),
            jax.ShapeDtypeStruct((t_all, D_MODEL), F32),
            jax.ShapeDtypeStruct((t_all, LANES), F32),
        ],
        input_output_aliases=aliases,
        compiler_params=_cparams(("arbitrary",)),
        name="resid",
    )(*args)


MOE_TILE = 256
COMB_TILE = 128
DMA_UNROLL = 8


def _route_plan(route, n_tiles):
    n_pairs = 2 * route.shape[0]
    pair_e = route[:, ROUTE_E1:ROUTE_E2 + 1].astype(jnp.int32).reshape(-1)
    onehot = (pair_e[:, None] == jnp.arange(N_EXPERTS, dtype=jnp.int32)[None, :]).astype(jnp.int32)
    csum = jnp.cumsum(onehot, axis=0)
    rank = jnp.sum((csum - onehot) * onehot, axis=1)
    tiles_e = (csum[-1] + MOE_TILE - 1) // MOE_TILE
    tile_end = jnp.cumsum(tiles_e)
    first_row = (tile_end - tiles_e) * MOE_TILE
    slot = jnp.sum(onehot * first_row[None, :], axis=1) + rank
    tok_of_slot = jnp.zeros(((n_tiles + 1) * MOE_TILE,), jnp.int32).at[slot].set(
        jnp.arange(n_pairs, dtype=jnp.int32) // 2)
    tile_start = (tile_end - tiles_e).astype(jnp.int32)
    slot_tab = slot.reshape(-1, COMB_TILE, 2).transpose(0, 2, 1).reshape(-1).astype(jnp.int32)
    return tile_start, tiles_e.astype(jnp.int32), tok_of_slot, slot_tab


ROW_DMA_PRIORITY = 0
BULK_DMA_PRIORITY = 1


def _gmm_kernel(ts_ref, ne_ref, tok_ref, u_hbm, wg_hbm, wu_hbm, wd_hbm, o_hbm,
                xbuf, obuf, gsem, osem, wgb, wub, wdb, wg_buf, wu_buf, wd_buf, wsem,
                *, n_tiles):
    e = pl.program_id(0)
    last = pl.num_programs(0) - 1
    n_used = ts_ref[last] + ne_ref[last]

    def row_copy(tile, r, buf):
        tok = tok_ref[tile * MOE_TILE + r]
        return pltpu.make_async_copy(
            u_hbm.at[tok], xbuf.at[buf, r // SUBLANES, :, r % SUBLANES, :], gsem.at[buf])

    def out_copy(tile, buf):
        return pltpu.make_async_copy(
            obuf.at[buf], o_hbm.at[pl.ds(tile * MOE_TILE, MOE_TILE), :], osem.at[buf])

    def gather_wait(tile, buf):
        del tile
        pltpu.make_async_copy(obuf.at[0], obuf.at[1], gsem.at[buf]).wait()

    @pl.when(e == 0)
    def _():
        def body(r, carry):
            row_copy(0, r, 0).start(priority=ROW_DMA_PRIORITY)
            return carry
        lax.fori_loop(0, MOE_TILE, body, 0, unroll=DMA_UNROLL)

    def weight_copies(expert, buf):
        return (pltpu.make_async_copy(wg_hbm.at[expert], wg_buf.at[buf], wsem.at[buf]),
                pltpu.make_async_copy(wu_hbm.at[expert], wu_buf.at[buf], wsem.at[buf]),
                pltpu.make_async_copy(wd_hbm.at[expert], wd_buf.at[buf], wsem.at[buf]))

    wcur = lax.rem(e, 2)

    @pl.when(e == 0)
    def _():
        for cp in weight_copies(0, 0):
            cp.start(priority=BULK_DMA_PRIORITY)

    @pl.when(e < last)
    def _():
        for cp in weight_copies(e + 1, 1 - wcur):
            cp.start(priority=BULK_DMA_PRIORITY)

    for cp in weight_copies(e, wcur):
        cp.wait()
    wgb[...] = wg_buf[wcur].astype(BF16)
    wub[...] = wu_buf[wcur].astype(BF16)
    wdb[...] = wd_buf[wcur].astype(BF16)

    def tile_body(j, carry):
        t = ts_ref[e] + j
        cur = lax.rem(t, 2)

        @pl.when(t >= 2)
        def _():
            out_copy(t - 2, cur).wait()

        gather_wait(t, cur)
        x = jnp.concatenate(
            [xbuf[cur, :, c].reshape(MOE_TILE, LANES) for c in range(ROW_CHUNKS)],
            axis=1).astype(BF16)
        for r in range(MOE_TILE):
            row_copy(t + 1, r, 1 - cur).start(priority=ROW_DMA_PRIORITY)
        hid = _silu(_dot(x, wgb[...])) * _dot(x, wub[...])
        obuf[cur] = _dot(hid.astype(BF16), wdb[...])
        out_copy(t, cur).start(priority=BULK_DMA_PRIORITY)
        return carry

    lax.fori_loop(0, ne_ref[e], tile_body, 0)

    @pl.when(e == last)
    def _():
        gather_wait(n_used, lax.rem(n_used, 2))
        out_copy(n_used - 2, lax.rem(n_used, 2)).wait()
        out_copy(n_used - 1, lax.rem(n_used - 1, 2)).wait()
        obuf[0] = jnp.zeros(obuf.shape[1:], F32)

        def fill(t, carry):
            cp = out_copy(t, 0)
            cp.start()
            cp.wait()
            return carry
        lax.fori_loop(n_used, n_tiles + 1, fill, 0)


def _gmm(u_all, plan, wg, wu, wd, n_tiles):
    tile_start, tiles_e, tok_of_slot, _ = plan
    assert 2 * u_all.shape[0] >= 2 * MOE_TILE
    grid_spec = pltpu.PrefetchScalarGridSpec(
        num_scalar_prefetch=3,
        grid=(N_EXPERTS,),
        in_specs=[
            pl.BlockSpec(memory_space=pl.ANY),
            pl.BlockSpec(memory_space=pl.ANY),
            pl.BlockSpec(memory_space=pl.ANY),
            pl.BlockSpec(memory_space=pl.ANY),
        ],
        out_specs=pl.BlockSpec(memory_space=pl.ANY),
        scratch_shapes=[
            pltpu.VMEM((2, MOE_TILE // SUBLANES, ROW_CHUNKS, SUBLANES, LANES), F32),
            pltpu.VMEM((2, MOE_TILE, D_MODEL), F32),
            pltpu.SemaphoreType.DMA((2,)),
            pltpu.SemaphoreType.DMA((2,)),
            pltpu.VMEM((D_MODEL, D_EXPERT), BF16),
            pltpu.VMEM((D_MODEL, D_EXPERT), BF16),
            pltpu.VMEM((D_EXPERT, D_MODEL), BF16),
            pltpu.VMEM((2, D_MODEL, D_EXPERT), F32),
            pltpu.VMEM((2, D_MODEL, D_EXPERT), F32),
            pltpu.VMEM((2, D_EXPERT, D_MODEL), F32),
            pltpu.SemaphoreType.DMA((2,)),
        ],
    )
    return pl.pallas_call(
        functools.partial(_gmm_kernel, n_tiles=n_tiles),
        grid_spec=grid_spec,
        out_shape=jax.ShapeDtypeStruct(((n_tiles + 1) * MOE_TILE, D_MODEL), F32),
        compiler_params=_cparams(("arbitrary",)),
        name="gmm",
    )(tile_start, tiles_e, tok_of_slot, u_all, wg, wu, wd)


def _combine_kernel(slot_ref, route_ref, h_ref, o_hbm, yp_ref, ys_ref, gbuf, sem, *, n_prompt):
    i = pl.program_id(0)
    n = pl.num_programs(0)
    cur = lax.rem(i, 2)
    rows = 2 * COMB_TILE

    def row_copy(tile, j, buf):
        slot = slot_ref[tile * rows + j]
        return pltpu.make_async_copy(
            o_hbm.at[pl.ds(slot, 1), :], gbuf.at[buf, pl.ds(j, 1), :], sem.at[buf])

    def issue(tile, buf):
        def body(j, carry):
            row_copy(tile, j, buf).start()
            return carry
        lax.fori_loop(0, rows, body, 0, unroll=DMA_UNROLL)

    def wait(tile, buf):
        del tile
        pltpu.make_async_copy(o_hbm.at[pl.ds(0, rows), :], gbuf.at[buf], sem.at[buf]).wait()

    @pl.when(i == 0)
    def _():
        issue(0, 0)

    @pl.when(i + 1 < n)
    def _():
        issue(i + 1, 1 - cur)

    wait(i, cur)
    w1 = route_ref[:, ROUTE_W1:ROUTE_W1 + 1]
    w2 = route_ref[:, ROUTE_W2:ROUTE_W2 + 1]
    y = h_ref[...] + w1 * gbuf[cur, 0:COMB_TILE, :] + w2 * gbuf[cur, COMB_TILE:rows, :]

    @pl.when(i < n_prompt)
    def _():
        yp_ref[...] = y

    @pl.when(i >= n_prompt)
    def _():
        ys_ref[...] = y


def _combine(route, h_all, o_sorted, plan, t_prompt):
    t_all = h_all.shape[0]
    n_prompt = t_prompt // COMB_TILE
    slot_tab = plan[3]
    grid_spec = pltpu.PrefetchScalarGridSpec(
        num_scalar_prefetch=1,
        grid=(t_all // COMB_TILE,),
        in_specs=[
            pl.BlockSpec((COMB_TILE, LANES), lambda i, st: (i, 0)),
            pl.BlockSpec((COMB_TILE, D_MODEL), lambda i, st: (i, 0)),
            pl.BlockSpec(memory_space=pl.ANY),
        ],
        out_specs=[
            pl.BlockSpec((COMB_TILE, D_MODEL), lambda i, st: (jnp.minimum(i, n_prompt - 1), 0)),
            pl.BlockSpec((COMB_TILE, D_MODEL), lambda i, st: (0, 0)),
        ],
        scratch_shapes=[
            pltpu.VMEM((2, 2 * COMB_TILE, D_MODEL), F32),
            pltpu.SemaphoreType.DMA((2,)),
        ],
    )
    return pl.pallas_call(
        functools.partial(_combine_kernel, n_prompt=n_prompt),
        grid_spec=grid_spec,
        out_shape=[
            jax.ShapeDtypeStruct((t_prompt, D_MODEL), F32),
            jax.ShapeDtypeStruct((t_all - t_prompt, D_MODEL), F32),
        ],
        compiler_params=_cparams(("arbitrary",)),
        name="combine",
    )(slot_tab, route, h_all, o_sorted)


def _tiles(n_rows):
    return dict(
        proj_rows=min(n_rows, 1024), proj_cols=512,
        qk_norm_rows=min(n_rows, 512),
        attn_block=512,
        merge_rows=min(n_rows, 1024), merge_cols=512,
        resid_rows=min(n_rows, 256),
    )


def _layer_tokens(x2d, w):
    tiles = _tiles(x2d.shape[0])
    proj, dt_raw = _proj(x2d, w["norm_attn_w"], w["w_a"], w["w_b"], w["w_dt"],
                         tiles["proj_rows"], tiles["proj_cols"])
    qn, kn, kb, vb = _qk_norm(proj, w["q_norm_w"], w["k_norm_w"], tiles["qk_norm_rows"])
    return proj, dt_raw, qn, kn, kb, vb


def _branch_merge(x2d, o, s, proj, w, t_all, row_off, bufs):
    tiles = _tiles(x2d.shape[0])
    merged = _merge(o, s, w["w_att_out"], w["w_ssm_out"], proj,
                    tiles["merge_rows"], tiles["merge_cols"])
    return _resid(x2d, merged, w["w_o"], w["norm_ffn_w"], w["wr_hi"], w["wr_lo"], w["br"],
                  tiles["resid_rows"], t_all, row_off, bufs)


def kernel(x_prompt, x_sample, cache_k, cache_v, state_ssm, state_conv, page_table, norm_attn_w, w_in, q_norm_w, k_norm_w, lambda_q1, lambda_k1, lambda_q2, lambda_k2, subln_w, w_att_out, conv_w, conv_b, dt_bias, a_log, d_skip, ssm_norm_w, w_ssm_out, w_o, norm_ffn_w, w_group_router, b_group_router, w_expert_router, b_expert_router, w_gate, w_up, w_down):
    layer = 0
    nb, seq, _ = x_prompt.shape
    db, dec_seq, _ = x_sample.shape

    w_in_l = w_in[layer]
    c_dt = Q_WIDTH + K_WIDTH + V_WIDTH + D_SSM + CONV_DIM
    w_a = w_in_l
    w_b = w_in_l[:, c_dt + N_SSM_HEADS:]
    w_dt = jnp.pad(w_in_l[:, c_dt:c_dt + N_SSM_HEADS], ((0, 0), (0, LANES - N_SSM_HEADS))).astype(BF16)
    wr = jnp.concatenate([w_expert_router[layer], w_group_router[layer]], axis=1)
    wr = jnp.pad(wr, ((0, 0), (0, LANES - wr.shape[1])))
    wr_hi = wr.astype(BF16)
    wr_lo = (wr - wr_hi.astype(F32)).astype(BF16)
    br = jnp.concatenate([b_expert_router[layer], b_group_router[layer]])
    br = jnp.pad(br, (0, LANES - br.shape[0])).reshape(1, LANES)
    pad_h = (0, LANES - N_SSM_HEADS)
    w = dict(
        norm_attn_w=norm_attn_w[layer].reshape(1, D_MODEL), w_a=w_a, w_b=w_b, w_dt=w_dt,
        q_norm_w=q_norm_w[layer], k_norm_w=k_norm_w[layer],
        w_att_out=w_att_out[layer].astype(BF16), w_ssm_out=w_ssm_out[layer].astype(BF16),
        w_o=w_o[layer].astype(BF16), norm_ffn_w=norm_ffn_w[layer].reshape(1, D_MODEL),
        wr_hi=wr_hi, wr_lo=wr_lo, br=br,
        w_gate=w_gate[layer], w_up=w_up[layer], w_down=w_down[layer],
    )
    ssm_prm = dict(
        conv_w=conv_w[layer], conv_b=conv_b[layer].reshape(1, CONV_DIM),
        dt_bias=jnp.pad(dt_bias[layer], pad_h).reshape(1, LANES),
        dt_bias_t=dt_bias[layer].reshape(N_SSM_HEADS, 1),
        a=jnp.pad(-jnp.exp(a_log[layer]), pad_h).reshape(1, LANES),
        a_t=(-jnp.exp(a_log[layer])).reshape(N_SSM_HEADS, 1),
        d_skip=jnp.repeat(d_skip[layer], SSM_HEAD_DIM).reshape(1, D_SSM),
        ssm_norm_w=ssm_norm_w[layer].reshape(1, D_SSM),
    )
    lam_vecs = jnp.stack([lambda_q1[layer], lambda_k1[layer], lambda_q2[layer], lambda_k2[layer]])
    sw = subln_w[layer]
    ssd_cols = (COL_X // D_SSM, COL_B // (N_GROUPS * D_STATE), COL_C // (N_GROUPS * D_STATE),
                COL_Z // D_SSM)

    xp = x_prompt.reshape(nb * seq, D_MODEL)
    n_tok = db * dec_seq
    t_prompt = nb * seq
    t_all = t_prompt + n_tok
    proj_p, dt_p, qn_p, kn_p, kb_p, vb_p = _layer_tokens(xp, w)
    o_p = _attn_prompt(qn_p, kb_p, vb_p, lam_vecs, sw, _tiles(t_prompt)["attn_block"])
    s_p, ssm_p = _ssd(
        proj_p.reshape(nb, seq, PROJ_WIDTH), ssd_cols, dt_p.reshape(nb, seq, LANES),
        jnp.zeros((nb, HALO, CONV_DIM), F32), jnp.zeros((nb, N_SSM_HEADS, SSM_HEAD_DIM, D_STATE), F32),
        ssm_prm, SSD_CHUNK, SSD_CHUNK)
    bufs = _branch_merge(xp, o_p, s_p.reshape(t_prompt, D_SSM), proj_p, w, t_all, 0, None)
    keep = CONV_WIDTH - 1
    conv_p = proj_p.reshape(nb, seq, PROJ_WIDTH)[:, seq - keep:, COL_X:COL_X + CONV_DIM]

    xs = x_sample.reshape(db * dec_seq, D_MODEL)
    proj_s, dt_s, qn_s, kn_s, _, _ = _layer_tokens(xs, w)
    v_s = proj_s[:, COL_V:COL_V + V_WIDTH]
    o_s = _attn_sample(qn_s, kn_s, v_s, cache_k[layer], cache_v[layer], page_table, lam_vecs, sw)
    rows_s = SUBLANES
    pad_rows = ((0, 0), (0, rows_s - dec_seq), (0, 0))
    src_s = jnp.pad(proj_s[:, COL_Z:COL_GA].reshape(db, dec_seq, COL_GA - COL_Z), pad_rows)
    halo_s = jnp.pad(state_conv[layer], ((0, 0), (HALO - (CONV_WIDTH - 1), 0), (0, 0)))
    cols_s = ((COL_X - COL_Z) // D_SSM, (COL_B - COL_Z) // (N_GROUPS * D_STATE),
              (COL_C - COL_Z) // (N_GROUPS * D_STATE), 0)
    s_s, ssm_s = _ssd(
        src_s, cols_s, jnp.pad(dt_s.reshape(db, dec_seq, LANES), pad_rows), halo_s,
        state_ssm[layer], ssm_prm, rows_s, dec_seq)
    s_s = s_s[:, :dec_seq].reshape(n_tok, D_SSM)
    h_all, u_all, route = _branch_merge(xs, o_s, s_s, proj_s, w, t_all, t_prompt, bufs)

    n_tiles = 2 * t_all // MOE_TILE + N_EXPERTS
    plan = _route_plan(route, n_tiles)
    o_sorted = _gmm(u_all.reshape(t_all, ROW_CHUNKS, LANES), plan, w["w_gate"], w["w_up"], w["w_down"], n_tiles)
    y_p, y_s = _combine(route, h_all, o_sorted, plan, t_prompt)
    conv_s =proj_s.reshape(db, dec_seq, PROJ_WIDTH)[:, dec_seq - keep:, COL_X:COL_X + CONV_DIM]

    return (
        y_p.reshape(nb, seq, D_MODEL),
        y_s.reshape(db, dec_seq, D_MODEL),
        kn_p.reshape(1, nb, seq, N_KV_HEADS, 2 * HEAD_DIM),
        proj_p[:, COL_V:COL_V + V_WIDTH].reshape(1, nb, seq, N_KV_HEADS, V_DIM),
        ssm_p.reshape(1, nb, N_SSM_HEADS, SSM_HEAD_DIM, D_STATE),
        conv_p[None],
        kn_s.reshape(1, db, dec_seq, N_KV_HEADS, 2 * HEAD_DIM),
        v_s.reshape(1, db, dec_seq, N_KV_HEADS, V_DIM),
        ssm_s.reshape(1, db, N_SSM_HEADS, SSM_HEAD_DIM, D_STATE),
        conv_s[None],
    )
```

```python
import functools
import math

import jax
import jax.numpy as jnp
import ml_dtypes
import numpy as np
from jax import lax
from jax.experimental import pallas as pl
from jax.experimental.pallas import tpu as pltpu

F32 = jnp.float32
BF16 = jnp.bfloat16

D_MODEL = 2048
N_HEADS = 8
N_KV_HEADS = 4
GQA_REP = N_HEADS // N_KV_HEADS
HEAD_DIM = 64
V_DIM = 2 * HEAD_DIM
Q_WIDTH = N_HEADS * 2 * HEAD_DIM
K_WIDTH = N_KV_HEADS * 2 * HEAD_DIM
V_WIDTH = N_KV_HEADS * V_DIM
ATT_WIDTH = N_HEADS * V_DIM
D_SSM = D_MODEL
SSM_HEAD_DIM = 64
N_SSM_HEADS = D_SSM // SSM_HEAD_DIM
N_GROUPS = 4
HEADS_PER_GROUP = N_SSM_HEADS // N_GROUPS
D_STATE = 128
CONV_WIDTH = 4
CONV_DIM = D_SSM + 2 * N_GROUPS * D_STATE
SSD_CHUNK = 128
N_EXPERT_GROUPS = 4
EXPERTS_PER_GROUP = 8
N_EXPERTS = N_EXPERT_GROUPS * EXPERTS_PER_GROUP
D_EXPERT = D_MODEL // 4
PAGE_SIZE = 128
EPS = 1e-6
LAM_INIT = 0.8 - 0.6 * math.exp(-0.3 * 0)

LANES = 128
SUBLANES = 8
NEG_BIG = -1e30
VMEM_LIMIT = 56 * 1024 * 1024

COL_Q = 0
COL_K = COL_Q + Q_WIDTH
COL_V = COL_K + K_WIDTH
COL_Z = COL_V + V_WIDTH
COL_X = COL_Z + D_SSM
COL_B = COL_X + D_SSM
COL_C = COL_B + N_GROUPS * D_STATE
COL_GA = COL_C + N_GROUPS * D_STATE
COL_GS = COL_GA + D_MODEL
PROJ_WIDTH = COL_GS + D_MODEL

ALIBI_SLOPES = [2.0 ** (-8.0 * (h + 1) / N_HEADS) for h in range(N_HEADS)]


def _cparams(sem):
    return pltpu.CompilerParams(dimension_semantics=sem, vmem_limit_bytes=VMEM_LIMIT)


def _dot(a, b):
    return jnp.dot(a, b, preferred_element_type=F32)


def _dot_nt(a, b):
    return lax.dot_general(a, b, (((1,), (1,)), ((), ())), preferred_element_type=F32)


def _dot_tn(a, b):
    return lax.dot_general(a, b, (((0,), (0,)), ((), ())), preferred_element_type=F32)


def _split2(x):
    hi = x.astype(BF16)
    lo = (x - hi.astype(F32)).astype(BF16)
    return hi, lo


def _split3(x):
    hi = x.astype(BF16)
    r = x - hi.astype(F32)
    mid = r.astype(BF16)
    lo = (r - mid.astype(F32)).astype(BF16)
    return hi, mid, lo


def _dot_x2(x, sel):
    hi, lo = _split2(x)
    return _dot(hi, sel) + _dot(lo, sel)


def _dot_x3(x, sel):
    hi, mid, lo = _split3(x)
    return _dot(hi, sel) + _dot(mid, sel) + _dot(lo, sel)


def _sigmoid(x):
    return 1.0 / (1.0 + jnp.exp(-x))


def _silu(x):
    return x * _sigmoid(x)


def _softplus(x):
    return jnp.maximum(x, 0.0) + jnp.log1p(jnp.exp(-jnp.abs(x)))


NORM_ROWS = 256


def _proj_kernel(x_ref, nw_ref, wa_ref, wb_ref, wdt_ref, o_ref, dt_ref, u_scr, *, n_a):
    j = pl.program_id(1)

    @pl.when(j == 0)
    def _():
        tm = x_ref.shape[0]
        for lo in range(0, tm, min(tm, NORM_ROWS)):
            hi = lo + min(tm, NORM_ROWS)
            x = x_ref[lo:hi, :]
            ms = jnp.mean(x * x, axis=-1, keepdims=True)
            u_scr[lo:hi, :] = (x * lax.rsqrt(ms + EPS) * nw_ref[...]).astype(BF16)
        dt_ref[...] = _dot(u_scr[...], wdt_ref[...])

    @pl.when(j < n_a)
    def _():
        o_ref[...] = _dot(u_scr[...], wa_ref[...])

    @pl.when(j >= n_a)
    def _():
        o_ref[...] = _dot(u_scr[...], wb_ref[...])


def _proj(x, norm_w, w_a, w_b, w_dt, tm, tn):
    t = x.shape[0]
    n_a = w_a.shape[1] // tn
    return pl.pallas_call(
        functools.partial(_proj_kernel, n_a=n_a),
        grid=(t // tm, PROJ_WIDTH // tn),
        in_specs=[
            pl.BlockSpec((tm, D_MODEL), lambda i, j: (i, 0)),
            pl.BlockSpec((1, D_MODEL), lambda i, j: (0, 0)),
            pl.BlockSpec((D_MODEL, tn), lambda i, j: (0, jnp.minimum(j, n_a - 1))),
            pl.BlockSpec((D_MODEL, tn), lambda i, j: (0, jnp.maximum(j - n_a, 0))),
            pl.BlockSpec((D_MODEL, LANES), lambda i, j: (0, 0)),
        ],
        out_specs=[
            pl.BlockSpec((tm, tn), lambda i, j: (i, j)),
            pl.BlockSpec((tm, LANES), lambda i, j: (i, 0)),
        ],
        out_shape=[
            jax.ShapeDtypeStruct((t, PROJ_WIDTH), F32),
            jax.ShapeDtypeStruct((t, LANES), F32),
        ],
        scratch_shapes=[pltpu.VMEM((tm, D_MODEL), BF16)],
        compiler_params=_cparams(("arbitrary", "arbitrary")),
        name="proj",
    )(x, norm_w, w_a, w_b, w_dt)


LOG2E = math.log2(math.e)
Q_SCALE = LOG2E * HEAD_DIM ** -0.5


def _qknorm_kernel(p_ref, qw_ref, kw_ref, g_ref, qn_ref, kn_ref, kb_ref, vb_ref):
    gsum = g_ref[...]
    n_q = Q_WIDTH // LANES
    for c in range((Q_WIDTH + K_WIDTH) // LANES):
        x = p_ref[:, c * LANES:(c + 1) * LANES]
        ss = _dot_x2(x * x, gsum)
        y = x * lax.rsqrt(ss * (1.0 / HEAD_DIM) + EPS)
        if c < n_q:
            qn_ref[:, c * LANES:(c + 1) * LANES] = (y * qw_ref[...] * Q_SCALE).astype(BF16)
        else:
            kn = y * kw_ref[...]
            kn_ref[:, (c - n_q) * LANES:(c - n_q + 1) * LANES] = kn
            kb_ref[:, (c - n_q) * LANES:(c - n_q + 1) * LANES] = kn.astype(BF16)
    vb_ref[...] = p_ref[:, COL_V:COL_V + V_WIDTH].astype(BF16)


def _qk_norm(proj, q_norm_w, k_norm_w, tm):
    t = proj.shape[0]
    group = np.kron(np.eye(LANES // HEAD_DIM), np.ones((HEAD_DIM, HEAD_DIM)))
    qw = jnp.tile(q_norm_w, LANES // HEAD_DIM).reshape(1, LANES)
    kw = jnp.tile(k_norm_w, LANES // HEAD_DIM).reshape(1, LANES)
    return pl.pallas_call(
        _qknorm_kernel,
        grid=(t // tm,),
        in_specs=[
            pl.BlockSpec((tm, Q_WIDTH + K_WIDTH + V_WIDTH), lambda i: (i, 0)),
            pl.BlockSpec((1, LANES), lambda i: (0, 0)),
            pl.BlockSpec((1, LANES), lambda i: (0, 0)),
            pl.BlockSpec((LANES, LANES), lambda i: (0, 0)),
        ],
        out_specs=[
            pl.BlockSpec((tm, Q_WIDTH), lambda i: (i, 0)),
            pl.BlockSpec((tm, K_WIDTH), lambda i: (i, 0)),
            pl.BlockSpec((tm, K_WIDTH), lambda i: (i, 0)),
            pl.BlockSpec((tm, V_WIDTH), lambda i: (i, 0)),
        ],
        out_shape=[
            jax.ShapeDtypeStruct((t, Q_WIDTH), BF16),
            jax.ShapeDtypeStruct((t, K_WIDTH), F32),
            jax.ShapeDtypeStruct((t, K_WIDTH), BF16),
            jax.ShapeDtypeStruct((t, V_WIDTH), BF16),
        ],
        compiler_params=_cparams(("arbitrary",)),
        name="qk_norm",
    )(proj, qw, kw, jnp.asarray(group, BF16))


def _diff_lambda(lam_ref):
    lamv = lam_ref[...]
    s1 = jnp.sum(lamv[0:1] * lamv[1:2], axis=1, keepdims=True)
    s2 = jnp.sum(lamv[2:3] * lamv[3:4], axis=1, keepdims=True)
    return jnp.exp(s1) - jnp.exp(s2) + LAM_INIT


def _subln(o, w):
    ms = jnp.mean(o * o, axis=-1, keepdims=True)
    return o * lax.rsqrt(ms + EPS) * w * (1.0 - LAM_INIT)


N_SLOPE_PARTS = 3


def _bf16_parts(x, n):
    parts, rem = [], np.float32(x)
    for _ in range(n):
        p = np.float32(rem.astype(ml_dtypes.bfloat16))
        parts.append(float(p))
        rem = np.float32(rem - p)
    return parts


def _alibi_tables(tk):
    qcols = np.zeros((N_HEADS, 16, LANES), np.float32)
    csum = np.zeros((N_HEADS,), np.float32)
    for h, slope in enumerate(ALIBI_SLOPES):
        parts = _bf16_parts(slope * LOG2E, N_SLOPE_PARTS)
        csum[h] = np.float32(sum(np.float32(p) for p in parts))
        for i, p in enumerate(parts):
            qcols[h, :, i] = p * LANES
            qcols[h, :, N_SLOPE_PARTS + i] = p
    pos = np.arange(tk)
    kcols = np.zeros((tk, LANES), np.float32)
    kcols[:, 0:N_SLOPE_PARTS] = (pos // LANES)[:, None]
    kcols[:, N_SLOPE_PARTS:2 * N_SLOPE_PARTS] = (pos % LANES)[:, None]
    return jnp.asarray(qcols, BF16), jnp.asarray(kcols, BF16), jnp.asarray(csum, F32)


def _attn_p_kernel(qi_ref, ki_ref, cf_ref, q_ref, k_ref, v_ref, qc_ref, kc_ref, lam_ref, sw_ref,
                   o_ref, qa_scr, m_scr, l_scr, acc_scr, *, tq):
    g = pl.program_id(0)
    t = pl.program_id(1)
    qi = qi_ref[t]
    ki = ki_ref[t]
    n_chunk = tq // LANES

    @pl.when(ki == 0)
    def _():
        lane = lax.broadcasted_iota(jnp.int32, (tq, LANES), 1)
        for r in range(GQA_REP):
            qq = q_ref[:, r * LANES:(r + 1) * LANES]
            qc = jnp.broadcast_to(qc_ref[r, 0:1, :], (tq, LANES))
            for c in range(2):
                idx = 2 * r + c
                keep = (lane < HEAD_DIM) if c == 0 else (lane >= HEAD_DIM)
                qa_scr[idx * tq:(idx + 1) * tq, 0:LANES] = jnp.where(keep, qq, jnp.zeros_like(qq))
                qa_scr[idx * tq:(idx + 1) * tq, LANES:2 * LANES] = qc
        m_scr[...] = jnp.full(m_scr.shape, NEG_BIG, F32)
        l_scr[...] = jnp.zeros(l_scr.shape, F32)
        acc_scr[...] = jnp.zeros(acc_scr.shape, F32)

    def step(diag):
        k_aug = jnp.concatenate([k_ref[...], kc_ref[...]], axis=1)
        block_dist = ((qi - ki) * tq).astype(F32)
        for r in range(GQA_REP):
            off = -cf_ref[g * GQA_REP + r] * block_dist
            lo, hi = r * 2 * tq, (r + 1) * 2 * tq
            s = _dot_nt(qa_scr[lo:hi, :], k_aug)
            if diag:
                row_in = lax.broadcasted_iota(jnp.int32, (2 * tq, tq), 0) & (tq - 1)
                col = lax.broadcasted_iota(jnp.int32, (2 * tq, tq), 1)
                s = jnp.where(col <= row_in, s, NEG_BIG)
            chunks = [s[:, j * LANES:(j + 1) * LANES] for j in range(n_chunk)]
            m_prev = m_scr[lo:hi]
            m_blk = jnp.max(functools.reduce(jnp.maximum, chunks), axis=1, keepdims=True) + off
            m_new = jnp.maximum(m_prev, m_blk)
            alpha = jnp.exp2(m_prev - m_new)
            m_sub = m_new - off
            pj = [jnp.exp2(ch - m_sub) for ch in chunks]
            l_scr[lo:hi] = alpha * l_scr[lo:hi] + functools.reduce(jnp.add, pj)
            m_scr[lo:hi] = m_new
            pv = _dot(jnp.concatenate(pj, axis=1).astype(BF16), v_ref[...])
            acc_scr[lo:hi] = alpha * acc_scr[lo:hi] + pv

    @pl.when(ki < qi)
    def _():
        step(False)

    @pl.when(ki == qi)
    def _():
        step(True)
        lam = _diff_lambda(lam_ref)
        for r in range(GQA_REP):
            i1, i2 = 2 * r * tq, (2 * r + 1) * tq
            l1 = jnp.sum(l_scr[i1:i1 + tq], axis=1, keepdims=True)
            l2 = jnp.sum(l_scr[i2:i2 + tq], axis=1, keepdims=True)
            o = acc_scr[i1:i1 + tq] / l1 - lam * (acc_scr[i2:i2 + tq] / l2)
            o_ref[:, r * LANES:(r + 1) * LANES] = _subln(o, sw_ref[...]).astype(BF16)


def _attn_prompt(qn, kb, vb, lam_vecs, subln_w, tq):
    t = qn.shape[0]
    nq = t // tq
    pairs = [(i, j) for i in range(nq) for j in range(i + 1)]
    qi_tab = jnp.asarray([p[0] for p in pairs], jnp.int32)
    ki_tab = jnp.asarray([p[1] for p in pairs], jnp.int32)
    qcols, kcols, csum = _alibi_tables(tq)
    n_sub = 2 * GQA_REP
    grid_spec = pltpu.PrefetchScalarGridSpec(
        num_scalar_prefetch=3,
        grid=(N_KV_HEADS, len(pairs)),
        in_specs=[
            pl.BlockSpec((tq, GQA_REP * LANES), lambda g, t, qi, ki, cf: (qi[t], g)),
            pl.BlockSpec((tq, LANES), lambda g, t, qi, ki, cf: (ki[t], g)),
            pl.BlockSpec((tq, V_DIM), lambda g, t, qi, ki, cf: (ki[t], g)),
            pl.BlockSpec((GQA_REP, 16, LANES), lambda g, t, qi, ki, cf: (g, 0, 0)),
            pl.BlockSpec((tq, LANES), lambda g, t, qi, ki, cf: (0, 0)),
            pl.BlockSpec((4, HEAD_DIM), lambda g, t, qi, ki, cf: (0, 0)),
            pl.BlockSpec((1, V_DIM), lambda g, t, qi, ki, cf: (0, 0)),
        ],
        out_specs=pl.BlockSpec((tq, GQA_REP * V_DIM), lambda g, t, qi, ki, cf: (qi[t], g)),
        scratch_shapes=[
            pltpu.VMEM((n_sub * tq, 2 * LANES), BF16),
            pltpu.VMEM((n_sub * tq, LANES), F32),
            pltpu.VMEM((n_sub * tq, LANES), F32),
            pltpu.VMEM((n_sub * tq, V_DIM), F32),
        ],
    )
    return pl.pallas_call(
        functools.partial(_attn_p_kernel, tq=tq),
        grid_spec=grid_spec,
        out_shape=jax.ShapeDtypeStruct((t, ATT_WIDTH), BF16),
        compiler_params=_cparams(("arbitrary", "arbitrary")),
        name="attn_p",
    )(qi_tab, ki_tab, csum, qn, kb, vb, qcols, kcols, lam_vecs, subln_w.reshape(1, V_DIM))


PAGES_PER_STEP = 32
PAGE_GROUP = PAGES_PER_STEP
ROWS_S = 2 * 4 * N_HEADS


def _attn_s_kernel(pt_ref, q_ref, d0_ref, mask_ref, sl_ref, bn_ref, kn_ref, vn_ref, lam_ref,
                   sw_ref, *rest, n_steps):
    k_refs = [r.at[0] for r in rest[:PAGES_PER_STEP]]
    v_refs = [r.at[0] for r in rest[PAGES_PER_STEP:2 * PAGES_PER_STEP]]
    o_ref = rest[2 * PAGES_PER_STEP]
    m_scr, l_scr, acc_scr = rest[2 * PAGES_PER_STEP + 1:]
    s_id = pl.program_id(1)

    @pl.when(s_id == 0)
    def _():
        m_scr[...] = jnp.full(m_scr.shape, NEG_BIG, F32)
        l_scr[...] = jnp.zeros(l_scr.shape, F32)
        acc_scr[...] = jnp.zeros(acc_scr.shape, F32)

    q = q_ref[0]

    def update(scores, values):
        m_prev = m_scr[...]
        m_new = m_prev
        for sc in scores:
            m_new = jnp.maximum(m_new, jnp.max(sc, axis=1, keepdims=True))
        alpha = jnp.exp2(m_prev - m_new)
        l_new = alpha * l_scr[...]
        acc = alpha * acc_scr[...]
        for sc, vv in zip(scores, values):
            p = jnp.exp2(sc - m_new)
            l_new = l_new + jnp.sum(p, axis=1, keepdims=True)
            acc = acc + _dot(p.astype(BF16), vv)
        m_scr[...] = m_new
        l_scr[...] = l_new
        acc_scr[...] = acc

    for first in range(0, PAGES_PER_STEP, PAGE_GROUP):
        scores, values = [], []
        for i in range(first, first + PAGE_GROUP):
            page_start = ((s_id * PAGES_PER_STEP + i) * PAGE_SIZE).astype(F32)
            bias = sl_ref[...] * (d0_ref[...] - page_start) + mask_ref[...]
            scores.append(_dot_nt(q, k_refs[i][...].astype(BF16)) + bias)
            values.append(v_refs[i][...].astype(BF16))
        update(scores, values)

    @pl.when(s_id == n_steps - 1)
    def _():
        sc = _dot_nt(q, kn_ref[0].astype(BF16)) + bn_ref[...]
        update([sc], [vn_ref[0].astype(BF16)])
        lam = _diff_lambda(lam_ref)
        half = ROWS_S // 2
        o1 = acc_scr[0:half] / l_scr[0:half]
        o2 = acc_scr[half:ROWS_S] / l_scr[half:ROWS_S]
        o_ref[0] = _subln(o1 - lam * o2, sw_ref[...]).astype(BF16)


def _attn_sample(qn_s, kn_s, v_s, cache_k, cache_v, page_table, lam_vecs, subln_w):
    db, n_pages = page_table.shape
    dec_seq = qn_s.shape[0] // db
    past = n_pages * PAGE_SIZE
    n_steps = n_pages // PAGES_PER_STEP
    page_rows = PAGE_SIZE * N_KV_HEADS
    n_phys = cache_k.shape[0]
    ck = cache_k.reshape(n_phys, page_rows, 2 * HEAD_DIM)
    cv = cache_v.reshape(n_phys, page_rows, V_DIM)

    q5 = qn_s.reshape(db, dec_seq, N_HEADS, 2, HEAD_DIM)
    zeros = jnp.zeros_like(q5[:, :, :, 0])
    q_all = jnp.stack([jnp.concatenate([q5[:, :, :, 0], zeros], axis=-1),
                       jnp.concatenate([zeros, q5[:, :, :, 1]], axis=-1)], axis=1)
    q_all = q_all.reshape(db, ROWS_S, LANES)

    r = np.arange(ROWS_S)
    tok_r = (r % (dec_seq * N_HEADS)) // N_HEADS
    head_r = r % N_HEADS
    slope_r = np.asarray(ALIBI_SLOPES)[head_r] * LOG2E
    c = np.arange(page_rows)
    key_c, grp_c = c // N_KV_HEADS, c % N_KV_HEADS
    same = (head_r[:, None] // GQA_REP) == grp_c[None, :]
    d0 = np.broadcast_to(past + tok_r[:, None] - key_c[None, :], (ROWS_S, page_rows))
    mask = np.where(same, 0.0, NEG_BIG)
    sl = np.broadcast_to(-slope_r[:, None], (ROWS_S, 1))
    cn = np.arange(LANES)
    tok_c, grp_n = cn // N_KV_HEADS, cn % N_KV_HEADS
    ok = ((head_r[:, None] // GQA_REP) == grp_n[None, :]) & (tok_c[None, :] <= tok_r[:, None])
    bn = np.where(ok, -slope_r[:, None] * (tok_r[:, None] - tok_c[None, :]), NEG_BIG)

    new_rows = dec_seq * N_KV_HEADS
    kn_pad = jnp.pad(kn_s.reshape(db, new_rows, LANES), ((0, 0), (0, LANES - new_rows), (0, 0)))
    vn_pad = jnp.pad(v_s.reshape(db, new_rows, LANES), ((0, 0), (0, LANES - new_rows), (0, 0)))

    def const(shape):
        return pl.BlockSpec(shape, lambda b, s, pt: (0,) * len(shape))

    def page_spec(i):
        return pl.BlockSpec(
            (1, page_rows, LANES),
            lambda b, s, pt: (pt[b * n_pages + s * PAGES_PER_STEP + i], 0, 0))

    grid_spec = pltpu.PrefetchScalarGridSpec(
        num_scalar_prefetch=1,
        grid=(db, n_steps),
        in_specs=[
            pl.BlockSpec((1, ROWS_S, LANES), lambda b, s, pt: (b, 0, 0)),
            const((ROWS_S, page_rows)),
            const((ROWS_S, page_rows)),
            const((ROWS_S, 1)),
            const((ROWS_S, LANES)),
            pl.BlockSpec((1, LANES, LANES), lambda b, s, pt: (b, 0, 0)),
            pl.BlockSpec((1, LANES, LANES), lambda b, s, pt: (b, 0, 0)),
            const((4, HEAD_DIM)),
            const((1, V_DIM)),
        ] + [page_spec(i) for i in range(PAGES_PER_STEP)] * 2,
        out_specs=pl.BlockSpec((1, ROWS_S // 2, V_DIM), lambda b, s, pt: (b, 0, 0)),
        scratch_shapes=[
            pltpu.VMEM((ROWS_S, 1), F32),
            pltpu.VMEM((ROWS_S, 1), F32),
            pltpu.VMEM((ROWS_S, V_DIM), F32),
        ],
    )
    o = pl.pallas_call(
        functools.partial(_attn_s_kernel, n_steps=n_steps),
        grid_spec=grid_spec,
        out_shape=jax.ShapeDtypeStruct((db, ROWS_S // 2, V_DIM), BF16),
        compiler_params=_cparams(("arbitrary", "arbitrary")),
        name="attn_s",
    )(page_table.reshape(-1), q_all, jnp.asarray(d0, F32), jnp.asarray(mask, F32),
      jnp.asarray(sl, F32), jnp.asarray(bn, F32), kn_pad, vn_pad, lam_vecs,
      subln_w.reshape(1, V_DIM),
      *([ck] * PAGES_PER_STEP), *([cv] * PAGES_PER_STEP))
    return o.reshape(db * dec_seq, ATT_WIDTH)


HALO = SUBLANES


def _ssd_kernel(xs_ref, b_ref, c_ref, z_ref, dt_ref, dtt_ref, halo_ref, init_ref,
                cw_ref, cb_ref, dtb_ref, dtbt_ref, a_ref, at_ref, dsk_ref, nw_ref,
                tri_ref, trit_ref, exp_ref, sel_ref,
                y_ref, fin_ref, win_scr, state_scr, *, rows_in, n_valid):
    ci = pl.program_id(1)
    n_chunks = pl.num_programs(1)
    lc = SSD_CHUNK
    bc_w = N_GROUPS * D_STATE

    @pl.when(ci == 0)
    def _():
        state_scr[...] = init_ref[0].reshape(D_SSM, D_STATE)
        win_scr[0:HALO, :] = halo_ref[0]

    if rows_in < lc:
        win_scr[HALO:HALO + lc, :] = jnp.zeros((lc, CONV_DIM), F32)
    win_scr[HALO:HALO + rows_in, 0:D_SSM] = xs_ref[0]
    win_scr[HALO:HALO + rows_in, D_SSM:D_SSM + bc_w] = b_ref[0]
    win_scr[HALO:HALO + rows_in, D_SSM + bc_w:CONV_DIM] = c_ref[0]

    acc = cb_ref[...]
    for tap in range(CONV_WIDTH):
        off = HALO - (CONV_WIDTH - 1) + tap
        acc = acc + win_scr[off:off + lc, :] * cw_ref[tap:tap + 1, :]
    conv = _silu(acc)
    win_scr[0:HALO, :] = win_scr[lc:lc + HALO, :]
    xs = conv[:, 0:D_SSM]
    bm = conv[:, D_SSM:D_SSM + bc_w].astype(BF16)
    cm = conv[:, D_SSM + bc_w:CONV_DIM].astype(BF16)

    if rows_in < lc:
        dt_in = jnp.concatenate([dt_ref[0], jnp.zeros((lc - rows_in, LANES), F32)], axis=0)
        dtt_in = jnp.concatenate(
            [dtt_ref[0], jnp.zeros((N_SSM_HEADS, lc - rows_in), F32)], axis=1)
    else:
        dt_in, dtt_in = dt_ref[0], dtt_ref[0]
    rowi = lax.broadcasted_iota(jnp.int32, (lc, LANES), 0)
    coli = lax.broadcasted_iota(jnp.int32, (N_SSM_HEADS, lc), 1)
    dt = jnp.where(rowi < n_valid, _softplus(dt_in + dtb_ref[...]), 0.0)
    dtt = jnp.where(coli < n_valid, _softplus(dtt_in + dtbt_ref[...]), 0.0)
    a_cs = _dot_x3_left(tri_ref[...], dt * a_ref[...])
    a_cst = _dot_x3(dtt * at_ref[...], trit_ref[...])
    a_last = a_cs[lc - 1:lc, :]
    exp_cs = jnp.exp(a_cs)
    exp_rest = jnp.exp(a_last - a_cs)
    expand = exp_ref[...]
    dtx = _dot_x2(dt, expand)
    ecx = _dot_x2(exp_cs, expand)
    erx = _dot_x2(exp_rest, expand)
    xc = xs * dtx
    xcb = xc.astype(BF16)
    xcd = (xc * erx).astype(BF16)

    last_t = jnp.exp(a_cst[:, lc - 1:lc])
    rdec = _dot_x2_left(sel_ref[...], jnp.broadcast_to(last_t, (N_SSM_HEADS, D_STATE)))

    tril = (lax.broadcasted_iota(jnp.int32, (lc, lc), 0)
            >= lax.broadcasted_iota(jnp.int32, (lc, lc), 1))
    lane = lax.broadcasted_iota(jnp.int32, (lc, LANES), 1)
    gw = HEADS_PER_GROUP * SSM_HEAD_DIM
    y_parts = []
    for g in range(N_GROUPS):
        bg = bm[:, g * D_STATE:(g + 1) * D_STATE]
        cg = cm[:, g * D_STATE:(g + 1) * D_STATE]
        cb = _dot_nt(cg, bg)
        st = state_scr[g * gw:(g + 1) * gw, :]
        y_off = _dot_nt(cg, st.astype(BF16)) * ecx[:, g * gw:(g + 1) * gw]
        new_st = _dot_tn(xcd[:, g * gw:(g + 1) * gw], bg)
        state_scr[g * gw:(g + 1) * gw, :] = st * rdec[g * gw:(g + 1) * gw, :] + new_st
        for j in range(HEADS_PER_GROUP // 2):
            pair = g * (HEADS_PER_GROUP // 2) + j
            blk = xcb[:, pair * LANES:(pair + 1) * LANES]
            y_pair = None
            for half in range(2):
                h = 2 * pair + half
                seg = a_cs[:, h:h + 1] - a_cst[h:h + 1, :]
                decay = jnp.exp(jnp.where(tril, seg, NEG_BIG))
                mh = (cb * decay).astype(BF16)
                keep = (lane < SSM_HEAD_DIM) if half == 0 else (lane >= SSM_HEAD_DIM)
                part = _dot(mh, jnp.where(keep, blk, jnp.zeros_like(blk)))
                y_pair = part if y_pair is None else y_pair + part
            y_parts.append(y_pair + y_off[:, (pair % (HEADS_PER_GROUP // 2)) * LANES:
                                          (pair % (HEADS_PER_GROUP // 2) + 1) * LANES])
    y = jnp.concatenate(y_parts, axis=1)
    y = y + dsk_ref[...] * xs
    if rows_in < lc:
        z = jnp.concatenate([z_ref[0], jnp.zeros((lc - rows_in, D_SSM), F32)], axis=0)
    else:
        z = z_ref[0]
    y = y * _silu(z)
    gn = D_SSM // N_GROUPS
    outs = []
    for g in range(N_GROUPS):
        yg = y[:, g * gn:(g + 1) * gn]
        ms = jnp.mean(yg * yg, axis=-1, keepdims=True)
        outs.append(yg * lax.rsqrt(ms + EPS) * nw_ref[:, g * gn:(g + 1) * gn])
    out = jnp.concatenate(outs, axis=1).astype(BF16)
    y_ref[0] = out[0:rows_in]

    @pl.when(ci == n_chunks - 1)
    def _():
        fin_ref[0] = state_scr[...].reshape(N_SSM_HEADS, SSM_HEAD_DIM, D_STATE)


def _dot_x3_left(sel, x):
    hi, mid, lo = _split3(x)
    return _dot(sel, hi) + _dot(sel, mid) + _dot(sel, lo)


def _dot_x2_left(sel, x):
    hi, lo = _split2(x)
    return _dot(sel, hi) + _dot(sel, lo)


def _ssd(src, col_blocks, dt_raw, halo, init_state, prm, rows_in, n_valid):
    nb, seq = src.shape[0], src.shape[1]
    n_chunks = max(1, seq // SSD_CHUNK)
    bc_w = N_GROUPS * D_STATE
    dtt = jnp.swapaxes(dt_raw[:, :, :N_SSM_HEADS], 1, 2)
    tri = np.tril(np.ones((SSD_CHUNK, SSD_CHUNK)))
    expand = np.zeros((LANES, D_SSM))
    expand[np.arange(D_SSM) // SSM_HEAD_DIM, np.arange(D_SSM)] = 1.0
    sel = expand[:N_SSM_HEADS].T
    cx, cbk, cck, cz = col_blocks

    def const(shape):
        return pl.BlockSpec(shape, lambda b, c: (0,) * len(shape))

    return pl.pallas_call(
        functools.partial(_ssd_kernel, rows_in=rows_in, n_valid=n_valid),
        grid=(nb, n_chunks),
        in_specs=[
            pl.BlockSpec((1, rows_in, D_SSM), lambda b, c: (b, c, cx)),
            pl.BlockSpec((1, rows_in, bc_w), lambda b, c: (b, c, cbk)),
            pl.BlockSpec((1, rows_in, bc_w), lambda b, c: (b, c, cck)),
            pl.BlockSpec((1, rows_in, D_SSM), lambda b, c: (b, c, cz)),
            pl.BlockSpec((1, rows_in, LANES), lambda b, c: (b, c, 0)),
            pl.BlockSpec((1, N_SSM_HEADS, rows_in), lambda b, c: (b, 0, c)),
            pl.BlockSpec((1, HALO, CONV_DIM), lambda b, c: (b, 0, 0)),
            pl.BlockSpec((1, N_SSM_HEADS, SSM_HEAD_DIM, D_STATE), lambda b, c: (b, 0, 0, 0)),
            const((CONV_WIDTH, CONV_DIM)),
            const((1, CONV_DIM)),
            const((1, LANES)),
            const((N_SSM_HEADS, 1)),
            const((1, LANES)),
            const((N_SSM_HEADS, 1)),
            const((1, D_SSM)),
            const((1, D_SSM)),
            const((SSD_CHUNK, SSD_CHUNK)),
            const((SSD_CHUNK, SSD_CHUNK)),
            const((LANES, D_SSM)),
            const((D_SSM, N_SSM_HEADS)),
        ],
        out_specs=[
            pl.BlockSpec((1, rows_in, D_SSM), lambda b, c: (b, c, 0)),
            pl.BlockSpec((1, N_SSM_HEADS, SSM_HEAD_DIM, D_STATE), lambda b, c: (b, 0, 0, 0)),
        ],
        out_shape=[
            jax.ShapeDtypeStruct((nb, seq, D_SSM), BF16),
            jax.ShapeDtypeStruct((nb, N_SSM_HEADS, SSM_HEAD_DIM, D_STATE), F32),
        ],
        scratch_shapes=[
            pltpu.VMEM((HALO + SSD_CHUNK, CONV_DIM), F32),
            pltpu.VMEM((D_SSM, D_STATE), F32),
        ],
        compiler_params=_cparams(("arbitrary", "arbitrary")),
        name="ssd",
    )(src, src, src, src, dt_raw, dtt, halo, init_state,
      prm["conv_w"], prm["conv_b"], prm["dt_bias"], prm["dt_bias_t"], prm["a"], prm["a_t"],
      prm["d_skip"], prm["ssm_norm_w"],
      jnp.asarray(tri, BF16), jnp.asarray(tri.T, BF16), jnp.asarray(expand, BF16),
      jnp.asarray(sel, BF16))


def _merge_kernel(o_ref, s_ref, wa_ref, ws_ref, ga_ref, gs_ref, out_ref):
    a = _dot(o_ref[...], wa_ref[...])
    s = _dot(s_ref[...], ws_ref[...])
    out_ref[...] = (_sigmoid(ga_ref[...]) * a + _sigmoid(gs_ref[...]) * s).astype(BF16)


def _merge(o, s, wa, ws, proj, tm, tn):
    t = o.shape[0]
    ga0, gs0 = COL_GA // tn, COL_GS // tn
    return pl.pallas_call(
        _merge_kernel,
        grid=(t // tm, D_MODEL // tn),
        in_specs=[
            pl.BlockSpec((tm, ATT_WIDTH), lambda i, j: (i, 0)),
            pl.BlockSpec((tm, D_SSM), lambda i, j: (i, 0)),
            pl.BlockSpec((ATT_WIDTH, tn), lambda i, j: (0, j)),
            pl.BlockSpec((D_SSM, tn), lambda i, j: (0, j)),
            pl.BlockSpec((tm, tn), lambda i, j: (i, ga0 + j)),
            pl.BlockSpec((tm, tn), lambda i, j: (i, gs0 + j)),
        ],
        out_specs=pl.BlockSpec((tm, tn), lambda i, j: (i, j)),
        out_shape=jax.ShapeDtypeStruct((t, D_MODEL), BF16),
        compiler_params=_cparams(("arbitrary", "arbitrary")),
        name="merge",
    )(o, s, wa, ws, proj, proj)


ROUTE_E1, ROUTE_E2, ROUTE_W1, ROUTE_W2 = 0, 1, 2, 3
ROW_CHUNKS = D_MODEL // LANES


def _resid_kernel(x_ref, m_ref, wo_ref, nw_ref, wrh_ref, wrl_ref, br_ref, *rest, n_real):
    h_ref, u_ref, route_ref = rest[-3:]

    @pl.when(pl.program_id(0) >= n_real)
    def _():
        h_ref[...] = jnp.zeros(h_ref.shape, F32)
        u_ref[...] = jnp.zeros(u_ref.shape, F32)
        route_ref[...] = jnp.zeros(route_ref.shape, F32)

    @pl.when(pl.program_id(0) < n_real)
    def _():
        _resid_tile(x_ref, m_ref, wo_ref, nw_ref, wrh_ref, wrl_ref, br_ref,
                    h_ref, u_ref, route_ref)


def _resid_tile(x_ref, m_ref, wo_ref, nw_ref, wrh_ref, wrl_ref, br_ref, h_ref, u_ref, route_ref):
    h = x_ref[...] + _dot(m_ref[...], wo_ref[...])
    h_ref[...] = h
    ms = jnp.mean(h * h, axis=-1, keepdims=True)
    u = h * lax.rsqrt(ms + EPS) * nw_ref[...]
    u_hi, u_lo = _split2(u)
    u_ref[...] = u
    logits = (_dot(u_hi, wrh_ref[...]) + _dot(u_lo, wrh_ref[...])
              + _dot(u_hi, wrl_ref[...]) + br_ref[...])
    lane = lax.broadcasted_iota(jnp.int32, logits.shape, 1)
    lane_f = lane.astype(F32)
    far = float(2 * LANES)

    def first_max(vals):
        top = jnp.max(vals, axis=1, keepdims=True)
        idx = jnp.min(jnp.where(vals == top, lane_f, far), axis=1, keepdims=True)
        return top, idx

    is_group = (lane >= N_EXPERTS) & (lane < N_EXPERTS + N_EXPERT_GROUPS)
    gl = jnp.where(is_group, logits, NEG_BIG)
    g_top, g_idx = first_max(gl)
    g_p = 1.0 / jnp.sum(jnp.exp(gl - g_top), axis=1, keepdims=True)
    lo_lane = (g_idx - N_EXPERTS) * EXPERTS_PER_GROUP
    in_group = (lane_f >= lo_lane) & (lane_f < lo_lane + EXPERTS_PER_GROUP)
    el = jnp.where(in_group, logits, NEG_BIG)
    m1, i1 = first_max(el)
    el2 = jnp.where(lane_f == i1, NEG_BIG, el)
    m2, i2 = first_max(el2)
    e = jnp.exp(m2 - m1)
    w1 = 1.0 / (1.0 + e)
    w2 = e / (1.0 + e)
    route = jnp.where(lane == ROUTE_E1, i1, 0.0)
    route = jnp.where(lane == ROUTE_E2, i2, route)
    route = jnp.where(lane == ROUTE_W1, g_p * w1, route)
    route_ref[...] = jnp.where(lane == ROUTE_W2, g_p * w2, route)


def _resid(x, merged, wo, norm_w, wr_hi, wr_lo, br, tm, t_all, row_off, bufs):
    t = x.shape[0]
    blk_off = row_off // tm
    n_real = t // tm
    n_fill = pl.cdiv(t_all - t, tm) if bufs is None else 0

    def const(shape):
        return pl.BlockSpec(shape, lambda i: (0,) * len(shape))

    in_specs = [
        pl.BlockSpec((tm, D_MODEL), lambda i: (jnp.minimum(i, n_real - 1), 0)),
        pl.BlockSpec((tm, D_MODEL), lambda i: (jnp.minimum(i, n_real - 1), 0)),
        const((D_MODEL, D_MODEL)),
        const((1, D_MODEL)),
        const((D_MODEL, LANES)),
        const((D_MODEL, LANES)),
        const((1, LANES)),
    ]
    args = [x, merged, wo, norm_w, wr_hi, wr_lo, br]
    aliases = {}
    if bufs is not None:
        aliases = {len(args) + k: k for k in range(len(bufs))}
        in_specs += [pl.BlockSpec(memory_space=pl.ANY)] * len(bufs)
        args += list(bufs)
    return pl.pallas_call(
        functools.partial(_resid_kernel, n_real=n_real),
        grid=(n_real + n_fill,),
        in_specs=in_specs,
        out_specs=[
            pl.BlockSpec((tm, D_MODEL), lambda i: (i + blk_off, 0)),
            pl.BlockSpec((tm, D_MODEL), lambda i: (i + blk_off, 0)),
            pl.BlockSpec((tm, LANES), lambda i: (i + blk_off, 0)),
        ],
        out_shape=[
            jax.ShapeDtypeStruct((t_all, D_MODEL), F32),
            jax.ShapeDtypeStruct((t_all, D_MODEL), F32),
            jax.ShapeDtypeStruct((t_all, LANES), F32),
        ],
        input_output_aliases=aliases,
        compiler_params=_cparams(("arbitrary",)),
        name="resid",
    )(*args)


MOE_TILE = 256
COMB_TILE = 128
DMA_UNROLL = 8


def _route_plan(route, n_tiles):
    n_pairs = 2 * route.shape[0]
    pair_e = route[:, ROUTE_E1:ROUTE_E2 + 1].astype(jnp.int32).reshape(-1)
    onehot = (pair_e[:, None] == jnp.arange(N_EXPERTS, dtype=jnp.int32)[None, :]).astype(jnp.int32)
    csum = jnp.cumsum(onehot, axis=0)
    rank = jnp.sum((csum - onehot) * onehot, axis=1)
    tiles_e = (csum[-1] + MOE_TILE - 1) // MOE_TILE
    tile_end = jnp.cumsum(tiles_e)
    first_row = (tile_end - tiles_e) * MOE_TILE
    slot = jnp.sum(onehot * first_row[None, :], axis=1) + rank
    tok_of_slot = jnp.zeros(((n_tiles + 1) * MOE_TILE,), jnp.int32).at[slot].set(
        jnp.arange(n_pairs, dtype=jnp.int32) // 2)
    tile_start = (tile_end - tiles_e).astype(jnp.int32)
    slot_tab = slot.reshape(-1, COMB_TILE, 2).transpose(0, 2, 1).reshape(-1).astype(jnp.int32)
    return tile_start, tiles_e.astype(jnp.int32), tok_of_slot, slot_tab


ROW_DMA_PRIORITY = 0
BULK_DMA_PRIORITY = 1


def _gmm_kernel(ts_ref, ne_ref, tok_ref, u_hbm, wg_hbm, wu_hbm, wd_hbm, o_hbm,
                xbuf, obuf, gsem, osem, wgb, wub, wdb, wg_buf, wu_buf, wd_buf, wsem,
                *, n_tiles):
    e = pl.program_id(0)
    last = pl.num_programs(0) - 1
    n_used = ts_ref[last] + ne_ref[last]

    def row_copy(tile, r, buf):
        tok = tok_ref[tile * MOE_TILE + r]
        return pltpu.make_async_copy(
            u_hbm.at[tok], xbuf.at[buf, r // SUBLANES, :, r % SUBLANES, :], gsem.at[buf])

    def out_copy(tile, buf):
        return pltpu.make_async_copy(
            obuf.at[buf], o_hbm.at[pl.ds(tile * MOE_TILE, MOE_TILE), :], osem.at[buf])

    def gather_wait(tile, buf):
        del tile
        pltpu.make_async_copy(obuf.at[0], obuf.at[1], gsem.at[buf]).wait()

    @pl.when(e == 0)
    def _():
        def body(r, carry):
            row_copy(0, r, 0).start(priority=ROW_DMA_PRIORITY)
            return carry
        lax.fori_loop(0, MOE_TILE, body, 0, unroll=DMA_UNROLL)

    def weight_copies(expert, buf):
        return (pltpu.make_async_copy(wg_hbm.at[expert], wg_buf.at[buf], wsem.at[buf]),
                pltpu.make_async_copy(wu_hbm.at[expert], wu_buf.at[buf], wsem.at[buf]),
                pltpu.make_async_copy(wd_hbm.at[expert], wd_buf.at[buf], wsem.at[buf]))

    wcur = lax.rem(e, 2)

    @pl.when(e == 0)
    def _():
        for cp in weight_copies(0, 0):
            cp.start(priority=BULK_DMA_PRIORITY)

    @pl.when(e < last)
    def _():
        for cp in weight_copies(e + 1, 1 - wcur):
            cp.start(priority=BULK_DMA_PRIORITY)

    for cp in weight_copies(e, wcur):
        cp.wait()
    wgb[...] = wg_buf[wcur].astype(BF16)
    wub[...] = wu_buf[wcur].astype(BF16)
    wdb[...] = wd_buf[wcur].astype(BF16)

    def tile_body(j, carry):
        t = ts_ref[e] + j
        cur = lax.rem(t, 2)

        @pl.when(t >= 2)
        def _():
            out_copy(t - 2, cur).wait()

        gather_wait(t, cur)
        x = jnp.concatenate(
            [xbuf[cur, :, c].reshape(MOE_TILE, LANES) for c in range(ROW_CHUNKS)],
            axis=1).astype(BF16)
        for r in range(MOE_TILE):
            row_copy(t + 1, r, 1 - cur).start(priority=ROW_DMA_PRIORITY)
        hid = _silu(_dot(x, wgb[...])) * _dot(x, wub[...])
        obuf[cur] = _dot(hid.astype(BF16), wdb[...])
        out_copy(t, cur).start(priority=BULK_DMA_PRIORITY)
        return carry

    lax.fori_loop(0, ne_ref[e], tile_body, 0)

    @pl.when(e == last)
    def _():
        gather_wait(n_used, lax.rem(n_used, 2))
        out_copy(n_used - 2, lax.rem(n_used, 2)).wait()
        out_copy(n_used - 1, lax.rem(n_used - 1, 2)).wait()
        obuf[0] = jnp.zeros(obuf.shape[1:], F32)

        def fill(t, carry):
            cp = out_copy(t, 0)
            cp.start()
            cp.wait()
            return carry
        lax.fori_loop(n_used, n_tiles + 1, fill, 0)


def _gmm(u_all, plan, wg, wu, wd, n_tiles):
    tile_start, tiles_e, tok_of_slot, _ = plan
    assert 2 * u_all.shape[0] >= 2 * MOE_TILE
    grid_spec = pltpu.PrefetchScalarGridSpec(
        num_scalar_prefetch=3,
        grid=(N_EXPERTS,),
        in_specs=[
            pl.BlockSpec(memory_space=pl.ANY),
            pl.BlockSpec(memory_space=pl.ANY),
            pl.BlockSpec(memory_space=pl.ANY),
            pl.BlockSpec(memory_space=pl.ANY),
        ],
        out_specs=pl.BlockSpec(memory_space=pl.ANY),
        scratch_shapes=[
            pltpu.VMEM((2, MOE_TILE // SUBLANES, ROW_CHUNKS, SUBLANES, LANES), F32),
            pltpu.VMEM((2, MOE_TILE, D_MODEL), F32),
            pltpu.SemaphoreType.DMA((2,)),
            pltpu.SemaphoreType.DMA((2,)),
            pltpu.VMEM((D_MODEL, D_EXPERT), BF16),
            pltpu.VMEM((D_MODEL, D_EXPERT), BF16),
            pltpu.VMEM((D_EXPERT, D_MODEL), BF16),
            pltpu.VMEM((2, D_MODEL, D_EXPERT), F32),
            pltpu.VMEM((2, D_MODEL, D_EXPERT), F32),
            pltpu.VMEM((2, D_EXPERT, D_MODEL), F32),
            pltpu.SemaphoreType.DMA((2,)),
        ],
    )
    return pl.pallas_call(
        functools.partial(_gmm_kernel, n_tiles=n_tiles),
        grid_spec=grid_spec,
        out_shape=jax.ShapeDtypeStruct(((n_tiles + 1) * MOE_TILE, D_MODEL), F32),
        compiler_params=_cparams(("arbitrary",)),
        name="gmm",
    )(tile_start, tiles_e, tok_of_slot, u_all, wg, wu, wd)


def _combine_kernel(slot_ref, route_ref, h_ref, o_hbm, yp_ref, ys_ref, gbuf, sem, *, n_prompt):
    i = pl.program_id(0)
    n = pl.num_programs(0)
    cur = lax.rem(i, 2)
    rows = 2 * COMB_TILE

    def row_copy(tile, j, buf):
        slot = slot_ref[tile * rows + j]
        return pltpu.make_async_copy(
            o_hbm.at[pl.ds(slot, 1), :], gbuf.at[buf, pl.ds(j, 1), :], sem.at[buf])

    def issue(tile, buf):
        def body(j, carry):
            row_copy(tile, j, buf).start()
            return carry
        lax.fori_loop(0, rows, body, 0, unroll=DMA_UNROLL)

    def wait(tile, buf):
        del tile
        pltpu.make_async_copy(o_hbm.at[pl.ds(0, rows), :], gbuf.at[buf], sem.at[buf]).wait()

    @pl.when(i == 0)
    def _():
        issue(0, 0)

    @pl.when(i + 1 < n)
    def _():
        issue(i + 1, 1 - cur)

    wait(i, cur)
    w1 = route_ref[:, ROUTE_W1:ROUTE_W1 + 1]
    w2 = route_ref[:, ROUTE_W2:ROUTE_W2 + 1]
    y = h_ref[...] + w1 * gbuf[cur, 0:COMB_TILE, :] + w2 * gbuf[cur, COMB_TILE:rows, :]

    @pl.when(i < n_prompt)
    def _():
        yp_ref[...] = y

    @pl.when(i >= n_prompt)
    def _():
        ys_ref[...] = y


def _combine(route, h_all, o_sorted, plan, t_prompt):
    t_all = h_all.shape[0]
    n_prompt = t_prompt // COMB_TILE
    slot_tab = plan[3]
    grid_spec = pltpu.PrefetchScalarGridSpec(
        num_scalar_prefetch=1,
        grid=(t_all // COMB_TILE,),
        in_specs=[
            pl.BlockSpec((COMB_TILE, LANES), lambda i, st: (i, 0)),
            pl.BlockSpec((COMB_TILE, D_MODEL), lambda i, st: (i, 0)),
            pl.BlockSpec(memory_space=pl.ANY),
        ],
        out_specs=[
            pl.BlockSpec((COMB_TILE, D_MODEL), lambda i, st: (jnp.minimum(i, n_prompt - 1), 0)),
            pl.BlockSpec((COMB_TILE, D_MODEL), lambda i, st: (0, 0)),
        ],
        scratch_shapes=[
            pltpu.VMEM((2, 2 * COMB_TILE, D_MODEL), F32),
            pltpu.SemaphoreType.DMA((2,)),
        ],
    )
    return pl.pallas_call(
        functools.partial(_combine_kernel, n_prompt=n_prompt),
        grid_spec=grid_spec,
        out_shape=[
            jax.ShapeDtypeStruct((t_prompt, D_MODEL), F32),
            jax.ShapeDtypeStruct((t_all - t_prompt, D_MODEL), F32),
        ],
        compiler_params=_cparams(("arbitrary",)),
        name="combine",
    )(slot_tab, route, h_all, o_sorted)


def _tiles(n_rows):
    return dict(
        proj_rows=min(n_rows, 1024), proj_cols=1024,
        qk_norm_rows=min(n_rows, 512),
        attn_block=512,
        merge_rows=min(n_rows, 1024), merge_cols=512,
        resid_rows=min(n_rows, 256),
    )


def _layer_tokens(x2d, w):
    tiles = _tiles(x2d.shape[0])
    proj, dt_raw = _proj(x2d, w["norm_attn_w"], w["w_a"], w["w_b"], w["w_dt"],
                         tiles["proj_rows"], tiles["proj_cols"])
    qn, kn, kb, vb = _qk_norm(proj, w["q_norm_w"], w["k_norm_w"], tiles["qk_norm_rows"])
    return proj, dt_raw, qn, kn, kb, vb


def _branch_merge(x2d, o, s, proj, w, t_all, row_off, bufs):
    tiles = _tiles(x2d.shape[0])
    merged = _merge(o, s, w["w_att_out"], w["w_ssm_out"], proj,
                    tiles["merge_rows"], tiles["merge_cols"])
    return _resid(x2d, merged, w["w_o"], w["norm_ffn_w"], w["wr_hi"], w["wr_lo"], w["br"],
                  tiles["resid_rows"], t_all, row_off, bufs)


def kernel(x_prompt, x_sample, cache_k, cache_v, state_ssm, state_conv, page_table, norm_attn_w, w_in, q_norm_w, k_norm_w, lambda_q1, lambda_k1, lambda_q2, lambda_k2, subln_w, w_att_out, conv_w, conv_b, dt_bias, a_log, d_skip, ssm_norm_w, w_ssm_out, w_o, norm_ffn_w, w_group_router, b_group_router, w_expert_router, b_expert_router, w_gate, w_up, w_down):
    layer = 0
    nb, seq, _ = x_prompt.shape
    db, dec_seq, _ = x_sample.shape

    w_in_l = w_in[layer]
    c_dt = Q_WIDTH + K_WIDTH + V_WIDTH + D_SSM + CONV_DIM
    w_a = w_in_l[:, :c_dt].astype(BF16)
    w_b = w_in_l[:, c_dt + N_SSM_HEADS:].astype(BF16)
    w_dt = jnp.pad(w_in_l[:, c_dt:c_dt + N_SSM_HEADS], ((0, 0), (0, LANES - N_SSM_HEADS))).astype(BF16)
    wr = jnp.concatenate([w_expert_router[layer], w_group_router[layer]], axis=1)
    wr = jnp.pad(wr, ((0, 0), (0, LANES - wr.shape[1])))
    wr_hi = wr.astype(BF16)
    wr_lo = (wr - wr_hi.astype(F32)).astype(BF16)
    br = jnp.concatenate([b_expert_router[layer], b_group_router[layer]])
    br = jnp.pad(br, (0, LANES - br.shape[0])).reshape(1, LANES)
    pad_h = (0, LANES - N_SSM_HEADS)
    w = dict(
        norm_attn_w=norm_attn_w[layer].reshape(1, D_MODEL), w_a=w_a, w_b=w_b, w_dt=w_dt,
        q_norm_w=q_norm_w[layer], k_norm_w=k_norm_w[layer],
        w_att_out=w_att_out[layer].astype(BF16), w_ssm_out=w_ssm_out[layer].astype(BF16),
        w_o=w_o[layer].astype(BF16), norm_ffn_w=norm_ffn_w[layer].reshape(1, D_MODEL),
        wr_hi=wr_hi, wr_lo=wr_lo, br=br,
        w_gate=w_gate[layer], w_up=w_up[layer], w_down=w_down[layer],
    )
    ssm_prm = dict(
        conv_w=conv_w[layer], conv_b=conv_b[layer].reshape(1, CONV_DIM),
        dt_bias=jnp.pad(dt_bias[layer], pad_h).reshape(1, LANES),
        dt_bias_t=dt_bias[layer].reshape(N_SSM_HEADS, 1),
        a=jnp.pad(-jnp.exp(a_log[layer]), pad_h).reshape(1, LANES),
        a_t=(-jnp.exp(a_log[layer])).reshape(N_SSM_HEADS, 1),
        d_skip=jnp.repeat(d_skip[layer], SSM_HEAD_DIM).reshape(1, D_SSM),
        ssm_norm_w=ssm_norm_w[layer].reshape(1, D_SSM),
    )
    lam_vecs = jnp.stack([lambda_q1[layer], lambda_k1[layer], lambda_q2[layer], lambda_k2[layer]])
    sw = subln_w[layer]
    ssd_cols = (COL_X // D_SSM, COL_B // (N_GROUPS * D_STATE), COL_C // (N_GROUPS * D_STATE),
                COL_Z // D_SSM)

    xp = x_prompt.reshape(nb * seq, D_MODEL)
    n_tok = db * dec_seq
    t_prompt = nb * seq
    t_all = t_prompt + n_tok
    proj_p, dt_p, qn_p, kn_p, kb_p, vb_p = _layer_tokens(xp, w)
    o_p = _attn_prompt(qn_p, kb_p, vb_p, lam_vecs, sw, _tiles(t_prompt)["attn_block"])
    s_p, ssm_p = _ssd(
        proj_p.reshape(nb, seq, PROJ_WIDTH), ssd_cols, dt_p.reshape(nb, seq, LANES),
        jnp.zeros((nb, HALO, CONV_DIM), F32), jnp.zeros((nb, N_SSM_HEADS, SSM_HEAD_DIM, D_STATE), F32),
        ssm_prm, SSD_CHUNK, SSD_CHUNK)
    bufs = _branch_merge(xp, o_p, s_p.reshape(t_prompt, D_SSM), proj_p, w, t_all, 0, None)
    keep = CONV_WIDTH - 1
    conv_p = proj_p.reshape(nb, seq, PROJ_WIDTH)[:, seq - keep:, COL_X:COL_X + CONV_DIM]

    xs = x_sample.reshape(db * dec_seq, D_MODEL)
    proj_s, dt_s, qn_s, kn_s, _, _ = _layer_tokens(xs, w)
    v_s = proj_s[:, COL_V:COL_V + V_WIDTH]
    o_s = _attn_sample(qn_s, kn_s, v_s, cache_k[layer], cache_v[layer], page_table, lam_vecs, sw)
    rows_s = SUBLANES
    pad_rows = ((0, 0), (0, rows_s - dec_seq), (0, 0))
    src_s = jnp.pad(proj_s[:, COL_Z:COL_GA].reshape(db, dec_seq, COL_GA - COL_Z), pad_rows)
    halo_s = jnp.pad(state_conv[layer], ((0, 0), (HALO - (CONV_WIDTH - 1), 0), (0, 0)))
    cols_s = ((COL_X - COL_Z) // D_SSM, (COL_B - COL_Z) // (N_GROUPS * D_STATE),
              (COL_C - COL_Z) // (N_GROUPS * D_STATE), 0)
    s_s, ssm_s = _ssd(
        src_s, cols_s, jnp.pad(dt_s.reshape(db, dec_seq, LANES), pad_rows), halo_s,
        state_ssm[layer], ssm_prm, rows_s, dec_seq)
    s_s = s_s[:, :dec_seq].reshape(n_tok, D_SSM)
    h_all, u_all, route = _branch_merge(xs, o_s, s_s, proj_s, w, t_all, t_prompt, bufs)

    n_tiles = 2 * t_all // MOE_TILE + N_EXPERTS
    plan = _route_plan(route, n_tiles)
    o_sorted = _gmm(u_all.reshape(t_all, ROW_CHUNKS, LANES), plan, w["w_gate"], w["w_up"], w["w_down"], n_tiles)
    y_p, y_s = _combine(route, h_all, o_sorted, plan, t_prompt)
    conv_s =proj_s.reshape(db, dec_seq, PROJ_WIDTH)[:, dec_seq - keep:, COL_X:COL_X + CONV_DIM]

    return (
        y_p.reshape(nb, seq, D_MODEL),
        y_s.reshape(db, dec_seq, D_MODEL),
        kn_p.reshape(1, nb, seq, N_KV_HEADS, 2 * HEAD_DIM),
        proj_p[:, COL_V:COL_V + V_WIDTH].reshape(1, nb, seq, N_KV_HEADS, V_DIM),
        ssm_p.reshape(1, nb, N_SSM_HEADS, SSM_HEAD_DIM, D_STATE),
        conv_p[None],
        kn_s.reshape(1, db, dec_seq, N_KV_HEADS, 2 * HEAD_DIM),
        v_s.reshape(1, db, dec_seq, N_KV_HEADS, V_DIM),
        ssm_s.reshape(1, db, N_SSM_HEADS, SSM_HEAD_DIM, D_STATE),
        conv_s[None],
    )
```
